```python
import math
import jax, jax.numpy as jnp
from jax import lax
import numpy as np

D_MODEL = 1024
BATCH = 4
SEQ = 8192
DEPTH = 2
DEC_BATCH = 16
DEC_SEQ = 16
PAST_LEN = 2048

CHUNK = 64
Q_BLOCK = 128
EPS = 1e-5
SSM_EXPAND = 2
D_INNER = SSM_EXPAND * D_MODEL
SSM_HEAD_DIM = 64
SSM_HEADS = D_INNER // SSM_HEAD_DIM
SSM_GROUPS = 4
HEADS_PER_GROUP = SSM_HEADS // SSM_GROUPS
D_STATE = 128
CONV_W = 4
CONV_DIM = D_INNER + 2 * SSM_GROUPS * D_STATE
IN_DIM = D_INNER + CONV_DIM + SSM_HEADS
SSD_BLOCK = CHUNK
N_HEADS = 8
DK = D_MODEL // N_HEADS // 2
DV = 2 * DK
ROT_DIM = DK // 4
ROPE_THETA = 500000.0
D_FF = 2816
N_EXPERTS = 8
TOP_K = 2
EXPERT_FF = 3584
N_A_LAYERS = DEPTH // 2
N_B_LAYERS = DEPTH - N_A_LAYERS
N_DENSE = (DEPTH + 1) // 2
N_MOE = DEPTH // 2

kernel_name = "yoco_mamba2_diffattn_moe_stream_step"


def rmsnorm(x, w):
    xf = x.astype(jnp.float32)
    xf = xf * lax.rsqrt(jnp.mean(xf * xf, axis=-1, keepdims=True) + EPS)
    return xf.astype(x.dtype) * w


def rope_partial(t, pos):
    half = ROT_DIM // 2
    inv_freq = ROPE_THETA ** (-jnp.arange(half, dtype=jnp.float32) / half)
    ang = pos.astype(jnp.float32)[:, None] * inv_freq[None, :]
    cos = jnp.cos(ang)[None, :, None, None, :]
    sin = jnp.sin(ang)[None, :, None, None, :]
    t1 = t[..., :half].astype(jnp.float32)
    t2 = t[..., half:ROT_DIM].astype(jnp.float32)
    rot = jnp.concatenate([t1 * cos - t2 * sin, t1 * sin + t2 * cos], axis=-1).astype(t.dtype)
    return jnp.concatenate([rot, t[..., ROT_DIM:]], axis=-1)


def ssd_scan(x, dt, a, b_in, c_in, init_state, block_len):
    bsz, L = x.shape[0], x.shape[1]
    nc = L // block_len
    G, R, P, N = SSM_GROUPS, HEADS_PER_GROUP, SSM_HEAD_DIM, D_STATE
    xf = x.astype(jnp.float32).reshape(bsz, nc, block_len, G, R, P)
    dtf = dt.astype(jnp.float32).reshape(bsz, nc, block_len, G, R)
    bf = b_in.astype(jnp.float32).reshape(bsz, nc, block_len, G, N)
    cf = c_in.astype(jnp.float32).reshape(bsz, nc, block_len, G, N)
    xdt = xf * dtf[..., None]
    cs = jnp.cumsum(dtf * a.astype(jnp.float32).reshape(G, R), axis=2)
    causal = jnp.tril(jnp.ones((block_len, block_len), dtype=bool))
    seg = cs[:, :, :, None] - cs[:, :, None, :]
    decay = jnp.exp(jnp.where(causal[:, :, None, None], seg, -jnp.inf))
    cb = jnp.einsum("bclgn,bcsgn->bclsg", cf, bf)
    y_diag = jnp.einsum("bclsg,bclsgr,bcsgrp->bclgrp", cb, decay, xdt)
    to_end = jnp.exp(cs[:, :, -1:] - cs)
    block_states = jnp.einsum("bclgn,bclgr,bclgrp->bcgrpn", bf, to_end, xdt)
    block_decay = jnp.exp(cs[:, :, -1])

    def step(state, inp):
        d, s_new = inp
        return state * d[..., None, None] + s_new, state

    final, entering = lax.scan(
        step,
        init_state.astype(jnp.float32).reshape(bsz, G, R, P, N),
        (jnp.moveaxis(block_decay, 1, 0), jnp.moveaxis(block_states, 1, 0)),
    )
    entering = jnp.moveaxis(entering, 0, 1)
    y_off = jnp.einsum("bclgn,bcgrpn,bclgr->bclgrp", cf, entering, jnp.exp(cs))
    y = (y_diag + y_off).reshape(bsz, L, G * R, P)
    return y, final.reshape(bsz, G * R, P, N)


def mamba_layer(x, conv_state, ssm_state, i, P, block_len):
    bsz, L = x.shape[0], x.shape[1]
    h = rmsnorm(x, P["mamba_norm_w"][i])
    zxbcdt = h @ P["mamba_w_in"][i]
    z = zxbcdt[..., :D_INNER]
    xbc = zxbcdt[..., D_INNER:D_INNER + CONV_DIM]
    dt_raw = zxbcdt[..., D_INNER + CONV_DIM:]
    ext = jnp.concatenate([conv_state.astype(xbc.dtype), xbc], axis=1)
    new_conv = ext[:, L:]
    w = P["mamba_conv_w"][i]
    conv = P["mamba_conv_b"][i] + ext[:, 0:L] * w[0]
    for k in range(1, CONV_W):
        conv = conv + ext[:, k:k + L] * w[k]
    conv = jax.nn.silu(conv)
    xs = conv[..., :D_INNER].reshape(bsz, L, SSM_HEADS, SSM_HEAD_DIM)
    b_in = conv[..., D_INNER:D_INNER + SSM_GROUPS * D_STATE].reshape(bsz, L, SSM_GROUPS, D_STATE)
    c_in = conv[..., D_INNER + SSM_GROUPS * D_STATE:].reshape(bsz, L, SSM_GROUPS, D_STATE)
    dt = jax.nn.softplus(dt_raw.astype(jnp.float32) + P["mamba_dt_bias"][i].astype(jnp.float32))
    a = -jnp.exp(P["mamba_a_log"][i].astype(jnp.float32))
    y, new_ssm = ssd_scan(xs, dt, a, b_in, c_in, ssm_state, block_len)
    y = y + P["mamba_d"][i].astype(jnp.float32)[:, None] * xs.astype(jnp.float32)
    y = y.astype(x.dtype).reshape(bsz, L, D_INNER) * jax.nn.silu(z)
    gw = P["mamba_gn_w"][i].reshape(SSM_GROUPS, D_INNER // SSM_GROUPS)
    y = rmsnorm(y.reshape(bsz, L, SSM_GROUPS, D_INNER // SSM_GROUPS), gw).reshape(bsz, L, D_INNER)
    return x + y @ P["mamba_w_out"][i], new_conv, new_ssm.astype(x.dtype)


def shared_kv(x, pos, P):
    bsz, L = x.shape[0], x.shape[1]
    h = rmsnorm(x, P["kv_norm_w"])
    k = rope_partial((h @ P["w_k"]).reshape(bsz, L, N_HEADS, 2, DK), pos)
    v = (h @ P["w_v"]).reshape(bsz, L, N_HEADS, DV)
    return k.reshape(bsz, L, N_HEADS, 2 * DK), v


def diff_attend(q, q_pos, k, v, k_pos, lam):
    s = jnp.einsum("bqhcd,bkhcd->bhcqk", q, k).astype(jnp.float32) * (DK ** -0.5)
    visible = (k_pos[None, :] // CHUNK) <= (q_pos[:, None] // CHUNK)
    s = jnp.where(visible, s, -jnp.inf)
    p = jax.nn.softmax(s, axis=-1)
    a = p[:, :, 0] - lam * p[:, :, 1]
    return jnp.einsum("bhqk,bkhv->bqhv", a.astype(v.dtype), v)


def diff_layer(x, pos, k_all, v_all, k_pos, j, layer_idx, P, q_block):
    bsz, L = x.shape[0], x.shape[1]
    lambda_init = 0.8 - 0.6 * math.exp(-0.3 * layer_idx)
    h = rmsnorm(x, P["attn_norm_w"][j])
    q = rope_partial((h @ P["w_q"][j]).reshape(bsz, L, N_HEADS, 2, DK), pos)
    lam = (jnp.exp(jnp.sum(P["lambda_q1"][j].astype(jnp.float32) * P["lambda_k1"][j].astype(jnp.float32)))
           - jnp.exp(jnp.sum(P["lambda_q2"][j].astype(jnp.float32) * P["lambda_k2"][j].astype(jnp.float32)))
           + lambda_init)
    kk = k_all.reshape(k_all.shape[0], k_all.shape[1], N_HEADS, 2, DK)
    nb = L // q_block
    qb = jnp.swapaxes(q.reshape(bsz, nb, q_block, N_HEADS, 2, DK), 0, 1)
    pb = pos.reshape(nb, q_block)
    o = lax.map(lambda args: diff_attend(args[0], args[1], kk, v_all, k_pos, lam), (qb, pb))
    o = jnp.swapaxes(o, 0, 1).reshape(bsz, L, N_HEADS, DV)
    o = rmsnorm(o, P["subln_w"][j]) * (1.0 - lambda_init)
    return x + o.reshape(bsz, L, N_HEADS * DV) @ P["w_o"][j]


def swiglu_ffn(x, i, P):
    h = rmsnorm(x, P["ffn_norm_w"][i])
    return x + (jax.nn.silu(h @ P["ffn_w_gate"][i]) * (h @ P["ffn_w_up"][i])) @ P["ffn_w_down"][i]


def moe_ffn(x, i, P):
    h = rmsnorm(x, P["moe_norm_w"][i])
    logits = (h @ P["moe_w_router"][i]).astype(jnp.float32)
    top_val, top_idx = lax.top_k(logits, TOP_K)
    gates = jax.nn.softmax(top_val, axis=-1)
    combine = jnp.einsum("blk,blke->ble", gates,
                         jax.nn.one_hot(top_idx, N_EXPERTS, dtype=jnp.float32)).astype(x.dtype)
    wg, wu, wd = P["moe_w_gate"][i], P["moe_w_up"][i], P["moe_w_down"][i]
    out = jnp.zeros_like(x)
    for e in range(N_EXPERTS):
        act = jax.nn.silu(h @ wg[e]) * (h @ wu[e])
        out = out + combine[..., e:e + 1] * (act @ wd[e])
    return x + out


def run_trunk(x, pos, conv_in, ssm_in, past_k, past_v, ssd_block, q_block, P):
    new_conv, new_ssm = [], []
    k_new = v_new = k_all = v_all = k_pos = None
    for l in range(DEPTH):
        if l < N_A_LAYERS:
            x, c, s = mamba_layer(x, conv_in[l], ssm_in[l], l, P, ssd_block)
            new_conv.append(c)
            new_ssm.append(s)
        else:
            x = diff_layer(x, pos, k_all, v_all, k_pos, l - N_A_LAYERS, l, P, q_block)
        if l % 2 == 0:
            x = swiglu_ffn(x, l // 2, P)
        else:
            x = moe_ffn(x, l // 2, P)
        if l == N_A_LAYERS - 1:
            k_new, v_new = shared_kv(x, pos, P)
            k_all = jnp.concatenate([past_k.astype(k_new.dtype), k_new], axis=1)
            v_all = jnp.concatenate([past_v.astype(v_new.dtype), v_new], axis=1)
            k_pos = jnp.arange(k_all.shape[1], dtype=jnp.int32)
    y = rmsnorm(x, P["final_norm_w"])
    return y, jnp.stack(new_conv, axis=0), jnp.stack(new_ssm, axis=0), k_new, v_new


def setup_inputs(seed: int = 0) -> dict:
    key = jax.random.key(seed)
    ks = iter(jax.random.split(key, 48))

    def nrm(shape, scale):
        return jax.random.normal(next(ks), shape, jnp.float32) * scale

    def gain(shape):
        return 1.0 + nrm(shape, 0.02)

    inp = {}
    inp["x_prompt"] = nrm((BATCH, SEQ, D_MODEL), 1.0)
    inp["x_sample"] = nrm((DEC_BATCH, DEC_SEQ, D_MODEL), 1.0)
    inp["cache_conv"] = nrm((N_A_LAYERS, DEC_BATCH, CONV_W - 1, CONV_DIM), 1.0)
    inp["state_ssm"] = nrm((N_A_LAYERS, DEC_BATCH, SSM_HEADS, SSM_HEAD_DIM, D_STATE), 0.1)
    inp["cache_k"] = nrm((DEC_BATCH, PAST_LEN, N_HEADS, 2 * DK), 1.0)
    inp["cache_v"] = nrm((DEC_BATCH, PAST_LEN, N_HEADS, DV), 1.0)
    inp["mamba_norm_w"] = gain((N_A_LAYERS, D_MODEL))
    inp["mamba_w_in"] = nrm((N_A_LAYERS, D_MODEL, IN_DIM), D_MODEL ** -0.5)
    inp["mamba_conv_w"] = nrm((N_A_LAYERS, CONV_W, CONV_DIM), CONV_W ** -0.5)
    inp["mamba_conv_b"] = nrm((N_A_LAYERS, CONV_DIM), 0.02)
    dt0 = jnp.exp(jax.random.uniform(next(ks), (N_A_LAYERS, SSM_HEADS), jnp.float32)
                  * (math.log(0.1) - math.log(0.001)) + math.log(0.001))
    inp["mamba_dt_bias"] = dt0 + jnp.log(-jnp.expm1(-dt0))
    inp["mamba_a_log"] = jnp.log(jax.random.uniform(next(ks), (N_A_LAYERS, SSM_HEADS), jnp.float32,
                                                     minval=1.0, maxval=16.0))
    inp["mamba_d"] = gain((N_A_LAYERS, SSM_HEADS))
    inp["mamba_gn_w"] = gain((N_A_LAYERS, D_INNER))
    inp["mamba_w_out"] = nrm((N_A_LAYERS, D_INNER, D_MODEL), D_INNER ** -0.5)
    inp["kv_norm_w"] = gain((D_MODEL,))
    inp["w_k"] = nrm((D_MODEL, N_HEADS * 2 * DK), D_MODEL ** -0.5)
    inp["w_v"] = nrm((D_MODEL, N_HEADS * DV), D_MODEL ** -0.5)
    inp["attn_norm_w"] = gain((N_B_LAYERS, D_MODEL))
    inp["w_q"] = nrm((N_B_LAYERS, D_MODEL, N_HEADS * 2 * DK), D_MODEL ** -0.5)
    inp["lambda_q1"] = nrm((N_B_LAYERS, DK), 0.1)
    inp["lambda_k1"] = nrm((N_B_LAYERS, DK), 0.1)
    inp["lambda_q2"] = nrm((N_B_LAYERS, DK), 0.1)
    inp["lambda_k2"] = nrm((N_B_LAYERS, DK), 0.1)
    inp["subln_w"] = gain((N_B_LAYERS, DV))
    inp["w_o"] = nrm((N_B_LAYERS, N_HEADS * DV, D_MODEL), (N_HEADS * DV) ** -0.5)
    inp["ffn_norm_w"] = gain((N_DENSE, D_MODEL))
    inp["ffn_w_gate"] = nrm((N_DENSE, D_MODEL, D_FF), D_MODEL ** -0.5)
    inp["ffn_w_up"] = nrm((N_DENSE, D_MODEL, D_FF), D_MODEL ** -0.5)
    inp["ffn_w_down"] = nrm((N_DENSE, D_FF, D_MODEL), D_FF ** -0.5)
    inp["moe_norm_w"] = gain((N_MOE, D_MODEL))
    inp["moe_w_router"] = nrm((N_MOE, D_MODEL, N_EXPERTS), D_MODEL ** -0.5)
    inp["moe_w_gate"] = nrm((N_MOE, N_EXPERTS, D_MODEL, EXPERT_FF), D_MODEL ** -0.5)
    inp["moe_w_up"] = nrm((N_MOE, N_EXPERTS, D_MODEL, EXPERT_FF), D_MODEL ** -0.5)
    inp["moe_w_down"] = nrm((N_MOE, N_EXPERTS, EXPERT_FF, D_MODEL), EXPERT_FF ** -0.5)
    inp["final_norm_w"] = gain((D_MODEL,))
    return inp


def reference(x_prompt, x_sample, cache_conv, state_ssm, cache_k, cache_v,
              mamba_norm_w, mamba_w_in, mamba_conv_w, mamba_conv_b, mamba_dt_bias, mamba_a_log,
              mamba_d, mamba_gn_w, mamba_w_out,
              kv_norm_w, w_k, w_v,
              attn_norm_w, w_q, lambda_q1, lambda_k1, lambda_q2, lambda_k2, subln_w, w_o,
              ffn_norm_w, ffn_w_gate, ffn_w_up, ffn_w_down,
              moe_norm_w, moe_w_router, moe_w_gate, moe_w_up, moe_w_down,
              final_norm_w):
    P = dict(mamba_norm_w=mamba_norm_w, mamba_w_in=mamba_w_in, mamba_conv_w=mamba_conv_w,
             mamba_conv_b=mamba_conv_b, mamba_dt_bias=mamba_dt_bias, mamba_a_log=mamba_a_log,
             mamba_d=mamba_d, mamba_gn_w=mamba_gn_w, mamba_w_out=mamba_w_out,
             kv_norm_w=kv_norm_w, w_k=w_k, w_v=w_v,
             attn_norm_w=attn_norm_w, w_q=w_q, lambda_q1=lambda_q1, lambda_k1=lambda_k1,
             lambda_q2=lambda_q2, lambda_k2=lambda_k2, subln_w=subln_w, w_o=w_o,
             ffn_norm_w=ffn_norm_w, ffn_w_gate=ffn_w_gate, ffn_w_up=ffn_w_up, ffn_w_down=ffn_w_down,
             moe_norm_w=moe_norm_w, moe_w_router=moe_w_router, moe_w_gate=moe_w_gate,
             moe_w_up=moe_w_up, moe_w_down=moe_w_down, final_norm_w=final_norm_w)
    bp, lp = x_prompt.shape[0], x_prompt.shape[1]
    pos_p = jnp.arange(lp, dtype=jnp.int32)
    conv0 = jnp.zeros((N_A_LAYERS, bp, CONV_W - 1, CONV_DIM), x_prompt.dtype)
    ssm0 = jnp.zeros((N_A_LAYERS, bp, SSM_HEADS, SSM_HEAD_DIM, D_STATE), x_prompt.dtype)
    k0 = jnp.zeros((bp, 0, N_HEADS, 2 * DK), x_prompt.dtype)
    v0 = jnp.zeros((bp, 0, N_HEADS, DV), x_prompt.dtype)
    y_prompt, conv_p, ssm_p, k_p, v_p = run_trunk(x_prompt, pos_p, conv0, ssm0, k0, v0,
                                                  SSD_BLOCK, Q_BLOCK, P)
    ls = x_sample.shape[1]
    pos_s = cache_k.shape[1] + jnp.arange(ls, dtype=jnp.int32)
    y_sample, conv_s, ssm_s, k_s, v_s = run_trunk(x_sample, pos_s, cache_conv, state_ssm,
                                                  cache_k, cache_v, ls, ls, P)
    return (y_prompt, y_sample, conv_p, ssm_p, k_p, v_p, conv_s, ssm_s, k_s, v_s)
```

```python
import functools
import math

import numpy as np
import jax
import jax.numpy as jnp
from jax import lax
from jax.experimental import pallas as pl
from jax.experimental.pallas import tpu as pltpu

F32 = jnp.float32
BF16 = jnp.bfloat16

EPS = 1e-5
D_MODEL = 1024
D_INNER = 2048
SSM_HEAD_DIM = 64
SSM_HEADS = 32
SSM_GROUPS = 4
D_STATE = 128
GROUP_W = D_INNER // SSM_GROUPS
CONV_W = 4
CONV_DIM = D_INNER + 2 * SSM_GROUPS * D_STATE
DT_PAD = 256
PROJ_W = D_INNER + CONV_DIM + DT_PAD
N_HEADS = 8
DK = 64
DV = 128
ROT_DIM = 16
ROPE_THETA = 500000.0
CHUNK = 64
CHUNK_SHIFT = 6
assert 1 << CHUNK_SHIFT == CHUNK
N_EXPERTS = 8
LANES = 128
NEG_BIG = -1e30
VMEM_LIMIT = 56 * 1024 * 1024


def _params(*sem):
    return pltpu.CompilerParams(dimension_semantics=sem, vmem_limit_bytes=VMEM_LIMIT)


def _sigmoid(x):
    return 1.0 / (1.0 + jnp.exp(-x))


def _rms_scale(x):
    return x * lax.rsqrt(jnp.mean(x * x, axis=-1, keepdims=True) + EPS)


def _split3(x):
    hi = x.astype(BF16)
    r1 = x - hi.astype(F32)
    mid = r1.astype(BF16)
    lo = (r1 - mid.astype(F32)).astype(BF16)
    return hi, mid, lo


def _dot(a, b):
    return jnp.dot(a, b, preferred_element_type=F32)


def _dot_exact_rhs(x, m_bf16):
    hi, mid, lo = _split3(x)
    return _dot(hi, m_bf16) + _dot(mid, m_bf16) + _dot(lo, m_bf16)


def _dot_exact_lhs(m_bf16, x):
    hi, mid, lo = _split3(x)
    return _dot(m_bf16, hi) + _dot(m_bf16, mid) + _dot(m_bf16, lo)


def _rms_matmul_kernel(x_ref, nw_ref, w_ref, o_ref, h_ref):
    @pl.when(pl.program_id(1) == 0)
    def _():
        h_ref[...] = (_rms_scale(x_ref[...]) * nw_ref[...]).astype(BF16)

    o_ref[...] = _dot(h_ref[...], w_ref[...]).astype(o_ref.dtype)


def rms_matmul(x, nw, w, *, tm, tn):
    t, d = x.shape
    n = w.shape[1]
    return pl.pallas_call(
        _rms_matmul_kernel,
        grid=(t // tm, n // tn),
        in_specs=[pl.BlockSpec((tm, d), lambda i, j: (i, 0)),
                  pl.BlockSpec((1, d), lambda i, j: (0, 0)),
                  pl.BlockSpec((d, tn), lambda i, j: (0, j))],
        out_specs=pl.BlockSpec((tm, tn), lambda i, j: (i, j)),
        out_shape=jax.ShapeDtypeStruct((t, n), F32),
        scratch_shapes=[pltpu.VMEM((tm, d), BF16)],
        compiler_params=_params("arbitrary", "arbitrary"),
        name="rms_matmul",
    )(x, nw, w)


def _ssd_kernel(z_ref, xs_ref, b_ref, c_ref, dt_ref, cin_ref, sin_ref, cw_ref, cb_ref,
                dtb_ref, alog_ref, dexp_ref, gnw_ref, e_ref,
                y_ref, sout_ref, state_ref, cbuf_ref, *, c, cv):
    step = pl.program_id(1)

    @pl.when(step == 0)
    def _():
        state_ref[...] = sin_ref[0].T
        cbuf_ref[0:8, :] = cin_ref[0]

    cbuf_ref[8:8 + cv, 0:D_INNER] = xs_ref[0]
    cbuf_ref[8:8 + cv, D_INNER:D_INNER + GROUP_W] = b_ref[0]
    cbuf_ref[8:8 + cv, D_INNER + GROUP_W:CONV_DIM] = c_ref[0]
    if cv < c:
        cbuf_ref[8 + cv:8 + c, :] = jnp.zeros((c - cv, CONV_DIM), F32)

    conv = cb_ref[...] + cbuf_ref[5:5 + c, :] * cw_ref[0:1, :]
    for k in range(1, CONV_W):
        conv = conv + cbuf_ref[5 + k:5 + k + c, :] * cw_ref[k:k + 1, :]
    conv = conv * _sigmoid(conv)
    if cv == c:
        cbuf_ref[5:8, :] = cbuf_ref[5 + c:8 + c, :]

    xs = conv[:, 0:D_INNER]
    bm = conv[:, D_INNER:D_INNER + GROUP_W]
    cm = conv[:, D_INNER + GROUP_W:CONV_DIM]

    dt_in = dt_ref[0][:, 0:LANES] + dtb_ref[...]
    dt = jnp.maximum(dt_in, 0.0) + jnp.log1p(jnp.exp(-jnp.abs(dt_in)))
    if cv < c:
        dt = jnp.concatenate([dt, jnp.zeros((c - cv, LANES), F32)], axis=0)
    a = -jnp.exp(alog_ref[...])
    dta = dt * a

    row = lax.broadcasted_iota(jnp.int32, (c, c), 0)
    col = lax.broadcasted_iota(jnp.int32, (c, c), 1)
    tril = col <= row
    tri = jnp.where(tril, 1.0, 0.0).astype(BF16)
    cs = _dot_exact_lhs(tri, dta)
    cs_t = cs.T

    e = e_ref[...]
    dt_x = _dot_exact_rhs(dt, e)
    cs_x = _dot_exact_rhs(cs, e)
    cs_end_x = cs_x[c - 1:c, :]
    xdt = xs * dt_x
    decay_in = jnp.exp(cs_x)
    xdt_b = xdt.astype(BF16)
    xdt_end = (xdt * jnp.exp(cs_end_x - cs_x)).astype(BF16)
    state_decay = jnp.exp(cs_end_x)

    lane = lax.broadcasted_iota(jnp.int32, (c, LANES), 1)
    low_half = lane < SSM_HEAD_DIM

    y_groups = []
    for g in range(SSM_GROUPS):
        gs = slice(g * GROUP_W, (g + 1) * GROUP_W)
        ns = slice(g * D_STATE, (g + 1) * D_STATE)
        bg_t = bm[:, ns].T.astype(BF16)
        cg = cm[:, ns].astype(BF16)
        cb = _dot(cg, bg_t)
        sg = state_ref[:, gs]
        y_off = _dot(cg, sg.astype(BF16)) * decay_in[:, gs]
        pairs = []
        for j in range(GROUP_W // LANES):
            h0 = g * (GROUP_W // SSM_HEAD_DIM) + 2 * j
            xp = xdt_b[:, g * GROUP_W + j * LANES:g * GROUP_W + (j + 1) * LANES]
            ys = []
            for h in (h0, h0 + 1):
                seg = cs[:, h:h + 1] - cs_t[h:h + 1, :]
                m = (cb * jnp.where(tril, jnp.exp(seg), 0.0)).astype(BF16)
                ys.append(_dot(m, xp))
            pairs.append(jnp.where(low_half, ys[0], ys[1]))
        y_groups.append(jnp.concatenate(pairs, axis=1) + y_off)
        state_ref[:, gs] = sg * state_decay[:, gs] + _dot(bg_t, xdt_end[:, gs])

    y = jnp.concatenate(y_groups, axis=1) + dexp_ref[...] * xs
    if cv < c:
        y = y[0:cv]
    zv = z_ref[0]
    y = y * (zv * _sigmoid(zv))
    outs = []
    for g in range(SSM_GROUPS):
        gs = slice(g * GROUP_W, (g + 1) * GROUP_W)
        outs.append(_rms_scale(y[:, gs]))
    y_ref[0] = (jnp.concatenate(outs, axis=1) * gnw_ref[...]).astype(y_ref.dtype)

    @pl.when(step == pl.num_programs(1) - 1)
    def _():
        sout_ref[0] = state_ref[...].T


def ssd(proj, conv_in, state_in, cw, cb, dtb, alog, dexp, gnw, *, c, cv):
    nb, length, _ = proj.shape
    steps = length // cv
    head_of_channel = np.arange(D_INNER) // SSM_HEAD_DIM
    expand = jnp.asarray(np.arange(LANES)[:, None] == head_of_channel[None, :], dtype=BF16)
    const = lambda b, l: (0, 0)
    kern = functools.partial(_ssd_kernel, c=c, cv=cv)
    return pl.pallas_call(
        kern,
        grid=(nb, steps),
        in_specs=[
            pl.BlockSpec((1, cv, D_INNER), lambda b, l: (b, l, 0)),
            pl.BlockSpec((1, cv, D_INNER), lambda b, l: (b, l, 1)),
            pl.BlockSpec((1, cv, GROUP_W), lambda b, l: (b, l, 2 * D_INNER // GROUP_W)),
            pl.BlockSpec((1, cv, GROUP_W), lambda b, l: (b, l, 2 * D_INNER // GROUP_W + 1)),
            pl.BlockSpec((1, cv, DT_PAD), lambda b, l: (b, l, (D_INNER + CONV_DIM) // DT_PAD)),
            pl.BlockSpec((1, 8, CONV_DIM), lambda b, l: (b, 0, 0)),
            pl.BlockSpec((1, D_INNER, D_STATE), lambda b, l: (b, 0, 0)),
            pl.BlockSpec((CONV_W, CONV_DIM), const),
            pl.BlockSpec((1, CONV_DIM), const),
            pl.BlockSpec((1, LANES), const),
            pl.BlockSpec((1, LANES), const),
            pl.BlockSpec((1, D_INNER), const),
            pl.BlockSpec((1, D_INNER), const),
            pl.BlockSpec((LANES, D_INNER), const),
        ],
        out_specs=[
            pl.BlockSpec((1, cv, D_INNER), lambda b, l: (b, l, 0)),
            pl.BlockSpec((1, D_INNER, D_STATE), lambda b, l: (b, 0, 0)),
        ],
        out_shape=[
            jax.ShapeDtypeStruct((nb, length, D_INNER), BF16),
            jax.ShapeDtypeStruct((nb, D_INNER, D_STATE), F32),
        ],
        scratch_shapes=[pltpu.VMEM((D_STATE, D_INNER), F32),
                        pltpu.VMEM((8 + c, CONV_DIM), F32)],
        compiler_params=_params("arbitrary", "arbitrary"),
        name="ssd",
    )(proj, proj, proj, proj, proj, conv_in, state_in, cw, cb, dtb, alog, dexp, gnw, expand)


def _mm_res_kernel(a_ref, w_ref, r_ref, o_ref):
    o_ref[...] = r_ref[...] + _dot(a_ref[...], w_ref[...])


def matmul_residual(a, w, res, *, tm):
    t, k = a.shape
    n = w.shape[1]
    return pl.pallas_call(
        _mm_res_kernel,
        grid=(t // tm,),
        in_specs=[pl.BlockSpec((tm, k), lambda i: (i, 0)),
                  pl.BlockSpec((k, n), lambda i: (0, 0)),
                  pl.BlockSpec((tm, n), lambda i: (i, 0))],
        out_specs=pl.BlockSpec((tm, n), lambda i: (i, 0)),
        out_shape=jax.ShapeDtypeStruct((t, n), F32),
        compiler_params=_params("arbitrary"),
        name="matmul_residual",
    )(a, w, res)


def _ffn_kernel(x_ref, nw_ref, wg_ref, wu_ref, wd_ref, o_ref, h_ref, acc_ref):
    f = pl.program_id(1)

    @pl.when(f == 0)
    def _():
        x = x_ref[...]
        h_ref[...] = (_rms_scale(x) * nw_ref[...]).astype(BF16)
        acc_ref[...] = x

    h = h_ref[...]
    g = _dot(h, wg_ref[...])
    u = _dot(h, wu_ref[...])
    act = (g * _sigmoid(g) * u).astype(BF16)
    acc_ref[...] += _dot(act, wd_ref[...])

    @pl.when(f == pl.num_programs(1) - 1)
    def _():
        o_ref[...] = acc_ref[...]


def swiglu_ffn(x, nw, wg, wu, wd, *, tm, tf):
    t, d = x.shape
    ff = wg.shape[1]
    return pl.pallas_call(
        _ffn_kernel,
        grid=(t // tm, ff // tf),
        in_specs=[pl.BlockSpec((tm, d), lambda i, f: (i, 0)),
                  pl.BlockSpec((1, d), lambda i, f: (0, 0)),
                  pl.BlockSpec((d, tf), lambda i, f: (0, f)),
                  pl.BlockSpec((d, tf), lambda i, f: (0, f)),
                  pl.BlockSpec((tf, d), lambda i, f: (f, 0))],
        out_specs=pl.BlockSpec((tm, d), lambda i, f: (i, 0)),
        out_shape=jax.ShapeDtypeStruct((t, d), F32),
        scratch_shapes=[pltpu.VMEM((tm, d), BF16), pltpu.VMEM((tm, d), F32)],
        compiler_params=_params("arbitrary", "arbitrary"),
        name="swiglu_ffn",
    )(x, nw, wg, wu, wd)


def _qkv_kernel(x_ref, kvw_ref, aw_ref, wk_ref, wv_ref, wq_ref, cos_ref, sa_ref, sb_ref,
                k_ref, v_ref, q_ref, kb_ref, vb_ref):
    xn = _rms_scale(x_ref[...])
    hkv = (xn * kvw_ref[...]).astype(BF16)
    hq = (xn * aw_ref[...]).astype(BF16)
    cos, sa, sb = cos_ref[...], sa_ref[...], sb_ref[...]

    def rope(t):
        blocks = []
        for j in range(t.shape[1] // LANES):
            tb = t[:, j * LANES:(j + 1) * LANES]
            blocks.append(tb * cos + pltpu.roll(tb, LANES - ROT_DIM // 2, 1) * sa
                          + pltpu.roll(tb, ROT_DIM // 2, 1) * sb)
        return jnp.concatenate(blocks, axis=1)

    k = rope(_dot(hkv, wk_ref[...]))
    v = _dot(hkv, wv_ref[...])
    q = rope(_dot(hq, wq_ref[...]))
    k_ref[...] = k
    v_ref[...] = v
    kb_ref[...] = k.astype(BF16)
    vb_ref[...] = v.astype(BF16)
    q_ref[...] = (q * (DK ** -0.5)).astype(BF16)


def _rope_tables(pos):
    half = ROT_DIM // 2
    inv_freq = ROPE_THETA ** (-jnp.arange(half, dtype=F32) / half)
    ang = pos.astype(F32)[:, None] * inv_freq[None, :]
    cos, sin = jnp.cos(ang), jnp.sin(ang)
    n = pos.shape[0]
    ones = jnp.ones((n, DK - ROT_DIM), F32)
    zeros_h = jnp.zeros((n, half), F32)
    zeros_r = jnp.zeros((n, DK - ROT_DIM), F32)
    cos_t = jnp.concatenate([cos, cos, ones], axis=1)
    sa_t = jnp.concatenate([-sin, zeros_h, zeros_r], axis=1)
    sb_t = jnp.concatenate([zeros_h, sin, zeros_r], axis=1)
    rep = lambda t: jnp.tile(t, (1, LANES // DK))
    return rep(cos_t), rep(sa_t), rep(sb_t)


def qkv_proj(x, kvw, aw, wk, wv, wq, tables, *, tm):
    t, d = x.shape
    period = tables[0].shape[0] // tm
    row = lambda i: (i, 0)
    const = lambda i: (0, 0)
    tab = lambda i: (i % period, 0)
    f32_out = jax.ShapeDtypeStruct((t, d), F32)
    bf_out = jax.ShapeDtypeStruct((t, d), BF16)
    return pl.pallas_call(
        _qkv_kernel,
        grid=(t // tm,),
        in_specs=[pl.BlockSpec((tm, d), row),
                  pl.BlockSpec((1, d), const), pl.BlockSpec((1, d), const),
                  pl.BlockSpec((d, d), const), pl.BlockSpec((d, d), const), pl.BlockSpec((d, d), const),
                  pl.BlockSpec((tm, LANES), tab), pl.BlockSpec((tm, LANES), tab),
                  pl.BlockSpec((tm, LANES), tab)],
        out_specs=[pl.BlockSpec((tm, d), row)] * 5,
        out_shape=[f32_out, f32_out, bf_out, bf_out, bf_out],
        compiler_params=_params("arbitrary"),
        name="qkv_proj",
    )(x, kvw, aw, wk, wv, wq, *tables)


def _lambda_value(lp_ref, lambda_init):
    lp = lp_ref[...]
    s1 = jnp.sum(lp[0:1] * lp[1:2], axis=-1, keepdims=True)
    s2 = jnp.sum(lp[2:3] * lp[3:4], axis=-1, keepdims=True)
    return jnp.exp(s1) - jnp.exp(s2) + lambda_init


def _diff_finish(o1, o2, lam, subw, lambda_init):
    o = o1 - lam * o2
    return _rms_scale(o) * subw * (1.0 - lambda_init)


def _attn_kernel(qt_ref, kt_ref, q_ref, k_ref, v_ref, lp_ref, subw_ref, o_ref,
                 qs_ref, m_ref, acc_ref, *, tq, tk, lambda_init):
    p = pl.program_id(1)
    qi = qt_ref[p]
    ki = kt_ref[p]
    lane = lax.broadcasted_iota(jnp.int32, (tq, LANES), 1)

    @pl.when(ki == 0)
    def _():
        m_ref[...] = jnp.full(m_ref.shape, NEG_BIG, F32)
        acc_ref[...] = jnp.zeros(acc_ref.shape, F32)
        for h in range(N_HEADS):
            qh = q_ref[0, :, h * LANES:(h + 1) * LANES]
            zero = jnp.zeros_like(qh)
            qs_ref[h, 0:tq, :] = jnp.where(lane < DK, qh, zero)
            qs_ref[h, tq:2 * tq, :] = jnp.where(lane < DK, zero, qh)

    qpos = qi * tq + (lax.broadcasted_iota(jnp.int32, (2 * tq, tk), 0) & (tq - 1))
    kpos = ki * tk + lax.broadcasted_iota(jnp.int32, (2 * tq, tk), 1)
    visible = (kpos >> CHUNK_SHIFT) <= (qpos >> CHUNK_SHIFT)
    ones = jnp.ones((tk, LANES), BF16)

    for h in range(N_HEADS):
        kh = k_ref[0, :, h * LANES:(h + 1) * LANES]
        vh = v_ref[0, :, h * LANES:(h + 1) * LANES]
        s = lax.dot_general(qs_ref[h], kh, (((1,), (1,)), ((), ())), preferred_element_type=F32)
        s = jnp.where(visible, s, NEG_BIG)
        m_prev = m_ref[h]
        m_new = jnp.maximum(m_prev, jnp.max(s, axis=-1, keepdims=True))
        alpha = jnp.exp(m_prev - m_new)
        pr = jnp.exp(s - pltpu.repeat(m_new, tk // LANES, 1)).astype(BF16)
        pv = _dot(pr, jnp.concatenate([vh, ones], axis=1))
        acc_ref[h] = acc_ref[h] * pltpu.repeat(alpha, 2, 1) + pv
        m_ref[h] = m_new

    @pl.when(ki == qi)
    def _():
        lam = _lambda_value(lp_ref, lambda_init)
        for h in range(N_HEADS):
            acc = acc_ref[h]
            o = acc[:, 0:LANES] / acc[:, LANES:2 * LANES]
            res = _diff_finish(o[0:tq], o[tq:2 * tq], lam, subw_ref[...], lambda_init)
            o_ref[0, :, h * LANES:(h + 1) * LANES] = res.astype(o_ref.dtype)


def diff_attention(q, k, v, lam_params, subw, *, tq, lambda_init):
    nb, length, d = q.shape
    tk = tq
    nq = length // tq
    qt = np.concatenate([np.full(i + 1, i) for i in range(nq)]).astype(np.int32)
    kt = np.concatenate([np.arange(i + 1) for i in range(nq)]).astype(np.int32)
    kern = functools.partial(_attn_kernel, tq=tq, tk=tk, lambda_init=lambda_init)
    grid_spec = pltpu.PrefetchScalarGridSpec(
        num_scalar_prefetch=2,
        grid=(nb, len(qt)),
        in_specs=[pl.BlockSpec((1, tq, d), lambda b, p, qt_r, kt_r: (b, qt_r[p], 0)),
                  pl.BlockSpec((1, tk, d), lambda b, p, qt_r, kt_r: (b, kt_r[p], 0)),
                  pl.BlockSpec((1, tk, d), lambda b, p, qt_r, kt_r: (b, kt_r[p], 0)),
                  pl.BlockSpec((4, DK), lambda b, p, qt_r, kt_r: (0, 0)),
                  pl.BlockSpec((1, DV), lambda b, p, qt_r, kt_r: (0, 0))],
        out_specs=pl.BlockSpec((1, tq, d), lambda b, p, qt_r, kt_r: (b, qt_r[p], 0)),
        scratch_shapes=[pltpu.VMEM((N_HEADS, 2 * tq, LANES), BF16),
                        pltpu.VMEM((N_HEADS, 2 * tq, LANES), F32),
                        pltpu.VMEM((N_HEADS, 2 * tq, 2 * LANES), F32)],
    )
    return pl.pallas_call(
        kern,
        grid_spec=grid_spec,
        out_shape=jax.ShapeDtypeStruct((nb, length, d), BF16),
        compiler_params=_params("arbitrary", "arbitrary"),
        name="diff_attention",
    )(jnp.asarray(qt), jnp.asarray(kt), q, k, v, lam_params, subw)


def _attn_cached_kernel(q_ref, ck_ref, cv_ref, kn_ref, vn_ref, lp_ref, subw_ref, o_ref,
                        *, nq, lambda_init):
    q = q_ref[0]
    lane = lax.broadcasted_iota(jnp.int32, (nq, LANES), 1)
    zero = jnp.zeros_like(q)
    qs = jnp.concatenate([jnp.where(lane < DK, q, zero), jnp.where(lane < DK, zero, q)], axis=0)
    nt = (((1,), (1,)), ((), ()))
    pad = jnp.zeros((LANES - nq, LANES), BF16)
    kc = ck_ref[0].astype(BF16)
    vc = cv_ref[0].astype(BF16)
    kn = jnp.concatenate([kn_ref[0].astype(BF16), pad], axis=0)
    vn = jnp.concatenate([vn_ref[0].astype(BF16), pad], axis=0)
    s_c = lax.dot_general(qs, kc, nt, preferred_element_type=F32)
    s_n = lax.dot_general(qs, kn, nt, preferred_element_type=F32)
    col = lax.broadcasted_iota(jnp.int32, s_n.shape, 1)
    s_n = jnp.where(col < nq, s_n, NEG_BIG)
    m = jnp.maximum(jnp.max(s_c, axis=-1, keepdims=True), jnp.max(s_n, axis=-1, keepdims=True))
    p_c = jnp.exp(s_c - m)
    p_n = jnp.exp(s_n - m)
    denom = jnp.sum(p_c, axis=-1, keepdims=True) + jnp.sum(p_n, axis=-1, keepdims=True)
    o = (_dot(p_c.astype(BF16), vc) + _dot(p_n.astype(BF16), vn)) / denom
    lam = _lambda_value(lp_ref, lambda_init)
    res = _diff_finish(o[0:nq], o[nq:2 * nq], lam, subw_ref[...], lambda_init)
    o_ref[0] = res.astype(o_ref.dtype)


def diff_attention_cached(q, cache_k, cache_v, k_new, v_new, lam_params, subw, *, lambda_init):
    nb, nq, d = q.shape
    past = cache_k.shape[1]
    kern = functools.partial(_attn_cached_kernel, nq=nq, lambda_init=lambda_init)
    head = lambda b, h: (b, 0, h)
    const = lambda b, h: (0, 0)
    return pl.pallas_call(
        kern,
        grid=(nb, N_HEADS),
        in_specs=[pl.BlockSpec((1, nq, LANES), head),
                  pl.BlockSpec((1, past, LANES), head),
                  pl.BlockSpec((1, past, LANES), head),
                  pl.BlockSpec((1, nq, LANES), head),
                  pl.BlockSpec((1, nq, LANES), head),
                  pl.BlockSpec((4, DK), const),
                  pl.BlockSpec((1, DV), const)],
        out_specs=pl.BlockSpec((1, nq, LANES), head),
        out_shape=jax.ShapeDtypeStruct((nb, nq, d), BF16),
        compiler_params=_params("arbitrary", "arbitrary"),
        name="diff_attention_cached",
    )(q, cache_k, cache_v, k_new, v_new, lam_params, subw)


def _route(logits):
    lane = lax.broadcasted_iota(jnp.int32, logits.shape, 1)
    valid = lane < N_EXPERTS
    lg = jnp.where(valid, logits, NEG_BIG)
    m1 = jnp.max(lg, axis=-1, keepdims=True)
    lane_f = lane.astype(F32)
    i1 = jnp.min(jnp.where(lg == m1, lane_f, float(LANES)), axis=-1, keepdims=True)
    lg2 = jnp.where(lane_f == i1, NEG_BIG, lg)
    m2 = jnp.max(lg2, axis=-1, keepdims=True)
    i2 = jnp.min(jnp.where(lg2 == m2, lane_f, float(LANES)), axis=-1, keepdims=True)
    e2 = jnp.exp(m2 - m1)
    g1 = 1.0 / (1.0 + e2)
    g2 = e2 / (1.0 + e2)
    return jnp.where(lane_f == i1, g1, 0.0) + jnp.where(lane_f == i2, g2, 0.0)


def _moe_kernel(x_ref, nw_ref, wr_ref, wg_ref, wu_ref, wd_ref, fw_ref, o_ref,
                h_ref, comb_ref, ce_ref, acc_ref):
    e = pl.program_id(1)
    f = pl.program_id(2)
    first = jnp.logical_and(e == 0, f == 0)

    @pl.when(first)
    def _():
        x = x_ref[...]
        hf = _rms_scale(x) * nw_ref[...]
        h_ref[...] = hf.astype(BF16)
        comb_ref[...] = _route(_dot_exact_rhs_general(hf, wr_ref[...]))
        acc_ref[...] = x

    @pl.when(f == 0)
    def _():
        lane = lax.broadcasted_iota(jnp.int32, comb_ref.shape, 1)
        ce = jnp.sum(jnp.where(lane == e, comb_ref[...], 0.0), axis=-1, keepdims=True)
        ce_ref[...] = jnp.broadcast_to(ce, ce_ref.shape)

    h = h_ref[...]
    g = _dot(h, wg_ref[...])
    u = _dot(h, wu_ref[...])
    tf = g.shape[1]
    act = (g * _sigmoid(g) * u * pltpu.repeat(ce_ref[...], tf // LANES, 1)).astype(BF16)
    acc_ref[...] += _dot(act, wd_ref[...])

    last = jnp.logical_and(e == pl.num_programs(1) - 1, f == pl.num_programs(2) - 1)

    @pl.when(last)
    def _():
        o_ref[...] = _rms_scale(acc_ref[...]) * fw_ref[...]


def _dot_exact_rhs_general(x, w):
    xh, xm, xl = _split3(x)
    wh, wm, wl = _split3(w)
    return (_dot(xh, wh) + (_dot(xh, wm) + _dot(xm, wh))
            + (_dot(xh, wl) + _dot(xm, wm) + _dot(xl, wh)))


def moe_ffn_final(x, nw, wr, wg, wu, wd, fw, *, tm, tf):
    t, d = x.shape
    ne, _, ff = wg.shape
    const = lambda i, e, f: (0, 0)
    return pl.pallas_call(
        _moe_kernel,
        grid=(t // tm, ne, ff // tf),
        in_specs=[pl.BlockSpec((tm, d), lambda i, e, f: (i, 0)),
                  pl.BlockSpec((1, d), const),
                  pl.BlockSpec((d, LANES), const),
                  pl.BlockSpec((None, d, tf), lambda i, e, f: (e, 0, f)),
                  pl.BlockSpec((None, d, tf), lambda i, e, f: (e, 0, f)),
                  pl.BlockSpec((None, tf, d), lambda i, e, f: (e, f, 0)),
                  pl.BlockSpec((1, d), const)],
        out_specs=pl.BlockSpec((tm, d), lambda i, e, f: (i, 0)),
        out_shape=jax.ShapeDtypeStruct((t, d), F32),
        scratch_shapes=[pltpu.VMEM((tm, d), BF16), pltpu.VMEM((tm, LANES), F32),
                        pltpu.VMEM((tm, LANES), F32), pltpu.VMEM((tm, d), F32)],
        compiler_params=_params("arbitrary", "arbitrary", "arbitrary"),
        name="moe_ffn_final",
    )(x, nw, wr, wg, wu, wd, fw)


def _row(v):
    return v.reshape(1, -1).astype(F32)


def _pad_lanes(v, width):
    v = _row(v)
    return jnp.pad(v, ((0, 0), (0, width - v.shape[1])))


def _trunk(x, pos, conv_in, state_in, past_k, past_v, W, *, ssd_chunk, ssd_valid, tm, tq):
    nb, length, d = x.shape
    t = nb * length
    x0 = x.reshape(t, d)

    proj = rms_matmul(x0, W["mamba_norm_w"], W["w_in"], tm=tm, tn=768)
    proj3 = proj.reshape(nb, length, PROJ_W)
    new_conv = proj3[:, length - (CONV_W - 1):, D_INNER:D_INNER + CONV_DIM]
    conv8 = jnp.pad(conv_in, ((0, 0), (8 - (CONV_W - 1), 0), (0, 0)))
    y, state_out = ssd(proj3, conv8, state_in.reshape(nb, D_INNER, D_STATE),
                       W["conv_w"], W["conv_b"], W["dt_bias"], W["a_log"], W["d_exp"], W["gn_w"],
                       c=ssd_chunk, cv=ssd_valid)
    new_ssm = state_out.reshape(nb, SSM_HEADS, SSM_HEAD_DIM, D_STATE)
    x1 = matmul_residual(y.reshape(t, D_INNER), W["w_out"], x0, tm=tm)

    x2 = swiglu_ffn(x1, W["ffn_norm_w"], W["ffn_wg"], W["ffn_wu"], W["ffn_wd"], tm=tm, tf=1408)

    tables = _rope_tables(pos)
    k, v, q, kb, vb = qkv_proj(x2, W["kv_norm_w"], W["attn_norm_w"], W["w_k"], W["w_v"], W["w_q"],
                               tables, tm=tm)
    lambda_init = 0.8 - 0.6 * math.exp(-0.3 * 1)
    q3 = q.reshape(nb, length, d)
    if past_k is None:
        o = diff_attention(q3, kb.reshape(nb, length, d), vb.reshape(nb, length, d),
                           W["lam"], W["subln_w"], tq=tq, lambda_init=lambda_init)
    else:
        o = diff_attention_cached(q3, past_k.reshape(nb, -1, d), past_v.reshape(nb, -1, d),
                                  k.reshape(nb, length, d), v.reshape(nb, length, d),
                                  W["lam"], W["subln_w"], lambda_init=lambda_init)
    x3 = matmul_residual(o.reshape(t, d), W["w_o"], x2, tm=tm)

    yout = moe_ffn_final(x3, W["moe_norm_w"], W["moe_wr"], W["moe_wg"], W["moe_wu"], W["moe_wd"],
                         W["final_norm_w"], tm=tm, tf=896)
    return (yout.reshape(nb, length, d), new_conv[None], new_ssm[None],
            k.reshape(nb, length, N_HEADS, 2 * DK), v.reshape(nb, length, N_HEADS, DV))


def kernel(x_prompt, x_sample, cache_conv, state_ssm, cache_k, cache_v, mamba_norm_w, mamba_w_in, mamba_conv_w, mamba_conv_b, mamba_dt_bias, mamba_a_log, mamba_d, mamba_gn_w, mamba_w_out, kv_norm_w, w_k, w_v, attn_norm_w, w_q, lambda_q1, lambda_k1, lambda_q2, lambda_k2, subln_w, w_o, ffn_norm_w, ffn_w_gate, ffn_w_up, ffn_w_down, moe_norm_w, moe_w_router, moe_w_gate, moe_w_up, moe_w_down, final_norm_w):
    w_in = mamba_w_in[0]
    n_dt = w_in.shape[1] - D_INNER - CONV_DIM
    W = dict(
        mamba_norm_w=_row(mamba_norm_w[0]),
        w_in=jnp.pad(w_in, ((0, 0), (0, DT_PAD - n_dt))).astype(BF16),
        conv_w=mamba_conv_w[0].astype(F32),
        conv_b=_row(mamba_conv_b[0]),
        dt_bias=_pad_lanes(mamba_dt_bias[0], LANES),
        a_log=_pad_lanes(mamba_a_log[0], LANES),
        d_exp=_row(jnp.repeat(mamba_d[0], SSM_HEAD_DIM)),
        gn_w=_row(mamba_gn_w[0]),
        w_out=mamba_w_out[0].astype(BF16),
        kv_norm_w=_row(kv_norm_w), w_k=w_k.astype(BF16), w_v=w_v.astype(BF16),
        attn_norm_w=_row(attn_norm_w[0]), w_q=w_q[0].astype(BF16),
        lam=jnp.stack([lambda_q1[0], lambda_k1[0], lambda_q2[0], lambda_k2[0]]).astype(F32),
        subln_w=_row(subln_w[0]), w_o=w_o[0].astype(BF16),
        ffn_norm_w=_row(ffn_norm_w[0]),
        ffn_wg=ffn_w_gate[0].astype(BF16), ffn_wu=ffn_w_up[0].astype(BF16),
        ffn_wd=ffn_w_down[0].astype(BF16),
        moe_norm_w=_row(moe_norm_w[0]),
        moe_wr=jnp.pad(moe_w_router[0].astype(F32), ((0, 0), (0, LANES - N_EXPERTS))),
        moe_wg=moe_w_gate[0].astype(BF16), moe_wu=moe_w_up[0].astype(BF16),
        moe_wd=moe_w_down[0].astype(BF16),
        final_norm_w=_row(final_norm_w),
    )
    bp, lp = x_prompt.shape[0], x_prompt.shape[1]
    bs, ls = x_sample.shape[0], x_sample.shape[1]
    past = cache_k.shape[1]

    conv0 = jnp.zeros((bp, CONV_W - 1, CONV_DIM), F32)
    ssm0 = jnp.zeros((bp, SSM_HEADS, SSM_HEAD_DIM, D_STATE), F32)
    y_p, conv_p, ssm_p, k_p, v_p = _trunk(
        x_prompt, jnp.arange(lp, dtype=jnp.int32), conv0, ssm0, None, None, W,
        ssd_chunk=128, ssd_valid=128, tm=512, tq=512)

    pos_s = jnp.tile(past + jnp.arange(ls, dtype=jnp.int32), bs)
    y_s, conv_s, ssm_s, k_s, v_s = _trunk(
        x_sample, pos_s, cache_conv[0], state_ssm[0], cache_k, cache_v, W,
        ssd_chunk=128, ssd_valid=ls, tm=bs * ls, tq=None)
    return (y_p, y_s, conv_p, ssm_p, k_p, v_p, conv_s, ssm_s, k_s, v_s)
```

```python
import functools
import math

import numpy as np
import jax
import jax.numpy as jnp
from jax import lax
from jax.experimental import pallas as pl
from jax.experimental.pallas import tpu as pltpu

F32 = jnp.float32
BF16 = jnp.bfloat16

EPS = 1e-5
D_MODEL = 1024
D_INNER = 2048
SSM_HEAD_DIM = 64
SSM_HEADS = 32
SSM_GROUPS = 4
D_STATE = 128
GROUP_W = D_INNER // SSM_GROUPS
CONV_W = 4
CONV_DIM = D_INNER + 2 * SSM_GROUPS * D_STATE
DT_PAD = 256
PROJ_W = D_INNER + CONV_DIM + DT_PAD
N_HEADS = 8
DK = 64
DV = 128
ROT_DIM = 16
ROPE_THETA = 500000.0
CHUNK = 64
CHUNK_SHIFT = 6
assert 1 << CHUNK_SHIFT == CHUNK
N_EXPERTS = 8
LANES = 128
NEG_BIG = -1e30
VMEM_LIMIT = 56 * 1024 * 1024


def _params(*sem):
    return pltpu.CompilerParams(dimension_semantics=sem, vmem_limit_bytes=VMEM_LIMIT)


def _sigmoid(x):
    return 1.0 / (1.0 + jnp.exp(-x))


def _rms_scale(x):
    return x * lax.rsqrt(jnp.mean(x * x, axis=-1, keepdims=True) + EPS)


def _split3(x):
    hi = x.astype(BF16)
    r1 = x - hi.astype(F32)
    mid = r1.astype(BF16)
    lo = (r1 - mid.astype(F32)).astype(BF16)
    return hi, mid, lo


def _dot(a, b):
    return jnp.dot(a, b, preferred_element_type=F32)


def _dot_exact_rhs(x, m_bf16):
    hi, mid, lo = _split3(x)
    return _dot(hi, m_bf16) + _dot(mid, m_bf16) + _dot(lo, m_bf16)


def _dot_exact_lhs(m_bf16, x):
    hi, mid, lo = _split3(x)
    return _dot(m_bf16, hi) + _dot(m_bf16, mid) + _dot(m_bf16, lo)


def _rms_matmul_kernel(x_ref, nw_ref, w_ref, o_ref, h_ref):
    @pl.when(pl.program_id(1) == 0)
    def _():
        h_ref[...] = (_rms_scale(x_ref[...]) * nw_ref[...]).astype(BF16)

    o_ref[...] = _dot(h_ref[...], w_ref[...]).astype(o_ref.dtype)


def rms_matmul(x, nw, w, *, tm, tn):
    t, d = x.shape
    n = w.shape[1]
    return pl.pallas_call(
        _rms_matmul_kernel,
        grid=(t // tm, n // tn),
        in_specs=[pl.BlockSpec((tm, d), lambda i, j: (i, 0)),
                  pl.BlockSpec((1, d), lambda i, j: (0, 0)),
                  pl.BlockSpec((d, tn), lambda i, j: (0, j))],
        out_specs=pl.BlockSpec((tm, tn), lambda i, j: (i, j)),
        out_shape=jax.ShapeDtypeStruct((t, n), F32),
        scratch_shapes=[pltpu.VMEM((tm, d), BF16)],
        compiler_params=_params("arbitrary", "arbitrary"),
        name="rms_matmul",
    )(x, nw, w)


def _ssd_kernel(z_ref, xs_ref, b_ref, c_ref, dt_ref, cin_ref, sin_ref, cw_ref, cb_ref,
                dtb_ref, alog_ref, dexp_ref, gnw_ref, e_ref,
                y_ref, sout_ref, state_ref, cbuf_ref, *, c, cv):
    step = pl.program_id(1)

    @pl.when(step == 0)
    def _():
        state_ref[...] = sin_ref[0].T
        cbuf_ref[0:8, :] = cin_ref[0]

    cbuf_ref[8:8 + cv, 0:D_INNER] = xs_ref[0]
    cbuf_ref[8:8 + cv, D_INNER:D_INNER + GROUP_W] = b_ref[0]
    cbuf_ref[8:8 + cv, D_INNER + GROUP_W:CONV_DIM] = c_ref[0]
    if cv < c:
        cbuf_ref[8 + cv:8 + c, :] = jnp.zeros((c - cv, CONV_DIM), F32)

    conv = cb_ref[...] + cbuf_ref[5:5 + c, :] * cw_ref[0:1, :]
    for k in range(1, CONV_W):
        conv = conv + cbuf_ref[5 + k:5 + k + c, :] * cw_ref[k:k + 1, :]
    conv = conv * _sigmoid(conv)
    if cv == c:
        cbuf_ref[5:8, :] = cbuf_ref[5 + c:8 + c, :]

    xs = conv[:, 0:D_INNER]
    bm = conv[:, D_INNER:D_INNER + GROUP_W]
    cm = conv[:, D_INNER + GROUP_W:CONV_DIM]

    dt_in = dt_ref[0][:, 0:LANES] + dtb_ref[...]
    dt = jnp.maximum(dt_in, 0.0) + jnp.log1p(jnp.exp(-jnp.abs(dt_in)))
    if cv < c:
        dt = jnp.concatenate([dt, jnp.zeros((c - cv, LANES), F32)], axis=0)
    a = -jnp.exp(alog_ref[...])
    dta = dt * a

    row = lax.broadcasted_iota(jnp.int32, (c, c), 0)
    col = lax.broadcasted_iota(jnp.int32, (c, c), 1)
    tril = col <= row
    tri = jnp.where(tril, 1.0, 0.0).astype(BF16)
    cs = _dot_exact_lhs(tri, dta)
    cs_t = cs.T

    e = e_ref[...]
    dt_x = _dot_exact_rhs(dt, e)
    cs_x = _dot_exact_rhs(cs, e)
    cs_end_x = cs_x[c - 1:c, :]
    xdt = xs * dt_x
    decay_in = jnp.exp(cs_x)
    xdt_b = xdt.astype(BF16)
    xdt_end = (xdt * jnp.exp(cs_end_x - cs_x)).astype(BF16)
    state_decay = jnp.exp(cs_end_x)

    lane = lax.broadcasted_iota(jnp.int32, (c, LANES), 1)
    low_half = lane < SSM_HEAD_DIM

    y_groups = []
    for g in range(SSM_GROUPS):
        gs = slice(g * GROUP_W, (g + 1) * GROUP_W)
        ns = slice(g * D_STATE, (g + 1) * D_STATE)
        bg_t = bm[:, ns].T.astype(BF16)
        cg = cm[:, ns].astype(BF16)
        cb = _dot(cg, bg_t)
        sg = state_ref[:, gs]
        y_off = _dot(cg, sg.astype(BF16)) * decay_in[:, gs]
        pairs = []
        for j in range(GROUP_W // LANES):
            h0 = g * (GROUP_W // SSM_HEAD_DIM) + 2 * j
            xp = xdt_b[:, g * GROUP_W + j * LANES:g * GROUP_W + (j + 1) * LANES]
            ys = []
            for h in (h0, h0 + 1):
                seg = cs[:, h:h + 1] - cs_t[h:h + 1, :]
                m = (cb * jnp.where(tril, jnp.exp(seg), 0.0)).astype(BF16)
                ys.append(_dot(m, xp))
            pairs.append(jnp.where(low_half, ys[0], ys[1]))
        y_groups.append(jnp.concatenate(pairs, axis=1) + y_off)
        state_ref[:, gs] = sg * state_decay[:, gs] + _dot(bg_t, xdt_end[:, gs])

    y = jnp.concatenate(y_groups, axis=1) + dexp_ref[...] * xs
    if cv < c:
        y = y[0:cv]
    zv = z_ref[0]
    y = y * (zv * _sigmoid(zv))
    outs = []
    for g in range(SSM_GROUPS):
        gs = slice(g * GROUP_W, (g + 1) * GROUP_W)
        outs.append(_rms_scale(y[:, gs]))
    y_ref[0] = (jnp.concatenate(outs, axis=1) * gnw_ref[...]).astype(y_ref.dtype)

    @pl.when(step == pl.num_programs(1) - 1)
    def _():
        sout_ref[0] = state_ref[...].T


def ssd(proj, conv_in, state_in, cw, cb, dtb, alog, dexp, gnw, *, c, cv):
    nb, length, _ = proj.shape
    steps = length // cv
    head_of_channel = np.arange(D_INNER) // SSM_HEAD_DIM
    expand = jnp.asarray(np.arange(LANES)[:, None] == head_of_channel[None, :], dtype=BF16)
    const = lambda b, l: (0, 0)
    kern = functools.partial(_ssd_kernel, c=c, cv=cv)
    return pl.pallas_call(
        kern,
        grid=(nb, steps),
        in_specs=[
            pl.BlockSpec((1, cv, D_INNER), lambda b, l: (b, l, 0)),
            pl.BlockSpec((1, cv, D_INNER), lambda b, l: (b, l, 1)),
            pl.BlockSpec((1, cv, GROUP_W), lambda b, l: (b, l, 2 * D_INNER // GROUP_W)),
            pl.BlockSpec((1, cv, GROUP_W), lambda b, l: (b, l, 2 * D_INNER // GROUP_W + 1)),
            pl.BlockSpec((1, cv, DT_PAD), lambda b, l: (b, l, (D_INNER + CONV_DIM) // DT_PAD)),
            pl.BlockSpec((1, 8, CONV_DIM), lambda b, l: (b, 0, 0)),
            pl.BlockSpec((1, D_INNER, D_STATE), lambda b, l: (b, 0, 0)),
            pl.BlockSpec((CONV_W, CONV_DIM), const),
            pl.BlockSpec((1, CONV_DIM), const),
            pl.BlockSpec((1, LANES), const),
            pl.BlockSpec((1, LANES), const),
            pl.BlockSpec((1, D_INNER), const),
            pl.BlockSpec((1, D_INNER), const),
            pl.BlockSpec((LANES, D_INNER), const),
        ],
        out_specs=[
            pl.BlockSpec((1, cv, D_INNER), lambda b, l: (b, l, 0)),
            pl.BlockSpec((1, D_INNER, D_STATE), lambda b, l: (b, 0, 0)),
        ],
        out_shape=[
            jax.ShapeDtypeStruct((nb, length, D_INNER), BF16),
            jax.ShapeDtypeStruct((nb, D_INNER, D_STATE), F32),
        ],
        scratch_shapes=[pltpu.VMEM((D_STATE, D_INNER), F32),
                        pltpu.VMEM((8 + c, CONV_DIM), F32)],
        compiler_params=_params("arbitrary", "arbitrary"),
        name="ssd",
    )(proj, proj, proj, proj, proj, conv_in, state_in, cw, cb, dtb, alog, dexp, gnw, expand)


def _mm_res_kernel(a_ref, w_ref, r_ref, o_ref):
    o_ref[...] = r_ref[...] + _dot(a_ref[...], w_ref[...])


def matmul_residual(a, w, res, *, tm):
    t, k = a.shape
    n = w.shape[1]
    return pl.pallas_call(
        _mm_res_kernel,
        grid=(t // tm,),
        in_specs=[pl.BlockSpec((tm, k), lambda i: (i, 0)),
                  pl.BlockSpec((k, n), lambda i: (0, 0)),
                  pl.BlockSpec((tm, n), lambda i: (i, 0))],
        out_specs=pl.BlockSpec((tm, n), lambda i: (i, 0)),
        out_shape=jax.ShapeDtypeStruct((t, n), F32),
        compiler_params=_params("arbitrary"),
        name="matmul_residual",
    )(a, w, res)


def _ffn_kernel(x_ref, nw_ref, wg_ref, wu_ref, wd_ref, o_ref, h_ref, acc_ref):
    f = pl.program_id(1)

    @pl.when(f == 0)
    def _():
        x = x_ref[...]
        h_ref[...] = (_rms_scale(x) * nw_ref[...]).astype(BF16)
        acc_ref[...] = x

    h = h_ref[...]
    g = _dot(h, wg_ref[...])
    u = _dot(h, wu_ref[...])
    act = (g * _sigmoid(g) * u).astype(BF16)
    acc_ref[...] += _dot(act, wd_ref[...])

    @pl.when(f == pl.num_programs(1) - 1)
    def _():
        o_ref[...] = acc_ref[...]


def swiglu_ffn(x, nw, wg, wu, wd, *, tm, tf):
    t, d = x.shape
    ff = wg.shape[1]
    return pl.pallas_call(
        _ffn_kernel,
        grid=(t // tm, ff // tf),
        in_specs=[pl.BlockSpec((tm, d), lambda i, f: (i, 0)),
                  pl.BlockSpec((1, d), lambda i, f: (0, 0)),
                  pl.BlockSpec((d, tf), lambda i, f: (0, f)),
                  pl.BlockSpec((d, tf), lambda i, f: (0, f)),
                  pl.BlockSpec((tf, d), lambda i, f: (f, 0))],
        out_specs=pl.BlockSpec((tm, d), lambda i, f: (i, 0)),
        out_shape=jax.ShapeDtypeStruct((t, d), F32),
        scratch_shapes=[pltpu.VMEM((tm, d), BF16), pltpu.VMEM((tm, d), F32)],
        compiler_params=_params("arbitrary", "arbitrary"),
        name="swiglu_ffn",
    )(x, nw, wg, wu, wd)


def _qkv_kernel(x_ref, kvw_ref, aw_ref, wk_ref, wv_ref, wq_ref, cos_ref, sa_ref, sb_ref,
                k_ref, v_ref, q_ref, kb_ref, vb_ref):
    xn = _rms_scale(x_ref[...])
    hkv = (xn * kvw_ref[...]).astype(BF16)
    hq = (xn * aw_ref[...]).astype(BF16)
    cos, sa, sb = cos_ref[...], sa_ref[...], sb_ref[...]

    def rope(t):
        blocks = []
        for j in range(t.shape[1] // LANES):
            tb = t[:, j * LANES:(j + 1) * LANES]
            blocks.append(tb * cos + pltpu.roll(tb, LANES - ROT_DIM // 2, 1) * sa
                          + pltpu.roll(tb, ROT_DIM // 2, 1) * sb)
        return jnp.concatenate(blocks, axis=1)

    k = rope(_dot(hkv, wk_ref[...]))
    v = _dot(hkv, wv_ref[...])
    q = rope(_dot(hq, wq_ref[...]))
    k_ref[...] = k
    v_ref[...] = v
    kb_ref[...] = k.astype(BF16)
    vb_ref[...] = v.astype(BF16)
    q_ref[...] = (q * (DK ** -0.5)).astype(BF16)


def _rope_tables(pos):
    half = ROT_DIM // 2
    inv_freq = ROPE_THETA ** (-jnp.arange(half, dtype=F32) / half)
    ang = pos.astype(F32)[:, None] * inv_freq[None, :]
    cos, sin = jnp.cos(ang), jnp.sin(ang)
    n = pos.shape[0]
    ones = jnp.ones((n, DK - ROT_DIM), F32)
    zeros_h = jnp.zeros((n, half), F32)
    zeros_r = jnp.zeros((n, DK - ROT_DIM), F32)
    cos_t = jnp.concatenate([cos, cos, ones], axis=1)
    sa_t = jnp.concatenate([-sin, zeros_h, zeros_r], axis=1)
    sb_t = jnp.concatenate([zeros_h, sin, zeros_r], axis=1)
    rep = lambda t: jnp.tile(t, (1, LANES // DK))
    return rep(cos_t), rep(sa_t), rep(sb_t)


def qkv_proj(x, kvw, aw, wk, wv, wq, tables, *, tm):
    t, d = x.shape
    period = tables[0].shape[0] // tm
    row = lambda i: (i, 0)
    const = lambda i: (0, 0)
    tab = lambda i: (i % period, 0)
    f32_out = jax.ShapeDtypeStruct((t, d), F32)
    bf_out = jax.ShapeDtypeStruct((t, d), BF16)
    return pl.pallas_call(
        _qkv_kernel,
        grid=(t // tm,),
        in_specs=[pl.BlockSpec((tm, d), row),
                  pl.BlockSpec((1, d), const), pl.BlockSpec((1, d), const),
                  pl.BlockSpec((d, d), const), pl.BlockSpec((d, d), const), pl.BlockSpec((d, d), const),
                  pl.BlockSpec((tm, LANES), tab), pl.BlockSpec((tm, LANES), tab),
                  pl.BlockSpec((tm, LANES), tab)],
        out_specs=[pl.BlockSpec((tm, d), row)] * 5,
        out_shape=[f32_out, f32_out, bf_out, bf_out, bf_out],
        compiler_params=_params("arbitrary"),
        name="qkv_proj",
    )(x, kvw, aw, wk, wv, wq, *tables)


def _lambda_value(lp_ref, lambda_init):
    lp = lp_ref[...]
    s1 = jnp.sum(lp[0:1] * lp[1:2], axis=-1, keepdims=True)
    s2 = jnp.sum(lp[2:3] * lp[3:4], axis=-1, keepdims=True)
    return jnp.exp(s1) - jnp.exp(s2) + lambda_init


def _diff_finish(o1, o2, lam, subw, lambda_init):
    o = o1 - lam * o2
    return _rms_scale(o) * subw * (1.0 - lambda_init)


def _attn_kernel(qt_ref, kt_ref, q_ref, k_ref, v_ref, lp_ref, subw_ref, o_ref,
                 qs_ref, m_ref, acc_ref, *, tq, tk, lambda_init):
    p = pl.program_id(1)
    qi = qt_ref[p]
    ki = kt_ref[p]
    lane = lax.broadcasted_iota(jnp.int32, (tq, LANES), 1)

    @pl.when(ki == 0)
    def _():
        m_ref[...] = jnp.full(m_ref.shape, NEG_BIG, F32)
        acc_ref[...] = jnp.zeros(acc_ref.shape, F32)
        for h in range(N_HEADS):
            qh = q_ref[0, :, h * LANES:(h + 1) * LANES]
            zero = jnp.zeros_like(qh)
            qs_ref[h, 0:tq, :] = jnp.where(lane < DK, qh, zero)
            qs_ref[h, tq:2 * tq, :] = jnp.where(lane < DK, zero, qh)

    qpos = qi * tq + (lax.broadcasted_iota(jnp.int32, (2 * tq, tk), 0) & (tq - 1))
    kpos = ki * tk + lax.broadcasted_iota(jnp.int32, (2 * tq, tk), 1)
    visible = (kpos >> CHUNK_SHIFT) <= (qpos >> CHUNK_SHIFT)
    ones = jnp.ones((tk, LANES), BF16)

    for h in range(N_HEADS):
        kh = k_ref[0, :, h * LANES:(h + 1) * LANES]
        vh = v_ref[0, :, h * LANES:(h + 1) * LANES]
        s = lax.dot_general(qs_ref[h], kh, (((1,), (1,)), ((), ())), preferred_element_type=F32)
        s = jnp.where(visible, s, NEG_BIG)
        m_prev = m_ref[h]
        m_new = jnp.maximum(m_prev, jnp.max(s, axis=-1, keepdims=True))
        alpha = jnp.exp(m_prev - m_new)
        pr = jnp.exp(s - pltpu.repeat(m_new, tk // LANES, 1)).astype(BF16)
        pv = _dot(pr, jnp.concatenate([vh, ones], axis=1))
        acc_ref[h] = acc_ref[h] * pltpu.repeat(alpha, 2, 1) + pv
        m_ref[h] = m_new

    @pl.when(ki == qi)
    def _():
        lam = _lambda_value(lp_ref, lambda_init)
        for h in range(N_HEADS):
            acc = acc_ref[h]
            o = acc[:, 0:LANES] / acc[:, LANES:2 * LANES]
            res = _diff_finish(o[0:tq], o[tq:2 * tq], lam, subw_ref[...], lambda_init)
            o_ref[0, :, h * LANES:(h + 1) * LANES] = res.astype(o_ref.dtype)


def diff_attention(q, k, v, lam_params, subw, *, tq, lambda_init):
    nb, length, d = q.shape
    tk = tq
    nq = length // tq
    qt = np.concatenate([np.full(i + 1, i) for i in range(nq)]).astype(np.int32)
    kt = np.concatenate([np.arange(i + 1) for i in range(nq)]).astype(np.int32)
    kern = functools.partial(_attn_kernel, tq=tq, tk=tk, lambda_init=lambda_init)
    grid_spec = pltpu.PrefetchScalarGridSpec(
        num_scalar_prefetch=2,
        grid=(nb, len(qt)),
        in_specs=[pl.BlockSpec((1, tq, d), lambda b, p, qt_r, kt_r: (b, qt_r[p], 0)),
                  pl.BlockSpec((1, tk, d), lambda b, p, qt_r, kt_r: (b, kt_r[p], 0)),
                  pl.BlockSpec((1, tk, d), lambda b, p, qt_r, kt_r: (b, kt_r[p], 0)),
                  pl.BlockSpec((4, DK), lambda b, p, qt_r, kt_r: (0, 0)),
                  pl.BlockSpec((1, DV), lambda b, p, qt_r, kt_r: (0, 0))],
        out_specs=pl.BlockSpec((1, tq, d), lambda b, p, qt_r, kt_r: (b, qt_r[p], 0)),
        scratch_shapes=[pltpu.VMEM((N_HEADS, 2 * tq, LANES), BF16),
                        pltpu.VMEM((N_HEADS, 2 * tq, LANES), F32),
                        pltpu.VMEM((N_HEADS, 2 * tq, 2 * LANES), F32)],
    )
    return pl.pallas_call(
        kern,
        grid_spec=grid_spec,
        out_shape=jax.ShapeDtypeStruct((nb, length, d), BF16),
        compiler_params=_params("arbitrary", "arbitrary"),
        name="diff_attention",
    )(jnp.asarray(qt), jnp.asarray(kt), q, k, v, lam_params, subw)


def _attn_cached_kernel(q_ref, ck_ref, cv_ref, kn_ref, vn_ref, lp_ref, subw_ref, o_ref,
                        *, nq, lambda_init):
    q = q_ref[0]
    lane = lax.broadcasted_iota(jnp.int32, (nq, LANES), 1)
    zero = jnp.zeros_like(q)
    qs = jnp.concatenate([jnp.where(lane < DK, q, zero), jnp.where(lane < DK, zero, q)], axis=0)
    nt = (((1,), (1,)), ((), ()))
    pad = jnp.zeros((LANES - nq, LANES), BF16)
    kc = ck_ref[0].astype(BF16)
    vc = cv_ref[0].astype(BF16)
    kn = jnp.concatenate([kn_ref[0].astype(BF16), pad], axis=0)
    vn = jnp.concatenate([vn_ref[0].astype(BF16), pad], axis=0)
    s_c = lax.dot_general(qs, kc, nt, preferred_element_type=F32)
    s_n = lax.dot_general(qs, kn, nt, preferred_element_type=F32)
    col = lax.broadcasted_iota(jnp.int32, s_n.shape, 1)
    s_n = jnp.where(col < nq, s_n, NEG_BIG)
    m = jnp.maximum(jnp.max(s_c, axis=-1, keepdims=True), jnp.max(s_n, axis=-1, keepdims=True))
    p_c = jnp.exp(s_c - m)
    p_n = jnp.exp(s_n - m)
    denom = jnp.sum(p_c, axis=-1, keepdims=True) + jnp.sum(p_n, axis=-1, keepdims=True)
    o = (_dot(p_c.astype(BF16), vc) + _dot(p_n.astype(BF16), vn)) / denom
    lam = _lambda_value(lp_ref, lambda_init)
    res = _diff_finish(o[0:nq], o[nq:2 * nq], lam, subw_ref[...], lambda_init)
    o_ref[0] = res.astype(o_ref.dtype)


def diff_attention_cached(q, cache_k, cache_v, k_new, v_new, lam_params, subw, *, lambda_init):
    nb, nq, d = q.shape
    past = cache_k.shape[1]
    kern = functools.partial(_attn_cached_kernel, nq=nq, lambda_init=lambda_init)
    head = lambda b, h: (b, 0, h)
    const = lambda b, h: (0, 0)
    return pl.pallas_call(
        kern,
        grid=(nb, N_HEADS),
        in_specs=[pl.BlockSpec((1, nq, LANES), head),
                  pl.BlockSpec((1, past, LANES), head),
                  pl.BlockSpec((1, past, LANES), head),
                  pl.BlockSpec((1, nq, LANES), head),
                  pl.BlockSpec((1, nq, LANES), head),
                  pl.BlockSpec((4, DK), const),
                  pl.BlockSpec((1, DV), const)],
        out_specs=pl.BlockSpec((1, nq, LANES), head),
        out_shape=jax.ShapeDtypeStruct((nb, nq, d), BF16),
        compiler_params=_params("arbitrary", "arbitrary"),
        name="diff_attention_cached",
    )(q, cache_k, cache_v, k_new, v_new, lam_params, subw)


def _route(logits):
    lane = lax.broadcasted_iota(jnp.int32, logits.shape, 1)
    valid = lane < N_EXPERTS
    lg = jnp.where(valid, logits, NEG_BIG)
    m1 = jnp.max(lg, axis=-1, keepdims=True)
    lane_f = lane.astype(F32)
    i1 = jnp.min(jnp.where(lg == m1, lane_f, float(LANES)), axis=-1, keepdims=True)
    lg2 = jnp.where(lane_f == i1, NEG_BIG, lg)
    m2 = jnp.max(lg2, axis=-1, keepdims=True)
    i2 = jnp.min(jnp.where(lg2 == m2, lane_f, float(LANES)), axis=-1, keepdims=True)
    e2 = jnp.exp(m2 - m1)
    g1 = 1.0 / (1.0 + e2)
    g2 = e2 / (1.0 + e2)
    return jnp.where(lane_f == i1, g1, 0.0) + jnp.where(lane_f == i2, g2, 0.0)


def _moe_route_kernel(x_ref, nw_ref, wr_ref, h_ref, comb_ref, rk_ref, rkt_ref, cnt_ref, *, rt):
    t = x_ref.shape[0]
    hf = _rms_scale(x_ref[...]) * nw_ref[...]
    h_ref[...] = hf.astype(BF16)
    comb = _route(_dot_exact_rhs_general(hf, wr_ref[...]))
    comb_ref[...] = comb
    sel = jnp.where(comb > 0.0, 1.0, 0.0)
    row = lax.broadcasted_iota(jnp.int32, (rt, rt), 0)
    col = lax.broadcasted_iota(jnp.int32, (rt, rt), 1)
    strict = jnp.where(col < row, 1.0, 0.0).astype(BF16)
    carry = jnp.zeros((1, LANES), F32)
    ranks = []
    for s in range(t // rt):
        sel_s = sel[s * rt:(s + 1) * rt]
        rank_s = _dot(strict, sel_s.astype(BF16)) + carry
        ranks.append(jnp.where(sel_s > 0.0, rank_s, -1.0))
        carry = carry + jnp.sum(sel_s, axis=0, keepdims=True)
    rk = jnp.concatenate(ranks, axis=0) if len(ranks) > 1 else ranks[0]
    rk_ref[...] = rk.astype(jnp.int32)
    rkt_ref[0] = rk.T.astype(jnp.int32)
    cnt_ref[0] = jnp.broadcast_to(carry, (8, LANES)).astype(jnp.int32)


def moe_route(x, nw, wr, *, tb, rt):
    t, d = x.shape
    nb = t // tb
    row = lambda b: (b, 0)
    const = lambda b: (0, 0)
    return pl.pallas_call(
        functools.partial(_moe_route_kernel, rt=rt),
        grid=(nb,),
        in_specs=[pl.BlockSpec((tb, d), row), pl.BlockSpec((1, d), const),
                  pl.BlockSpec((d, LANES), const)],
        out_specs=[pl.BlockSpec((tb, d), row), pl.BlockSpec((tb, LANES), row),
                   pl.BlockSpec((tb, LANES), row),
                   pl.BlockSpec((1, LANES, tb), lambda b: (b, 0, 0)),
                   pl.BlockSpec((1, 8, LANES), lambda b: (b, 0, 0))],
        out_shape=[jax.ShapeDtypeStruct((t, d), BF16),
                   jax.ShapeDtypeStruct((t, LANES), F32),
                   jax.ShapeDtypeStruct((t, LANES), jnp.int32),
                   jax.ShapeDtypeStruct((nb, LANES, tb), jnp.int32),
                   jax.ShapeDtypeStruct((nb, 8, LANES), jnp.int32)],
        compiler_params=_params("arbitrary"),
        name="moe_route",
    )(x, nw, wr)


def _moe_expert_kernel(cnt_ref, x_ref, h_ref, comb_ref, rk_ref, rkt_ref, wg_ref, wu_ref, wd_ref,
                       fw_ref, o_ref, rkc_ref, xg_ref, gs_ref, yacc_ref, *, rt):
    b = pl.program_id(0)
    e = pl.program_id(1)
    f = pl.program_id(2)
    t = x_ref.shape[0]
    n_rows = cnt_ref[b * N_EXPERTS + e]
    n_tiles = (n_rows + (rt - 1)) // rt

    @pl.when(jnp.logical_and(e == 0, f == 0))
    def _():
        o_ref[...] = x_ref[...]

    def tile(j):
        return pl.ds(pl.multiple_of(j * rt, rt), rt)

    @pl.when(f == 0)
    def _():
        lane = lax.broadcasted_iota(jnp.int32, (t, LANES), 1)
        pick = lane == e
        rkc = jnp.sum(jnp.where(pick, rk_ref[...], 0).astype(F32), axis=-1, keepdims=True)
        rkc_ref[...] = jnp.broadcast_to(rkc, (t, LANES)).astype(jnp.int32)
        gate = jnp.sum(jnp.where(pick, comb_ref[...], 0.0), axis=-1, keepdims=True)
        g_hi, g_mid, g_lo = _split3(jnp.broadcast_to(gate, (t, LANES)))
        rk_row = rkt_ref[0, pl.ds(e, 1), :]

        def gather(j, carry):
            rows = lax.broadcasted_iota(jnp.int32, (rt, t), 0) + j * rt
            p = jnp.where(rk_row == rows, 1.0, 0.0).astype(BF16)
            xg_ref[tile(j), :] = _dot(p, h_ref[...]).astype(BF16)
            gs_ref[tile(j), :] = _dot(p, g_hi) + _dot(p, g_mid) + _dot(p, g_lo)
            yacc_ref[tile(j), :] = jnp.zeros((rt, yacc_ref.shape[1]), F32)
            return carry

        lax.fori_loop(0, n_tiles, gather, 0)

    def expert(j, carry):
        xg = xg_ref[tile(j), :]
        g = _dot(xg, wg_ref[...])
        u = _dot(xg, wu_ref[...])
        gates = pltpu.repeat(gs_ref[tile(j), :], g.shape[1] // LANES, 1)
        act = (g * _sigmoid(g) * u * gates).astype(BF16)
        yacc_ref[tile(j), :] += _dot(act, wd_ref[...])
        return carry

    lax.fori_loop(0, n_tiles, expert, 0)

    @pl.when(f == pl.num_programs(2) - 1)
    def _():
        def scatter(j, carry):
            cols = lax.broadcasted_iota(jnp.int32, (t, rt), 1) + j * rt
            s = jnp.where(pltpu.repeat(rkc_ref[...], rt // LANES, 1) == cols, 1.0, 0.0).astype(BF16)
            o_ref[...] += _dot(s, yacc_ref[tile(j), :].astype(BF16))
            return carry

        lax.fori_loop(0, n_tiles, scatter, 0)

    @pl.when(jnp.logical_and(e == pl.num_programs(1) - 1, f == pl.num_programs(2) - 1))
    def _():
        o_ref[...] = _rms_scale(o_ref[...]) * fw_ref[...]


def _dot_exact_rhs_general(x, w):
    xh, xm, xl = _split3(x)
    wh, wm, wl = _split3(w)
    return (_dot(xh, wh) + (_dot(xh, wm) + _dot(xm, wh))
            + (_dot(xh, wl) + _dot(xm, wm) + _dot(xl, wh)))


def moe_ffn_final(x, nw, wr, wg, wu, wd, fw, *, tb, rt, tf):
    t, d = x.shape
    ne, _, ff = wg.shape
    nb = t // tb
    h, comb, rk, rkt, cnt = moe_route(x, nw, wr, tb=tb, rt=rt)
    counts = cnt[:, 0, :ne].reshape(nb * ne)
    once = pl.Buffered(1)
    blk = lambda b, e, f, c: (b, 0)
    const = lambda b, e, f, c: (0, 0)
    grid_spec = pltpu.PrefetchScalarGridSpec(
        num_scalar_prefetch=1,
        grid=(nb, ne, ff // tf),
        in_specs=[pl.BlockSpec((tb, d), blk, pipeline_mode=once),
                  pl.BlockSpec((tb, d), blk, pipeline_mode=once),
                  pl.BlockSpec((tb, LANES), blk, pipeline_mode=once),
                  pl.BlockSpec((tb, LANES), blk, pipeline_mode=once),
                  pl.BlockSpec((1, LANES, tb), lambda b, e, f, c: (b, 0, 0), pipeline_mode=once),
                  pl.BlockSpec((None, d, tf), lambda b, e, f, c: (e, 0, f)),
                  pl.BlockSpec((None, d, tf), lambda b, e, f, c: (e, 0, f)),
                  pl.BlockSpec((None, tf, d), lambda b, e, f, c: (e, f, 0)),
                  pl.BlockSpec((1, d), const)],
        out_specs=pl.BlockSpec((tb, d), blk),
        scratch_shapes=[pltpu.VMEM((tb, LANES), jnp.int32),
                        pltpu.VMEM((tb, d), BF16),
                        pltpu.VMEM((tb, LANES), F32),
                        pltpu.VMEM((tb, d), F32)],
    )
    return pl.pallas_call(
        functools.partial(_moe_expert_kernel, rt=rt),
        grid_spec=grid_spec,
        out_shape=jax.ShapeDtypeStruct((t, d), F32),
        compiler_params=_params("arbitrary", "arbitrary", "arbitrary"),
        name="moe_experts",
    )(counts, x, h, comb, rk, rkt, wg, wu, wd, fw)


def _row(v):
    return v.reshape(1, -1).astype(F32)


def _pad_lanes(v, width):
    v = _row(v)
    return jnp.pad(v, ((0, 0), (0, width - v.shape[1])))


def _trunk(x, pos, conv_in, state_in, past_k, past_v, W, *, ssd_chunk, ssd_valid, tm, tq):
    nb, length, d = x.shape
    t = nb * length
    x0 = x.reshape(t, d)

    proj = rms_matmul(x0, W["mamba_norm_w"], W["w_in"], tm=tm, tn=768)
    proj3 = proj.reshape(nb, length, PROJ_W)
    new_conv = proj3[:, length - (CONV_W - 1):, D_INNER:D_INNER + CONV_DIM]
    conv8 = jnp.pad(conv_in, ((0, 0), (8 - (CONV_W - 1), 0), (0, 0)))
    y, state_out = ssd(proj3, conv8, state_in.reshape(nb, D_INNER, D_STATE),
                       W["conv_w"], W["conv_b"], W["dt_bias"], W["a_log"], W["d_exp"], W["gn_w"],
                       c=ssd_chunk, cv=ssd_valid)
    new_ssm = state_out.reshape(nb, SSM_HEADS, SSM_HEAD_DIM, D_STATE)
    x1 = matmul_residual(y.reshape(t, D_INNER), W["w_out"], x0, tm=tm)

    x2 = swiglu_ffn(x1, W["ffn_norm_w"], W["ffn_wg"], W["ffn_wu"], W["ffn_wd"], tm=tm, tf=1408)

    tables = _rope_tables(pos)
    k, v, q, kb, vb = qkv_proj(x2, W["kv_norm_w"], W["attn_norm_w"], W["w_k"], W["w_v"], W["w_q"],
                               tables, tm=tm)
    lambda_init = 0.8 - 0.6 * math.exp(-0.3 * 1)
    q3 = q.reshape(nb, length, d)
    if past_k is None:
        o = diff_attention(q3, kb.reshape(nb, length, d), vb.reshape(nb, length, d),
                           W["lam"], W["subln_w"], tq=tq, lambda_init=lambda_init)
    else:
        o = diff_attention_cached(q3, past_k.reshape(nb, -1, d), past_v.reshape(nb, -1, d),
                                  k.reshape(nb, length, d), v.reshape(nb, length, d),
                                  W["lam"], W["subln_w"], lambda_init=lambda_init)
    x3 = matmul_residual(o.reshape(t, d), W["w_o"], x2, tm=tm)

    yout = moe_ffn_final(x3, W["moe_norm_w"], W["moe_wr"], W["moe_wg"], W["moe_wu"], W["moe_wd"],
                         W["final_norm_w"], tb=min(t, 2048), rt=256, tf=512)
    return (yout.reshape(nb, length, d), new_conv[None], new_ssm[None],
            k.reshape(nb, length, N_HEADS, 2 * DK), v.reshape(nb, length, N_HEADS, DV))


def kernel(x_prompt, x_sample, cache_conv, state_ssm, cache_k, cache_v, mamba_norm_w, mamba_w_in, mamba_conv_w, mamba_conv_b, mamba_dt_bias, mamba_a_log, mamba_d, mamba_gn_w, mamba_w_out, kv_norm_w, w_k, w_v, attn_norm_w, w_q, lambda_q1, lambda_k1, lambda_q2, lambda_k2, subln_w, w_o, ffn_norm_w, ffn_w_gate, ffn_w_up, ffn_w_down, moe_norm_w, moe_w_router, moe_w_gate, moe_w_up, moe_w_down, final_norm_w):
    w_in = mamba_w_in[0]
    n_dt = w_in.shape[1] - D_INNER - CONV_DIM
    W = dict(
        mamba_norm_w=_row(mamba_norm_w[0]),
        w_in=jnp.pad(w_in, ((0, 0), (0, DT_PAD - n_dt))).astype(BF16),
        conv_w=mamba_conv_w[0].astype(F32),
        conv_b=_row(mamba_conv_b[0]),
        dt_bias=_pad_lanes(mamba_dt_bias[0], LANES),
        a_log=_pad_lanes(mamba_a_log[0], LANES),
        d_exp=_row(jnp.repeat(mamba_d[0], SSM_HEAD_DIM)),
        gn_w=_row(mamba_gn_w[0]),
        w_out=mamba_w_out[0].astype(BF16),
        kv_norm_w=_row(kv_norm_w), w_k=w_k.astype(BF16), w_v=w_v.astype(BF16),
        attn_norm_w=_row(attn_norm_w[0]), w_q=w_q[0].astype(BF16),
        lam=jnp.stack([lambda_q1[0], lambda_k1[0], lambda_q2[0], lambda_k2[0]]).astype(F32),
        subln_w=_row(subln_w[0]), w_o=w_o[0].astype(BF16),
        ffn_norm_w=_row(ffn_norm_w[0]),
        ffn_wg=ffn_w_gate[0].astype(BF16), ffn_wu=ffn_w_up[0].astype(BF16),
        ffn_wd=ffn_w_down[0].astype(BF16),
        moe_norm_w=_row(moe_norm_w[0]),
        moe_wr=jnp.pad(moe_w_router[0].astype(F32), ((0, 0), (0, LANES - N_EXPERTS))),
        moe_wg=moe_w_gate[0].astype(BF16), moe_wu=moe_w_up[0].astype(BF16),
        moe_wd=moe_w_down[0].astype(BF16),
        final_norm_w=_row(final_norm_w),
    )
    bp, lp = x_prompt.shape[0], x_prompt.shape[1]
    bs, ls = x_sample.shape[0], x_sample.shape[1]
    past = cache_k.shape[1]

    conv0 = jnp.zeros((bp, CONV_W - 1, CONV_DIM), F32)
    ssm0 = jnp.zeros((bp, SSM_HEADS, SSM_HEAD_DIM, D_STATE), F32)
    y_p, conv_p, ssm_p, k_p, v_p = _trunk(
        x_prompt, jnp.arange(lp, dtype=jnp.int32), conv0, ssm0, None, None, W,
        ssd_chunk=128, ssd_valid=128, tm=512, tq=512)

    pos_s = jnp.tile(past + jnp.arange(ls, dtype=jnp.int32), bs)
    y_s, conv_s, ssm_s, k_s, v_s = _trunk(
        x_sample, pos_s, cache_conv[0], state_ssm[0], cache_k, cache_v, W,
        ssd_chunk=128, ssd_valid=ls, tm=bs * ls, tq=None)
    return (y_p, y_s, conv_p, ssm_p, k_p, v_p, conv_s, ssm_s, k_s, v_s)
```

```python
import functools
import math

import numpy as np
import jax
import jax.numpy as jnp
from jax import lax
from jax.experimental import pallas as pl
from jax.experimental.pallas import tpu as pltpu

F32 = jnp.float32
BF16 = jnp.bfloat16

EPS = 1e-5
D_MODEL = 1024
D_INNER = 2048
SSM_HEAD_DIM = 64
SSM_HEADS = 32
SSM_GROUPS = 4
D_STATE = 128
GROUP_W = D_INNER // SSM_GROUPS
CONV_W = 4
CONV_DIM = D_INNER + 2 * SSM_GROUPS * D_STATE
DT_PAD = 256
PROJ_W = D_INNER + CONV_DIM + DT_PAD
N_HEADS = 8
DK = 64
DV = 128
ROT_DIM = 16
ROPE_THETA = 500000.0
CHUNK = 64
CHUNK_SHIFT = 6
assert 1 << CHUNK_SHIFT == CHUNK
N_EXPERTS = 8
LANES = 128
NEG_BIG = -1e30
LOG2E = math.log2(math.e)
Q_COLS = 256
ONES_ROWS = 16
SCORE_LOOKAHEAD = 3
VMEM_LIMIT = 56 * 1024 * 1024


def _params(*sem):
    return pltpu.CompilerParams(dimension_semantics=sem, vmem_limit_bytes=VMEM_LIMIT)


def _sigmoid(x):
    return 1.0 / (1.0 + jnp.exp(-x))


def _rms_scale(x):
    return x * lax.rsqrt(jnp.mean(x * x, axis=-1, keepdims=True) + EPS)


def _split3(x):
    hi = x.astype(BF16)
    r1 = x - hi.astype(F32)
    mid = r1.astype(BF16)
    lo = (r1 - mid.astype(F32)).astype(BF16)
    return hi, mid, lo


def _dot(a, b):
    return jnp.dot(a, b, preferred_element_type=F32)


def _lane_tile(x, n):
    return jnp.concatenate([x] * n, axis=1) if n > 1 else x


def _dot_exact_rhs(x, m_bf16):
    hi, mid, lo = _split3(x)
    return _dot(hi, m_bf16) + _dot(mid, m_bf16) + _dot(lo, m_bf16)


def _dot_exact_lhs(m_bf16, x):
    hi, mid, lo = _split3(x)
    return _dot(m_bf16, hi) + _dot(m_bf16, mid) + _dot(m_bf16, lo)


def _rms_matmul_kernel(x_ref, nw_ref, w_ref, o_ref, h_ref):
    @pl.when(pl.program_id(1) == 0)
    def _():
        h_ref[...] = (_rms_scale(x_ref[...]) * nw_ref[...]).astype(BF16)

    o_ref[...] = _dot(h_ref[...], w_ref[...]).astype(o_ref.dtype)


def rms_matmul(x, nw, w, *, tm, tn):
    t, d = x.shape
    n = w.shape[1]
    return pl.pallas_call(
        _rms_matmul_kernel,
        grid=(t // tm, n // tn),
        in_specs=[pl.BlockSpec((tm, d), lambda i, j: (i, 0)),
                  pl.BlockSpec((1, d), lambda i, j: (0, 0)),
                  pl.BlockSpec((d, tn), lambda i, j: (0, j))],
        out_specs=pl.BlockSpec((tm, tn), lambda i, j: (i, j)),
        out_shape=jax.ShapeDtypeStruct((t, n), F32),
        scratch_shapes=[pltpu.VMEM((tm, d), BF16)],
        compiler_params=_params("arbitrary", "arbitrary"),
        name="rms_matmul",
    )(x, nw, w)


def _ssd_kernel(z_ref, xs_ref, b_ref, c_ref, dt_ref, cin_ref, sin_ref, cw_ref, cb_ref,
                dtb_ref, alog_ref, dexp_ref, gnw_ref, e_ref,
                y_ref, sout_ref, state_ref, cbuf_ref, *, c, cv):
    step = pl.program_id(1)

    @pl.when(step == 0)
    def _():
        state_ref[...] = sin_ref[0].T
        cbuf_ref[0:8, :] = cin_ref[0]

    cbuf_ref[8:8 + cv, 0:D_INNER] = xs_ref[0]
    cbuf_ref[8:8 + cv, D_INNER:D_INNER + GROUP_W] = b_ref[0]
    cbuf_ref[8:8 + cv, D_INNER + GROUP_W:CONV_DIM] = c_ref[0]
    if cv < c:
        cbuf_ref[8 + cv:8 + c, :] = jnp.zeros((c - cv, CONV_DIM), F32)

    conv = cb_ref[...] + cbuf_ref[5:5 + c, :] * cw_ref[0:1, :]
    for k in range(1, CONV_W):
        conv = conv + cbuf_ref[5 + k:5 + k + c, :] * cw_ref[k:k + 1, :]
    conv = conv * _sigmoid(conv)
    if cv == c:
        cbuf_ref[5:8, :] = cbuf_ref[5 + c:8 + c, :]

    xs = conv[:, 0:D_INNER]
    bm = conv[:, D_INNER:D_INNER + GROUP_W]
    cm = conv[:, D_INNER + GROUP_W:CONV_DIM]

    dt_in = dt_ref[0][:, 0:LANES] + dtb_ref[...]
    dt = jnp.maximum(dt_in, 0.0) + jnp.log1p(jnp.exp(-jnp.abs(dt_in)))
    if cv < c:
        dt = jnp.concatenate([dt, jnp.zeros((c - cv, LANES), F32)], axis=0)
    a = -jnp.exp(alog_ref[...])
    dta = dt * a

    row = lax.broadcasted_iota(jnp.int32, (c, c), 0)
    col = lax.broadcasted_iota(jnp.int32, (c, c), 1)
    tril = col <= row
    tri = jnp.where(tril, 1.0, 0.0).astype(BF16)
    cs = _dot_exact_lhs(tri, dta)
    cs_t = cs.T

    e = e_ref[...]
    dt_x = _dot_exact_rhs(dt, e)
    cs_x = _dot_exact_rhs(cs, e)
    cs_end_x = cs_x[c - 1:c, :]
    xdt = xs * dt_x
    decay_in = jnp.exp(cs_x)
    xdt_b = xdt.astype(BF16)
    xdt_end = (xdt * jnp.exp(cs_end_x - cs_x)).astype(BF16)
    state_decay = jnp.exp(cs_end_x)

    lane = lax.broadcasted_iota(jnp.int32, (c, LANES), 1)
    low_half = lane < SSM_HEAD_DIM

    y_groups = []
    for g in range(SSM_GROUPS):
        gs = slice(g * GROUP_W, (g + 1) * GROUP_W)
        ns = slice(g * D_STATE, (g + 1) * D_STATE)
        bg_t = bm[:, ns].T.astype(BF16)
        cg = cm[:, ns].astype(BF16)
        cb = _dot(cg, bg_t)
        sg = state_ref[:, gs]
        y_off = _dot(cg, sg.astype(BF16)) * decay_in[:, gs]
        pairs = []
        for j in range(GROUP_W // LANES):
            h0 = g * (GROUP_W // SSM_HEAD_DIM) + 2 * j
            xp = xdt_b[:, g * GROUP_W + j * LANES:g * GROUP_W + (j + 1) * LANES]
            ys = []
            for h in (h0, h0 + 1):
                seg = cs[:, h:h + 1] - cs_t[h:h + 1, :]
                m = (cb * jnp.where(tril, jnp.exp(seg), 0.0)).astype(BF16)
                ys.append(_dot(m, xp))
            pairs.append(jnp.where(low_half, ys[0], ys[1]))
        y_groups.append(jnp.concatenate(pairs, axis=1) + y_off)
        state_ref[:, gs] = sg * state_decay[:, gs] + _dot(bg_t, xdt_end[:, gs])

    y = jnp.concatenate(y_groups, axis=1) + dexp_ref[...] * xs
    if cv < c:
        y = y[0:cv]
    zv = z_ref[0]
    y = y * (zv * _sigmoid(zv))
    outs = []
    for g in range(SSM_GROUPS):
        gs = slice(g * GROUP_W, (g + 1) * GROUP_W)
        outs.append(_rms_scale(y[:, gs]))
    y_ref[0] = (jnp.concatenate(outs, axis=1) * gnw_ref[...]).astype(y_ref.dtype)

    @pl.when(step == pl.num_programs(1) - 1)
    def _():
        sout_ref[0] = state_ref[...].T


def ssd(proj, conv_in, state_in, cw, cb, dtb, alog, dexp, gnw, *, c, cv):
    nb, length, _ = proj.shape
    steps = length // cv
    head_of_channel = np.arange(D_INNER) // SSM_HEAD_DIM
    expand = jnp.asarray(np.arange(LANES)[:, None] == head_of_channel[None, :], dtype=BF16)
    const = lambda b, l: (0, 0)
    kern = functools.partial(_ssd_kernel, c=c, cv=cv)
    return pl.pallas_call(
        kern,
        grid=(nb, steps),
        in_specs=[
            pl.BlockSpec((1, cv, D_INNER), lambda b, l: (b, l, 0)),
            pl.BlockSpec((1, cv, D_INNER), lambda b, l: (b, l, 1)),
            pl.BlockSpec((1, cv, GROUP_W), lambda b, l: (b, l, 2 * D_INNER // GROUP_W)),
            pl.BlockSpec((1, cv, GROUP_W), lambda b, l: (b, l, 2 * D_INNER // GROUP_W + 1)),
            pl.BlockSpec((1, cv, DT_PAD), lambda b, l: (b, l, (D_INNER + CONV_DIM) // DT_PAD)),
            pl.BlockSpec((1, 8, CONV_DIM), lambda b, l: (b, 0, 0)),
            pl.BlockSpec((1, D_INNER, D_STATE), lambda b, l: (b, 0, 0)),
            pl.BlockSpec((CONV_W, CONV_DIM), const),
            pl.BlockSpec((1, CONV_DIM), const),
            pl.BlockSpec((1, LANES), const),
            pl.BlockSpec((1, LANES), const),
            pl.BlockSpec((1, D_INNER), const),
            pl.BlockSpec((1, D_INNER), const),
            pl.BlockSpec((LANES, D_INNER), const),
        ],
        out_specs=[
            pl.BlockSpec((1, cv, D_INNER), lambda b, l: (b, l, 0)),
            pl.BlockSpec((1, D_INNER, D_STATE), lambda b, l: (b, 0, 0)),
        ],
        out_shape=[
            jax.ShapeDtypeStruct((nb, length, D_INNER), BF16),
            jax.ShapeDtypeStruct((nb, D_INNER, D_STATE), F32),
        ],
        scratch_shapes=[pltpu.VMEM((D_STATE, D_INNER), F32),
                        pltpu.VMEM((8 + c, CONV_DIM), F32)],
        compiler_params=_params("arbitrary", "arbitrary"),
        name="ssd",
    )(proj, proj, proj, proj, proj, conv_in, state_in, cw, cb, dtb, alog, dexp, gnw, expand)


def _mm_res_kernel(a_ref, w_ref, r_ref, o_ref):
    o_ref[...] = r_ref[...] + _dot(a_ref[...], w_ref[...])


def matmul_residual(a, w, res, *, tm):
    t, k = a.shape
    n = w.shape[1]
    return pl.pallas_call(
        _mm_res_kernel,
        grid=(t // tm,),
        in_specs=[pl.BlockSpec((tm, k), lambda i: (i, 0)),
                  pl.BlockSpec((k, n), lambda i: (0, 0)),
                  pl.BlockSpec((tm, n), lambda i: (i, 0))],
        out_specs=pl.BlockSpec((tm, n), lambda i: (i, 0)),
        out_shape=jax.ShapeDtypeStruct((t, n), F32),
        compiler_params=_params("arbitrary"),
        name="matmul_residual",
    )(a, w, res)


def _ffn_kernel(x_ref, nw_ref, wg_ref, wu_ref, wd_ref, o_ref, h_ref, acc_ref):
    f = pl.program_id(1)

    @pl.when(f == 0)
    def _():
        x = x_ref[...]
        h_ref[...] = (_rms_scale(x) * nw_ref[...]).astype(BF16)
        acc_ref[...] = x

    h = h_ref[...]
    g = _dot(h, wg_ref[...])
    u = _dot(h, wu_ref[...])
    act = (g * _sigmoid(g) * u).astype(BF16)
    acc_ref[...] += _dot(act, wd_ref[...])

    @pl.when(f == pl.num_programs(1) - 1)
    def _():
        o_ref[...] = acc_ref[...]


def swiglu_ffn(x, nw, wg, wu, wd, *, tm, tf):
    t, d = x.shape
    ff = wg.shape[1]
    return pl.pallas_call(
        _ffn_kernel,
        grid=(t // tm, ff // tf),
        in_specs=[pl.BlockSpec((tm, d), lambda i, f: (i, 0)),
                  pl.BlockSpec((1, d), lambda i, f: (0, 0)),
                  pl.BlockSpec((d, tf), lambda i, f: (0, f)),
                  pl.BlockSpec((d, tf), lambda i, f: (0, f)),
                  pl.BlockSpec((tf, d), lambda i, f: (f, 0))],
        out_specs=pl.BlockSpec((tm, d), lambda i, f: (i, 0)),
        out_shape=jax.ShapeDtypeStruct((t, d), F32),
        scratch_shapes=[pltpu.VMEM((tm, d), BF16), pltpu.VMEM((tm, d), F32)],
        compiler_params=_params("arbitrary", "arbitrary"),
        name="swiglu_ffn",
    )(x, nw, wg, wu, wd)


def _qkv_kernel(x_ref, kvw_ref, aw_ref, wk_ref, wv_ref, wq_ref, cos_ref, sa_ref, sb_ref,
                k_ref, v_ref, *extra_refs, transposed):
    xn = _rms_scale(x_ref[...])
    hkv = (xn * kvw_ref[...]).astype(BF16)
    hq = (xn * aw_ref[...]).astype(BF16)
    cos, sa, sb = cos_ref[...], sa_ref[...], sb_ref[...]

    def rope(t):
        blocks = []
        for j in range(t.shape[1] // LANES):
            tb = t[:, j * LANES:(j + 1) * LANES]
            blocks.append(tb * cos + pltpu.roll(tb, LANES - ROT_DIM // 2, 1) * sa
                          + pltpu.roll(tb, ROT_DIM // 2, 1) * sb)
        return jnp.concatenate(blocks, axis=1)

    k = rope(_dot(hkv, wk_ref[...]))
    v = _dot(hkv, wv_ref[...])
    q = rope(_dot(hq, wq_ref[...]))
    k_ref[...] = k
    v_ref[...] = v
    if transposed:
        kb_ref, vt_ref, qt_ref = extra_refs
        kb_ref[...] = k.astype(BF16)
        vt_ref[0] = v.T.astype(BF16)
        qt_ref[0] = (q * (DK ** -0.5 * LOG2E)).T.astype(BF16)
    else:
        (q_ref,) = extra_refs
        q_ref[...] = (q * (DK ** -0.5)).astype(BF16)


def _rope_tables(pos):
    half = ROT_DIM // 2
    inv_freq = ROPE_THETA ** (-jnp.arange(half, dtype=F32) / half)
    ang = pos.astype(F32)[:, None] * inv_freq[None, :]
    cos, sin = jnp.cos(ang), jnp.sin(ang)
    n = pos.shape[0]
    ones = jnp.ones((n, DK - ROT_DIM), F32)
    zeros_h = jnp.zeros((n, half), F32)
    zeros_r = jnp.zeros((n, DK - ROT_DIM), F32)
    cos_t = jnp.concatenate([cos, cos, ones], axis=1)
    sa_t = jnp.concatenate([-sin, zeros_h, zeros_r], axis=1)
    sb_t = jnp.concatenate([zeros_h, sin, zeros_r], axis=1)
    rep = lambda t: jnp.tile(t, (1, LANES // DK))
    return rep(cos_t), rep(sa_t), rep(sb_t)


def qkv_proj(x, kvw, aw, wk, wv, wq, tables, *, tm, streams, transposed):
    t, d = x.shape
    period = tables[0].shape[0] // tm
    row = lambda i: (i, 0)
    const = lambda i: (0, 0)
    tab = lambda i: (i % period, 0)
    f32_out = jax.ShapeDtypeStruct((t, d), F32)
    bf_out = jax.ShapeDtypeStruct((t, d), BF16)
    out_specs = [pl.BlockSpec((tm, d), row)] * 3
    out_shape = [f32_out, f32_out, bf_out]
    if transposed:
        per = t // streams // tm
        tr_spec = pl.BlockSpec((1, d, tm), lambda i: (i // per, 0, i % per))
        tr_out = jax.ShapeDtypeStruct((streams, d, t // streams), BF16)
        out_specs += [tr_spec, tr_spec]
        out_shape += [tr_out, tr_out]
    return pl.pallas_call(
        functools.partial(_qkv_kernel, transposed=transposed),
        grid=(t // tm,),
        in_specs=[pl.BlockSpec((tm, d), row),
                  pl.BlockSpec((1, d), const), pl.BlockSpec((1, d), const),
                  pl.BlockSpec((d, d), const), pl.BlockSpec((d, d), const), pl.BlockSpec((d, d), const),
                  pl.BlockSpec((tm, LANES), tab), pl.BlockSpec((tm, LANES), tab),
                  pl.BlockSpec((tm, LANES), tab)],
        out_specs=out_specs,
        out_shape=out_shape,
        compiler_params=_params("arbitrary"),
        name="qkv_proj",
    )(x, kvw, aw, wk, wv, wq, *tables)


def _lambda_value(lp_ref, lambda_init):
    lp = lp_ref[...]
    s1 = jnp.sum(lp[0:1] * lp[1:2], axis=-1, keepdims=True)
    s2 = jnp.sum(lp[2:3] * lp[3:4], axis=-1, keepdims=True)
    return jnp.exp(s1) - jnp.exp(s2) + lambda_init


def _diff_finish(o1, o2, lam, subw, lambda_init):
    o = o1 - lam * o2
    return _rms_scale(o) * subw * (1.0 - lambda_init)


def _attn_kernel(qt_ref, kt_ref, q_ref, k_ref, v_ref, lp_ref, subw_ref, o_ref,
                 qs_ref, *state_refs, tq, tk, lambda_init):
    p = pl.program_id(1)
    qi = qt_ref[p]
    ki = kt_ref[p]
    n_qc = 2 * tq // Q_COLS
    m_refs, acc_refs = state_refs[:n_qc], state_refs[n_qc:]

    @pl.when(ki == 0)
    def _():
        for m_ref, acc_ref in zip(m_refs, acc_refs):
            m_ref[...] = jnp.full(m_ref.shape, NEG_BIG, F32)
            acc_ref[...] = jnp.zeros(acc_ref.shape, F32)
        first_sub = lax.broadcasted_iota(jnp.int32, (LANES, tq), 0) < DK
        for h in range(N_HEADS):
            qh = q_ref[0, h * LANES:(h + 1) * LANES, :]
            zero = jnp.zeros_like(qh)
            qs_ref[h, :, 0:tq] = jnp.where(first_sub, qh, zero)
            qs_ref[h, :, tq:2 * tq] = jnp.where(first_sub, zero, qh)

    ones = jnp.ones((ONES_ROWS, tk), BF16)

    def sweep(masked):
        if masked:
            k_chunk = (ki * tk + lax.broadcasted_iota(jnp.int32, (tk, Q_COLS), 0)) >> CHUNK_SHIFT
            q_lane = lax.broadcasted_iota(jnp.int32, (tk, Q_COLS), 1)

        def scores(h, c):
            kh = k_ref[0, :, h * LANES:(h + 1) * LANES]
            s = _dot(kh, qs_ref[h, :, c * Q_COLS:(c + 1) * Q_COLS])
            if masked:
                q_chunk = (qi * tq + (c * Q_COLS) % tq + q_lane) >> CHUNK_SHIFT
                s = jnp.where(k_chunk <= q_chunk, s, NEG_BIG)
            return s

        groups = [(h, c) for h in range(N_HEADS) for c in range(n_qc)]
        pending = [scores(*g) for g in groups[:SCORE_LOOKAHEAD]]
        for i, (h, c) in enumerate(groups):
            s = pending.pop(0)
            if i + SCORE_LOOKAHEAD < len(groups):
                pending.append(scores(*groups[i + SCORE_LOOKAHEAD]))
            vt = jnp.concatenate([v_ref[0, h * LANES:(h + 1) * LANES, :], ones], axis=0)
            m_prev = m_refs[c][h:h + 1, :]
            m_new = jnp.maximum(m_prev, jnp.max(s, axis=0, keepdims=True))
            alpha = jnp.exp2(m_prev - m_new)
            pr = jnp.exp2(s - m_new).astype(BF16)
            acc_refs[c][h] = acc_refs[c][h] * alpha + _dot(vt, pr)
            m_refs[c][h:h + 1, :] = m_new

    @pl.when(ki < qi)
    def _():
        sweep(False)

    @pl.when(ki == qi)
    def _():
        sweep(True)
        lam = _lambda_value(lp_ref, lambda_init)
        subw = subw_ref[...]

        for h in range(N_HEADS):
            hs = slice(h * LANES, (h + 1) * LANES)
            for r in range(tq // Q_COLS):
                a1 = acc_refs[r][h]
                a2 = acc_refs[r + tq // Q_COLS][h]
                o_t = a1[0:DV] / a1[DV:DV + 1] - lam * (a2[0:DV] / a2[DV:DV + 1])
                res = _rms_scale(o_t.T) * subw * (1.0 - lambda_init)
                o_ref[0, r * Q_COLS:(r + 1) * Q_COLS, hs] = res.astype(o_ref.dtype)


def diff_attention(q_t, k, v_t, lam_params, subw, *, tq, lambda_init):
    nb, d, length = q_t.shape
    tk = tq
    nq = length // tq
    qt = np.concatenate([np.full(i + 1, i) for i in range(nq)]).astype(np.int32)
    kt = np.concatenate([np.arange(i + 1) for i in range(nq)]).astype(np.int32)
    kern = functools.partial(_attn_kernel, tq=tq, tk=tk, lambda_init=lambda_init)
    n_qc = 2 * tq // Q_COLS
    grid_spec = pltpu.PrefetchScalarGridSpec(
        num_scalar_prefetch=2,
        grid=(nb, len(qt)),
        in_specs=[pl.BlockSpec((1, d, tq), lambda b, p, qt_r, kt_r: (b, 0, qt_r[p])),
                  pl.BlockSpec((1, tk, d), lambda b, p, qt_r, kt_r: (b, kt_r[p], 0)),
                  pl.BlockSpec((1, d, tk), lambda b, p, qt_r, kt_r: (b, 0, kt_r[p])),
                  pl.BlockSpec((4, DK), lambda b, p, qt_r, kt_r: (0, 0)),
                  pl.BlockSpec((1, DV), lambda b, p, qt_r, kt_r: (0, 0))],
        out_specs=pl.BlockSpec((1, tq, d), lambda b, p, qt_r, kt_r: (b, qt_r[p], 0)),
        scratch_shapes=([pltpu.VMEM((N_HEADS, LANES, 2 * tq), BF16)]
                        + [pltpu.VMEM((N_HEADS, Q_COLS), F32)] * n_qc
                        + [pltpu.VMEM((N_HEADS, DV + ONES_ROWS, Q_COLS), F32)] * n_qc),
    )
    return pl.pallas_call(
        kern,
        grid_spec=grid_spec,
        out_shape=jax.ShapeDtypeStruct((nb, length, d), BF16),
        compiler_params=_params("arbitrary", "arbitrary"),
        name="diff_attention",
    )(jnp.asarray(qt), jnp.asarray(kt), q_t, k, v_t, lam_params, subw)


def _attn_cached_kernel(q_ref, ck_ref, cv_ref, kn_ref, vn_ref, lp_ref, subw_ref, o_ref,
                        *, nq, lambda_init):
    q = q_ref[0]
    lane = lax.broadcasted_iota(jnp.int32, (nq, LANES), 1)
    zero = jnp.zeros_like(q)
    qs = jnp.concatenate([jnp.where(lane < DK, q, zero), jnp.where(lane < DK, zero, q)], axis=0)
    nt = (((1,), (1,)), ((), ()))
    pad = jnp.zeros((LANES - nq, LANES), BF16)
    kc = ck_ref[0].astype(BF16)
    vc = cv_ref[0].astype(BF16)
    kn = jnp.concatenate([kn_ref[0].astype(BF16), pad], axis=0)
    vn = jnp.concatenate([vn_ref[0].astype(BF16), pad], axis=0)
    s_c = lax.dot_general(qs, kc, nt, preferred_element_type=F32)
    s_n = lax.dot_general(qs, kn, nt, preferred_element_type=F32)
    col = lax.broadcasted_iota(jnp.int32, s_n.shape, 1)
    s_n = jnp.where(col < nq, s_n, NEG_BIG)
    m = jnp.maximum(jnp.max(s_c, axis=-1, keepdims=True), jnp.max(s_n, axis=-1, keepdims=True))
    p_c = jnp.exp(s_c - m)
    p_n = jnp.exp(s_n - m)
    denom = jnp.sum(p_c, axis=-1, keepdims=True) + jnp.sum(p_n, axis=-1, keepdims=True)
    o = (_dot(p_c.astype(BF16), vc) + _dot(p_n.astype(BF16), vn)) / denom
    lam = _lambda_value(lp_ref, lambda_init)
    res = _diff_finish(o[0:nq], o[nq:2 * nq], lam, subw_ref[...], lambda_init)
    o_ref[0] = res.astype(o_ref.dtype)


def diff_attention_cached(q, cache_k, cache_v, k_new, v_new, lam_params, subw, *, lambda_init):
    nb, nq, d = q.shape
    past = cache_k.shape[1]
    kern = functools.partial(_attn_cached_kernel, nq=nq, lambda_init=lambda_init)
    head = lambda b, h: (b, 0, h)
    const = lambda b, h: (0, 0)
    return pl.pallas_call(
        kern,
        grid=(nb, N_HEADS),
        in_specs=[pl.BlockSpec((1, nq, LANES), head),
                  pl.BlockSpec((1, past, LANES), head),
                  pl.BlockSpec((1, past, LANES), head),
                  pl.BlockSpec((1, nq, LANES), head),
                  pl.BlockSpec((1, nq, LANES), head),
                  pl.BlockSpec((4, DK), const),
                  pl.BlockSpec((1, DV), const)],
        out_specs=pl.BlockSpec((1, nq, LANES), head),
        out_shape=jax.ShapeDtypeStruct((nb, nq, d), BF16),
        compiler_params=_params("arbitrary", "arbitrary"),
        name="diff_attention_cached",
    )(q, cache_k, cache_v, k_new, v_new, lam_params, subw)


def _route(logits):
    lane = lax.broadcasted_iota(jnp.int32, logits.shape, 1)
    valid = lane < N_EXPERTS
    lg = jnp.where(valid, logits, NEG_BIG)
    m1 = jnp.max(lg, axis=-1, keepdims=True)
    lane_f = lane.astype(F32)
    i1 = jnp.min(jnp.where(lg == m1, lane_f, float(LANES)), axis=-1, keepdims=True)
    lg2 = jnp.where(lane_f == i1, NEG_BIG, lg)
    m2 = jnp.max(lg2, axis=-1, keepdims=True)
    i2 = jnp.min(jnp.where(lg2 == m2, lane_f, float(LANES)), axis=-1, keepdims=True)
    e2 = jnp.exp(m2 - m1)
    g1 = 1.0 / (1.0 + e2)
    g2 = e2 / (1.0 + e2)
    return jnp.where(lane_f == i1, g1, 0.0) + jnp.where(lane_f == i2, g2, 0.0)


def _moe_route_kernel(x_ref, nw_ref, wr_ref, h_ref, comb_ref, rk_ref, rkt_ref, cnt_ref, *, rt):
    t = x_ref.shape[0]
    hf = _rms_scale(x_ref[...]) * nw_ref[...]
    h_ref[...] = hf.astype(BF16)
    comb = _route(_dot_exact_rhs_general(hf, wr_ref[...]))
    comb_ref[...] = comb
    sel = jnp.where(comb > 0.0, 1.0, 0.0)
    row = lax.broadcasted_iota(jnp.int32, (rt, rt), 0)
    col = lax.broadcasted_iota(jnp.int32, (rt, rt), 1)
    strict = jnp.where(col < row, 1.0, 0.0).astype(BF16)
    carry = jnp.zeros((1, LANES), F32)
    ranks = []
    for s in range(t // rt):
        sel_s = sel[s * rt:(s + 1) * rt]
        rank_s = _dot(strict, sel_s.astype(BF16)) + carry
        ranks.append(jnp.where(sel_s > 0.0, rank_s, -1.0))
        carry = carry + jnp.sum(sel_s, axis=0, keepdims=True)
    rk = jnp.concatenate(ranks, axis=0) if len(ranks) > 1 else ranks[0]
    rk_ref[...] = rk.astype(jnp.int32)
    rkt_ref[0] = rk.T.astype(jnp.int32)
    cnt_ref[0] = jnp.broadcast_to(carry, (8, LANES)).astype(jnp.int32)


def moe_route(x, nw, wr, *, tb, rt):
    t, d = x.shape
    nb = t // tb
    row = lambda b: (b, 0)
    const = lambda b: (0, 0)
    return pl.pallas_call(
        functools.partial(_moe_route_kernel, rt=rt),
        grid=(nb,),
        in_specs=[pl.BlockSpec((tb, d), row), pl.BlockSpec((1, d), const),
                  pl.BlockSpec((d, LANES), const)],
        out_specs=[pl.BlockSpec((tb, d), row), pl.BlockSpec((tb, LANES), row),
                   pl.BlockSpec((tb, LANES), row),
                   pl.BlockSpec((1, LANES, tb), lambda b: (b, 0, 0)),
                   pl.BlockSpec((1, 8, LANES), lambda b: (b, 0, 0))],
        out_shape=[jax.ShapeDtypeStruct((t, d), BF16),
                   jax.ShapeDtypeStruct((t, LANES), F32),
                   jax.ShapeDtypeStruct((t, LANES), jnp.int32),
                   jax.ShapeDtypeStruct((nb, LANES, tb), jnp.int32),
                   jax.ShapeDtypeStruct((nb, 8, LANES), jnp.int32)],
        compiler_params=_params("arbitrary"),
        name="moe_route",
    )(x, nw, wr)


def _moe_expert_kernel(cnt_ref, x_ref, h_ref, comb_ref, rk_ref, rkt_ref, wg_ref, wu_ref, wd_ref,
                       fw_ref, o_ref, rkc_ref, xg_ref, gs_ref, yacc_ref, *, rt):
    b = pl.program_id(0)
    e = pl.program_id(1)
    f = pl.program_id(2)
    t = x_ref.shape[0]
    n_rows = cnt_ref[b * N_EXPERTS + e]
    n_tiles = (n_rows + (rt - 1)) // rt

    @pl.when(jnp.logical_and(e == 0, f == 0))
    def _():
        o_ref[...] = x_ref[...]

    def tile(j):
        return pl.ds(pl.multiple_of(j * rt, rt), rt)

    @pl.when(f == 0)
    def _():
        lane = lax.broadcasted_iota(jnp.int32, (t, LANES), 1)
        pick = lane == e
        rkc = jnp.sum(jnp.where(pick, rk_ref[...], 0).astype(F32), axis=-1, keepdims=True)
        rkc_ref[...] = jnp.broadcast_to(rkc, (t, LANES)).astype(jnp.int32)
        gate = jnp.sum(jnp.where(pick, comb_ref[...], 0.0), axis=-1, keepdims=True)
        g_hi, g_mid, g_lo = _split3(jnp.broadcast_to(gate, (t, LANES)))
        g3 = jnp.where(lane == 0, g_hi.astype(F32),
                       jnp.where(lane == 1, g_mid.astype(F32),
                                 jnp.where(lane == 2, g_lo.astype(F32), 0.0))).astype(BF16)
        rk_row = rkt_ref[0, pl.ds(e, 1), :]

        def gather(j, carry):
            rows = lax.broadcasted_iota(jnp.int32, (rt, t), 0) + j * rt
            p = jnp.where(rk_row == rows, 1.0, 0.0).astype(BF16)
            xg_ref[tile(j), :] = _dot(p, h_ref[...]).astype(BF16)
            gate_rows = jnp.sum(_dot(p, g3), axis=-1, keepdims=True)
            gs_ref[tile(j), :] = jnp.broadcast_to(gate_rows, (rt, LANES))
            yacc_ref[tile(j), :] = jnp.zeros((rt, yacc_ref.shape[1]), F32)
            return carry

        lax.fori_loop(0, n_tiles, gather, 0)

    def expert(j, carry):
        xg = xg_ref[tile(j), :]
        g = _dot(xg, wg_ref[...])
        u = _dot(xg, wu_ref[...])
        gates = _lane_tile(gs_ref[tile(j), :], g.shape[1] // LANES)
        act = (g * _sigmoid(g) * u * gates).astype(BF16)
        yacc_ref[tile(j), :] += _dot(act, wd_ref[...])
        return carry

    lax.fori_loop(0, n_tiles, expert, 0)

    @pl.when(f == pl.num_programs(2) - 1)
    def _():
        def scatter(j, carry):
            cols = lax.broadcasted_iota(jnp.int32, (t, rt), 1) + j * rt
            s = jnp.where(_lane_tile(rkc_ref[...], rt // LANES) == cols, 1.0, 0.0).astype(BF16)
            o_ref[...] += _dot(s, yacc_ref[tile(j), :].astype(BF16))
            return carry

        lax.fori_loop(0, n_tiles, scatter, 0)

    @pl.when(jnp.logical_and(e == pl.num_programs(1) - 1, f == pl.num_programs(2) - 1))
    def _():
        o_ref[...] = _rms_scale(o_ref[...]) * fw_ref[...]


def _dot_exact_rhs_general(x, w):
    xh, xm, xl = _split3(x)
    wh, wm, wl = _split3(w)
    return (_dot(xh, wh) + (_dot(xh, wm) + _dot(xm, wh))
            + (_dot(xh, wl) + _dot(xm, wm) + _dot(xl, wh)))


def moe_ffn_final(x, nw, wr, wg, wu, wd, fw, *, tb, rt, tf):
    t, d = x.shape
    ne, _, ff = wg.shape
    nb = t // tb
    h, comb, rk, rkt, cnt = moe_route(x, nw, wr, tb=tb, rt=rt)
    counts = cnt[:, 0, :ne].reshape(nb * ne)
    once = pl.Buffered(1)
    blk = lambda b, e, f, c: (b, 0)
    const = lambda b, e, f, c: (0, 0)
    grid_spec = pltpu.PrefetchScalarGridSpec(
        num_scalar_prefetch=1,
        grid=(nb, ne, ff // tf),
        in_specs=[pl.BlockSpec((tb, d), blk, pipeline_mode=once),
                  pl.BlockSpec((tb, d), blk, pipeline_mode=once),
                  pl.BlockSpec((tb, LANES), blk, pipeline_mode=once),
                  pl.BlockSpec((tb, LANES), blk, pipeline_mode=once),
                  pl.BlockSpec((1, LANES, tb), lambda b, e, f, c: (b, 0, 0), pipeline_mode=once),
                  pl.BlockSpec((None, d, tf), lambda b, e, f, c: (e, 0, f)),
                  pl.BlockSpec((None, d, tf), lambda b, e, f, c: (e, 0, f)),
                  pl.BlockSpec((None, tf, d), lambda b, e, f, c: (e, f, 0)),
                  pl.BlockSpec((1, d), const)],
        out_specs=pl.BlockSpec((tb, d), blk),
        scratch_shapes=[pltpu.VMEM((tb, LANES), jnp.int32),
                        pltpu.VMEM((tb, d), BF16),
                        pltpu.VMEM((tb, LANES), F32),
                        pltpu.VMEM((tb, d), F32)],
    )
    return pl.pallas_call(
        functools.partial(_moe_expert_kernel, rt=rt),
        grid_spec=grid_spec,
        out_shape=jax.ShapeDtypeStruct((t, d), F32),
        compiler_params=_params("arbitrary", "arbitrary", "arbitrary"),
        name="moe_experts",
    )(counts, x, h, comb, rk, rkt, wg, wu, wd, fw)


def _row(v):
    return v.reshape(1, -1).astype(F32)


def _pad_lanes(v, width):
    v = _row(v)
    return jnp.pad(v, ((0, 0), (0, width - v.shape[1])))


def _trunk(x, pos, conv_in, state_in, past_k, past_v, W, *, ssd_chunk, ssd_valid, tm, tq):
    nb, length, d = x.shape
    t = nb * length
    x0 = x.reshape(t, d)

    proj = rms_matmul(x0, W["mamba_norm_w"], W["w_in"], tm=tm, tn=768)
    proj3 = proj.reshape(nb, length, PROJ_W)
    new_conv = proj3[:, length - (CONV_W - 1):, D_INNER:D_INNER + CONV_DIM]
    conv8 = jnp.pad(conv_in, ((0, 0), (8 - (CONV_W - 1), 0), (0, 0)))
    y, state_out = ssd(proj3, conv8, state_in.reshape(nb, D_INNER, D_STATE),
                       W["conv_w"], W["conv_b"], W["dt_bias"], W["a_log"], W["d_exp"], W["gn_w"],
                       c=ssd_chunk, cv=ssd_valid)
    new_ssm = state_out.reshape(nb, SSM_HEADS, SSM_HEAD_DIM, D_STATE)
    x1 = matmul_residual(y.reshape(t, D_INNER), W["w_out"], x0, tm=tm)

    x2 = swiglu_ffn(x1, W["ffn_norm_w"], W["ffn_wg"], W["ffn_wu"], W["ffn_wd"], tm=tm, tf=1408)

    tables = _rope_tables(pos)
    proj_args = (x2, W["kv_norm_w"], W["attn_norm_w"], W["w_k"], W["w_v"], W["w_q"], tables)
    lambda_init = 0.8 - 0.6 * math.exp(-0.3 * 1)
    if past_k is None:
        k, v, kb, v_t, q_t = qkv_proj(*proj_args, tm=tm, streams=nb, transposed=True)
        o = diff_attention(q_t, kb.reshape(nb, length, d), v_t,
                           W["lam"], W["subln_w"], tq=tq, lambda_init=lambda_init)
    else:
        k, v, q = qkv_proj(*proj_args, tm=tm, streams=nb, transposed=False)
        q3 = q.reshape(nb, length, d)
        o = diff_attention_cached(q3, past_k.reshape(nb, -1, d), past_v.reshape(nb, -1, d),
                                  k.reshape(nb, length, d), v.reshape(nb, length, d),
                                  W["lam"], W["subln_w"], lambda_init=lambda_init)
    x3 = matmul_residual(o.reshape(t, d), W["w_o"], x2, tm=tm)

    yout = moe_ffn_final(x3, W["moe_norm_w"], W["moe_wr"], W["moe_wg"], W["moe_wu"], W["moe_wd"],
                         W["final_norm_w"], tb=min(t, 2048), rt=256, tf=512)
    return (yout.reshape(nb, length, d), new_conv[None], new_ssm[None],
            k.reshape(nb, length, N_HEADS, 2 * DK), v.reshape(nb, length, N_HEADS, DV))


def kernel(x_prompt, x_sample, cache_conv, state_ssm, cache_k, cache_v, mamba_norm_w, mamba_w_in, mamba_conv_w, mamba_conv_b, mamba_dt_bias, mamba_a_log, mamba_d, mamba_gn_w, mamba_w_out, kv_norm_w, w_k, w_v, attn_norm_w, w_q, lambda_q1, lambda_k1, lambda_q2, lambda_k2, subln_w, w_o, ffn_norm_w, ffn_w_gate, ffn_w_up, ffn_w_down, moe_norm_w, moe_w_router, moe_w_gate, moe_w_up, moe_w_down, final_norm_w):
    w_in = mamba_w_in[0]
    n_dt = w_in.shape[1] - D_INNER - CONV_DIM
    W = dict(
        mamba_norm_w=_row(mamba_norm_w[0]),
        w_in=jnp.pad(w_in, ((0, 0), (0, DT_PAD - n_dt))).astype(BF16),
        conv_w=mamba_conv_w[0].astype(F32),
        conv_b=_row(mamba_conv_b[0]),
        dt_bias=_pad_lanes(mamba_dt_bias[0], LANES),
        a_log=_pad_lanes(mamba_a_log[0], LANES),
        d_exp=_row(jnp.repeat(mamba_d[0], SSM_HEAD_DIM)),
        gn_w=_row(mamba_gn_w[0]),
        w_out=mamba_w_out[0].astype(BF16),
        kv_norm_w=_row(kv_norm_w), w_k=w_k.astype(BF16), w_v=w_v.astype(BF16),
        attn_norm_w=_row(attn_norm_w[0]), w_q=w_q[0].astype(BF16),
        lam=jnp.stack([lambda_q1[0], lambda_k1[0], lambda_q2[0], lambda_k2[0]]).astype(F32),
        subln_w=_row(subln_w[0]), w_o=w_o[0].astype(BF16),
        ffn_norm_w=_row(ffn_norm_w[0]),
        ffn_wg=ffn_w_gate[0].astype(BF16), ffn_wu=ffn_w_up[0].astype(BF16),
        ffn_wd=ffn_w_down[0].astype(BF16),
        moe_norm_w=_row(moe_norm_w[0]),
        moe_wr=jnp.pad(moe_w_router[0].astype(F32), ((0, 0), (0, LANES - N_EXPERTS))),
        moe_wg=moe_w_gate[0].astype(BF16), moe_wu=moe_w_up[0].astype(BF16),
        moe_wd=moe_w_down[0].astype(BF16),
        final_norm_w=_row(final_norm_w),
    )
    bp, lp = x_prompt.shape[0], x_prompt.shape[1]
    bs, ls = x_sample.shape[0], x_sample.shape[1]
    past = cache_k.shape[1]

    conv0 = jnp.zeros((bp, CONV_W - 1, CONV_DIM), F32)
    ssm0 = jnp.zeros((bp, SSM_HEADS, SSM_HEAD_DIM, D_STATE), F32)
    y_p, conv_p, ssm_p, k_p, v_p = _trunk(
        x_prompt, jnp.arange(lp, dtype=jnp.int32), conv0, ssm0, None, None, W,
        ssd_chunk=128, ssd_valid=128, tm=512, tq=512)

    pos_s = jnp.tile(past + jnp.arange(ls, dtype=jnp.int32), bs)
    y_s, conv_s, ssm_s, k_s, v_s = _trunk(
        x_sample, pos_s, cache_conv[0], state_ssm[0], cache_k, cache_v, W,
        ssd_chunk=128, ssd_valid=ls, tm=bs * ls, tq=None)
    return (y_p, y_s, conv_p, ssm_p, k_p, v_p, conv_s, ssm_s, k_s, v_s)
```

```python
import functools
import math

import numpy as np
import jax
import jax.numpy as jnp
from jax import lax
from jax.experimental import pallas as pl
from jax.experimental.pallas import tpu as pltpu

F32 = jnp.float32
BF16 = jnp.bfloat16

EPS = 1e-5
D_MODEL = 1024
D_INNER = 2048
SSM_HEAD_DIM = 64
SSM_HEADS = 32
SSM_GROUPS = 4
D_STATE = 128
GROUP_W = D_INNER // SSM_GROUPS
CONV_W = 4
CONV_DIM = D_INNER + 2 * SSM_GROUPS * D_STATE
DT_PAD = 256
PROJ_W = D_INNER + CONV_DIM + DT_PAD
N_HEADS = 8
DK = 64
DV = 128
ROT_DIM = 16
ROPE_THETA = 500000.0
CHUNK = 64
CHUNK_SHIFT = 6
assert 1 << CHUNK_SHIFT == CHUNK
N_EXPERTS = 8
LANES = 128
NEG_BIG = -1e30
LOG2E = math.log2(math.e)
Q_COLS = 256
ONES_ROWS = 16
SCORE_LOOKAHEAD = 3
VMEM_LIMIT = 56 * 1024 * 1024


def _params(*sem):
    return pltpu.CompilerParams(dimension_semantics=sem, vmem_limit_bytes=VMEM_LIMIT)


def _sigmoid(x):
    return 1.0 / (1.0 + jnp.exp(-x))


def _rms_scale(x):
    return x * lax.rsqrt(jnp.mean(x * x, axis=-1, keepdims=True) + EPS)


def _split3(x):
    hi = x.astype(BF16)
    r1 = x - hi.astype(F32)
    mid = r1.astype(BF16)
    lo = (r1 - mid.astype(F32)).astype(BF16)
    return hi, mid, lo


def _dot(a, b):
    return jnp.dot(a, b, preferred_element_type=F32)


def _lane_tile(x, n):
    return jnp.concatenate([x] * n, axis=1) if n > 1 else x


def _dot_exact_rhs(x, m_bf16):
    hi, mid, lo = _split3(x)
    return _dot(hi, m_bf16) + _dot(mid, m_bf16) + _dot(lo, m_bf16)


def _dot_exact_lhs(m_bf16, x):
    hi, mid, lo = _split3(x)
    return _dot(m_bf16, hi) + _dot(m_bf16, mid) + _dot(m_bf16, lo)


def _rms_matmul_kernel(x_ref, nw_ref, w_ref, wdt_ref, o_ref, dt_ref, h_ref):
    @pl.when(pl.program_id(1) == 0)
    def _():
        h = (_rms_scale(x_ref[...]) * nw_ref[...]).astype(BF16)
        h_ref[...] = h
        dt_ref[...] = _dot(h, wdt_ref[...])

    o_ref[...] = _dot(h_ref[...], w_ref[...]).astype(o_ref.dtype)


def rms_matmul(x, nw, w, w_dt, *, tm, tn):
    t, d = x.shape
    n = w.shape[1]
    n_dt = w_dt.shape[1]
    return pl.pallas_call(
        _rms_matmul_kernel,
        grid=(t // tm, n // tn),
        in_specs=[pl.BlockSpec((tm, d), lambda i, j: (i, 0)),
                  pl.BlockSpec((1, d), lambda i, j: (0, 0)),
                  pl.BlockSpec((d, tn), lambda i, j: (0, j)),
                  pl.BlockSpec((d, n_dt), lambda i, j: (0, 0))],
        out_specs=[pl.BlockSpec((tm, tn), lambda i, j: (i, j)),
                   pl.BlockSpec((tm, n_dt), lambda i, j: (i, 0))],
        out_shape=[jax.ShapeDtypeStruct((t, n), BF16),
                   jax.ShapeDtypeStruct((t, n_dt), F32)],
        scratch_shapes=[pltpu.VMEM((tm, d), BF16)],
        compiler_params=_params("arbitrary", "arbitrary"),
        name="rms_matmul",
    )(x, nw, w, w_dt)


def _ssd_kernel(z_ref, xs_ref, b_ref, c_ref, dt_ref, cin_ref, sin_ref, cw_ref, cb_ref,
                dtb_ref, alog_ref, dexp_ref, gnw_ref, e_ref,
                y_ref, sout_ref, state_ref, cbuf_ref, *, c, cv):
    step = pl.program_id(1)

    @pl.when(step == 0)
    def _():
        state_ref[...] = sin_ref[0].T
        cbuf_ref[0:8, :] = cin_ref[0]

    cbuf_ref[8:8 + cv, 0:D_INNER] = xs_ref[0].astype(F32)
    cbuf_ref[8:8 + cv, D_INNER:D_INNER + GROUP_W] = b_ref[0].astype(F32)
    cbuf_ref[8:8 + cv, D_INNER + GROUP_W:CONV_DIM] = c_ref[0].astype(F32)
    if cv < c:
        cbuf_ref[8 + cv:8 + c, :] = jnp.zeros((c - cv, CONV_DIM), F32)

    conv = cb_ref[...] + cbuf_ref[5:5 + c, :] * cw_ref[0:1, :]
    for k in range(1, CONV_W):
        conv = conv + cbuf_ref[5 + k:5 + k + c, :] * cw_ref[k:k + 1, :]
    conv = conv * _sigmoid(conv)
    if cv == c:
        cbuf_ref[5:8, :] = cbuf_ref[5 + c:8 + c, :]

    xs = conv[:, 0:D_INNER]
    bm = conv[:, D_INNER:D_INNER + GROUP_W]
    cm = conv[:, D_INNER + GROUP_W:CONV_DIM]

    dt_in = dt_ref[0][:, 0:LANES] + dtb_ref[...]
    dt = jnp.maximum(dt_in, 0.0) + jnp.log1p(jnp.exp(-jnp.abs(dt_in)))
    if cv < c:
        dt = jnp.concatenate([dt, jnp.zeros((c - cv, LANES), F32)], axis=0)
    a = -jnp.exp(alog_ref[...])
    dta = dt * a

    row = lax.broadcasted_iota(jnp.int32, (c, c), 0)
    col = lax.broadcasted_iota(jnp.int32, (c, c), 1)
    tril = col <= row
    tri = jnp.where(tril, 1.0, 0.0).astype(BF16)
    cs = _dot_exact_lhs(tri, dta)
    cs_t = cs.T

    e = e_ref[...]
    dt_x = _dot_exact_rhs(dt, e)
    cs_x = _dot_exact_rhs(cs, e)
    cs_end_x = cs_x[c - 1:c, :]
    xdt = xs * dt_x
    decay_in = jnp.exp(cs_x)
    xdt_b = xdt.astype(BF16)
    xdt_end = (xdt * jnp.exp(cs_end_x - cs_x)).astype(BF16)
    state_decay = jnp.exp(cs_end_x)

    lane = lax.broadcasted_iota(jnp.int32, (c, LANES), 1)
    low_half = lane < SSM_HEAD_DIM

    y_groups = []
    for g in range(SSM_GROUPS):
        gs = slice(g * GROUP_W, (g + 1) * GROUP_W)
        ns = slice(g * D_STATE, (g + 1) * D_STATE)
        bg_t = bm[:, ns].T.astype(BF16)
        cg = cm[:, ns].astype(BF16)
        cb = _dot(cg, bg_t)
        sg = state_ref[:, gs]
        y_off = _dot(cg, sg.astype(BF16)) * decay_in[:, gs]
        pairs = []
        for j in range(GROUP_W // LANES):
            h0 = g * (GROUP_W // SSM_HEAD_DIM) + 2 * j
            xp = xdt_b[:, g * GROUP_W + j * LANES:g * GROUP_W + (j + 1) * LANES]
            ys = []
            for h in (h0, h0 + 1):
                seg = cs[:, h:h + 1] - cs_t[h:h + 1, :]
                m = (cb * jnp.where(tril, jnp.exp(seg), 0.0)).astype(BF16)
                ys.append(_dot(m, xp))
            pairs.append(jnp.where(low_half, ys[0], ys[1]))
        y_groups.append(jnp.concatenate(pairs, axis=1) + y_off)
        state_ref[:, gs] = sg * state_decay[:, gs] + _dot(bg_t, xdt_end[:, gs])

    y = jnp.concatenate(y_groups, axis=1) + dexp_ref[...] * xs
    if cv < c:
        y = y[0:cv]
    zv = z_ref[0].astype(F32)
    y = y * (zv * _sigmoid(zv))
    outs = []
    for g in range(SSM_GROUPS):
        gs = slice(g * GROUP_W, (g + 1) * GROUP_W)
        outs.append(_rms_scale(y[:, gs]))
    y_ref[0] = (jnp.concatenate(outs, axis=1) * gnw_ref[...]).astype(y_ref.dtype)

    @pl.when(step == pl.num_programs(1) - 1)
    def _():
        sout_ref[0] = state_ref[...].T


def ssd(proj, dt_raw, conv_in, state_in, cw, cb, dtb, alog, dexp, gnw, *, c, cv):
    nb, length, _ = proj.shape
    steps = length // cv
    head_of_channel = np.arange(D_INNER) // SSM_HEAD_DIM
    expand = jnp.asarray(np.arange(LANES)[:, None] == head_of_channel[None, :], dtype=BF16)
    const = lambda b, l: (0, 0)
    kern = functools.partial(_ssd_kernel, c=c, cv=cv)
    return pl.pallas_call(
        kern,
        grid=(nb, steps),
        in_specs=[
            pl.BlockSpec((1, cv, D_INNER), lambda b, l: (b, l, 0)),
            pl.BlockSpec((1, cv, D_INNER), lambda b, l: (b, l, 1)),
            pl.BlockSpec((1, cv, GROUP_W), lambda b, l: (b, l, 2 * D_INNER // GROUP_W)),
            pl.BlockSpec((1, cv, GROUP_W), lambda b, l: (b, l, 2 * D_INNER // GROUP_W + 1)),
            pl.BlockSpec((1, cv, DT_PAD), lambda b, l: (b, l, 0)),
            pl.BlockSpec((1, 8, CONV_DIM), lambda b, l: (b, 0, 0)),
            pl.BlockSpec((1, D_INNER, D_STATE), lambda b, l: (b, 0, 0)),
            pl.BlockSpec((CONV_W, CONV_DIM), const),
            pl.BlockSpec((1, CONV_DIM), const),
            pl.BlockSpec((1, LANES), const),
            pl.BlockSpec((1, LANES), const),
            pl.BlockSpec((1, D_INNER), const),
            pl.BlockSpec((1, D_INNER), const),
            pl.BlockSpec((LANES, D_INNER), const),
        ],
        out_specs=[
            pl.BlockSpec((1, cv, D_INNER), lambda b, l: (b, l, 0)),
            pl.BlockSpec((1, D_INNER, D_STATE), lambda b, l: (b, 0, 0)),
        ],
        out_shape=[
            jax.ShapeDtypeStruct((nb, length, D_INNER), BF16),
            jax.ShapeDtypeStruct((nb, D_INNER, D_STATE), F32),
        ],
        scratch_shapes=[pltpu.VMEM((D_STATE, D_INNER), F32),
                        pltpu.VMEM((8 + c, CONV_DIM), F32)],
        compiler_params=_params("arbitrary", "arbitrary"),
        name="ssd",
    )(proj, proj, proj, proj, dt_raw, conv_in, state_in, cw, cb, dtb, alog, dexp, gnw, expand)


def _mm_res_kernel(a_ref, w_ref, r_ref, o_ref):
    o_ref[...] = r_ref[...] + _dot(a_ref[...], w_ref[...])


def matmul_residual(a, w, res, *, tm):
    t, k = a.shape
    n = w.shape[1]
    return pl.pallas_call(
        _mm_res_kernel,
        grid=(t // tm,),
        in_specs=[pl.BlockSpec((tm, k), lambda i: (i, 0)),
                  pl.BlockSpec((k, n), lambda i: (0, 0)),
                  pl.BlockSpec((tm, n), lambda i: (i, 0))],
        out_specs=pl.BlockSpec((tm, n), lambda i: (i, 0)),
        out_shape=jax.ShapeDtypeStruct((t, n), F32),
        compiler_params=_params("arbitrary"),
        name="matmul_residual",
    )(a, w, res)


def _ffn_kernel(x_ref, nw_ref, wg_ref, wu_ref, wd_ref, o_ref, h_ref, acc_ref):
    f = pl.program_id(1)

    @pl.when(f == 0)
    def _():
        x = x_ref[...]
        h_ref[...] = (_rms_scale(x) * nw_ref[...]).astype(BF16)
        acc_ref[...] = x

    h = h_ref[...]
    g = _dot(h, wg_ref[...])
    u = _dot(h, wu_ref[...])
    act = (g * _sigmoid(g) * u).astype(BF16)
    acc_ref[...] += _dot(act, wd_ref[...])

    @pl.when(f == pl.num_programs(1) - 1)
    def _():
        o_ref[...] = acc_ref[...]


def swiglu_ffn(x, nw, wg, wu, wd, *, tm, tf):
    t, d = x.shape
    ff = wg.shape[1]
    return pl.pallas_call(
        _ffn_kernel,
        grid=(t // tm, ff // tf),
        in_specs=[pl.BlockSpec((tm, d), lambda i, f: (i, 0)),
                  pl.BlockSpec((1, d), lambda i, f: (0, 0)),
                  pl.BlockSpec((d, tf), lambda i, f: (0, f)),
                  pl.BlockSpec((d, tf), lambda i, f: (0, f)),
                  pl.BlockSpec((tf, d), lambda i, f: (f, 0))],
        out_specs=pl.BlockSpec((tm, d), lambda i, f: (i, 0)),
        out_shape=jax.ShapeDtypeStruct((t, d), F32),
        scratch_shapes=[pltpu.VMEM((tm, d), BF16), pltpu.VMEM((tm, d), F32)],
        compiler_params=_params("arbitrary", "arbitrary"),
        name="swiglu_ffn",
    )(x, nw, wg, wu, wd)


def _qkv_kernel(x_ref, kvw_ref, aw_ref, wk_ref, wv_ref, wq_ref, cos_ref, sa_ref, sb_ref,
                k_ref, v_ref, *extra_refs, transposed):
    xn = _rms_scale(x_ref[...])
    hkv = (xn * kvw_ref[...]).astype(BF16)
    hq = (xn * aw_ref[...]).astype(BF16)
    cos, sa, sb = cos_ref[...], sa_ref[...], sb_ref[...]

    def rope(t):
        blocks = []
        for j in range(t.shape[1] // LANES):
            tb = t[:, j * LANES:(j + 1) * LANES]
            blocks.append(tb * cos + pltpu.roll(tb, LANES - ROT_DIM // 2, 1) * sa
                          + pltpu.roll(tb, ROT_DIM // 2, 1) * sb)
        return jnp.concatenate(blocks, axis=1)

    k = rope(_dot(hkv, wk_ref[...]))
    v = _dot(hkv, wv_ref[...])
    q = rope(_dot(hq, wq_ref[...]))
    for h in range(N_HEADS):
        k_ref[:, h, :] = k[:, h * LANES:(h + 1) * LANES]
        v_ref[:, h, :] = v[:, h * LANES:(h + 1) * LANES]
    if transposed:
        kb_ref, vt_ref, qt_ref = extra_refs
        kb_ref[...] = k.astype(BF16)
        vt_ref[0] = v.T.astype(BF16)
        qt_ref[0] = (q * (DK ** -0.5 * LOG2E)).T.astype(BF16)
    else:
        (q_ref,) = extra_refs
        q_ref[...] = (q * (DK ** -0.5)).astype(BF16)


def _rope_tables(pos):
    half = ROT_DIM // 2
    inv_freq = ROPE_THETA ** (-jnp.arange(half, dtype=F32) / half)
    ang = pos.astype(F32)[:, None] * inv_freq[None, :]
    cos, sin = jnp.cos(ang), jnp.sin(ang)
    n = pos.shape[0]
    ones = jnp.ones((n, DK - ROT_DIM), F32)
    zeros_h = jnp.zeros((n, half), F32)
    zeros_r = jnp.zeros((n, DK - ROT_DIM), F32)
    cos_t = jnp.concatenate([cos, cos, ones], axis=1)
    sa_t = jnp.concatenate([-sin, zeros_h, zeros_r], axis=1)
    sb_t = jnp.concatenate([zeros_h, sin, zeros_r], axis=1)
    rep = lambda t: jnp.tile(t, (1, LANES // DK))
    return rep(cos_t), rep(sa_t), rep(sb_t)


def qkv_proj(x, kvw, aw, wk, wv, wq, tables, *, tm, streams, transposed):
    t, d = x.shape
    period = tables[0].shape[0] // tm
    row = lambda i: (i, 0)
    const = lambda i: (0, 0)
    tab = lambda i: (i % period, 0)
    heads_out = jax.ShapeDtypeStruct((t, N_HEADS, LANES), F32)
    heads_spec = pl.BlockSpec((tm, N_HEADS, LANES), lambda i: (i, 0, 0))
    out_specs = [heads_spec, heads_spec, pl.BlockSpec((tm, d), row)]
    out_shape = [heads_out, heads_out, jax.ShapeDtypeStruct((t, d), BF16)]
    if transposed:
        per = t // streams // tm
        tr_spec = pl.BlockSpec((1, d, tm), lambda i: (i // per, 0, i % per))
        tr_out = jax.ShapeDtypeStruct((streams, d, t // streams), BF16)
        out_specs += [tr_spec, tr_spec]
        out_shape += [tr_out, tr_out]
    return pl.pallas_call(
        functools.partial(_qkv_kernel, transposed=transposed),
        grid=(t // tm,),
        in_specs=[pl.BlockSpec((tm, d), row),
                  pl.BlockSpec((1, d), const), pl.BlockSpec((1, d), const),
                  pl.BlockSpec((d, d), const), pl.BlockSpec((d, d), const), pl.BlockSpec((d, d), const),
                  pl.BlockSpec((tm, LANES), tab), pl.BlockSpec((tm, LANES), tab),
                  pl.BlockSpec((tm, LANES), tab)],
        out_specs=out_specs,
        out_shape=out_shape,
        compiler_params=_params("arbitrary"),
        name="qkv_proj",
    )(x, kvw, aw, wk, wv, wq, *tables)


def _lambda_value(lp_ref, lambda_init):
    lp = lp_ref[...]
    s1 = jnp.sum(lp[0:1] * lp[1:2], axis=-1, keepdims=True)
    s2 = jnp.sum(lp[2:3] * lp[3:4], axis=-1, keepdims=True)
    return jnp.exp(s1) - jnp.exp(s2) + lambda_init


def _diff_finish(o1, o2, lam, subw, lambda_init):
    o = o1 - lam * o2
    return _rms_scale(o) * subw * (1.0 - lambda_init)


def _attn_kernel(qt_ref, kt_ref, q_ref, k_ref, v_ref, lp_ref, subw_ref, o_ref,
                 qs_ref, *state_refs, tq, tk, lambda_init):
    p = pl.program_id(1)
    qi = qt_ref[p]
    ki = kt_ref[p]
    n_qc = 2 * tq // Q_COLS
    m_refs, acc_refs = state_refs[:n_qc], state_refs[n_qc:]

    @pl.when(ki == 0)
    def _():
        for m_ref, acc_ref in zip(m_refs, acc_refs):
            m_ref[...] = jnp.full(m_ref.shape, NEG_BIG, F32)
            acc_ref[...] = jnp.zeros(acc_ref.shape, F32)
        first_sub = lax.broadcasted_iota(jnp.int32, (LANES, tq), 0) < DK
        for h in range(N_HEADS):
            qh = q_ref[0, h * LANES:(h + 1) * LANES, :]
            zero = jnp.zeros_like(qh)
            qs_ref[h, :, 0:tq] = jnp.where(first_sub, qh, zero)
            qs_ref[h, :, tq:2 * tq] = jnp.where(first_sub, zero, qh)

    ones = jnp.ones((ONES_ROWS, tk), BF16)

    def sweep(masked):
        if masked:
            k_chunk = (ki * tk + lax.broadcasted_iota(jnp.int32, (tk, Q_COLS), 0)) >> CHUNK_SHIFT
            q_lane = lax.broadcasted_iota(jnp.int32, (tk, Q_COLS), 1)

        def scores(h, c):
            kh = k_ref[0, :, h * LANES:(h + 1) * LANES]
            s = _dot(kh, qs_ref[h, :, c * Q_COLS:(c + 1) * Q_COLS])
            if masked:
                q_chunk = (qi * tq + (c * Q_COLS) % tq + q_lane) >> CHUNK_SHIFT
                s = jnp.where(k_chunk <= q_chunk, s, NEG_BIG)
            return s

        groups = [(h, c) for h in range(N_HEADS) for c in range(n_qc)]
        pending = [scores(*g) for g in groups[:SCORE_LOOKAHEAD]]
        for i, (h, c) in enumerate(groups):
            s = pending.pop(0)
            if i + SCORE_LOOKAHEAD < len(groups):
                pending.append(scores(*groups[i + SCORE_LOOKAHEAD]))
            vt = jnp.concatenate([v_ref[0, h * LANES:(h + 1) * LANES, :], ones], axis=0)
            m_prev = m_refs[c][h:h + 1, :]
            m_new = jnp.maximum(m_prev, jnp.max(s, axis=0, keepdims=True))
            alpha = jnp.exp2(m_prev - m_new)
            pr = jnp.exp2(s - m_new).astype(BF16)
            acc_refs[c][h] = acc_refs[c][h] * alpha + _dot(vt, pr)
            m_refs[c][h:h + 1, :] = m_new

    @pl.when(ki < qi)
    def _():
        sweep(False)

    @pl.when(ki == qi)
    def _():
        sweep(True)
        lam = _lambda_value(lp_ref, lambda_init)
        subw = subw_ref[...]

        for h in range(N_HEADS):
            hs = slice(h * LANES, (h + 1) * LANES)
            for r in range(tq // Q_COLS):
                a1 = acc_refs[r][h]
                a2 = acc_refs[r + tq // Q_COLS][h]
                o_t = a1[0:DV] / a1[DV:DV + 1] - lam * (a2[0:DV] / a2[DV:DV + 1])
                res = _rms_scale(o_t.T) * subw * (1.0 - lambda_init)
                o_ref[0, r * Q_COLS:(r + 1) * Q_COLS, hs] = res.astype(o_ref.dtype)


def diff_attention(q_t, k, v_t, lam_params, subw, *, tq, lambda_init):
    nb, d, length = q_t.shape
    tk = tq
    nq = length // tq
    qt = np.concatenate([np.full(i + 1, i) for i in range(nq)]).astype(np.int32)
    kt = np.concatenate([np.arange(i + 1) for i in range(nq)]).astype(np.int32)
    kern = functools.partial(_attn_kernel, tq=tq, tk=tk, lambda_init=lambda_init)
    n_qc = 2 * tq // Q_COLS
    grid_spec = pltpu.PrefetchScalarGridSpec(
        num_scalar_prefetch=2,
        grid=(nb, len(qt)),
        in_specs=[pl.BlockSpec((1, d, tq), lambda b, p, qt_r, kt_r: (b, 0, qt_r[p])),
                  pl.BlockSpec((1, tk, d), lambda b, p, qt_r, kt_r: (b, kt_r[p], 0)),
                  pl.BlockSpec((1, d, tk), lambda b, p, qt_r, kt_r: (b, 0, kt_r[p])),
                  pl.BlockSpec((4, DK), lambda b, p, qt_r, kt_r: (0, 0)),
                  pl.BlockSpec((1, DV), lambda b, p, qt_r, kt_r: (0, 0))],
        out_specs=pl.BlockSpec((1, tq, d), lambda b, p, qt_r, kt_r: (b, qt_r[p], 0)),
        scratch_shapes=([pltpu.VMEM((N_HEADS, LANES, 2 * tq), BF16)]
                        + [pltpu.VMEM((N_HEADS, Q_COLS), F32)] * n_qc
                        + [pltpu.VMEM((N_HEADS, DV + ONES_ROWS, Q_COLS), F32)] * n_qc),
    )
    return pl.pallas_call(
        kern,
        grid_spec=grid_spec,
        out_shape=jax.ShapeDtypeStruct((nb, length, d), BF16),
        compiler_params=_params("arbitrary", "arbitrary"),
        name="diff_attention",
    )(jnp.asarray(qt), jnp.asarray(kt), q_t, k, v_t, lam_params, subw)


def _attn_cached_kernel(q_ref, ck_ref, cv_ref, kn_ref, vn_ref, lp_ref, subw_ref, o_ref,
                        *, nq, lambda_init):
    q = q_ref[0]
    lane = lax.broadcasted_iota(jnp.int32, (nq, LANES), 1)
    zero = jnp.zeros_like(q)
    qs = jnp.concatenate([jnp.where(lane < DK, q, zero), jnp.where(lane < DK, zero, q)], axis=0)
    nt = (((1,), (1,)), ((), ()))
    pad = jnp.zeros((LANES - nq, LANES), BF16)
    kc = ck_ref[0].astype(BF16)
    vc = cv_ref[0].astype(BF16)
    kn = jnp.concatenate([kn_ref[0].astype(BF16), pad], axis=0)
    vn = jnp.concatenate([vn_ref[0].astype(BF16), pad], axis=0)
    s_c = lax.dot_general(qs, kc, nt, preferred_element_type=F32)
    s_n = lax.dot_general(qs, kn, nt, preferred_element_type=F32)
    col = lax.broadcasted_iota(jnp.int32, s_n.shape, 1)
    s_n = jnp.where(col < nq, s_n, NEG_BIG)
    m = jnp.maximum(jnp.max(s_c, axis=-1, keepdims=True), jnp.max(s_n, axis=-1, keepdims=True))
    p_c = jnp.exp(s_c - m)
    p_n = jnp.exp(s_n - m)
    denom = jnp.sum(p_c, axis=-1, keepdims=True) + jnp.sum(p_n, axis=-1, keepdims=True)
    o = (_dot(p_c.astype(BF16), vc) + _dot(p_n.astype(BF16), vn)) / denom
    lam = _lambda_value(lp_ref, lambda_init)
    res = _diff_finish(o[0:nq], o[nq:2 * nq], lam, subw_ref[...], lambda_init)
    o_ref[0] = res.astype(o_ref.dtype)


def diff_attention_cached(q, cache_k, cache_v, k_new, v_new, lam_params, subw, *, lambda_init):
    nb, nq, d = q.shape
    past = cache_k.shape[1]
    kern = functools.partial(_attn_cached_kernel, nq=nq, lambda_init=lambda_init)
    head = lambda b, h: (b, 0, h)
    const = lambda b, h: (0, 0)
    return pl.pallas_call(
        kern,
        grid=(nb, N_HEADS),
        in_specs=[pl.BlockSpec((1, nq, LANES), head),
                  pl.BlockSpec((1, past, LANES), head),
                  pl.BlockSpec((1, past, LANES), head),
                  pl.BlockSpec((1, nq, LANES), head),
                  pl.BlockSpec((1, nq, LANES), head),
                  pl.BlockSpec((4, DK), const),
                  pl.BlockSpec((1, DV), const)],
        out_specs=pl.BlockSpec((1, nq, LANES), head),
        out_shape=jax.ShapeDtypeStruct((nb, nq, d), BF16),
        compiler_params=_params("arbitrary", "arbitrary"),
        name="diff_attention_cached",
    )(q, cache_k, cache_v, k_new, v_new, lam_params, subw)


def _route(logits):
    lane = lax.broadcasted_iota(jnp.int32, logits.shape, 1)
    valid = lane < N_EXPERTS
    lg = jnp.where(valid, logits, NEG_BIG)
    m1 = jnp.max(lg, axis=-1, keepdims=True)
    lane_f = lane.astype(F32)
    i1 = jnp.min(jnp.where(lg == m1, lane_f, float(LANES)), axis=-1, keepdims=True)
    lg2 = jnp.where(lane_f == i1, NEG_BIG, lg)
    m2 = jnp.max(lg2, axis=-1, keepdims=True)
    i2 = jnp.min(jnp.where(lg2 == m2, lane_f, float(LANES)), axis=-1, keepdims=True)
    e2 = jnp.exp(m2 - m1)
    g1 = 1.0 / (1.0 + e2)
    g2 = e2 / (1.0 + e2)
    return jnp.where(lane_f == i1, g1, 0.0) + jnp.where(lane_f == i2, g2, 0.0)


def _moe_route_kernel(x_ref, nw_ref, wr_ref, h_ref, comb_ref, rk_ref, rkt_ref, cnt_ref, *, rt):
    t = x_ref.shape[0]
    hf = _rms_scale(x_ref[...]) * nw_ref[...]
    h_ref[...] = hf.astype(BF16)
    comb = _route(_dot_exact_rhs_general(hf, wr_ref[...]))
    comb_ref[...] = comb
    sel = jnp.where(comb > 0.0, 1.0, 0.0)
    row = lax.broadcasted_iota(jnp.int32, (rt, rt), 0)
    col = lax.broadcasted_iota(jnp.int32, (rt, rt), 1)
    strict = jnp.where(col < row, 1.0, 0.0).astype(BF16)
    carry = jnp.zeros((1, LANES), F32)
    ranks = []
    for s in range(t // rt):
        sel_s = sel[s * rt:(s + 1) * rt]
        rank_s = _dot(strict, sel_s.astype(BF16)) + carry
        ranks.append(jnp.where(sel_s > 0.0, rank_s, -1.0))
        carry = carry + jnp.sum(sel_s, axis=0, keepdims=True)
    rk = jnp.concatenate(ranks, axis=0) if len(ranks) > 1 else ranks[0]
    rk_ref[...] = rk.astype(jnp.int32)
    rkt_ref[0] = rk.T.astype(jnp.int32)
    cnt_ref[0] = jnp.broadcast_to(carry, (8, LANES)).astype(jnp.int32)


def moe_route(x, nw, wr, *, tb, rt):
    t, d = x.shape
    nb = t // tb
    row = lambda b: (b, 0)
    const = lambda b: (0, 0)
    return pl.pallas_call(
        functools.partial(_moe_route_kernel, rt=rt),
        grid=(nb,),
        in_specs=[pl.BlockSpec((tb, d), row), pl.BlockSpec((1, d), const),
                  pl.BlockSpec((d, LANES), const)],
        out_specs=[pl.BlockSpec((tb, d), row), pl.BlockSpec((tb, LANES), row),
                   pl.BlockSpec((tb, LANES), row),
                   pl.BlockSpec((1, LANES, tb), lambda b: (b, 0, 0)),
                   pl.BlockSpec((1, 8, LANES), lambda b: (b, 0, 0))],
        out_shape=[jax.ShapeDtypeStruct((t, d), BF16),
                   jax.ShapeDtypeStruct((t, LANES), F32),
                   jax.ShapeDtypeStruct((t, LANES), jnp.int32),
                   jax.ShapeDtypeStruct((nb, LANES, tb), jnp.int32),
                   jax.ShapeDtypeStruct((nb, 8, LANES), jnp.int32)],
        compiler_params=_params("arbitrary"),
        name="moe_route",
    )(x, nw, wr)


def _moe_expert_kernel(cnt_ref, x_ref, h_ref, comb_ref, rk_ref, rkt_ref, wg_ref, wu_ref, wd_ref,
                       fw_ref, o_ref, rkc_ref, xg_ref, gs_ref, yacc_ref, *, rt):
    b = pl.program_id(0)
    e = pl.program_id(1)
    f = pl.program_id(2)
    t = x_ref.shape[0]
    n_rows = cnt_ref[b * N_EXPERTS + e]
    n_tiles = (n_rows + (rt - 1)) // rt

    @pl.when(jnp.logical_and(e == 0, f == 0))
    def _():
        o_ref[...] = x_ref[...]

    def tile(j):
        return pl.ds(pl.multiple_of(j * rt, rt), rt)

    @pl.when(f == 0)
    def _():
        lane = lax.broadcasted_iota(jnp.int32, (t, LANES), 1)
        pick = lane == e
        rkc = jnp.sum(jnp.where(pick, rk_ref[...], 0).astype(F32), axis=-1, keepdims=True)
        rkc_ref[...] = jnp.broadcast_to(rkc, (t, LANES)).astype(jnp.int32)
        gate = jnp.sum(jnp.where(pick, comb_ref[...], 0.0), axis=-1, keepdims=True)
        g_hi, g_mid, g_lo = _split3(jnp.broadcast_to(gate, (t, LANES)))
        g3 = jnp.where(lane == 0, g_hi.astype(F32),
                       jnp.where(lane == 1, g_mid.astype(F32),
                                 jnp.where(lane == 2, g_lo.astype(F32), 0.0))).astype(BF16)
        rk_row = rkt_ref[0, pl.ds(e, 1), :]

        def gather(j, carry):
            rows = lax.broadcasted_iota(jnp.int32, (rt, t), 0) + j * rt
            p = jnp.where(rk_row == rows, 1.0, 0.0).astype(BF16)
            xg_ref[tile(j), :] = _dot(p, h_ref[...]).astype(BF16)
            gate_rows = jnp.sum(_dot(p, g3), axis=-1, keepdims=True)
            gs_ref[tile(j), :] = jnp.broadcast_to(gate_rows, (rt, LANES))
            yacc_ref[tile(j), :] = jnp.zeros((rt, yacc_ref.shape[1]), F32)
            return carry

        lax.fori_loop(0, n_tiles, gather, 0)

    def expert(j, carry):
        xg = xg_ref[tile(j), :]
        g = _dot(xg, wg_ref[...])
        u = _dot(xg, wu_ref[...])
        gates = _lane_tile(gs_ref[tile(j), :], g.shape[1] // LANES)
        act = (g * _sigmoid(g) * u * gates).astype(BF16)
        yacc_ref[tile(j), :] += _dot(act, wd_ref[...])
        return carry

    lax.fori_loop(0, n_tiles, expert, 0)

    @pl.when(f == pl.num_programs(2) - 1)
    def _():
        def scatter(j, carry):
            cols = lax.broadcasted_iota(jnp.int32, (t, rt), 1) + j * rt
            s = jnp.where(_lane_tile(rkc_ref[...], rt // LANES) == cols, 1.0, 0.0).astype(BF16)
            o_ref[...] += _dot(s, yacc_ref[tile(j), :].astype(BF16))
            return carry

        lax.fori_loop(0, n_tiles, scatter, 0)

    @pl.when(jnp.logical_and(e == pl.num_programs(1) - 1, f == pl.num_programs(2) - 1))
    def _():
        o_ref[...] = _rms_scale(o_ref[...]) * fw_ref[...]


def _dot_exact_rhs_general(x, w):
    xh, xm, xl = _split3(x)
    wh, wm, wl = _split3(w)
    return (_dot(xh, wh) + (_dot(xh, wm) + _dot(xm, wh))
            + (_dot(xh, wl) + _dot(xm, wm) + _dot(xl, wh)))


def moe_ffn_final(x, nw, wr, wg, wu, wd, fw, *, tb, rt, tf):
    t, d = x.shape
    ne, _, ff = wg.shape
    nb = t // tb
    h, comb, rk, rkt, cnt = moe_route(x, nw, wr, tb=tb, rt=rt)
    counts = cnt[:, 0, :ne].reshape(nb * ne)
    once = pl.Buffered(1)
    blk = lambda b, e, f, c: (b, 0)
    const = lambda b, e, f, c: (0, 0)
    grid_spec = pltpu.PrefetchScalarGridSpec(
        num_scalar_prefetch=1,
        grid=(nb, ne, ff // tf),
        in_specs=[pl.BlockSpec((tb, d), blk, pipeline_mode=once),
                  pl.BlockSpec((tb, d), blk, pipeline_mode=once),
                  pl.BlockSpec((tb, LANES), blk, pipeline_mode=once),
                  pl.BlockSpec((tb, LANES), blk, pipeline_mode=once),
                  pl.BlockSpec((1, LANES, tb), lambda b, e, f, c: (b, 0, 0), pipeline_mode=once),
                  pl.BlockSpec((None, d, tf), lambda b, e, f, c: (e, 0, f)),
                  pl.BlockSpec((None, d, tf), lambda b, e, f, c: (e, 0, f)),
                  pl.BlockSpec((None, tf, d), lambda b, e, f, c: (e, f, 0)),
                  pl.BlockSpec((1, d), const)],
        out_specs=pl.BlockSpec((tb, d), blk),
        scratch_shapes=[pltpu.VMEM((tb, LANES), jnp.int32),
                        pltpu.VMEM((tb, d), BF16),
                        pltpu.VMEM((tb, LANES), F32),
                        pltpu.VMEM((tb, d), F32)],
    )
    return pl.pallas_call(
        functools.partial(_moe_expert_kernel, rt=rt),
        grid_spec=grid_spec,
        out_shape=jax.ShapeDtypeStruct((t, d), F32),
        compiler_params=_params("arbitrary", "arbitrary", "arbitrary"),
        name="moe_experts",
    )(counts, x, h, comb, rk, rkt, wg, wu, wd, fw)


def _row(v):
    return v.reshape(1, -1).astype(F32)


def _pad_lanes(v, width):
    v = _row(v)
    return jnp.pad(v, ((0, 0), (0, width - v.shape[1])))


def _trunk(x, pos, conv_in, state_in, past_k, past_v, W, *, ssd_chunk, ssd_valid, tm, tq):
    nb, length, d = x.shape
    t = nb * length
    x0 = x.reshape(t, d)

    proj, dt_raw = rms_matmul(x0, W["mamba_norm_w"], W["w_in"], W["w_dt"], tm=min(t, 1024), tn=1024)
    proj3 = proj.reshape(nb, length, D_INNER + CONV_DIM)
    new_conv = proj3[:, length - (CONV_W - 1):, D_INNER:D_INNER + CONV_DIM].astype(F32)
    conv8 = jnp.pad(conv_in, ((0, 0), (8 - (CONV_W - 1), 0), (0, 0)))
    y, state_out = ssd(proj3, dt_raw.reshape(nb, length, DT_PAD), conv8,
                       state_in.reshape(nb, D_INNER, D_STATE),
                       W["conv_w"], W["conv_b"], W["dt_bias"], W["a_log"], W["d_exp"], W["gn_w"],
                       c=ssd_chunk, cv=ssd_valid)
    new_ssm = state_out.reshape(nb, SSM_HEADS, SSM_HEAD_DIM, D_STATE)
    x1 = matmul_residual(y.reshape(t, D_INNER), W["w_out"], x0, tm=tm)

    x2 = swiglu_ffn(x1, W["ffn_norm_w"], W["ffn_wg"], W["ffn_wu"], W["ffn_wd"], tm=tm, tf=1408)

    tables = _rope_tables(pos)
    proj_args = (x2, W["kv_norm_w"], W["attn_norm_w"], W["w_k"], W["w_v"], W["w_q"], tables)
    lambda_init = 0.8 - 0.6 * math.exp(-0.3 * 1)
    if past_k is None:
        k, v, kb, v_t, q_t = qkv_proj(*proj_args, tm=tm, streams=nb, transposed=True)
        o = diff_attention(q_t, kb.reshape(nb, length, d), v_t,
                           W["lam"], W["subln_w"], tq=tq, lambda_init=lambda_init)
    else:
        k, v, q = qkv_proj(*proj_args, tm=tm, streams=nb, transposed=False)
        q3 = q.reshape(nb, length, d)
        o = diff_attention_cached(q3, past_k.reshape(nb, -1, d), past_v.reshape(nb, -1, d),
                                  k.reshape(nb, length, d), v.reshape(nb, length, d),
                                  W["lam"], W["subln_w"], lambda_init=lambda_init)
    x3 = matmul_residual(o.reshape(t, d), W["w_o"], x2, tm=tm)

    yout = moe_ffn_final(x3, W["moe_norm_w"], W["moe_wr"], W["moe_wg"], W["moe_wu"], W["moe_wd"],
                         W["final_norm_w"], tb=min(t, 2048), rt=256, tf=512)
    return (yout.reshape(nb, length, d), new_conv[None], new_ssm[None],
            k.reshape(nb, length, N_HEADS, 2 * DK), v.reshape(nb, length, N_HEADS, DV))


def kernel(x_prompt, x_sample, cache_conv, state_ssm, cache_k, cache_v, mamba_norm_w, mamba_w_in, mamba_conv_w, mamba_conv_b, mamba_dt_bias, mamba_a_log, mamba_d, mamba_gn_w, mamba_w_out, kv_norm_w, w_k, w_v, attn_norm_w, w_q, lambda_q1, lambda_k1, lambda_q2, lambda_k2, subln_w, w_o, ffn_norm_w, ffn_w_gate, ffn_w_up, ffn_w_down, moe_norm_w, moe_w_router, moe_w_gate, moe_w_up, moe_w_down, final_norm_w):
    w_in = mamba_w_in[0]
    n_dt = w_in.shape[1] - D_INNER - CONV_DIM
    W = dict(
        mamba_norm_w=_row(mamba_norm_w[0]),
        w_in=w_in[:, :D_INNER + CONV_DIM].astype(BF16),
        w_dt=jnp.pad(w_in[:, D_INNER + CONV_DIM:], ((0, 0), (0, DT_PAD - n_dt))).astype(BF16),
        conv_w=mamba_conv_w[0].astype(F32),
        conv_b=_row(mamba_conv_b[0]),
        dt_bias=_pad_lanes(mamba_dt_bias[0], LANES),
        a_log=_pad_lanes(mamba_a_log[0], LANES),
        d_exp=_row(jnp.repeat(mamba_d[0], SSM_HEAD_DIM)),
        gn_w=_row(mamba_gn_w[0]),
        w_out=mamba_w_out[0].astype(BF16),
        kv_norm_w=_row(kv_norm_w), w_k=w_k.astype(BF16), w_v=w_v.astype(BF16),
        attn_norm_w=_row(attn_norm_w[0]), w_q=w_q[0].astype(BF16),
        lam=jnp.stack([lambda_q1[0], lambda_k1[0], lambda_q2[0], lambda_k2[0]]).astype(F32),
        subln_w=_row(subln_w[0]), w_o=w_o[0].astype(BF16),
        ffn_norm_w=_row(ffn_norm_w[0]),
        ffn_wg=ffn_w_gate[0].astype(BF16), ffn_wu=ffn_w_up[0].astype(BF16),
        ffn_wd=ffn_w_down[0].astype(BF16),
        moe_norm_w=_row(moe_norm_w[0]),
        moe_wr=jnp.pad(moe_w_router[0].astype(F32), ((0, 0), (0, LANES - N_EXPERTS))),
        moe_wg=moe_w_gate[0].astype(BF16), moe_wu=moe_w_up[0].astype(BF16),
        moe_wd=moe_w_down[0].astype(BF16),
        final_norm_w=_row(final_norm_w),
    )
    bp, lp = x_prompt.shape[0], x_prompt.shape[1]
    bs, ls = x_sample.shape[0], x_sample.shape[1]
    past = cache_k.shape[1]

    conv0 = jnp.zeros((bp, CONV_W - 1, CONV_DIM), F32)
    ssm0 = jnp.zeros((bp, SSM_HEADS, SSM_HEAD_DIM, D_STATE), F32)
    y_p, conv_p, ssm_p, k_p, v_p = _trunk(
        x_prompt, jnp.arange(lp, dtype=jnp.int32), conv0, ssm0, None, None, W,
        ssd_chunk=128, ssd_valid=128, tm=512, tq=512)

    pos_s = jnp.tile(past + jnp.arange(ls, dtype=jnp.int32), bs)
    y_s, conv_s, ssm_s, k_s, v_s = _trunk(
        x_sample, pos_s, cache_conv[0], state_ssm[0], cache_k, cache_v, W,
        ssd_chunk=128, ssd_valid=ls, tm=bs * ls, tq=None)
    return (y_p, y_s, conv_p, ssm_p, k_p, v_p, conv_s, ssm_s, k_s, v_s)
```

```python
import functools
import math

import numpy as np
import jax
import jax.numpy as jnp
from jax import lax
from jax.experimental import pallas as pl
from jax.experimental.pallas import tpu as pltpu

F32 = jnp.float32
BF16 = jnp.bfloat16

EPS = 1e-5
D_MODEL = 1024
D_INNER = 2048
SSM_HEAD_DIM = 64
SSM_HEADS = 32
SSM_GROUPS = 4
D_STATE = 128
GROUP_W = D_INNER // SSM_GROUPS
CONV_W = 4
CONV_DIM = D_INNER + 2 * SSM_GROUPS * D_STATE
DT_PAD = 256
PROJ_W = D_INNER + CONV_DIM + DT_PAD
N_HEADS = 8
DK = 64
DV = 128
ROT_DIM = 16
ROPE_THETA = 500000.0
CHUNK = 64
CHUNK_SHIFT = 6
assert 1 << CHUNK_SHIFT == CHUNK
N_EXPERTS = 8
LANES = 128
NEG_BIG = -1e30
LOG2E = math.log2(math.e)
Q_COLS = 256
ONES_ROWS = 16
SCORE_LOOKAHEAD = 3
VMEM_LIMIT = 56 * 1024 * 1024


def _params(*sem):
    return pltpu.CompilerParams(dimension_semantics=sem, vmem_limit_bytes=VMEM_LIMIT)


def _sigmoid(x):
    return 1.0 / (1.0 + jnp.exp(-x))


def _rms_scale(x):
    return x * lax.rsqrt(jnp.mean(x * x, axis=-1, keepdims=True) + EPS)


def _split3(x):
    hi = x.astype(BF16)
    r1 = x - hi.astype(F32)
    mid = r1.astype(BF16)
    lo = (r1 - mid.astype(F32)).astype(BF16)
    return hi, mid, lo


def _dot(a, b):
    return jnp.dot(a, b, preferred_element_type=F32)


def _lane_tile(x, n):
    return jnp.concatenate([x] * n, axis=1) if n > 1 else x


def _dot_exact_rhs(x, m_bf16):
    hi, mid, lo = _split3(x)
    return _dot(hi, m_bf16) + _dot(mid, m_bf16) + _dot(lo, m_bf16)


def _dot_exact_lhs(m_bf16, x):
    hi, mid, lo = _split3(x)
    return _dot(m_bf16, hi) + _dot(m_bf16, mid) + _dot(m_bf16, lo)


def _rms_matmul_kernel(x_ref, nw_ref, w_ref, wdt_ref, o_ref, dt_ref, h_ref):
    @pl.when(pl.program_id(1) == 0)
    def _():
        h = (_rms_scale(x_ref[...]) * nw_ref[...]).astype(BF16)
        h_ref[...] = h
        dt_ref[...] = _dot(h, wdt_ref[...])

    o_ref[...] = _dot(h_ref[...], w_ref[...]).astype(o_ref.dtype)


def rms_matmul(x, nw, w, w_dt, *, tm, tn):
    t, d = x.shape
    n = w.shape[1]
    n_dt = w_dt.shape[1]
    return pl.pallas_call(
        _rms_matmul_kernel,
        grid=(t // tm, n // tn),
        in_specs=[pl.BlockSpec((tm, d), lambda i, j: (i, 0)),
                  pl.BlockSpec((1, d), lambda i, j: (0, 0)),
                  pl.BlockSpec((d, tn), lambda i, j: (0, j)),
                  pl.BlockSpec((d, n_dt), lambda i, j: (0, 0))],
        out_specs=[pl.BlockSpec((tm, tn), lambda i, j: (i, j)),
                   pl.BlockSpec((tm, n_dt), lambda i, j: (i, 0))],
        out_shape=[jax.ShapeDtypeStruct((t, n), BF16),
                   jax.ShapeDtypeStruct((t, n_dt), F32)],
        scratch_shapes=[pltpu.VMEM((tm, d), BF16)],
        compiler_params=_params("arbitrary", "arbitrary"),
        name="rms_matmul",
    )(x, nw, w, w_dt)


def _ssd_kernel(z_ref, xs_ref, b_ref, c_ref, dt_ref, cin_ref, sin_ref, cw_ref, cb_ref,
                dtb_ref, alog_ref, dexp_ref, gnw_ref, e_ref,
                y_ref, sout_ref, state_ref, cbuf_ref, *, c, cv):
    step = pl.program_id(1)

    @pl.when(step == 0)
    def _():
        state_ref[...] = sin_ref[0].T
        cbuf_ref[0:8, :] = cin_ref[0]

    cbuf_ref[8:8 + cv, 0:D_INNER] = xs_ref[0].astype(F32)
    cbuf_ref[8:8 + cv, D_INNER:D_INNER + GROUP_W] = b_ref[0].astype(F32)
    cbuf_ref[8:8 + cv, D_INNER + GROUP_W:CONV_DIM] = c_ref[0].astype(F32)
    if cv < c:
        cbuf_ref[8 + cv:8 + c, :] = jnp.zeros((c - cv, CONV_DIM), F32)

    conv = cb_ref[...] + cbuf_ref[5:5 + c, :] * cw_ref[0:1, :]
    for k in range(1, CONV_W):
        conv = conv + cbuf_ref[5 + k:5 + k + c, :] * cw_ref[k:k + 1, :]
    conv = conv * _sigmoid(conv)
    if cv == c:
        cbuf_ref[5:8, :] = cbuf_ref[5 + c:8 + c, :]

    xs = conv[:, 0:D_INNER]
    bm = conv[:, D_INNER:D_INNER + GROUP_W]
    cm = conv[:, D_INNER + GROUP_W:CONV_DIM]

    dt_in = dt_ref[0][:, 0:LANES] + dtb_ref[...]
    dt = jnp.maximum(dt_in, 0.0) + jnp.log1p(jnp.exp(-jnp.abs(dt_in)))
    if cv < c:
        dt = jnp.concatenate([dt, jnp.zeros((c - cv, LANES), F32)], axis=0)
    a = -jnp.exp(alog_ref[...])
    dta = dt * a

    row = lax.broadcasted_iota(jnp.int32, (c, c), 0)
    col = lax.broadcasted_iota(jnp.int32, (c, c), 1)
    tril = col <= row
    tri = jnp.where(tril, 1.0, 0.0).astype(BF16)
    cs = _dot_exact_lhs(tri, dta)
    cs_t = cs.T

    e = e_ref[...]
    dt_x = _dot_exact_rhs(dt, e)
    cs_x = _dot_exact_rhs(cs, e)
    cs_end_x = cs_x[c - 1:c, :]
    xdt = xs * dt_x
    decay_in = jnp.exp(cs_x)
    xdt_b = xdt.astype(BF16)
    xdt_end = (xdt * jnp.exp(cs_end_x - cs_x)).astype(BF16)
    state_decay = jnp.exp(cs_end_x)

    lane = lax.broadcasted_iota(jnp.int32, (c, LANES), 1)
    low_half = lane < SSM_HEAD_DIM

    y_groups = []
    for g in range(SSM_GROUPS):
        gs = slice(g * GROUP_W, (g + 1) * GROUP_W)
        ns = slice(g * D_STATE, (g + 1) * D_STATE)
        bg_t = bm[:, ns].T.astype(BF16)
        cg = cm[:, ns].astype(BF16)
        cb = _dot(cg, bg_t)
        sg = state_ref[:, gs]
        y_off = _dot(cg, sg.astype(BF16)) * decay_in[:, gs]
        pairs = []
        for j in range(GROUP_W // LANES):
            h0 = g * (GROUP_W // SSM_HEAD_DIM) + 2 * j
            xp = xdt_b[:, g * GROUP_W + j * LANES:g * GROUP_W + (j + 1) * LANES]
            ys = []
            for h in (h0, h0 + 1):
                seg = cs[:, h:h + 1] - cs_t[h:h + 1, :]
                m = (cb * jnp.where(tril, jnp.exp(seg), 0.0)).astype(BF16)
                ys.append(_dot(m, xp))
            pairs.append(jnp.where(low_half, ys[0], ys[1]))
        y_groups.append(jnp.concatenate(pairs, axis=1) + y_off)
        state_ref[:, gs] = sg * state_decay[:, gs] + _dot(bg_t, xdt_end[:, gs])

    y = jnp.concatenate(y_groups, axis=1) + dexp_ref[...] * xs
    if cv < c:
        y = y[0:cv]
    zv = z_ref[0].astype(F32)
    y = y * (zv * _sigmoid(zv))
    outs = []
    for g in range(SSM_GROUPS):
        gs = slice(g * GROUP_W, (g + 1) * GROUP_W)
        outs.append(_rms_scale(y[:, gs]))
    y_ref[0] = (jnp.concatenate(outs, axis=1) * gnw_ref[...]).astype(y_ref.dtype)

    @pl.when(step == pl.num_programs(1) - 1)
    def _():
        sout_ref[0] = state_ref[...].T


def ssd(proj, dt_raw, conv_in, state_in, cw, cb, dtb, alog, dexp, gnw, *, c, cv):
    nb, length, _ = proj.shape
    steps = length // cv
    head_of_channel = np.arange(D_INNER) // SSM_HEAD_DIM
    expand = jnp.asarray(np.arange(LANES)[:, None] == head_of_channel[None, :], dtype=BF16)
    const = lambda b, l: (0, 0)
    kern = functools.partial(_ssd_kernel, c=c, cv=cv)
    return pl.pallas_call(
        kern,
        grid=(nb, steps),
        in_specs=[
            pl.BlockSpec((1, cv, D_INNER), lambda b, l: (b, l, 0)),
            pl.BlockSpec((1, cv, D_INNER), lambda b, l: (b, l, 1)),
            pl.BlockSpec((1, cv, GROUP_W), lambda b, l: (b, l, 2 * D_INNER // GROUP_W)),
            pl.BlockSpec((1, cv, GROUP_W), lambda b, l: (b, l, 2 * D_INNER // GROUP_W + 1)),
            pl.BlockSpec((1, cv, DT_PAD), lambda b, l: (b, l, 0)),
            pl.BlockSpec((1, 8, CONV_DIM), lambda b, l: (b, 0, 0)),
            pl.BlockSpec((1, D_INNER, D_STATE), lambda b, l: (b, 0, 0)),
            pl.BlockSpec((CONV_W, CONV_DIM), const),
            pl.BlockSpec((1, CONV_DIM), const),
            pl.BlockSpec((1, LANES), const),
            pl.BlockSpec((1, LANES), const),
            pl.BlockSpec((1, D_INNER), const),
            pl.BlockSpec((1, D_INNER), const),
            pl.BlockSpec((LANES, D_INNER), const),
        ],
        out_specs=[
            pl.BlockSpec((1, cv, D_INNER), lambda b, l: (b, l, 0)),
            pl.BlockSpec((1, D_INNER, D_STATE), lambda b, l: (b, 0, 0)),
        ],
        out_shape=[
            jax.ShapeDtypeStruct((nb, length, D_INNER), BF16),
            jax.ShapeDtypeStruct((nb, D_INNER, D_STATE), F32),
        ],
        scratch_shapes=[pltpu.VMEM((D_STATE, D_INNER), F32),
                        pltpu.VMEM((8 + c, CONV_DIM), F32)],
        compiler_params=_params("arbitrary", "arbitrary"),
        name="ssd",
    )(proj, proj, proj, proj, dt_raw, conv_in, state_in, cw, cb, dtb, alog, dexp, gnw, expand)


def _mm_res_kernel(a_ref, w_ref, r_ref, o_ref):
    o_ref[...] = r_ref[...] + _dot(a_ref[...], w_ref[...])


def matmul_residual(a, w, res, *, tm):
    t, k = a.shape
    n = w.shape[1]
    return pl.pallas_call(
        _mm_res_kernel,
        grid=(t // tm,),
        in_specs=[pl.BlockSpec((tm, k), lambda i: (i, 0)),
                  pl.BlockSpec((k, n), lambda i: (0, 0)),
                  pl.BlockSpec((tm, n), lambda i: (i, 0))],
        out_specs=pl.BlockSpec((tm, n), lambda i: (i, 0)),
        out_shape=jax.ShapeDtypeStruct((t, n), F32),
        compiler_params=_params("arbitrary"),
        name="matmul_residual",
    )(a, w, res)


def _ffn_kernel(x_ref, nw_ref, wg_ref, wu_ref, wd_ref, o_ref, h_ref, acc_ref):
    f = pl.program_id(1)

    @pl.when(f == 0)
    def _():
        x = x_ref[...]
        h_ref[...] = (_rms_scale(x) * nw_ref[...]).astype(BF16)
        acc_ref[...] = x

    h = h_ref[...]
    g = _dot(h, wg_ref[...])
    u = _dot(h, wu_ref[...])
    act = (g * _sigmoid(g) * u).astype(BF16)
    acc_ref[...] += _dot(act, wd_ref[...])

    @pl.when(f == pl.num_programs(1) - 1)
    def _():
        o_ref[...] = acc_ref[...]


def swiglu_ffn(x, nw, wg, wu, wd, *, tm, tf):
    t, d = x.shape
    ff = wg.shape[1]
    return pl.pallas_call(
        _ffn_kernel,
        grid=(t // tm, ff // tf),
        in_specs=[pl.BlockSpec((tm, d), lambda i, f: (i, 0)),
                  pl.BlockSpec((1, d), lambda i, f: (0, 0)),
                  pl.BlockSpec((d, tf), lambda i, f: (0, f)),
                  pl.BlockSpec((d, tf), lambda i, f: (0, f)),
                  pl.BlockSpec((tf, d), lambda i, f: (f, 0))],
        out_specs=pl.BlockSpec((tm, d), lambda i, f: (i, 0)),
        out_shape=jax.ShapeDtypeStruct((t, d), F32),
        scratch_shapes=[pltpu.VMEM((tm, d), BF16), pltpu.VMEM((tm, d), F32)],
        compiler_params=_params("arbitrary", "arbitrary"),
        name="swiglu_ffn",
    )(x, nw, wg, wu, wd)


def _qkv_kernel(x_ref, kvw_ref, aw_ref, wk_ref, wv_ref, wq_ref, cos_ref, sa_ref, sb_ref,
                k_ref, v_ref, *extra_refs, transposed):
    xn = _rms_scale(x_ref[...])
    hkv = (xn * kvw_ref[...]).astype(BF16)
    hq = (xn * aw_ref[...]).astype(BF16)
    cos, sa, sb = cos_ref[...], sa_ref[...], sb_ref[...]

    def rope(t):
        blocks = []
        for j in range(t.shape[1] // LANES):
            tb = t[:, j * LANES:(j + 1) * LANES]
            blocks.append(tb * cos + pltpu.roll(tb, LANES - ROT_DIM // 2, 1) * sa
                          + pltpu.roll(tb, ROT_DIM // 2, 1) * sb)
        return jnp.concatenate(blocks, axis=1)

    k = rope(_dot(hkv, wk_ref[...]))
    v = _dot(hkv, wv_ref[...])
    q = rope(_dot(hq, wq_ref[...]))
    tm = k.shape[0]
    for h in range(N_HEADS):
        rows = pl.ds(h, tm, stride=N_HEADS)
        k_ref[rows, :] = k[:, h * LANES:(h + 1) * LANES]
        v_ref[rows, :] = v[:, h * LANES:(h + 1) * LANES]
    if transposed:
        kb_ref, vt_ref, qt_ref = extra_refs
        kb_ref[...] = k.astype(BF16)
        vt_ref[0] = v.T.astype(BF16)
        qt_ref[0] = (q * (DK ** -0.5 * LOG2E)).T.astype(BF16)
    else:
        (q_ref,) = extra_refs
        q_ref[...] = (q * (DK ** -0.5)).astype(BF16)


def _rope_tables(pos):
    half = ROT_DIM // 2
    inv_freq = ROPE_THETA ** (-jnp.arange(half, dtype=F32) / half)
    ang = pos.astype(F32)[:, None] * inv_freq[None, :]
    cos, sin = jnp.cos(ang), jnp.sin(ang)
    n = pos.shape[0]
    ones = jnp.ones((n, DK - ROT_DIM), F32)
    zeros_h = jnp.zeros((n, half), F32)
    zeros_r = jnp.zeros((n, DK - ROT_DIM), F32)
    cos_t = jnp.concatenate([cos, cos, ones], axis=1)
    sa_t = jnp.concatenate([-sin, zeros_h, zeros_r], axis=1)
    sb_t = jnp.concatenate([zeros_h, sin, zeros_r], axis=1)
    rep = lambda t: jnp.tile(t, (1, LANES // DK))
    return rep(cos_t), rep(sa_t), rep(sb_t)


def qkv_proj(x, kvw, aw, wk, wv, wq, tables, *, tm, streams, transposed):
    t, d = x.shape
    period = tables[0].shape[0] // tm
    row = lambda i: (i, 0)
    const = lambda i: (0, 0)
    tab = lambda i: (i % period, 0)
    heads_out = jax.ShapeDtypeStruct((t * N_HEADS, LANES), F32)
    heads_spec = pl.BlockSpec((tm * N_HEADS, LANES), row)
    out_specs = [heads_spec, heads_spec, pl.BlockSpec((tm, d), row)]
    out_shape = [heads_out, heads_out, jax.ShapeDtypeStruct((t, d), BF16)]
    if transposed:
        per = t // streams // tm
        tr_spec = pl.BlockSpec((1, d, tm), lambda i: (i // per, 0, i % per))
        tr_out = jax.ShapeDtypeStruct((streams, d, t // streams), BF16)
        out_specs += [tr_spec, tr_spec]
        out_shape += [tr_out, tr_out]
    return pl.pallas_call(
        functools.partial(_qkv_kernel, transposed=transposed),
        grid=(t // tm,),
        in_specs=[pl.BlockSpec((tm, d), row),
                  pl.BlockSpec((1, d), const), pl.BlockSpec((1, d), const),
                  pl.BlockSpec((d, d), const), pl.BlockSpec((d, d), const), pl.BlockSpec((d, d), const),
                  pl.BlockSpec((tm, LANES), tab), pl.BlockSpec((tm, LANES), tab),
                  pl.BlockSpec((tm, LANES), tab)],
        out_specs=out_specs,
        out_shape=out_shape,
        compiler_params=_params("arbitrary"),
        name="qkv_proj",
    )(x, kvw, aw, wk, wv, wq, *tables)


def _lambda_value(lp_ref, lambda_init):
    lp = lp_ref[...]
    s1 = jnp.sum(lp[0:1] * lp[1:2], axis=-1, keepdims=True)
    s2 = jnp.sum(lp[2:3] * lp[3:4], axis=-1, keepdims=True)
    return jnp.exp(s1) - jnp.exp(s2) + lambda_init


def _diff_finish(o1, o2, lam, subw, lambda_init):
    o = o1 - lam * o2
    return _rms_scale(o) * subw * (1.0 - lambda_init)


def _attn_kernel(qt_ref, kt_ref, q_ref, k_ref, v_ref, lp_ref, subw_ref, o_ref,
                 qs_ref, *state_refs, tq, tk, lambda_init):
    p = pl.program_id(1)
    qi = qt_ref[p]
    ki = kt_ref[p]
    n_qc = 2 * tq // Q_COLS
    m_refs, acc_refs = state_refs[:n_qc], state_refs[n_qc:]

    @pl.when(ki == 0)
    def _():
        for m_ref, acc_ref in zip(m_refs, acc_refs):
            m_ref[...] = jnp.full(m_ref.shape, NEG_BIG, F32)
            acc_ref[...] = jnp.zeros(acc_ref.shape, F32)
        first_sub = lax.broadcasted_iota(jnp.int32, (LANES, tq), 0) < DK
        for h in range(N_HEADS):
            qh = q_ref[0, h * LANES:(h + 1) * LANES, :]
            zero = jnp.zeros_like(qh)
            qs_ref[h, :, 0:tq] = jnp.where(first_sub, qh, zero)
            qs_ref[h, :, tq:2 * tq] = jnp.where(first_sub, zero, qh)

    ones = jnp.ones((ONES_ROWS, tk), BF16)

    def sweep(masked):
        if masked:
            k_chunk = (ki * tk + lax.broadcasted_iota(jnp.int32, (tk, Q_COLS), 0)) >> CHUNK_SHIFT
            q_lane = lax.broadcasted_iota(jnp.int32, (tk, Q_COLS), 1)

        def scores(h, c):
            kh = k_ref[0, :, h * LANES:(h + 1) * LANES]
            s = _dot(kh, qs_ref[h, :, c * Q_COLS:(c + 1) * Q_COLS])
            if masked:
                q_chunk = (qi * tq + (c * Q_COLS) % tq + q_lane) >> CHUNK_SHIFT
                s = jnp.where(k_chunk <= q_chunk, s, NEG_BIG)
            return s

        groups = [(h, c) for h in range(N_HEADS) for c in range(n_qc)]
        pending = [scores(*g) for g in groups[:SCORE_LOOKAHEAD]]
        for i, (h, c) in enumerate(groups):
            s = pending.pop(0)
            if i + SCORE_LOOKAHEAD < len(groups):
                pending.append(scores(*groups[i + SCORE_LOOKAHEAD]))
            vt = jnp.concatenate([v_ref[0, h * LANES:(h + 1) * LANES, :], ones], axis=0)
            m_prev = m_refs[c][h:h + 1, :]
            m_new = jnp.maximum(m_prev, jnp.max(s, axis=0, keepdims=True))
            alpha = jnp.exp2(m_prev - m_new)
            pr = jnp.exp2(s - m_new).astype(BF16)
            acc_refs[c][h] = acc_refs[c][h] * alpha + _dot(vt, pr)
            m_refs[c][h:h + 1, :] = m_new

    @pl.when(ki < qi)
    def _():
        sweep(False)

    @pl.when(ki == qi)
    def _():
        sweep(True)
        lam = _lambda_value(lp_ref, lambda_init)
        subw = subw_ref[...]

        for h in range(N_HEADS):
            hs = slice(h * LANES, (h + 1) * LANES)
            for r in range(tq // Q_COLS):
                a1 = acc_refs[r][h]
                a2 = acc_refs[r + tq // Q_COLS][h]
                o_t = a1[0:DV] / a1[DV:DV + 1] - lam * (a2[0:DV] / a2[DV:DV + 1])
                res = _rms_scale(o_t.T) * subw * (1.0 - lambda_init)
                o_ref[0, r * Q_COLS:(r + 1) * Q_COLS, hs] = res.astype(o_ref.dtype)


def diff_attention(q_t, k, v_t, lam_params, subw, *, tq, lambda_init):
    nb, d, length = q_t.shape
    tk = tq
    nq = length // tq
    qt = np.concatenate([np.full(i + 1, i) for i in range(nq)]).astype(np.int32)
    kt = np.concatenate([np.arange(i + 1) for i in range(nq)]).astype(np.int32)
    kern = functools.partial(_attn_kernel, tq=tq, tk=tk, lambda_init=lambda_init)
    n_qc = 2 * tq // Q_COLS
    grid_spec = pltpu.PrefetchScalarGridSpec(
        num_scalar_prefetch=2,
        grid=(nb, len(qt)),
        in_specs=[pl.BlockSpec((1, d, tq), lambda b, p, qt_r, kt_r: (b, 0, qt_r[p])),
                  pl.BlockSpec((1, tk, d), lambda b, p, qt_r, kt_r: (b, kt_r[p], 0)),
                  pl.BlockSpec((1, d, tk), lambda b, p, qt_r, kt_r: (b, 0, kt_r[p])),
                  pl.BlockSpec((4, DK), lambda b, p, qt_r, kt_r: (0, 0)),
                  pl.BlockSpec((1, DV), lambda b, p, qt_r, kt_r: (0, 0))],
        out_specs=pl.BlockSpec((1, tq, d), lambda b, p, qt_r, kt_r: (b, qt_r[p], 0)),
        scratch_shapes=([pltpu.VMEM((N_HEADS, LANES, 2 * tq), BF16)]
                        + [pltpu.VMEM((N_HEADS, Q_COLS), F32)] * n_qc
                        + [pltpu.VMEM((N_HEADS, DV + ONES_ROWS, Q_COLS), F32)] * n_qc),
    )
    return pl.pallas_call(
        kern,
        grid_spec=grid_spec,
        out_shape=jax.ShapeDtypeStruct((nb, length, d), BF16),
        compiler_params=_params("arbitrary", "arbitrary"),
        name="diff_attention",
    )(jnp.asarray(qt), jnp.asarray(kt), q_t, k, v_t, lam_params, subw)


def _attn_cached_kernel(q_ref, ck_ref, cv_ref, kn_ref, vn_ref, lp_ref, subw_ref, o_ref,
                        *, nq, lambda_init):
    past = ck_ref.shape[1] // N_HEADS
    lane = lax.broadcasted_iota(jnp.int32, (nq, LANES), 1)
    nt = (((1,), (1,)), ((), ()))
    pad = jnp.zeros((LANES - nq, LANES), BF16)
    lam = _lambda_value(lp_ref, lambda_init)
    for h in range(N_HEADS):
        q = q_ref[0, :, h * LANES:(h + 1) * LANES]
        zero = jnp.zeros_like(q)
        qs = jnp.concatenate([jnp.where(lane < DK, q, zero), jnp.where(lane < DK, zero, q)], axis=0)
        old_rows = pl.ds(h, past, stride=N_HEADS)
        new_rows = pl.ds(h, nq, stride=N_HEADS)
        kc = ck_ref[0, old_rows, :].astype(BF16)
        vc = cv_ref[0, old_rows, :].astype(BF16)
        kn = jnp.concatenate([kn_ref[0, new_rows, :].astype(BF16), pad], axis=0)
        vn = jnp.concatenate([vn_ref[0, new_rows, :].astype(BF16), pad], axis=0)
        s_c = lax.dot_general(qs, kc, nt, preferred_element_type=F32)
        s_n = lax.dot_general(qs, kn, nt, preferred_element_type=F32)
        col = lax.broadcasted_iota(jnp.int32, s_n.shape, 1)
        s_n = jnp.where(col < nq, s_n, NEG_BIG)
        m = jnp.maximum(jnp.max(s_c, axis=-1, keepdims=True), jnp.max(s_n, axis=-1, keepdims=True))
        p_c = jnp.exp(s_c - m)
        p_n = jnp.exp(s_n - m)
        denom = jnp.sum(p_c, axis=-1, keepdims=True) + jnp.sum(p_n, axis=-1, keepdims=True)
        o = (_dot(p_c.astype(BF16), vc) + _dot(p_n.astype(BF16), vn)) / denom
        res = _diff_finish(o[0:nq], o[nq:2 * nq], lam, subw_ref[...], lambda_init)
        o_ref[0, :, h * LANES:(h + 1) * LANES] = res.astype(o_ref.dtype)


def diff_attention_cached(q, cache_k, cache_v, k_new, v_new, lam_params, subw, *, lambda_init):
    nb, nq, d = q.shape
    past = cache_k.shape[1]
    kern = functools.partial(_attn_cached_kernel, nq=nq, lambda_init=lambda_init)
    stream = lambda b: (b, 0, 0)
    const = lambda b: (0, 0)
    rows = lambda a: a.reshape(nb, a.shape[1] * N_HEADS, LANES)
    cache_k, cache_v, k_new, v_new = rows(cache_k), rows(cache_v), rows(k_new), rows(v_new)
    return pl.pallas_call(
        kern,
        grid=(nb,),
        in_specs=[pl.BlockSpec((1, nq, d), lambda b: (b, 0, 0)),
                  pl.BlockSpec((1, past * N_HEADS, LANES), stream),
                  pl.BlockSpec((1, past * N_HEADS, LANES), stream),
                  pl.BlockSpec((1, nq * N_HEADS, LANES), stream),
                  pl.BlockSpec((1, nq * N_HEADS, LANES), stream),
                  pl.BlockSpec((4, DK), const),
                  pl.BlockSpec((1, DV), const)],
        out_specs=pl.BlockSpec((1, nq, d), lambda b: (b, 0, 0)),
        out_shape=jax.ShapeDtypeStruct((nb, nq, d), BF16),
        compiler_params=_params("arbitrary"),
        name="diff_attention_cached",
    )(q, cache_k, cache_v, k_new, v_new, lam_params, subw)


def _route(logits):
    lane = lax.broadcasted_iota(jnp.int32, logits.shape, 1)
    valid = lane < N_EXPERTS
    lg = jnp.where(valid, logits, NEG_BIG)
    m1 = jnp.max(lg, axis=-1, keepdims=True)
    lane_f = lane.astype(F32)
    i1 = jnp.min(jnp.where(lg == m1, lane_f, float(LANES)), axis=-1, keepdims=True)
    lg2 = jnp.where(lane_f == i1, NEG_BIG, lg)
    m2 = jnp.max(lg2, axis=-1, keepdims=True)
    i2 = jnp.min(jnp.where(lg2 == m2, lane_f, float(LANES)), axis=-1, keepdims=True)
    e2 = jnp.exp(m2 - m1)
    g1 = 1.0 / (1.0 + e2)
    g2 = e2 / (1.0 + e2)
    return jnp.where(lane_f == i1, g1, 0.0) + jnp.where(lane_f == i2, g2, 0.0)


def _moe_route_kernel(x_ref, nw_ref, wr_ref, h_ref, comb_ref, rk_ref, rkt_ref, cnt_ref, *, rt):
    t = x_ref.shape[0]
    hf = _rms_scale(x_ref[...]) * nw_ref[...]
    h_ref[...] = hf.astype(BF16)
    comb = _route(_dot_exact_rhs_general(hf, wr_ref[...]))
    comb_ref[...] = comb
    sel = jnp.where(comb > 0.0, 1.0, 0.0)
    row = lax.broadcasted_iota(jnp.int32, (rt, rt), 0)
    col = lax.broadcasted_iota(jnp.int32, (rt, rt), 1)
    strict = jnp.where(col < row, 1.0, 0.0).astype(BF16)
    carry = jnp.zeros((1, LANES), F32)
    ranks = []
    for s in range(t // rt):
        sel_s = sel[s * rt:(s + 1) * rt]
        rank_s = _dot(strict, sel_s.astype(BF16)) + carry
        ranks.append(jnp.where(sel_s > 0.0, rank_s, -1.0))
        carry = carry + jnp.sum(sel_s, axis=0, keepdims=True)
    rk = jnp.concatenate(ranks, axis=0) if len(ranks) > 1 else ranks[0]
    rk_ref[...] = rk.astype(jnp.int32)
    rkt_ref[0] = rk.T.astype(jnp.int32)
    cnt_ref[0] = jnp.broadcast_to(carry, (8, LANES)).astype(jnp.int32)


def moe_route(x, nw, wr, *, tb, rt):
    t, d = x.shape
    nb = t // tb
    row = lambda b: (b, 0)
    const = lambda b: (0, 0)
    return pl.pallas_call(
        functools.partial(_moe_route_kernel, rt=rt),
        grid=(nb,),
        in_specs=[pl.BlockSpec((tb, d), row), pl.BlockSpec((1, d), const),
                  pl.BlockSpec((d, LANES), const)],
        out_specs=[pl.BlockSpec((tb, d), row), pl.BlockSpec((tb, LANES), row),
                   pl.BlockSpec((tb, LANES), row),
                   pl.BlockSpec((1, LANES, tb), lambda b: (b, 0, 0)),
                   pl.BlockSpec((1, 8, LANES), lambda b: (b, 0, 0))],
        out_shape=[jax.ShapeDtypeStruct((t, d), BF16),
                   jax.ShapeDtypeStruct((t, LANES), F32),
                   jax.ShapeDtypeStruct((t, LANES), jnp.int32),
                   jax.ShapeDtypeStruct((nb, LANES, tb), jnp.int32),
                   jax.ShapeDtypeStruct((nb, 8, LANES), jnp.int32)],
        compiler_params=_params("arbitrary"),
        name="moe_route",
    )(x, nw, wr)


def _moe_expert_kernel(cnt_ref, x_ref, h_ref, comb_ref, rk_ref, rkt_ref, wg_ref, wu_ref, wd_ref,
                       fw_ref, o_ref, rkc_ref, xg_ref, gs_ref, yacc_ref, *, rt):
    b = pl.program_id(0)
    e = pl.program_id(1)
    f = pl.program_id(2)
    t = x_ref.shape[0]
    n_rows = cnt_ref[b * N_EXPERTS + e]
    half = rt // 2
    n_tiles = (n_rows + (half - 1)) // rt
    tail_row0 = pl.multiple_of(n_tiles * rt, rt)
    has_tail = n_rows > n_tiles * rt

    def over_tiles(fn):
        def body(j, carry):
            fn(pl.multiple_of(j * rt, rt), rt)
            return carry

        lax.fori_loop(0, n_tiles, body, 0)

        @pl.when(has_tail)
        def _():
            fn(tail_row0, half)

    @pl.when(jnp.logical_and(e == 0, f == 0))
    def _():
        o_ref[...] = x_ref[...]

    @pl.when(f == 0)
    def _():
        lane = lax.broadcasted_iota(jnp.int32, (t, LANES), 1)
        pick = lane == e
        rkc = jnp.sum(jnp.where(pick, rk_ref[...], 0).astype(F32), axis=-1, keepdims=True)
        rkc_ref[...] = jnp.broadcast_to(rkc, (t, LANES)).astype(jnp.int32)
        gate = jnp.sum(jnp.where(pick, comb_ref[...], 0.0), axis=-1, keepdims=True)
        g_hi, g_mid, g_lo = _split3(jnp.broadcast_to(gate, (t, LANES)))
        g3 = jnp.where(lane == 0, g_hi.astype(F32),
                       jnp.where(lane == 1, g_mid.astype(F32),
                                 jnp.where(lane == 2, g_lo.astype(F32), 0.0))).astype(BF16)
        rk_row = rkt_ref[0, pl.ds(e, 1), :]

        def gather(row0, size):
            tile = pl.ds(row0, size)
            rows = lax.broadcasted_iota(jnp.int32, (size, t), 0) + row0
            p = jnp.where(rk_row == rows, 1.0, 0.0).astype(BF16)
            xg_ref[tile, :] = _dot(p, h_ref[...]).astype(BF16)
            gate_rows = jnp.sum(_dot(p, g3), axis=-1, keepdims=True)
            gs_ref[tile, :] = jnp.broadcast_to(gate_rows, (size, LANES))
            yacc_ref[tile, :] = jnp.zeros((size, yacc_ref.shape[1]), F32)

        over_tiles(gather)

    def expert(row0, size):
        tile = pl.ds(row0, size)
        xg = xg_ref[tile, :]
        g = _dot(xg, wg_ref[...])
        u = _dot(xg, wu_ref[...])
        gates = _lane_tile(gs_ref[tile, :], g.shape[1] // LANES)
        act = (g * _sigmoid(g) * u * gates).astype(BF16)
        yacc_ref[tile, :] += _dot(act, wd_ref[...])

    over_tiles(expert)

    @pl.when(f == pl.num_programs(2) - 1)
    def _():
        def scatter(row0, size):
            cols = lax.broadcasted_iota(jnp.int32, (t, size), 1) + row0
            s = jnp.where(_lane_tile(rkc_ref[...], size // LANES) == cols, 1.0, 0.0).astype(BF16)
            o_ref[...] += _dot(s, yacc_ref[pl.ds(row0, size), :].astype(BF16))

        over_tiles(scatter)

    @pl.when(jnp.logical_and(e == pl.num_programs(1) - 1, f == pl.num_programs(2) - 1))
    def _():
        o_ref[...] = _rms_scale(o_ref[...]) * fw_ref[...]


def _dot_exact_rhs_general(x, w):
    xh, xm, xl = _split3(x)
    wh, wm, wl = _split3(w)
    return (_dot(xh, wh) + (_dot(xh, wm) + _dot(xm, wh))
            + (_dot(xh, wl) + _dot(xm, wm) + _dot(xl, wh)))


def moe_ffn_final(x, nw, wr, wg, wu, wd, fw, *, tb, rt, tf):
    t, d = x.shape
    ne, _, ff = wg.shape
    nb = t // tb
    h, comb, rk, rkt, cnt = moe_route(x, nw, wr, tb=tb, rt=rt)
    counts = cnt[:, 0, :ne].reshape(nb * ne)
    once = pl.Buffered(1)
    blk = lambda b, e, f, c: (b, 0)
    const = lambda b, e, f, c: (0, 0)
    grid_spec = pltpu.PrefetchScalarGridSpec(
        num_scalar_prefetch=1,
        grid=(nb, ne, ff // tf),
        in_specs=[pl.BlockSpec((tb, d), blk, pipeline_mode=once),
                  pl.BlockSpec((tb, d), blk, pipeline_mode=once),
                  pl.BlockSpec((tb, LANES), blk, pipeline_mode=once),
                  pl.BlockSpec((tb, LANES), blk, pipeline_mode=once),
                  pl.BlockSpec((1, LANES, tb), lambda b, e, f, c: (b, 0, 0), pipeline_mode=once),
                  pl.BlockSpec((None, d, tf), lambda b, e, f, c: (e, 0, f)),
                  pl.BlockSpec((None, d, tf), lambda b, e, f, c: (e, 0, f)),
                  pl.BlockSpec((None, tf, d), lambda b, e, f, c: (e, f, 0)),
                  pl.BlockSpec((1, d), const)],
        out_specs=pl.BlockSpec((tb, d), blk),
        scratch_shapes=[pltpu.VMEM((tb, LANES), jnp.int32),
                        pltpu.VMEM((tb, d), BF16),
                        pltpu.VMEM((tb, LANES), F32),
                        pltpu.VMEM((tb, d), F32)],
    )
    return pl.pallas_call(
        functools.partial(_moe_expert_kernel, rt=rt),
        grid_spec=grid_spec,
        out_shape=jax.ShapeDtypeStruct((t, d), F32),
        compiler_params=_params("arbitrary", "arbitrary", "arbitrary"),
        name="moe_experts",
    )(counts, x, h, comb, rk, rkt, wg, wu, wd, fw)


def _row(v):
    return v.reshape(1, -1).astype(F32)


def _pad_lanes(v, width):
    v = _row(v)
    return jnp.pad(v, ((0, 0), (0, width - v.shape[1])))


def _trunk(x, pos, conv_in, state_in, past_k, past_v, W, *, ssd_chunk, ssd_valid, tm, tq):
    nb, length, d = x.shape
    t = nb * length
    x0 = x.reshape(t, d)

    proj, dt_raw = rms_matmul(x0, W["mamba_norm_w"], W["w_in"], W["w_dt"], tm=min(t, 1024), tn=1024)
    proj3 = proj.reshape(nb, length, D_INNER + CONV_DIM)
    new_conv = proj3[:, length - (CONV_W - 1):, D_INNER:D_INNER + CONV_DIM].astype(F32)
    conv8 = jnp.pad(conv_in, ((0, 0), (8 - (CONV_W - 1), 0), (0, 0)))
    y, state_out = ssd(proj3, dt_raw.reshape(nb, length, DT_PAD), conv8,
                       state_in.reshape(nb, D_INNER, D_STATE),
                       W["conv_w"], W["conv_b"], W["dt_bias"], W["a_log"], W["d_exp"], W["gn_w"],
                       c=ssd_chunk, cv=ssd_valid)
    new_ssm = state_out.reshape(nb, SSM_HEADS, SSM_HEAD_DIM, D_STATE)
    x1 = matmul_residual(y.reshape(t, D_INNER), W["w_out"], x0, tm=tm)

    x2 = swiglu_ffn(x1, W["ffn_norm_w"], W["ffn_wg"], W["ffn_wu"], W["ffn_wd"], tm=tm, tf=1408)

    tables = _rope_tables(pos)
    proj_args = (x2, W["kv_norm_w"], W["attn_norm_w"], W["w_k"], W["w_v"], W["w_q"], tables)
    lambda_init = 0.8 - 0.6 * math.exp(-0.3 * 1)
    if past_k is None:
        k, v, kb, v_t, q_t = qkv_proj(*proj_args, tm=tm, streams=nb, transposed=True)
        o = diff_attention(q_t, kb.reshape(nb, length, d), v_t,
                           W["lam"], W["subln_w"], tq=tq, lambda_init=lambda_init)
    else:
        k, v, q = qkv_proj(*proj_args, tm=tm, streams=nb, transposed=False)
        q3 = q.reshape(nb, length, d)
        o = diff_attention_cached(q3, past_k, past_v,
                                  k.reshape(nb, length, N_HEADS, LANES),
                                  v.reshape(nb, length, N_HEADS, LANES),
                                  W["lam"], W["subln_w"], lambda_init=lambda_init)
    x3 = matmul_residual(o.reshape(t, d), W["w_o"], x2, tm=tm)

    yout = moe_ffn_final(x3, W["moe_norm_w"], W["moe_wr"], W["moe_wg"], W["moe_wu"], W["moe_wd"],
                         W["final_norm_w"], tb=min(t, 2048), rt=256, tf=512)
    return (yout.reshape(nb, length, d), new_conv[None], new_ssm[None],
            k.reshape(nb, length, N_HEADS, 2 * DK), v.reshape(nb, length, N_HEADS, DV))


def kernel(x_prompt, x_sample, cache_conv, state_ssm, cache_k, cache_v, mamba_norm_w, mamba_w_in, mamba_conv_w, mamba_conv_b, mamba_dt_bias, mamba_a_log, mamba_d, mamba_gn_w, mamba_w_out, kv_norm_w, w_k, w_v, attn_norm_w, w_q, lambda_q1, lambda_k1, lambda_q2, lambda_k2, subln_w, w_o, ffn_norm_w, ffn_w_gate, ffn_w_up, ffn_w_down, moe_norm_w, moe_w_router, moe_w_gate, moe_w_up, moe_w_down, final_norm_w):
    w_in = mamba_w_in[0]
    n_dt = w_in.shape[1] - D_INNER - CONV_DIM
    W = dict(
        mamba_norm_w=_row(mamba_norm_w[0]),
        w_in=w_in[:, :D_INNER + CONV_DIM].astype(BF16),
        w_dt=jnp.pad(w_in[:, D_INNER + CONV_DIM:], ((0, 0), (0, DT_PAD - n_dt))).astype(BF16),
        conv_w=mamba_conv_w[0].astype(F32),
        conv_b=_row(mamba_conv_b[0]),
        dt_bias=_pad_lanes(mamba_dt_bias[0], LANES),
        a_log=_pad_lanes(mamba_a_log[0], LANES),
        d_exp=_row(jnp.repeat(mamba_d[0], SSM_HEAD_DIM)),
        gn_w=_row(mamba_gn_w[0]),
        w_out=mamba_w_out[0].astype(BF16),
        kv_norm_w=_row(kv_norm_w), w_k=w_k.astype(BF16), w_v=w_v.astype(BF16),
        attn_norm_w=_row(attn_norm_w[0]), w_q=w_q[0].astype(BF16),
        lam=jnp.stack([lambda_q1[0], lambda_k1[0], lambda_q2[0], lambda_k2[0]]).astype(F32),
        subln_w=_row(subln_w[0]), w_o=w_o[0].astype(BF16),
        ffn_norm_w=_row(ffn_norm_w[0]),
        ffn_wg=ffn_w_gate[0].astype(BF16), ffn_wu=ffn_w_up[0].astype(BF16),
        ffn_wd=ffn_w_down[0].astype(BF16),
        moe_norm_w=_row(moe_norm_w[0]),
        moe_wr=jnp.pad(moe_w_router[0].astype(F32), ((0, 0), (0, LANES - N_EXPERTS))),
        moe_wg=moe_w_gate[0].astype(BF16), moe_wu=moe_w_up[0].astype(BF16),
        moe_wd=moe_w_down[0].astype(BF16),
        final_norm_w=_row(final_norm_w),
    )
    bp, lp = x_prompt.shape[0], x_prompt.shape[1]
    bs, ls = x_sample.shape[0], x_sample.shape[1]
    past = cache_k.shape[1]

    conv0 = jnp.zeros((bp, CONV_W - 1, CONV_DIM), F32)
    ssm0 = jnp.zeros((bp, SSM_HEADS, SSM_HEAD_DIM, D_STATE), F32)
    y_p, conv_p, ssm_p, k_p, v_p = _trunk(
        x_prompt, jnp.arange(lp, dtype=jnp.int32), conv0, ssm0, None, None, W,
        ssd_chunk=128, ssd_valid=128, tm=512, tq=512)

    pos_s = jnp.tile(past + jnp.arange(ls, dtype=jnp.int32), bs)
    y_s, conv_s, ssm_s, k_s, v_s = _trunk(
        x_sample, pos_s, cache_conv[0], state_ssm[0], cache_k, cache_v, W,
        ssd_chunk=128, ssd_valid=ls, tm=bs * ls, tq=None)
    return (y_p, y_s, conv_p, ssm_p, k_p, v_p, conv_s, ssm_s, k_s, v_s)
```

```python
import functools
import math

import numpy as np
import jax
import jax.numpy as jnp
from jax import lax
from jax.experimental import pallas as pl
from jax.experimental.pallas import tpu as pltpu

F32 = jnp.float32
BF16 = jnp.bfloat16

EPS = 1e-5
D_MODEL = 1024
D_INNER = 2048
SSM_HEAD_DIM = 64
SSM_HEADS = 32
SSM_GROUPS = 4
D_STATE = 128
GROUP_W = D_INNER // SSM_GROUPS
CONV_W = 4
CONV_DIM = D_INNER + 2 * SSM_GROUPS * D_STATE
DT_PAD = 256
PROJ_W = D_INNER + CONV_DIM + DT_PAD
N_HEADS = 8
DK = 64
DV = 128
ROT_DIM = 16
ROPE_THETA = 500000.0
CHUNK = 64
CHUNK_SHIFT = 6
assert 1 << CHUNK_SHIFT == CHUNK
N_EXPERTS = 8
LANES = 128
NEG_BIG = -1e30
LOG2E = math.log2(math.e)
Q_COLS = 256
ONES_ROWS = 16
SCORE_LOOKAHEAD = 3
VMEM_LIMIT = 56 * 1024 * 1024


def _params(*sem):
    return pltpu.CompilerParams(dimension_semantics=sem, vmem_limit_bytes=VMEM_LIMIT)


def _sigmoid(x):
    return 1.0 / (1.0 + jnp.exp(-x))


def _rms_scale(x):
    return x * lax.rsqrt(jnp.mean(x * x, axis=-1, keepdims=True) + EPS)


def _split3(x):
    hi = x.astype(BF16)
    r1 = x - hi.astype(F32)
    mid = r1.astype(BF16)
    lo = (r1 - mid.astype(F32)).astype(BF16)
    return hi, mid, lo


def _dot(a, b):
    return jnp.dot(a, b, preferred_element_type=F32)


def _lane_tile(x, n):
    return jnp.concatenate([x] * n, axis=1) if n > 1 else x


def _dot_exact_rhs(x, m_bf16):
    hi, mid, lo = _split3(x)
    return _dot(hi, m_bf16) + _dot(mid, m_bf16) + _dot(lo, m_bf16)


def _dot_exact_lhs(m_bf16, x):
    hi, mid, lo = _split3(x)
    return _dot(m_bf16, hi) + _dot(m_bf16, mid) + _dot(m_bf16, lo)


def _rms_matmul_kernel(x_ref, nw_ref, w_ref, wdt_ref, o_ref, dt_ref, h_ref):
    @pl.when(pl.program_id(1) == 0)
    def _():
        h = (_rms_scale(x_ref[...]) * nw_ref[...]).astype(BF16)
        h_ref[...] = h
        dt_ref[...] = _dot(h, wdt_ref[...])

    o_ref[...] = _dot(h_ref[...], w_ref[...]).astype(o_ref.dtype)


def rms_matmul(x, nw, w, w_dt, *, tm, tn):
    t, d = x.shape
    n = w.shape[1]
    n_dt = w_dt.shape[1]
    return pl.pallas_call(
        _rms_matmul_kernel,
        grid=(t // tm, n // tn),
        in_specs=[pl.BlockSpec((tm, d), lambda i, j: (i, 0)),
                  pl.BlockSpec((1, d), lambda i, j: (0, 0)),
                  pl.BlockSpec((d, tn), lambda i, j: (0, j)),
                  pl.BlockSpec((d, n_dt), lambda i, j: (0, 0))],
        out_specs=[pl.BlockSpec((tm, tn), lambda i, j: (i, j)),
                   pl.BlockSpec((tm, n_dt), lambda i, j: (i, 0))],
        out_shape=[jax.ShapeDtypeStruct((t, n), BF16),
                   jax.ShapeDtypeStruct((t, n_dt), F32)],
        scratch_shapes=[pltpu.VMEM((tm, d), BF16)],
        compiler_params=_params("arbitrary", "arbitrary"),
        name="rms_matmul",
    )(x, nw, w, w_dt)


def _ssd_kernel(z_ref, xs_ref, b_ref, c_ref, dt_ref, cin_ref, sin_ref, cw_ref, cb_ref,
                dtb_ref, alog_ref, dexp_ref, gnw_ref, e_ref,
                y_ref, sout_ref, state_ref, cbuf_ref, *, c, cv):
    step = pl.program_id(1)

    @pl.when(step == 0)
    def _():
        state_ref[...] = sin_ref[0].T
        cbuf_ref[0:8, :] = cin_ref[0]

    cbuf_ref[8:8 + cv, 0:D_INNER] = xs_ref[0].astype(F32)
    cbuf_ref[8:8 + cv, D_INNER:D_INNER + GROUP_W] = b_ref[0].astype(F32)
    cbuf_ref[8:8 + cv, D_INNER + GROUP_W:CONV_DIM] = c_ref[0].astype(F32)
    if cv < c:
        cbuf_ref[8 + cv:8 + c, :] = jnp.zeros((c - cv, CONV_DIM), F32)

    conv = cb_ref[...] + cbuf_ref[5:5 + c, :] * cw_ref[0:1, :]
    for k in range(1, CONV_W):
        conv = conv + cbuf_ref[5 + k:5 + k + c, :] * cw_ref[k:k + 1, :]
    conv = conv * _sigmoid(conv)
    if cv == c:
        cbuf_ref[5:8, :] = cbuf_ref[5 + c:8 + c, :]

    xs = conv[:, 0:D_INNER]
    bm = conv[:, D_INNER:D_INNER + GROUP_W]
    cm = conv[:, D_INNER + GROUP_W:CONV_DIM]

    dt_in = dt_ref[0][:, 0:LANES] + dtb_ref[...]
    dt = jnp.maximum(dt_in, 0.0) + jnp.log1p(jnp.exp(-jnp.abs(dt_in)))
    if cv < c:
        dt = jnp.concatenate([dt, jnp.zeros((c - cv, LANES), F32)], axis=0)
    a = -jnp.exp(alog_ref[...])
    dta = dt * a

    row = lax.broadcasted_iota(jnp.int32, (c, c), 0)
    col = lax.broadcasted_iota(jnp.int32, (c, c), 1)
    tril = col <= row
    tri = jnp.where(tril, 1.0, 0.0).astype(BF16)
    cs = _dot_exact_lhs(tri, dta)
    cs_t = cs.T

    e = e_ref[...]
    dt_x = _dot_exact_rhs(dt, e)
    cs_x = _dot_exact_rhs(cs, e)
    cs_end_x = cs_x[c - 1:c, :]
    xdt = xs * dt_x
    decay_in = jnp.exp(cs_x)
    xdt_b = xdt.astype(BF16)
    xdt_end = (xdt * jnp.exp(cs_end_x - cs_x)).astype(BF16)
    state_decay = jnp.exp(cs_end_x)

    lane = lax.broadcasted_iota(jnp.int32, (c, LANES), 1)
    low_half = lane < SSM_HEAD_DIM

    y_groups = []
    for g in range(SSM_GROUPS):
        gs = slice(g * GROUP_W, (g + 1) * GROUP_W)
        ns = slice(g * D_STATE, (g + 1) * D_STATE)
        bg_t = bm[:, ns].T.astype(BF16)
        cg = cm[:, ns].astype(BF16)
        cb = _dot(cg, bg_t)
        sg = state_ref[:, gs]
        y_off = _dot(cg, sg.astype(BF16)) * decay_in[:, gs]
        pairs = []
        for j in range(GROUP_W // LANES):
            h0 = g * (GROUP_W // SSM_HEAD_DIM) + 2 * j
            xp = xdt_b[:, g * GROUP_W + j * LANES:g * GROUP_W + (j + 1) * LANES]
            ys = []
            for h in (h0, h0 + 1):
                seg = cs[:, h:h + 1] - cs_t[h:h + 1, :]
                m = (cb * jnp.where(tril, jnp.exp(seg), 0.0)).astype(BF16)
                ys.append(_dot(m, xp))
            pairs.append(jnp.where(low_half, ys[0], ys[1]))
        y_groups.append(jnp.concatenate(pairs, axis=1) + y_off)
        state_ref[:, gs] = sg * state_decay[:, gs] + _dot(bg_t, xdt_end[:, gs])

    y = jnp.concatenate(y_groups, axis=1) + dexp_ref[...] * xs
    if cv < c:
        y = y[0:cv]
    zv = z_ref[0].astype(F32)
    y = y * (zv * _sigmoid(zv))
    outs = []
    for g in range(SSM_GROUPS):
        gs = slice(g * GROUP_W, (g + 1) * GROUP_W)
        outs.append(_rms_scale(y[:, gs]))
    y_ref[0] = (jnp.concatenate(outs, axis=1) * gnw_ref[...]).astype(y_ref.dtype)

    @pl.when(step == pl.num_programs(1) - 1)
    def _():
        sout_ref[0] = state_ref[...].T


def ssd(proj, dt_raw, conv_in, state_in, cw, cb, dtb, alog, dexp, gnw, *, c, cv):
    nb, length, _ = proj.shape
    steps = length // cv
    head_of_channel = np.arange(D_INNER) // SSM_HEAD_DIM
    expand = jnp.asarray(np.arange(LANES)[:, None] == head_of_channel[None, :], dtype=BF16)
    const = lambda b, l: (0, 0)
    kern = functools.partial(_ssd_kernel, c=c, cv=cv)
    return pl.pallas_call(
        kern,
        grid=(nb, steps),
        in_specs=[
            pl.BlockSpec((1, cv, D_INNER), lambda b, l: (b, l, 0)),
            pl.BlockSpec((1, cv, D_INNER), lambda b, l: (b, l, 1)),
            pl.BlockSpec((1, cv, GROUP_W), lambda b, l: (b, l, 2 * D_INNER // GROUP_W)),
            pl.BlockSpec((1, cv, GROUP_W), lambda b, l: (b, l, 2 * D_INNER // GROUP_W + 1)),
            pl.BlockSpec((1, cv, DT_PAD), lambda b, l: (b, l, 0)),
            pl.BlockSpec((1, 8, CONV_DIM), lambda b, l: (b, 0, 0)),
            pl.BlockSpec((1, D_INNER, D_STATE), lambda b, l: (b, 0, 0)),
            pl.BlockSpec((CONV_W, CONV_DIM), const),
            pl.BlockSpec((1, CONV_DIM), const),
            pl.BlockSpec((1, LANES), const),
            pl.BlockSpec((1, LANES), const),
            pl.BlockSpec((1, D_INNER), const),
            pl.BlockSpec((1, D_INNER), const),
            pl.BlockSpec((LANES, D_INNER), const),
        ],
        out_specs=[
            pl.BlockSpec((1, cv, D_INNER), lambda b, l: (b, l, 0)),
            pl.BlockSpec((1, D_INNER, D_STATE), lambda b, l: (b, 0, 0)),
        ],
        out_shape=[
            jax.ShapeDtypeStruct((nb, length, D_INNER), BF16),
            jax.ShapeDtypeStruct((nb, D_INNER, D_STATE), F32),
        ],
        scratch_shapes=[pltpu.VMEM((D_STATE, D_INNER), F32),
                        pltpu.VMEM((8 + c, CONV_DIM), F32)],
        compiler_params=_params("arbitrary", "arbitrary"),
        name="ssd",
    )(proj, proj, proj, proj, dt_raw, conv_in, state_in, cw, cb, dtb, alog, dexp, gnw, expand)


def _mm_res_kernel(a_ref, w_ref, r_ref, o_ref):
    o_ref[...] = r_ref[...] + _dot(a_ref[...], w_ref[...])


def matmul_residual(a, w, res, *, tm):
    t, k = a.shape
    n = w.shape[1]
    return pl.pallas_call(
        _mm_res_kernel,
        grid=(t // tm,),
        in_specs=[pl.BlockSpec((tm, k), lambda i: (i, 0)),
                  pl.BlockSpec((k, n), lambda i: (0, 0)),
                  pl.BlockSpec((tm, n), lambda i: (i, 0))],
        out_specs=pl.BlockSpec((tm, n), lambda i: (i, 0)),
        out_shape=jax.ShapeDtypeStruct((t, n), F32),
        compiler_params=_params("arbitrary"),
        name="matmul_residual",
    )(a, w, res)


def _ffn_kernel(x_ref, nw_ref, wg_ref, wu_ref, wd_ref, o_ref, h_ref, acc_ref):
    f = pl.program_id(1)

    @pl.when(f == 0)
    def _():
        x = x_ref[...]
        h_ref[...] = (_rms_scale(x) * nw_ref[...]).astype(BF16)
        acc_ref[...] = x

    h = h_ref[...]
    g = _dot(h, wg_ref[...])
    u = _dot(h, wu_ref[...])
    act = (g * _sigmoid(g) * u).astype(BF16)
    acc_ref[...] += _dot(act, wd_ref[...])

    @pl.when(f == pl.num_programs(1) - 1)
    def _():
        o_ref[...] = acc_ref[...]


def swiglu_ffn(x, nw, wg, wu, wd, *, tm, tf):
    t, d = x.shape
    ff = wg.shape[1]
    return pl.pallas_call(
        _ffn_kernel,
        grid=(t // tm, ff // tf),
        in_specs=[pl.BlockSpec((tm, d), lambda i, f: (i, 0)),
                  pl.BlockSpec((1, d), lambda i, f: (0, 0)),
                  pl.BlockSpec((d, tf), lambda i, f: (0, f)),
                  pl.BlockSpec((d, tf), lambda i, f: (0, f)),
                  pl.BlockSpec((tf, d), lambda i, f: (f, 0))],
        out_specs=pl.BlockSpec((tm, d), lambda i, f: (i, 0)),
        out_shape=jax.ShapeDtypeStruct((t, d), F32),
        scratch_shapes=[pltpu.VMEM((tm, d), BF16), pltpu.VMEM((tm, d), F32)],
        compiler_params=_params("arbitrary", "arbitrary"),
        name="swiglu_ffn",
    )(x, nw, wg, wu, wd)


def _qkv_kernel(x_ref, kvw_ref, aw_ref, wk_ref, wv_ref, wq_ref, cos_ref, sa_ref, sb_ref,
                k_ref, v_ref, *extra_refs, transposed):
    xn = _rms_scale(x_ref[...])
    hkv = (xn * kvw_ref[...]).astype(BF16)
    hq = (xn * aw_ref[...]).astype(BF16)
    cos, sa, sb = cos_ref[...], sa_ref[...], sb_ref[...]

    def rope(t):
        blocks = []
        for j in range(t.shape[1] // LANES):
            tb = t[:, j * LANES:(j + 1) * LANES]
            blocks.append(tb * cos + pltpu.roll(tb, LANES - ROT_DIM // 2, 1) * sa
                          + pltpu.roll(tb, ROT_DIM // 2, 1) * sb)
        return jnp.concatenate(blocks, axis=1)

    k = rope(_dot(hkv, wk_ref[...]))
    v = _dot(hkv, wv_ref[...])
    q = rope(_dot(hq, wq_ref[...]))
    tm = k.shape[0]
    for h in range(N_HEADS):
        rows = pl.ds(h, tm, stride=N_HEADS)
        k_ref[rows, :] = k[:, h * LANES:(h + 1) * LANES]
        v_ref[rows, :] = v[:, h * LANES:(h + 1) * LANES]
    if transposed:
        kb_ref, vt_ref, qt_ref = extra_refs
        kb_ref[...] = k.astype(BF16)
        vt_ref[0] = v.T.astype(BF16)
        qt_ref[0] = (q * (DK ** -0.5 * LOG2E)).T.astype(BF16)
    else:
        (q_ref,) = extra_refs
        q_ref[...] = (q * (DK ** -0.5)).astype(BF16)


def _rope_tables(pos):
    half = ROT_DIM // 2
    inv_freq = ROPE_THETA ** (-jnp.arange(half, dtype=F32) / half)
    ang = pos.astype(F32)[:, None] * inv_freq[None, :]
    cos, sin = jnp.cos(ang), jnp.sin(ang)
    n = pos.shape[0]
    ones = jnp.ones((n, DK - ROT_DIM), F32)
    zeros_h = jnp.zeros((n, half), F32)
    zeros_r = jnp.zeros((n, DK - ROT_DIM), F32)
    cos_t = jnp.concatenate([cos, cos, ones], axis=1)
    sa_t = jnp.concatenate([-sin, zeros_h, zeros_r], axis=1)
    sb_t = jnp.concatenate([zeros_h, sin, zeros_r], axis=1)
    rep = lambda t: jnp.tile(t, (1, LANES // DK))
    return rep(cos_t), rep(sa_t), rep(sb_t)


def qkv_proj(x, kvw, aw, wk, wv, wq, tables, *, tm, streams, transposed):
    t, d = x.shape
    period = tables[0].shape[0] // tm
    row = lambda i: (i, 0)
    const = lambda i: (0, 0)
    tab = lambda i: (i % period, 0)
    heads_out = jax.ShapeDtypeStruct((t * N_HEADS, LANES), F32)
    heads_spec = pl.BlockSpec((tm * N_HEADS, LANES), row)
    out_specs = [heads_spec, heads_spec, pl.BlockSpec((tm, d), row)]
    out_shape = [heads_out, heads_out, jax.ShapeDtypeStruct((t, d), BF16)]
    if transposed:
        per = t // streams // tm
        tr_spec = pl.BlockSpec((1, d, tm), lambda i: (i // per, 0, i % per))
        tr_out = jax.ShapeDtypeStruct((streams, d, t // streams), BF16)
        out_specs += [tr_spec, tr_spec]
        out_shape += [tr_out, tr_out]
    return pl.pallas_call(
        functools.partial(_qkv_kernel, transposed=transposed),
        grid=(t // tm,),
        in_specs=[pl.BlockSpec((tm, d), row),
                  pl.BlockSpec((1, d), const), pl.BlockSpec((1, d), const),
                  pl.BlockSpec((d, d), const), pl.BlockSpec((d, d), const), pl.BlockSpec((d, d), const),
                  pl.BlockSpec((tm, LANES), tab), pl.BlockSpec((tm, LANES), tab),
                  pl.BlockSpec((tm, LANES), tab)],
        out_specs=out_specs,
        out_shape=out_shape,
        compiler_params=_params("arbitrary"),
        name="qkv_proj",
    )(x, kvw, aw, wk, wv, wq, *tables)


def _lambda_value(lp_ref, lambda_init):
    lp = lp_ref[...]
    s1 = jnp.sum(lp[0:1] * lp[1:2], axis=-1, keepdims=True)
    s2 = jnp.sum(lp[2:3] * lp[3:4], axis=-1, keepdims=True)
    return jnp.exp(s1) - jnp.exp(s2) + lambda_init


def _diff_finish(o1, o2, lam, subw, lambda_init):
    o = o1 - lam * o2
    return _rms_scale(o) * subw * (1.0 - lambda_init)


def _attn_kernel(qt_ref, kt_ref, q_ref, k_ref, v_ref, lp_ref, subw_ref, o_ref,
                 qs_ref, *state_refs, tq, tk, lambda_init):
    p = pl.program_id(1)
    qi = qt_ref[p]
    ki = kt_ref[p]
    n_qc = 2 * tq // Q_COLS
    m_refs, acc_refs = state_refs[:n_qc], state_refs[n_qc:]

    @pl.when(ki == 0)
    def _():
        for m_ref, acc_ref in zip(m_refs, acc_refs):
            m_ref[...] = jnp.full(m_ref.shape, NEG_BIG, F32)
            acc_ref[...] = jnp.zeros(acc_ref.shape, F32)
        first_sub = lax.broadcasted_iota(jnp.int32, (LANES, tq), 0) < DK
        for h in range(N_HEADS):
            qh = q_ref[0, h * LANES:(h + 1) * LANES, :]
            zero = jnp.zeros_like(qh)
            qs_ref[h, :, 0:tq] = jnp.where(first_sub, qh, zero)
            qs_ref[h, :, tq:2 * tq] = jnp.where(first_sub, zero, qh)

    ones = jnp.ones((ONES_ROWS, tk), BF16)

    def sweep(masked):
        if masked:
            k_chunk = (ki * tk + lax.broadcasted_iota(jnp.int32, (tk, Q_COLS), 0)) >> CHUNK_SHIFT
            q_lane = lax.broadcasted_iota(jnp.int32, (tk, Q_COLS), 1)

        def scores(h, c):
            kh = k_ref[0, :, h * LANES:(h + 1) * LANES]
            s = _dot(kh, qs_ref[h, :, c * Q_COLS:(c + 1) * Q_COLS])
            if masked:
                q_chunk = (qi * tq + (c * Q_COLS) % tq + q_lane) >> CHUNK_SHIFT
                s = jnp.where(k_chunk <= q_chunk, s, NEG_BIG)
            return s

        groups = [(h, c) for h in range(N_HEADS) for c in range(n_qc)]
        pending = [scores(*g) for g in groups[:SCORE_LOOKAHEAD]]
        for i, (h, c) in enumerate(groups):
            s = pending.pop(0)
            if i + SCORE_LOOKAHEAD < len(groups):
                pending.append(scores(*groups[i + SCORE_LOOKAHEAD]))
            vt = jnp.concatenate([v_ref[0, h * LANES:(h + 1) * LANES, :], ones], axis=0)
            m_prev = m_refs[c][h:h + 1, :]
            m_new = jnp.maximum(m_prev, jnp.max(s, axis=0, keepdims=True))
            alpha = jnp.exp2(m_prev - m_new)
            pr = jnp.exp2(s - m_new).astype(BF16)
            acc_refs[c][h] = acc_refs[c][h] * alpha + _dot(vt, pr)
            m_refs[c][h:h + 1, :] = m_new

    @pl.when(ki < qi)
    def _():
        sweep(False)

    @pl.when(ki == qi)
    def _():
        sweep(True)
        lam = _lambda_value(lp_ref, lambda_init)
        subw = subw_ref[...]

        for h in range(N_HEADS):
            hs = slice(h * LANES, (h + 1) * LANES)
            for r in range(tq // Q_COLS):
                a1 = acc_refs[r][h]
                a2 = acc_refs[r + tq // Q_COLS][h]
                o_t = a1[0:DV] / a1[DV:DV + 1] - lam * (a2[0:DV] / a2[DV:DV + 1])
                res = _rms_scale(o_t.T) * subw * (1.0 - lambda_init)
                o_ref[0, r * Q_COLS:(r + 1) * Q_COLS, hs] = res.astype(o_ref.dtype)


def diff_attention(q_t, k, v_t, lam_params, subw, *, tq, lambda_init):
    nb, d, length = q_t.shape
    tk = tq
    nq = length // tq
    qt = np.concatenate([np.full(i + 1, i) for i in range(nq)]).astype(np.int32)
    kt = np.concatenate([np.arange(i + 1) for i in range(nq)]).astype(np.int32)
    kern = functools.partial(_attn_kernel, tq=tq, tk=tk, lambda_init=lambda_init)
    n_qc = 2 * tq // Q_COLS
    grid_spec = pltpu.PrefetchScalarGridSpec(
        num_scalar_prefetch=2,
        grid=(nb, len(qt)),
        in_specs=[pl.BlockSpec((1, d, tq), lambda b, p, qt_r, kt_r: (b, 0, qt_r[p])),
                  pl.BlockSpec((1, tk, d), lambda b, p, qt_r, kt_r: (b, kt_r[p], 0)),
                  pl.BlockSpec((1, d, tk), lambda b, p, qt_r, kt_r: (b, 0, kt_r[p])),
                  pl.BlockSpec((4, DK), lambda b, p, qt_r, kt_r: (0, 0)),
                  pl.BlockSpec((1, DV), lambda b, p, qt_r, kt_r: (0, 0))],
        out_specs=pl.BlockSpec((1, tq, d), lambda b, p, qt_r, kt_r: (b, qt_r[p], 0)),
        scratch_shapes=([pltpu.VMEM((N_HEADS, LANES, 2 * tq), BF16)]
                        + [pltpu.VMEM((N_HEADS, Q_COLS), F32)] * n_qc
                        + [pltpu.VMEM((N_HEADS, DV + ONES_ROWS, Q_COLS), F32)] * n_qc),
    )
    return pl.pallas_call(
        kern,
        grid_spec=grid_spec,
        out_shape=jax.ShapeDtypeStruct((nb, length, d), BF16),
        compiler_params=_params("arbitrary", "arbitrary"),
        name="diff_attention",
    )(jnp.asarray(qt), jnp.asarray(kt), q_t, k, v_t, lam_params, subw)


def _attn_cached_kernel(q_ref, ck_ref, cv_ref, kn_ref, vn_ref, lp_ref, subw_ref, o_ref,
                        *, nq, lambda_init):
    past = ck_ref.shape[1] // N_HEADS
    lane = lax.broadcasted_iota(jnp.int32, (nq, LANES), 1)
    nt = (((1,), (1,)), ((), ()))
    pad = jnp.zeros((LANES - nq, LANES), BF16)
    lam = _lambda_value(lp_ref, lambda_init)
    for h in range(N_HEADS):
        q = q_ref[0, :, h * LANES:(h + 1) * LANES]
        zero = jnp.zeros_like(q)
        qs = jnp.concatenate([jnp.where(lane < DK, q, zero), jnp.where(lane < DK, zero, q)], axis=0)
        old_rows = pl.ds(h, past, stride=N_HEADS)
        new_rows = pl.ds(h, nq, stride=N_HEADS)
        kc = ck_ref[0, old_rows, :].astype(BF16)
        vc = cv_ref[0, old_rows, :].astype(BF16)
        kn = jnp.concatenate([kn_ref[0, new_rows, :].astype(BF16), pad], axis=0)
        vn = jnp.concatenate([vn_ref[0, new_rows, :].astype(BF16), pad], axis=0)
        s_c = lax.dot_general(qs, kc, nt, preferred_element_type=F32)
        s_n = lax.dot_general(qs, kn, nt, preferred_element_type=F32)
        col = lax.broadcasted_iota(jnp.int32, s_n.shape, 1)
        s_n = jnp.where(col < nq, s_n, NEG_BIG)
        m = jnp.maximum(jnp.max(s_c, axis=-1, keepdims=True), jnp.max(s_n, axis=-1, keepdims=True))
        p_c = jnp.exp(s_c - m)
        p_n = jnp.exp(s_n - m)
        denom = jnp.sum(p_c, axis=-1, keepdims=True) + jnp.sum(p_n, axis=-1, keepdims=True)
        o = (_dot(p_c.astype(BF16), vc) + _dot(p_n.astype(BF16), vn)) / denom
        res = _diff_finish(o[0:nq], o[nq:2 * nq], lam, subw_ref[...], lambda_init)
        o_ref[0, :, h * LANES:(h + 1) * LANES] = res.astype(o_ref.dtype)


def diff_attention_cached(q, cache_k, cache_v, k_new, v_new, lam_params, subw, *, lambda_init):
    nb, nq, d = q.shape
    past = cache_k.shape[1]
    kern = functools.partial(_attn_cached_kernel, nq=nq, lambda_init=lambda_init)
    stream = lambda b: (b, 0, 0)
    const = lambda b: (0, 0)
    rows = lambda a: a.reshape(nb, a.shape[1] * N_HEADS, LANES)
    cache_k, cache_v, k_new, v_new = rows(cache_k), rows(cache_v), rows(k_new), rows(v_new)
    return pl.pallas_call(
        kern,
        grid=(nb,),
        in_specs=[pl.BlockSpec((1, nq, d), lambda b: (b, 0, 0)),
                  pl.BlockSpec((1, past * N_HEADS, LANES), stream),
                  pl.BlockSpec((1, past * N_HEADS, LANES), stream),
                  pl.BlockSpec((1, nq * N_HEADS, LANES), stream),
                  pl.BlockSpec((1, nq * N_HEADS, LANES), stream),
                  pl.BlockSpec((4, DK), const),
                  pl.BlockSpec((1, DV), const)],
        out_specs=pl.BlockSpec((1, nq, d), lambda b: (b, 0, 0)),
        out_shape=jax.ShapeDtypeStruct((nb, nq, d), BF16),
        compiler_params=_params("arbitrary"),
        name="diff_attention_cached",
    )(q, cache_k, cache_v, k_new, v_new, lam_params, subw)


def _route(logits):
    lane = lax.broadcasted_iota(jnp.int32, logits.shape, 1)
    valid = lane < N_EXPERTS
    lg = jnp.where(valid, logits, NEG_BIG)
    m1 = jnp.max(lg, axis=-1, keepdims=True)
    lane_f = lane.astype(F32)
    i1 = jnp.min(jnp.where(lg == m1, lane_f, float(LANES)), axis=-1, keepdims=True)
    lg2 = jnp.where(lane_f == i1, NEG_BIG, lg)
    m2 = jnp.max(lg2, axis=-1, keepdims=True)
    i2 = jnp.min(jnp.where(lg2 == m2, lane_f, float(LANES)), axis=-1, keepdims=True)
    e2 = jnp.exp(m2 - m1)
    g1 = 1.0 / (1.0 + e2)
    g2 = e2 / (1.0 + e2)
    return jnp.where(lane_f == i1, g1, 0.0) + jnp.where(lane_f == i2, g2, 0.0)


def _moe_route_kernel(x_ref, nw_ref, wr_ref, h_ref, comb_ref, rk_ref, rkt_ref, cnt_ref, *, rt):
    t = x_ref.shape[0]
    hf = _rms_scale(x_ref[...]) * nw_ref[...]
    h_ref[...] = hf.astype(BF16)
    comb = _route(_dot_exact_rhs_general(hf, wr_ref[...]))
    comb_ref[...] = comb
    sel = jnp.where(comb > 0.0, 1.0, 0.0)
    row = lax.broadcasted_iota(jnp.int32, (rt, rt), 0)
    col = lax.broadcasted_iota(jnp.int32, (rt, rt), 1)
    strict = jnp.where(col < row, 1.0, 0.0).astype(BF16)
    carry = jnp.zeros((1, LANES), F32)
    ranks = []
    for s in range(t // rt):
        sel_s = sel[s * rt:(s + 1) * rt]
        rank_s = _dot(strict, sel_s.astype(BF16)) + carry
        ranks.append(jnp.where(sel_s > 0.0, rank_s, -1.0))
        carry = carry + jnp.sum(sel_s, axis=0, keepdims=True)
    rk = jnp.concatenate(ranks, axis=0) if len(ranks) > 1 else ranks[0]
    rk_ref[...] = rk.astype(jnp.int32)
    rkt_ref[0] = rk.T.astype(jnp.int32)
    cnt_ref[0] = jnp.broadcast_to(carry, (8, LANES)).astype(jnp.int32)


def moe_route(x, nw, wr, *, tb, rt):
    t, d = x.shape
    nb = t // tb
    row = lambda b: (b, 0)
    const = lambda b: (0, 0)
    return pl.pallas_call(
        functools.partial(_moe_route_kernel, rt=rt),
        grid=(nb,),
        in_specs=[pl.BlockSpec((tb, d), row), pl.BlockSpec((1, d), const),
                  pl.BlockSpec((d, LANES), const)],
        out_specs=[pl.BlockSpec((tb, d), row), pl.BlockSpec((tb, LANES), row),
                   pl.BlockSpec((tb, LANES), row),
                   pl.BlockSpec((1, LANES, tb), lambda b: (b, 0, 0)),
                   pl.BlockSpec((1, 8, LANES), lambda b: (b, 0, 0))],
        out_shape=[jax.ShapeDtypeStruct((t, d), BF16),
                   jax.ShapeDtypeStruct((t, LANES), F32),
                   jax.ShapeDtypeStruct((t, LANES), jnp.int32),
                   jax.ShapeDtypeStruct((nb, LANES, tb), jnp.int32),
                   jax.ShapeDtypeStruct((nb, 8, LANES), jnp.int32)],
        compiler_params=_params("arbitrary"),
        name="moe_route",
    )(x, nw, wr)


def _moe_expert_kernel(cnt_ref, x_ref, h_ref, comb_ref, rk_ref, rkt_ref, wg_ref, wu_ref, wd_ref,
                       fw_ref, o_ref, rkc_ref, xg_ref, gs_ref, yacc_ref, *, rt):
    b = pl.program_id(0)
    e = pl.program_id(1)
    f = pl.program_id(2)
    t = x_ref.shape[0]
    n_rows = cnt_ref[b * N_EXPERTS + e]
    half = rt // 2
    n_tiles = (n_rows + (half - 1)) // rt
    tail_row0 = pl.multiple_of(n_tiles * rt, rt)
    has_tail = n_rows > n_tiles * rt

    def over_tiles(fn):
        def body(j, carry):
            fn(pl.multiple_of(j * rt, rt), rt)
            return carry

        lax.fori_loop(0, n_tiles, body, 0)

        @pl.when(has_tail)
        def _():
            fn(tail_row0, half)

    @pl.when(jnp.logical_and(e == 0, f == 0))
    def _():
        o_ref[...] = x_ref[...]

    @pl.when(f == 0)
    def _():
        lane = lax.broadcasted_iota(jnp.int32, (t, LANES), 1)
        pick = lane == e
        rkc = jnp.sum(jnp.where(pick, rk_ref[...], 0).astype(F32), axis=-1, keepdims=True)
        rkc_ref[...] = jnp.broadcast_to(rkc, (t, LANES)).astype(jnp.int32)
        gate = jnp.sum(jnp.where(pick, comb_ref[...], 0.0), axis=-1, keepdims=True)
        g_hi, g_mid, g_lo = _split3(jnp.broadcast_to(gate, (t, LANES)))
        g3 = jnp.where(lane == 0, g_hi.astype(F32),
                       jnp.where(lane == 1, g_mid.astype(F32),
                                 jnp.where(lane == 2, g_lo.astype(F32), 0.0))).astype(BF16)
        rk_row = rkt_ref[0, pl.ds(e, 1), :]

        def gather(row0, size):
            tile = pl.ds(row0, size)
            rows = lax.broadcasted_iota(jnp.int32, (size, t), 0) + row0
            p = jnp.where(rk_row == rows, 1.0, 0.0).astype(BF16)
            xg_ref[tile, :] = _dot(p, h_ref[...]).astype(BF16)
            gate_rows = jnp.sum(_dot(p, g3), axis=-1, keepdims=True)
            gs_ref[tile, :] = jnp.broadcast_to(gate_rows, (size, LANES))
            yacc_ref[tile, :] = jnp.zeros((size, yacc_ref.shape[1]), F32)

        over_tiles(gather)

    def expert(row0, size):
        tile = pl.ds(row0, size)
        xg = xg_ref[tile, :]
        g = _dot(xg, wg_ref[...])
        u = _dot(xg, wu_ref[...])
        gates = _lane_tile(gs_ref[tile, :], g.shape[1] // LANES)
        act = (g * _sigmoid(g) * u * gates).astype(BF16)
        yacc_ref[tile, :] += _dot(act, wd_ref[...])

    over_tiles(expert)

    @pl.when(f == pl.num_programs(2) - 1)
    def _():
        def scatter(row0, size):
            cols = lax.broadcasted_iota(jnp.int32, (t, size), 1) + row0
            s = jnp.where(_lane_tile(rkc_ref[...], size // LANES) == cols, 1.0, 0.0).astype(BF16)
            o_ref[...] += _dot(s, yacc_ref[pl.ds(row0, size), :].astype(BF16))

        over_tiles(scatter)

    @pl.when(jnp.logical_and(e == pl.num_programs(1) - 1, f == pl.num_programs(2) - 1))
    def _():
        o_ref[...] = _rms_scale(o_ref[...]) * fw_ref[...]


def _dot_exact_rhs_general(x, w):
    xh, xm, xl = _split3(x)
    wh, wm, wl = _split3(w)
    return (_dot(xh, wh) + (_dot(xh, wm) + _dot(xm, wh))
            + (_dot(xh, wl) + _dot(xm, wm) + _dot(xl, wh)))


def moe_ffn_final(x, nw, wr, wg, wu, wd, fw, *, tb, rt, tf):
    t, d = x.shape
    ne, _, ff = wg.shape
    nb = t // tb
    h, comb, rk, rkt, cnt = moe_route(x, nw, wr, tb=tb, rt=rt)
    counts = cnt[:, 0, :ne].reshape(nb * ne)
    once = pl.Buffered(1)
    blk = lambda b, e, f, c: (b, 0)
    const = lambda b, e, f, c: (0, 0)
    grid_spec = pltpu.PrefetchScalarGridSpec(
        num_scalar_prefetch=1,
        grid=(nb, ne, ff // tf),
        in_specs=[pl.BlockSpec((tb, d), blk, pipeline_mode=once),
                  pl.BlockSpec((tb, d), blk, pipeline_mode=once),
                  pl.BlockSpec((tb, LANES), blk, pipeline_mode=once),
                  pl.BlockSpec((tb, LANES), blk, pipeline_mode=once),
                  pl.BlockSpec((1, LANES, tb), lambda b, e, f, c: (b, 0, 0), pipeline_mode=once),
                  pl.BlockSpec((None, d, tf), lambda b, e, f, c: (e, 0, f)),
                  pl.BlockSpec((None, d, tf), lambda b, e, f, c: (e, 0, f)),
                  pl.BlockSpec((None, tf, d), lambda b, e, f, c: (e, f, 0)),
                  pl.BlockSpec((1, d), const)],
        out_specs=pl.BlockSpec((tb, d), blk),
        scratch_shapes=[pltpu.VMEM((tb, LANES), jnp.int32),
                        pltpu.VMEM((tb, d), BF16),
                        pltpu.VMEM((tb, LANES), F32),
                        pltpu.VMEM((tb, d), F32)],
    )
    return pl.pallas_call(
        functools.partial(_moe_expert_kernel, rt=rt),
        grid_spec=grid_spec,
        out_shape=jax.ShapeDtypeStruct((t, d), F32),
        compiler_params=_params("arbitrary", "arbitrary", "arbitrary"),
        name="moe_experts",
    )(counts, x, h, comb, rk, rkt, wg, wu, wd, fw)


MOE_ROW_ALIGN = 16
MOE_ROW_TILE = 128
MOE_GROUP_TILE = 1024


def _moe_plan(cnt, *, tb, region):
    nb, ne = cnt.shape
    padded = (cnt + (MOE_ROW_ALIGN - 1)) // MOE_ROW_ALIGN * MOE_ROW_ALIGN
    off = jnp.cumsum(padded, axis=0) - padded
    total = jnp.sum(padded, axis=0)
    tiles = (total + MOE_ROW_TILE + MOE_GROUP_TILE - 1) // MOE_GROUP_TILE
    tile_end = jnp.cumsum(tiles)
    max_tiles = (2 * nb * tb + nb * ne * MOE_ROW_ALIGN + ne * MOE_ROW_TILE) // MOE_GROUP_TILE + ne + 1
    idx = jnp.arange(max_tiles, dtype=jnp.int32)
    last = jnp.maximum(tile_end[-1] - 1, 0)
    active = idx < tile_end[-1]
    ii = jnp.minimum(idx, last)
    te = jnp.searchsorted(tile_end, ii, side="right").astype(jnp.int32)
    k = ii - (tile_end - tiles)[te]
    tr = te * (region // MOE_GROUP_TILE) + k
    valid = jnp.clip(total[te] - k * MOE_GROUP_TILE, 0, MOE_GROUP_TILE)
    tv = jnp.where(active, valid, -1)
    return (off.reshape(-1).astype(jnp.int32), te, tr.astype(jnp.int32), tv.astype(jnp.int32))


def _moe_gather_kernel(cnt_ref, off_ref, h_ref, rkt_ref, xs_ref, buf_ref, sem_ref, *, region):
    b = pl.program_id(0)
    t = h_ref.shape[0]
    rt = MOE_ROW_TILE

    @pl.when(b == 0)
    def _():
        buf_ref[...] = jnp.zeros(buf_ref.shape, BF16)

    def copy(e, row0):
        return pltpu.make_async_copy(buf_ref.at[e], xs_ref.at[pl.ds(row0, t)], sem_ref.at[e])

    for e in range(N_EXPERTS):
        n = cnt_ref[b * N_EXPERTS + e]
        row0 = pl.multiple_of(e * region + off_ref[b * N_EXPERTS + e], MOE_ROW_ALIGN)

        @pl.when(b > 0)
        def _():
            copy(e, row0).wait()

        rk_row = rkt_ref[0, e:e + 1, :]

        def gather(j, carry):
            r0 = pl.multiple_of(j * rt, rt)
            rows = lax.broadcasted_iota(jnp.int32, (rt, t), 0) + r0
            p = jnp.where(rk_row == rows, 1.0, 0.0).astype(BF16)
            buf_ref[e, pl.ds(r0, rt), :] = _dot(p, h_ref[...]).astype(BF16)
            return carry

        lax.fori_loop(0, (n + (rt - 1)) // rt, gather, 0)
        copy(e, row0).start()

    @pl.when(b == pl.num_programs(0) - 1)
    def _():
        for e in range(N_EXPERTS):
            copy(e, e * region).wait()


def _moe_grouped_kernel(te_ref, tr_ref, tv_ref, x_ref, wg_ref, wu_ref, wd_ref, y_ref, xm_ref, acc_ref):
    i = pl.program_id(0)
    f = pl.program_id(1)
    valid = tv_ref[i]
    last_f = pl.num_programs(1) - 1

    @pl.when(valid > 0)
    def _():
        @pl.when(f == 0)
        def _():
            row = lax.broadcasted_iota(jnp.int32, x_ref.shape, 0)
            xm_ref[...] = jnp.where(row < valid, x_ref[...].astype(F32), 0.0).astype(BF16)
            acc_ref[...] = jnp.zeros(acc_ref.shape, F32)

        xm = xm_ref[...]
        g = _dot(xm, wg_ref[...])
        u = _dot(xm, wu_ref[...])
        acc_ref[...] += _dot((g * _sigmoid(g) * u).astype(BF16), wd_ref[...])

        @pl.when(f == last_f)
        def _():
            y_ref[...] = acc_ref[...].astype(y_ref.dtype)

    @pl.when(jnp.logical_and(valid == 0, f == last_f))
    def _():
        y_ref[...] = jnp.zeros(y_ref.shape, y_ref.dtype)


def _moe_combine_kernel(cnt_ref, off_ref, x_ref, comb_ref, rk_ref, ys_ref, fw_ref, o_ref,
                        buf_ref, sem_ref, *, region):
    b = pl.program_id(0)
    t = x_ref.shape[0]
    rt = MOE_ROW_TILE

    def copy(e):
        row0 = pl.multiple_of(e * region + off_ref[b * N_EXPERTS + e], MOE_ROW_ALIGN)
        return pltpu.make_async_copy(ys_ref.at[pl.ds(row0, t)], buf_ref.at[e], sem_ref.at[e])

    for e in range(N_EXPERTS):
        copy(e).start()
    o_ref[...] = x_ref[...]
    lane = lax.broadcasted_iota(jnp.int32, (t, LANES), 1)
    for e in range(N_EXPERTS):
        n = cnt_ref[b * N_EXPERTS + e]
        copy(e).wait()
        pick = lane == e
        rkc = jnp.sum(jnp.where(pick, rk_ref[...], 0).astype(F32), axis=-1, keepdims=True)
        rkc = jnp.broadcast_to(rkc, (t, LANES)).astype(jnp.int32)
        gate = jnp.broadcast_to(jnp.sum(jnp.where(pick, comb_ref[...], 0.0), axis=-1, keepdims=True),
                                (t, LANES))
        g_hi = gate.astype(BF16).astype(F32)
        g_lo = gate - g_hi

        def scatter(j, carry):
            r0 = pl.multiple_of(j * rt, rt)
            hit = rkc == lax.broadcasted_iota(jnp.int32, (t, rt), 1) + r0
            s = jnp.concatenate([jnp.where(hit, g_hi, 0.0), jnp.where(hit, g_lo, 0.0)], axis=1).astype(BF16)
            y = buf_ref[e, pl.ds(r0, rt), :]
            o_ref[...] += _dot(s, jnp.concatenate([y, y], axis=0))
            return carry

        lax.fori_loop(0, (n + (rt - 1)) // rt, scatter, 0)
    o_ref[...] = _rms_scale(o_ref[...]) * fw_ref[...]


def moe_ffn_final_sorted(x, nw, wr, wg, wu, wd, fw, *, tb, tf):
    t, d = x.shape
    ne, _, ff = wg.shape
    nb = t // tb
    st = MOE_GROUP_TILE
    region = -(-(tb * nb + MOE_ROW_ALIGN * nb + tb) // st) * st
    h, comb, rk, rkt, cnt = moe_route(x, nw, wr, tb=tb, rt=tb)
    counts = cnt[:, 0, :ne]
    off, te, tr, tv = _moe_plan(counts, tb=tb, region=region)
    counts = counts.reshape(nb * ne)
    n_tiles = te.shape[0]

    xs = pl.pallas_call(
        functools.partial(_moe_gather_kernel, region=region),
        grid_spec=pltpu.PrefetchScalarGridSpec(
            num_scalar_prefetch=2,
            grid=(nb,),
            in_specs=[pl.BlockSpec((tb, d), lambda b, c, o: (b, 0)),
                      pl.BlockSpec((1, LANES, tb), lambda b, c, o: (b, 0, 0))],
            out_specs=pl.BlockSpec(memory_space=pl.ANY),
            scratch_shapes=[pltpu.VMEM((ne, tb, d), BF16), pltpu.SemaphoreType.DMA((ne,))],
        ),
        out_shape=jax.ShapeDtypeStruct((ne * region, d), BF16),
        compiler_params=_params("arbitrary"),
        name="moe_gather",
    )(counts, off, h, rkt)

    last_f = ff // tf - 1
    used_f = lambda i, f, te_r, tr_r, tv_r: jnp.where(tv_r[i] > 0, f, last_f)
    ys = pl.pallas_call(
        _moe_grouped_kernel,
        grid_spec=pltpu.PrefetchScalarGridSpec(
            num_scalar_prefetch=3,
            grid=(n_tiles, ff // tf),
            in_specs=[pl.BlockSpec((st, d), lambda i, f, te_r, tr_r, tv_r: (tr_r[i], 0)),
                      pl.BlockSpec((None, d, tf), lambda i, f, te_r, tr_r, tv_r:
                                   (te_r[i], 0, used_f(i, f, te_r, tr_r, tv_r))),
                      pl.BlockSpec((None, d, tf), lambda i, f, te_r, tr_r, tv_r:
                                   (te_r[i], 0, used_f(i, f, te_r, tr_r, tv_r))),
                      pl.BlockSpec((None, tf, d), lambda i, f, te_r, tr_r, tv_r:
                                   (te_r[i], used_f(i, f, te_r, tr_r, tv_r), 0))],
            out_specs=pl.BlockSpec((st, d), lambda i, f, te_r, tr_r, tv_r: (tr_r[i], 0)),
            scratch_shapes=[pltpu.VMEM((st, d), BF16), pltpu.VMEM((st, d), F32)],
        ),
        out_shape=jax.ShapeDtypeStruct((ne * region, d), BF16),
        compiler_params=_params("arbitrary", "arbitrary"),
        name="moe_grouped",
    )(te, tr, tv, xs, wg, wu, wd)

    return pl.pallas_call(
        functools.partial(_moe_combine_kernel, region=region),
        grid_spec=pltpu.PrefetchScalarGridSpec(
            num_scalar_prefetch=2,
            grid=(nb,),
            in_specs=[pl.BlockSpec((tb, d), lambda b, c, o: (b, 0)),
                      pl.BlockSpec((tb, LANES), lambda b, c, o: (b, 0)),
                      pl.BlockSpec((tb, LANES), lambda b, c, o: (b, 0)),
                      pl.BlockSpec(memory_space=pl.ANY),
                      pl.BlockSpec((1, d), lambda b, c, o: (0, 0))],
            out_specs=pl.BlockSpec((tb, d), lambda b, c, o: (b, 0)),
            scratch_shapes=[pltpu.VMEM((ne, tb, d), BF16), pltpu.SemaphoreType.DMA((ne,))],
        ),
        out_shape=jax.ShapeDtypeStruct((t, d), F32),
        compiler_params=_params("arbitrary"),
        name="moe_combine",
    )(counts, off, x, comb, rk, ys, fw)


def _row(v):
    return v.reshape(1, -1).astype(F32)


def _pad_lanes(v, width):
    v = _row(v)
    return jnp.pad(v, ((0, 0), (0, width - v.shape[1])))


def _trunk(x, pos, conv_in, state_in, past_k, past_v, W, *, ssd_chunk, ssd_valid, tm, tq):
    nb, length, d = x.shape
    t = nb * length
    x0 = x.reshape(t, d)

    proj, dt_raw = rms_matmul(x0, W["mamba_norm_w"], W["w_in"], W["w_dt"], tm=min(t, 1024), tn=1024)
    proj3 = proj.reshape(nb, length, D_INNER + CONV_DIM)
    new_conv = proj3[:, length - (CONV_W - 1):, D_INNER:D_INNER + CONV_DIM].astype(F32)
    conv8 = jnp.pad(conv_in, ((0, 0), (8 - (CONV_W - 1), 0), (0, 0)))
    y, state_out = ssd(proj3, dt_raw.reshape(nb, length, DT_PAD), conv8,
                       state_in.reshape(nb, D_INNER, D_STATE),
                       W["conv_w"], W["conv_b"], W["dt_bias"], W["a_log"], W["d_exp"], W["gn_w"],
                       c=ssd_chunk, cv=ssd_valid)
    new_ssm = state_out.reshape(nb, SSM_HEADS, SSM_HEAD_DIM, D_STATE)
    x1 = matmul_residual(y.reshape(t, D_INNER), W["w_out"], x0, tm=tm)

    x2 = swiglu_ffn(x1, W["ffn_norm_w"], W["ffn_wg"], W["ffn_wu"], W["ffn_wd"], tm=tm, tf=1408)

    tables = _rope_tables(pos)
    proj_args = (x2, W["kv_norm_w"], W["attn_norm_w"], W["w_k"], W["w_v"], W["w_q"], tables)
    lambda_init = 0.8 - 0.6 * math.exp(-0.3 * 1)
    if past_k is None:
        k, v, kb, v_t, q_t = qkv_proj(*proj_args, tm=tm, streams=nb, transposed=True)
        o = diff_attention(q_t, kb.reshape(nb, length, d), v_t,
                           W["lam"], W["subln_w"], tq=tq, lambda_init=lambda_init)
    else:
        k, v, q = qkv_proj(*proj_args, tm=tm, streams=nb, transposed=False)
        q3 = q.reshape(nb, length, d)
        o = diff_attention_cached(q3, past_k, past_v,
                                  k.reshape(nb, length, N_HEADS, LANES),
                                  v.reshape(nb, length, N_HEADS, LANES),
                                  W["lam"], W["subln_w"], lambda_init=lambda_init)
    x3 = matmul_residual(o.reshape(t, d), W["w_o"], x2, tm=tm)

    moe_args = (x3, W["moe_norm_w"], W["moe_wr"], W["moe_wg"], W["moe_wu"], W["moe_wd"], W["final_norm_w"])
    if t >= 4 * MOE_GROUP_TILE:
        yout = moe_ffn_final_sorted(*moe_args, tb=512, tf=512)
    else:
        yout = moe_ffn_final(*moe_args, tb=t, rt=256, tf=512)
    return (yout.reshape(nb, length, d), new_conv[None], new_ssm[None],
            k.reshape(nb, length, N_HEADS, 2 * DK), v.reshape(nb, length, N_HEADS, DV))


def kernel(x_prompt, x_sample, cache_conv, state_ssm, cache_k, cache_v, mamba_norm_w, mamba_w_in, mamba_conv_w, mamba_conv_b, mamba_dt_bias, mamba_a_log, mamba_d, mamba_gn_w, mamba_w_out, kv_norm_w, w_k, w_v, attn_norm_w, w_q, lambda_q1, lambda_k1, lambda_q2, lambda_k2, subln_w, w_o, ffn_norm_w, ffn_w_gate, ffn_w_up, ffn_w_down, moe_norm_w, moe_w_router, moe_w_gate, moe_w_up, moe_w_down, final_norm_w):
    w_in = mamba_w_in[0]
    n_dt = w_in.shape[1] - D_INNER - CONV_DIM
    W = dict(
        mamba_norm_w=_row(mamba_norm_w[0]),
        w_in=w_in[:, :D_INNER + CONV_DIM].astype(BF16),
        w_dt=jnp.pad(w_in[:, D_INNER + CONV_DIM:], ((0, 0), (0, DT_PAD - n_dt))).astype(BF16),
        conv_w=mamba_conv_w[0].astype(F32),
        conv_b=_row(mamba_conv_b[0]),
        dt_bias=_pad_lanes(mamba_dt_bias[0], LANES),
        a_log=_pad_lanes(mamba_a_log[0], LANES),
        d_exp=_row(jnp.repeat(mamba_d[0], SSM_HEAD_DIM)),
        gn_w=_row(mamba_gn_w[0]),
        w_out=mamba_w_out[0].astype(BF16),
        kv_norm_w=_row(kv_norm_w), w_k=w_k.astype(BF16), w_v=w_v.astype(BF16),
        attn_norm_w=_row(attn_norm_w[0]), w_q=w_q[0].astype(BF16),
        lam=jnp.stack([lambda_q1[0], lambda_k1[0], lambda_q2[0], lambda_k2[0]]).astype(F32),
        subln_w=_row(subln_w[0]), w_o=w_o[0].astype(BF16),
        ffn_norm_w=_row(ffn_norm_w[0]),
        ffn_wg=ffn_w_gate[0].astype(BF16), ffn_wu=ffn_w_up[0].astype(BF16),
        ffn_wd=ffn_w_down[0].astype(BF16),
        moe_norm_w=_row(moe_norm_w[0]),
        moe_wr=jnp.pad(moe_w_router[0].astype(F32), ((0, 0), (0, LANES - N_EXPERTS))),
        moe_wg=moe_w_gate[0].astype(BF16), moe_wu=moe_w_up[0].astype(BF16),
        moe_wd=moe_w_down[0].astype(BF16),
        final_norm_w=_row(final_norm_w),
    )
    bp, lp = x_prompt.shape[0], x_prompt.shape[1]
    bs, ls = x_sample.shape[0], x_sample.shape[1]
    past = cache_k.shape[1]

    conv0 = jnp.zeros((bp, CONV_W - 1, CONV_DIM), F32)
    ssm0 = jnp.zeros((bp, SSM_HEADS, SSM_HEAD_DIM, D_STATE), F32)
    y_p, conv_p, ssm_p, k_p, v_p = _trunk(
        x_prompt, jnp.arange(lp, dtype=jnp.int32), conv0, ssm0, None, None, W,
        ssd_chunk=128, ssd_valid=128, tm=512, tq=512)

    pos_s = jnp.tile(past + jnp.arange(ls, dtype=jnp.int32), bs)
    y_s, conv_s, ssm_s, k_s, v_s = _trunk(
        x_sample, pos_s, cache_conv[0], state_ssm[0], cache_k, cache_v, W,
        ssd_chunk=128, ssd_valid=ls, tm=bs * ls, tq=None)
    return (y_p, y_s, conv_p, ssm_p, k_p, v_p, conv_s, ssm_s, k_s, v_s)
```

```python
import functools
import math

import numpy as np
import jax
import jax.numpy as jnp
from jax import lax
from jax.experimental import pallas as pl
from jax.experimental.pallas import tpu as pltpu

F32 = jnp.float32
BF16 = jnp.bfloat16

EPS = 1e-5
D_MODEL = 1024
D_INNER = 2048
SSM_HEAD_DIM = 64
SSM_HEADS = 32
SSM_GROUPS = 4
D_STATE = 128
GROUP_W = D_INNER // SSM_GROUPS
CONV_W = 4
CONV_HIST = 16
CONV_DIM = D_INNER + 2 * SSM_GROUPS * D_STATE
DT_PAD = 256
PROJ_W = D_INNER + CONV_DIM + DT_PAD
N_HEADS = 8
DK = 64
DV = 128
ROT_DIM = 16
ROPE_THETA = 500000.0
CHUNK = 64
CHUNK_SHIFT = 6
assert 1 << CHUNK_SHIFT == CHUNK
N_EXPERTS = 8
LANES = 128
NEG_BIG = -1e30
LOG2E = math.log2(math.e)
Q_COLS = 256
ONES_ROWS = 16
SCORE_LOOKAHEAD = 3
VMEM_LIMIT = 56 * 1024 * 1024


def _params(*sem):
    return pltpu.CompilerParams(dimension_semantics=sem, vmem_limit_bytes=VMEM_LIMIT)


def _sigmoid(x):
    return 1.0 / (1.0 + jnp.exp(-x))


def _rms_scale(x):
    return x * lax.rsqrt(jnp.mean(x * x, axis=-1, keepdims=True) + EPS)


def _split3(x):
    hi = x.astype(BF16)
    r1 = x - hi.astype(F32)
    mid = r1.astype(BF16)
    lo = (r1 - mid.astype(F32)).astype(BF16)
    return hi, mid, lo


def _dot(a, b):
    return jnp.dot(a, b, preferred_element_type=F32)


def _lane_tile(x, n):
    return jnp.concatenate([x] * n, axis=1) if n > 1 else x


def _dot_exact_rhs(x, m_bf16):
    hi, mid, lo = _split3(x)
    return _dot(hi, m_bf16) + _dot(mid, m_bf16) + _dot(lo, m_bf16)


def _dot_exact_lhs(m_bf16, x):
    hi, mid, lo = _split3(x)
    return _dot(m_bf16, hi) + _dot(m_bf16, mid) + _dot(m_bf16, lo)


def _rms_matmul_kernel(x_ref, nw_ref, w_ref, wdt_ref, o_ref, dt_ref, h_ref):
    @pl.when(pl.program_id(1) == 0)
    def _():
        h = (_rms_scale(x_ref[...]) * nw_ref[...]).astype(BF16)
        h_ref[...] = h
        dt_ref[...] = _dot(h, wdt_ref[...])

    o_ref[...] = _dot(h_ref[...], w_ref[...]).astype(o_ref.dtype)


def rms_matmul(x, nw, w, w_dt, *, tm, tn):
    t, d = x.shape
    n = w.shape[1]
    n_dt = w_dt.shape[1]
    return pl.pallas_call(
        _rms_matmul_kernel,
        grid=(t // tm, n // tn),
        in_specs=[pl.BlockSpec((tm, d), lambda i, j: (i, 0)),
                  pl.BlockSpec((1, d), lambda i, j: (0, 0)),
                  pl.BlockSpec((d, tn), lambda i, j: (0, j)),
                  pl.BlockSpec((d, n_dt), lambda i, j: (0, 0))],
        out_specs=[pl.BlockSpec((tm, tn), lambda i, j: (i, j)),
                   pl.BlockSpec((tm, n_dt), lambda i, j: (i, 0))],
        out_shape=[jax.ShapeDtypeStruct((t, n), BF16),
                   jax.ShapeDtypeStruct((t, n_dt), F32)],
        scratch_shapes=[pltpu.VMEM((tm, d), BF16)],
        compiler_params=_params("arbitrary", "arbitrary"),
        name="rms_matmul",
    )(x, nw, w, w_dt)


def _ssd_kernel(z_ref, xs_ref, b_ref, c_ref, dt_ref, cin_ref, sin_ref, cw_ref, cb_ref,
                dtb_ref, alog_ref, dexp_ref, gnw_ref, e_ref, shift_ref,
                y_ref, sout_ref, state_ref, ebuf_ref, *, c, cv):
    step = pl.program_id(1)
    hist = CONV_HIST

    @pl.when(step == 0)
    def _():
        state_ref[...] = sin_ref[0].T
        ebuf_ref[0:hist, :] = cin_ref[0, 0:hist, :]
        ebuf_ref[hist + c:2 * hist + c, :] = cin_ref[0, hist:2 * hist, :]

    ebuf_ref[hist:hist + cv, 0:D_INNER] = xs_ref[0]
    ebuf_ref[hist:hist + cv, D_INNER:D_INNER + GROUP_W] = b_ref[0]
    ebuf_ref[hist:hist + cv, D_INNER + GROUP_W:CONV_DIM] = c_ref[0]
    if cv < c:
        ebuf_ref[hist + cv:hist + c, :] = jnp.zeros((c - cv, CONV_DIM), BF16)

    shifted = _dot(shift_ref[...], ebuf_ref[...])
    conv = cb_ref[...] + ebuf_ref[hist:hist + c, :].astype(F32) * cw_ref[CONV_W - 1:CONV_W, :]
    for k in range(CONV_W - 1):
        conv = conv + shifted[k * c:(k + 1) * c, :] * cw_ref[k:k + 1, :]
    conv = conv * _sigmoid(conv)
    if cv == c:
        ebuf_ref[0:hist, :] = ebuf_ref[c:c + hist, :]
        ebuf_ref[hist + c:2 * hist + c, :] = jnp.zeros((hist, CONV_DIM), BF16)

    xs = conv[:, 0:D_INNER]
    bm = conv[:, D_INNER:D_INNER + GROUP_W]
    cm = conv[:, D_INNER + GROUP_W:CONV_DIM]

    dt_in = dt_ref[0][:, 0:LANES] + dtb_ref[...]
    dt = jnp.maximum(dt_in, 0.0) + jnp.log1p(jnp.exp(-jnp.abs(dt_in)))
    if cv < c:
        dt = jnp.concatenate([dt, jnp.zeros((c - cv, LANES), F32)], axis=0)
    a = -jnp.exp(alog_ref[...])
    dta = dt * a

    row = lax.broadcasted_iota(jnp.int32, (c, c), 0)
    col = lax.broadcasted_iota(jnp.int32, (c, c), 1)
    tril = col <= row
    tri = jnp.where(tril, 1.0, 0.0).astype(BF16)
    cs = _dot_exact_lhs(tri, dta)
    cs_t = cs.T

    e = e_ref[...]
    dt_x = _dot_exact_rhs(dt, e)
    cs_x = _dot_exact_rhs(cs, e)
    cs_end_x = cs_x[c - 1:c, :]
    xdt = xs * dt_x
    decay_in = jnp.exp(cs_x)
    xdt_b = xdt.astype(BF16)
    xdt_end = (xdt * jnp.exp(cs_end_x - cs_x)).astype(BF16)
    state_decay = jnp.exp(cs_end_x)

    lane = lax.broadcasted_iota(jnp.int32, (c, LANES), 1)
    low_half = lane < SSM_HEAD_DIM

    y_groups = []
    for g in range(SSM_GROUPS):
        gs = slice(g * GROUP_W, (g + 1) * GROUP_W)
        ns = slice(g * D_STATE, (g + 1) * D_STATE)
        bg_t = bm[:, ns].T.astype(BF16)
        cg = cm[:, ns].astype(BF16)
        cb = _dot(cg, bg_t)
        sg = state_ref[:, gs]
        y_off = _dot(cg, sg.astype(BF16)) * decay_in[:, gs]
        pairs = []
        for j in range(GROUP_W // LANES):
            h0 = g * (GROUP_W // SSM_HEAD_DIM) + 2 * j
            xp = xdt_b[:, g * GROUP_W + j * LANES:g * GROUP_W + (j + 1) * LANES]
            ys = []
            for h in (h0, h0 + 1):
                seg = cs[:, h:h + 1] - cs_t[h:h + 1, :]
                m = (cb * jnp.where(tril, jnp.exp(seg), 0.0)).astype(BF16)
                ys.append(_dot(m, xp))
            pairs.append(jnp.where(low_half, ys[0], ys[1]))
        y_groups.append(jnp.concatenate(pairs, axis=1) + y_off)
        state_ref[:, gs] = sg * state_decay[:, gs] + _dot(bg_t, xdt_end[:, gs])

    y = jnp.concatenate(y_groups, axis=1) + dexp_ref[...] * xs
    if cv < c:
        y = y[0:cv]
    zv = z_ref[0].astype(F32)
    y = y * (zv * _sigmoid(zv))
    outs = []
    for g in range(SSM_GROUPS):
        gs = slice(g * GROUP_W, (g + 1) * GROUP_W)
        outs.append(_rms_scale(y[:, gs]))
    y_ref[0] = (jnp.concatenate(outs, axis=1) * gnw_ref[...]).astype(y_ref.dtype)

    @pl.when(step == pl.num_programs(1) - 1)
    def _():
        sout_ref[0] = state_ref[...].T


def ssd(proj, dt_raw, conv_in, state_in, cw, cb, dtb, alog, dexp, gnw, *, c, cv):
    nb, length, _ = proj.shape
    steps = length // cv
    head_of_channel = np.arange(D_INNER) // SSM_HEAD_DIM
    expand = jnp.asarray(np.arange(LANES)[:, None] == head_of_channel[None, :], dtype=BF16)
    tap, pos = np.divmod(np.arange((CONV_W - 1) * c), c)
    src = pos + tap + CONV_HIST - (CONV_W - 1)
    cols = np.arange(c + 2 * CONV_HIST)[None, :]
    from_cache_low = (cols == (src + c + CONV_HIST)[:, None]) & ((pos + tap) < CONV_W - 1)[:, None]
    shift = jnp.asarray((cols == src[:, None]) | from_cache_low, dtype=BF16)
    const = lambda b, l: (0, 0)
    kern = functools.partial(_ssd_kernel, c=c, cv=cv)
    return pl.pallas_call(
        kern,
        grid=(nb, steps),
        in_specs=[
            pl.BlockSpec((1, cv, D_INNER), lambda b, l: (b, l, 0)),
            pl.BlockSpec((1, cv, D_INNER), lambda b, l: (b, l, 1)),
            pl.BlockSpec((1, cv, GROUP_W), lambda b, l: (b, l, 2 * D_INNER // GROUP_W)),
            pl.BlockSpec((1, cv, GROUP_W), lambda b, l: (b, l, 2 * D_INNER // GROUP_W + 1)),
            pl.BlockSpec((1, cv, DT_PAD), lambda b, l: (b, l, 0)),
            pl.BlockSpec((1, 2 * CONV_HIST, CONV_DIM), lambda b, l: (b, 0, 0)),
            pl.BlockSpec((1, D_INNER, D_STATE), lambda b, l: (b, 0, 0)),
            pl.BlockSpec((CONV_W, CONV_DIM), const),
            pl.BlockSpec((1, CONV_DIM), const),
            pl.BlockSpec((1, LANES), const),
            pl.BlockSpec((1, LANES), const),
            pl.BlockSpec((1, D_INNER), const),
            pl.BlockSpec((1, D_INNER), const),
            pl.BlockSpec((LANES, D_INNER), const),
            pl.BlockSpec(((CONV_W - 1) * c, c + 2 * CONV_HIST), const),
        ],
        out_specs=[
            pl.BlockSpec((1, cv, D_INNER), lambda b, l: (b, l, 0)),
            pl.BlockSpec((1, D_INNER, D_STATE), lambda b, l: (b, 0, 0)),
        ],
        out_shape=[
            jax.ShapeDtypeStruct((nb, length, D_INNER), BF16),
            jax.ShapeDtypeStruct((nb, D_INNER, D_STATE), F32),
        ],
        scratch_shapes=[pltpu.VMEM((D_STATE, D_INNER), F32),
                        pltpu.VMEM((c + 2 * CONV_HIST, CONV_DIM), BF16)],
        compiler_params=_params("arbitrary", "arbitrary"),
        name="ssd",
    )(proj, proj, proj, proj, dt_raw, conv_in, state_in, cw, cb, dtb, alog, dexp, gnw, expand, shift)


def _mm_res_kernel(a_ref, w_ref, r_ref, o_ref):
    o_ref[...] = r_ref[...] + _dot(a_ref[...], w_ref[...])


def matmul_residual(a, w, res, *, tm):
    t, k = a.shape
    n = w.shape[1]
    return pl.pallas_call(
        _mm_res_kernel,
        grid=(t // tm,),
        in_specs=[pl.BlockSpec((tm, k), lambda i: (i, 0)),
                  pl.BlockSpec((k, n), lambda i: (0, 0)),
                  pl.BlockSpec((tm, n), lambda i: (i, 0))],
        out_specs=pl.BlockSpec((tm, n), lambda i: (i, 0)),
        out_shape=jax.ShapeDtypeStruct((t, n), F32),
        compiler_params=_params("arbitrary"),
        name="matmul_residual",
    )(a, w, res)


def _ffn_kernel(x_ref, nw_ref, wg_ref, wu_ref, wd_ref, o_ref, h_ref, acc_ref):
    f = pl.program_id(1)

    @pl.when(f == 0)
    def _():
        x = x_ref[...]
        h_ref[...] = (_rms_scale(x) * nw_ref[...]).astype(BF16)
        acc_ref[...] = x

    h = h_ref[...]
    g = _dot(h, wg_ref[...])
    u = _dot(h, wu_ref[...])
    act = (g * _sigmoid(g) * u).astype(BF16)
    acc_ref[...] += _dot(act, wd_ref[...])

    @pl.when(f == pl.num_programs(1) - 1)
    def _():
        o_ref[...] = acc_ref[...]


def swiglu_ffn(x, nw, wg, wu, wd, *, tm, tf):
    t, d = x.shape
    ff = wg.shape[1]
    return pl.pallas_call(
        _ffn_kernel,
        grid=(t // tm, ff // tf),
        in_specs=[pl.BlockSpec((tm, d), lambda i, f: (i, 0)),
                  pl.BlockSpec((1, d), lambda i, f: (0, 0)),
                  pl.BlockSpec((d, tf), lambda i, f: (0, f)),
                  pl.BlockSpec((d, tf), lambda i, f: (0, f)),
                  pl.BlockSpec((tf, d), lambda i, f: (f, 0))],
        out_specs=pl.BlockSpec((tm, d), lambda i, f: (i, 0)),
        out_shape=jax.ShapeDtypeStruct((t, d), F32),
        scratch_shapes=[pltpu.VMEM((tm, d), BF16), pltpu.VMEM((tm, d), F32)],
        compiler_params=_params("arbitrary", "arbitrary"),
        name="swiglu_ffn",
    )(x, nw, wg, wu, wd)


def _qkv_kernel(x_ref, kvw_ref, aw_ref, wk_ref, wv_ref, wq_ref, cos_ref, sa_ref, sb_ref,
                k_ref, v_ref, *extra_refs, transposed):
    xn = _rms_scale(x_ref[...])
    hkv = (xn * kvw_ref[...]).astype(BF16)
    hq = (xn * aw_ref[...]).astype(BF16)
    cos, sa, sb = cos_ref[...], sa_ref[...], sb_ref[...]

    def rope(t):
        blocks = []
        for j in range(t.shape[1] // LANES):
            tb = t[:, j * LANES:(j + 1) * LANES]
            blocks.append(tb * cos + pltpu.roll(tb, LANES - ROT_DIM // 2, 1) * sa
                          + pltpu.roll(tb, ROT_DIM // 2, 1) * sb)
        return jnp.concatenate(blocks, axis=1)

    k = rope(_dot(hkv, wk_ref[...]))
    v = _dot(hkv, wv_ref[...])
    q = rope(_dot(hq, wq_ref[...]))
    tm = k.shape[0]
    for h in range(N_HEADS):
        rows = pl.ds(h, tm, stride=N_HEADS)
        k_ref[rows, :] = k[:, h * LANES:(h + 1) * LANES]
        v_ref[rows, :] = v[:, h * LANES:(h + 1) * LANES]
    if transposed:
        kb_ref, vt_ref, qt_ref = extra_refs
        kb_ref[...] = k.astype(BF16)
        vt_ref[0] = v.T.astype(BF16)
        qt_ref[0] = (q * (DK ** -0.5 * LOG2E)).T.astype(BF16)
    else:
        (q_ref,) = extra_refs
        q_ref[...] = (q * (DK ** -0.5)).astype(BF16)


def _rope_tables(pos):
    half = ROT_DIM // 2
    inv_freq = ROPE_THETA ** (-jnp.arange(half, dtype=F32) / half)
    ang = pos.astype(F32)[:, None] * inv_freq[None, :]
    cos, sin = jnp.cos(ang), jnp.sin(ang)
    n = pos.shape[0]
    ones = jnp.ones((n, DK - ROT_DIM), F32)
    zeros_h = jnp.zeros((n, half), F32)
    zeros_r = jnp.zeros((n, DK - ROT_DIM), F32)
    cos_t = jnp.concatenate([cos, cos, ones], axis=1)
    sa_t = jnp.concatenate([-sin, zeros_h, zeros_r], axis=1)
    sb_t = jnp.concatenate([zeros_h, sin, zeros_r], axis=1)
    rep = lambda t: jnp.tile(t, (1, LANES // DK))
    return rep(cos_t), rep(sa_t), rep(sb_t)


def qkv_proj(x, kvw, aw, wk, wv, wq, tables, *, tm, streams, transposed):
    t, d = x.shape
    period = tables[0].shape[0] // tm
    row = lambda i: (i, 0)
    const = lambda i: (0, 0)
    tab = lambda i: (i % period, 0)
    heads_out = jax.ShapeDtypeStruct((t * N_HEADS, LANES), F32)
    heads_spec = pl.BlockSpec((tm * N_HEADS, LANES), row)
    out_specs = [heads_spec, heads_spec, pl.BlockSpec((tm, d), row)]
    out_shape = [heads_out, heads_out, jax.ShapeDtypeStruct((t, d), BF16)]
    if transposed:
        per = t // streams // tm
        tr_spec = pl.BlockSpec((1, d, tm), lambda i: (i // per, 0, i % per))
        tr_out = jax.ShapeDtypeStruct((streams, d, t // streams), BF16)
        out_specs += [tr_spec, tr_spec]
        out_shape += [tr_out, tr_out]
    return pl.pallas_call(
        functools.partial(_qkv_kernel, transposed=transposed),
        grid=(t // tm,),
        in_specs=[pl.BlockSpec((tm, d), row),
                  pl.BlockSpec((1, d), const), pl.BlockSpec((1, d), const),
                  pl.BlockSpec((d, d), const), pl.BlockSpec((d, d), const), pl.BlockSpec((d, d), const),
                  pl.BlockSpec((tm, LANES), tab), pl.BlockSpec((tm, LANES), tab),
                  pl.BlockSpec((tm, LANES), tab)],
        out_specs=out_specs,
        out_shape=out_shape,
        compiler_params=_params("arbitrary"),
        name="qkv_proj",
    )(x, kvw, aw, wk, wv, wq, *tables)


def _lambda_value(lp_ref, lambda_init):
    lp = lp_ref[...]
    s1 = jnp.sum(lp[0:1] * lp[1:2], axis=-1, keepdims=True)
    s2 = jnp.sum(lp[2:3] * lp[3:4], axis=-1, keepdims=True)
    return jnp.exp(s1) - jnp.exp(s2) + lambda_init


def _diff_finish(o1, o2, lam, subw, lambda_init):
    o = o1 - lam * o2
    return _rms_scale(o) * subw * (1.0 - lambda_init)


def _attn_kernel(qt_ref, kt_ref, q_ref, k_ref, v_ref, lp_ref, subw_ref, o_ref,
                 qs_ref, *state_refs, tq, tk, lambda_init):
    p = pl.program_id(1)
    qi = qt_ref[p]
    ki = kt_ref[p]
    n_qc = 2 * tq // Q_COLS
    m_refs, acc_refs = state_refs[:n_qc], state_refs[n_qc:]

    @pl.when(ki == 0)
    def _():
        for m_ref, acc_ref in zip(m_refs, acc_refs):
            m_ref[...] = jnp.full(m_ref.shape, NEG_BIG, F32)
            acc_ref[...] = jnp.zeros(acc_ref.shape, F32)
        first_sub = lax.broadcasted_iota(jnp.int32, (LANES, tq), 0) < DK
        for h in range(N_HEADS):
            qh = q_ref[0, h * LANES:(h + 1) * LANES, :]
            zero = jnp.zeros_like(qh)
            qs_ref[h, :, 0:tq] = jnp.where(first_sub, qh, zero)
            qs_ref[h, :, tq:2 * tq] = jnp.where(first_sub, zero, qh)

    ones = jnp.ones((ONES_ROWS, tk), BF16)

    def sweep(masked):
        if masked:
            k_chunk = (ki * tk + lax.broadcasted_iota(jnp.int32, (tk, Q_COLS), 0)) >> CHUNK_SHIFT
            q_lane = lax.broadcasted_iota(jnp.int32, (tk, Q_COLS), 1)

        def scores(h, c):
            kh = k_ref[0, :, h * LANES:(h + 1) * LANES]
            s = _dot(kh, qs_ref[h, :, c * Q_COLS:(c + 1) * Q_COLS])
            if masked:
                q_chunk = (qi * tq + (c * Q_COLS) % tq + q_lane) >> CHUNK_SHIFT
                s = jnp.where(k_chunk <= q_chunk, s, NEG_BIG)
            return s

        groups = [(h, c) for h in range(N_HEADS) for c in range(n_qc)]
        pending = [scores(*g) for g in groups[:SCORE_LOOKAHEAD]]
        for i, (h, c) in enumerate(groups):
            s = pending.pop(0)
            if i + SCORE_LOOKAHEAD < len(groups):
                pending.append(scores(*groups[i + SCORE_LOOKAHEAD]))
            vt = jnp.concatenate([v_ref[0, h * LANES:(h + 1) * LANES, :], ones], axis=0)
            m_prev = m_refs[c][h:h + 1, :]
            m_new = jnp.maximum(m_prev, jnp.max(s, axis=0, keepdims=True))
            alpha = jnp.exp2(m_prev - m_new)
            pr = jnp.exp2(s - m_new).astype(BF16)
            acc_refs[c][h] = acc_refs[c][h] * alpha + _dot(vt, pr)
            m_refs[c][h:h + 1, :] = m_new

    @pl.when(ki < qi)
    def _():
        sweep(False)

    @pl.when(ki == qi)
    def _():
        sweep(True)
        lam = _lambda_value(lp_ref, lambda_init)
        subw = subw_ref[...]

        for h in range(N_HEADS):
            hs = slice(h * LANES, (h + 1) * LANES)
            for r in range(tq // Q_COLS):
                a1 = acc_refs[r][h]
                a2 = acc_refs[r + tq // Q_COLS][h]
                o_t = a1[0:DV] / a1[DV:DV + 1] - lam * (a2[0:DV] / a2[DV:DV + 1])
                res = _rms_scale(o_t.T) * subw * (1.0 - lambda_init)
                o_ref[0, r * Q_COLS:(r + 1) * Q_COLS, hs] = res.astype(o_ref.dtype)


def diff_attention(q_t, k, v_t, lam_params, subw, *, tq, lambda_init):
    nb, d, length = q_t.shape
    tk = tq
    nq = length // tq
    qt = np.concatenate([np.full(i + 1, i) for i in range(nq)]).astype(np.int32)
    kt = np.concatenate([np.arange(i + 1) for i in range(nq)]).astype(np.int32)
    kern = functools.partial(_attn_kernel, tq=tq, tk=tk, lambda_init=lambda_init)
    n_qc = 2 * tq // Q_COLS
    grid_spec = pltpu.PrefetchScalarGridSpec(
        num_scalar_prefetch=2,
        grid=(nb, len(qt)),
        in_specs=[pl.BlockSpec((1, d, tq), lambda b, p, qt_r, kt_r: (b, 0, qt_r[p])),
                  pl.BlockSpec((1, tk, d), lambda b, p, qt_r, kt_r: (b, kt_r[p], 0)),
                  pl.BlockSpec((1, d, tk), lambda b, p, qt_r, kt_r: (b, 0, kt_r[p])),
                  pl.BlockSpec((4, DK), lambda b, p, qt_r, kt_r: (0, 0)),
                  pl.BlockSpec((1, DV), lambda b, p, qt_r, kt_r: (0, 0))],
        out_specs=pl.BlockSpec((1, tq, d), lambda b, p, qt_r, kt_r: (b, qt_r[p], 0)),
        scratch_shapes=([pltpu.VMEM((N_HEADS, LANES, 2 * tq), BF16)]
                        + [pltpu.VMEM((N_HEADS, Q_COLS), F32)] * n_qc
                        + [pltpu.VMEM((N_HEADS, DV + ONES_ROWS, Q_COLS), F32)] * n_qc),
    )
    return pl.pallas_call(
        kern,
        grid_spec=grid_spec,
        out_shape=jax.ShapeDtypeStruct((nb, length, d), BF16),
        compiler_params=_params("arbitrary", "arbitrary"),
        name="diff_attention",
    )(jnp.asarray(qt), jnp.asarray(kt), q_t, k, v_t, lam_params, subw)


def _attn_cached_kernel(q_ref, ck_ref, cv_ref, kn_ref, vn_ref, lp_ref, subw_ref, o_ref,
                        *, nq, lambda_init):
    past = ck_ref.shape[1] // N_HEADS
    lane = lax.broadcasted_iota(jnp.int32, (nq, LANES), 1)
    nt = (((1,), (1,)), ((), ()))
    pad = jnp.zeros((LANES - nq, LANES), BF16)
    lam = _lambda_value(lp_ref, lambda_init)
    for h in range(N_HEADS):
        q = q_ref[0, :, h * LANES:(h + 1) * LANES]
        zero = jnp.zeros_like(q)
        qs = jnp.concatenate([jnp.where(lane < DK, q, zero), jnp.where(lane < DK, zero, q)], axis=0)
        old_rows = pl.ds(h, past, stride=N_HEADS)
        new_rows = pl.ds(h, nq, stride=N_HEADS)
        kc = ck_ref[0, old_rows, :].astype(BF16)
        vc = cv_ref[0, old_rows, :].astype(BF16)
        kn = jnp.concatenate([kn_ref[0, new_rows, :].astype(BF16), pad], axis=0)
        vn = jnp.concatenate([vn_ref[0, new_rows, :].astype(BF16), pad], axis=0)
        s_c = lax.dot_general(qs, kc, nt, preferred_element_type=F32)
        s_n = lax.dot_general(qs, kn, nt, preferred_element_type=F32)
        col = lax.broadcasted_iota(jnp.int32, s_n.shape, 1)
        s_n = jnp.where(col < nq, s_n, NEG_BIG)
        m = jnp.maximum(jnp.max(s_c, axis=-1, keepdims=True), jnp.max(s_n, axis=-1, keepdims=True))
        p_c = jnp.exp(s_c - m)
        p_n = jnp.exp(s_n - m)
        denom = jnp.sum(p_c, axis=-1, keepdims=True) + jnp.sum(p_n, axis=-1, keepdims=True)
        o = (_dot(p_c.astype(BF16), vc) + _dot(p_n.astype(BF16), vn)) / denom
        res = _diff_finish(o[0:nq], o[nq:2 * nq], lam, subw_ref[...], lambda_init)
        o_ref[0, :, h * LANES:(h + 1) * LANES] = res.astype(o_ref.dtype)


def diff_attention_cached(q, cache_k, cache_v, k_new, v_new, lam_params, subw, *, lambda_init):
    nb, nq, d = q.shape
    past = cache_k.shape[1]
    kern = functools.partial(_attn_cached_kernel, nq=nq, lambda_init=lambda_init)
    stream = lambda b: (b, 0, 0)
    const = lambda b: (0, 0)
    rows = lambda a: a.reshape(nb, a.shape[1] * N_HEADS, LANES)
    cache_k, cache_v, k_new, v_new = rows(cache_k), rows(cache_v), rows(k_new), rows(v_new)
    return pl.pallas_call(
        kern,
        grid=(nb,),
        in_specs=[pl.BlockSpec((1, nq, d), lambda b: (b, 0, 0)),
                  pl.BlockSpec((1, past * N_HEADS, LANES), stream),
                  pl.BlockSpec((1, past * N_HEADS, LANES), stream),
                  pl.BlockSpec((1, nq * N_HEADS, LANES), stream),
                  pl.BlockSpec((1, nq * N_HEADS, LANES), stream),
                  pl.BlockSpec((4, DK), const),
                  pl.BlockSpec((1, DV), const)],
        out_specs=pl.BlockSpec((1, nq, d), lambda b: (b, 0, 0)),
        out_shape=jax.ShapeDtypeStruct((nb, nq, d), BF16),
        compiler_params=_params("arbitrary"),
        name="diff_attention_cached",
    )(q, cache_k, cache_v, k_new, v_new, lam_params, subw)


def _route(logits):
    lane = lax.broadcasted_iota(jnp.int32, logits.shape, 1)
    valid = lane < N_EXPERTS
    lg = jnp.where(valid, logits, NEG_BIG)
    m1 = jnp.max(lg, axis=-1, keepdims=True)
    lane_f = lane.astype(F32)
    i1 = jnp.min(jnp.where(lg == m1, lane_f, float(LANES)), axis=-1, keepdims=True)
    lg2 = jnp.where(lane_f == i1, NEG_BIG, lg)
    m2 = jnp.max(lg2, axis=-1, keepdims=True)
    i2 = jnp.min(jnp.where(lg2 == m2, lane_f, float(LANES)), axis=-1, keepdims=True)
    e2 = jnp.exp(m2 - m1)
    g1 = 1.0 / (1.0 + e2)
    g2 = e2 / (1.0 + e2)
    return jnp.where(lane_f == i1, g1, 0.0) + jnp.where(lane_f == i2, g2, 0.0)


def _moe_route_kernel(x_ref, nw_ref, wr_ref, h_ref, comb_ref, rk_ref, rkt_ref, cnt_ref, *, rt):
    t = x_ref.shape[0]
    hf = _rms_scale(x_ref[...]) * nw_ref[...]
    h_ref[...] = hf.astype(BF16)
    comb = _route(_dot_exact_rhs_general(hf, wr_ref[...]))
    comb_ref[...] = comb
    sel = jnp.where(comb > 0.0, 1.0, 0.0)
    row = lax.broadcasted_iota(jnp.int32, (rt, rt), 0)
    col = lax.broadcasted_iota(jnp.int32, (rt, rt), 1)
    strict = jnp.where(col < row, 1.0, 0.0).astype(BF16)
    carry = jnp.zeros((1, LANES), F32)
    ranks = []
    for s in range(t // rt):
        sel_s = sel[s * rt:(s + 1) * rt]
        rank_s = _dot(strict, sel_s.astype(BF16)) + carry
        ranks.append(jnp.where(sel_s > 0.0, rank_s, -1.0))
        carry = carry + jnp.sum(sel_s, axis=0, keepdims=True)
    rk = jnp.concatenate(ranks, axis=0) if len(ranks) > 1 else ranks[0]
    rk_ref[...] = rk.astype(jnp.int32)
    rkt_ref[0] = rk.T.astype(jnp.int32)
    cnt_ref[0] = jnp.broadcast_to(carry, (8, LANES)).astype(jnp.int32)


def moe_route(x, nw, wr, *, tb, rt):
    t, d = x.shape
    nb = t // tb
    row = lambda b: (b, 0)
    const = lambda b: (0, 0)
    return pl.pallas_call(
        functools.partial(_moe_route_kernel, rt=rt),
        grid=(nb,),
        in_specs=[pl.BlockSpec((tb, d), row), pl.BlockSpec((1, d), const),
                  pl.BlockSpec((d, LANES), const)],
        out_specs=[pl.BlockSpec((tb, d), row), pl.BlockSpec((tb, LANES), row),
                   pl.BlockSpec((tb, LANES), row),
                   pl.BlockSpec((1, LANES, tb), lambda b: (b, 0, 0)),
                   pl.BlockSpec((1, 8, LANES), lambda b: (b, 0, 0))],
        out_shape=[jax.ShapeDtypeStruct((t, d), BF16),
                   jax.ShapeDtypeStruct((t, LANES), F32),
                   jax.ShapeDtypeStruct((t, LANES), jnp.int32),
                   jax.ShapeDtypeStruct((nb, LANES, tb), jnp.int32),
                   jax.ShapeDtypeStruct((nb, 8, LANES), jnp.int32)],
        compiler_params=_params("arbitrary"),
        name="moe_route",
    )(x, nw, wr)


def _moe_expert_kernel(cnt_ref, x_ref, h_ref, comb_ref, rk_ref, rkt_ref, wg_ref, wu_ref, wd_ref,
                       fw_ref, o_ref, rkc_ref, xg_ref, gs_ref, yacc_ref, *, rt):
    b = pl.program_id(0)
    e = pl.program_id(1)
    f = pl.program_id(2)
    t = x_ref.shape[0]
    n_rows = cnt_ref[b * N_EXPERTS + e]
    half = rt // 2
    n_tiles = (n_rows + (half - 1)) // rt
    tail_row0 = pl.multiple_of(n_tiles * rt, rt)
    has_tail = n_rows > n_tiles * rt

    def over_tiles(fn):
        def body(j, carry):
            fn(pl.multiple_of(j * rt, rt), rt)
            return carry

        lax.fori_loop(0, n_tiles, body, 0)

        @pl.when(has_tail)
        def _():
            fn(tail_row0, half)

    @pl.when(jnp.logical_and(e == 0, f == 0))
    def _():
        o_ref[...] = x_ref[...]

    @pl.when(f == 0)
    def _():
        lane = lax.broadcasted_iota(jnp.int32, (t, LANES), 1)
        pick = lane == e
        rkc = jnp.sum(jnp.where(pick, rk_ref[...], 0).astype(F32), axis=-1, keepdims=True)
        rkc_ref[...] = jnp.broadcast_to(rkc, (t, LANES)).astype(jnp.int32)
        gate = jnp.sum(jnp.where(pick, comb_ref[...], 0.0), axis=-1, keepdims=True)
        g_hi, g_mid, g_lo = _split3(jnp.broadcast_to(gate, (t, LANES)))
        g3 = jnp.where(lane == 0, g_hi.astype(F32),
                       jnp.where(lane == 1, g_mid.astype(F32),
                                 jnp.where(lane == 2, g_lo.astype(F32), 0.0))).astype(BF16)
        rk_row = rkt_ref[0, pl.ds(e, 1), :]

        def gather(row0, size):
            tile = pl.ds(row0, size)
            rows = lax.broadcasted_iota(jnp.int32, (size, t), 0) + row0
            p = jnp.where(rk_row == rows, 1.0, 0.0).astype(BF16)
            xg_ref[tile, :] = _dot(p, h_ref[...]).astype(BF16)
            gate_rows = jnp.sum(_dot(p, g3), axis=-1, keepdims=True)
            gs_ref[tile, :] = jnp.broadcast_to(gate_rows, (size, LANES))
            yacc_ref[tile, :] = jnp.zeros((size, yacc_ref.shape[1]), F32)

        over_tiles(gather)

    def expert(row0, size):
        tile = pl.ds(row0, size)
        xg = xg_ref[tile, :]
        g = _dot(xg, wg_ref[...])
        u = _dot(xg, wu_ref[...])
        gates = _lane_tile(gs_ref[tile, :], g.shape[1] // LANES)
        act = (g * _sigmoid(g) * u * gates).astype(BF16)
        yacc_ref[tile, :] += _dot(act, wd_ref[...])

    over_tiles(expert)

    @pl.when(f == pl.num_programs(2) - 1)
    def _():
        def scatter(row0, size):
            cols = lax.broadcasted_iota(jnp.int32, (t, size), 1) + row0
            s = jnp.where(_lane_tile(rkc_ref[...], size // LANES) == cols, 1.0, 0.0).astype(BF16)
            o_ref[...] += _dot(s, yacc_ref[pl.ds(row0, size), :].astype(BF16))

        over_tiles(scatter)

    @pl.when(jnp.logical_and(e == pl.num_programs(1) - 1, f == pl.num_programs(2) - 1))
    def _():
        o_ref[...] = _rms_scale(o_ref[...]) * fw_ref[...]


def _dot_exact_rhs_general(x, w):
    xh, xm, xl = _split3(x)
    wh, wm, wl = _split3(w)
    return (_dot(xh, wh) + (_dot(xh, wm) + _dot(xm, wh))
            + (_dot(xh, wl) + _dot(xm, wm) + _dot(xl, wh)))


def moe_ffn_final(x, nw, wr, wg, wu, wd, fw, *, tb, rt, tf):
    t, d = x.shape
    ne, _, ff = wg.shape
    nb = t // tb
    h, comb, rk, rkt, cnt = moe_route(x, nw, wr, tb=tb, rt=rt)
    counts = cnt[:, 0, :ne].reshape(nb * ne)
    once = pl.Buffered(1)
    blk = lambda b, e, f, c: (b, 0)
    const = lambda b, e, f, c: (0, 0)
    grid_spec = pltpu.PrefetchScalarGridSpec(
        num_scalar_prefetch=1,
        grid=(nb, ne, ff // tf),
        in_specs=[pl.BlockSpec((tb, d), blk, pipeline_mode=once),
                  pl.BlockSpec((tb, d), blk, pipeline_mode=once),
                  pl.BlockSpec((tb, LANES), blk, pipeline_mode=once),
                  pl.BlockSpec((tb, LANES), blk, pipeline_mode=once),
                  pl.BlockSpec((1, LANES, tb), lambda b, e, f, c: (b, 0, 0), pipeline_mode=once),
                  pl.BlockSpec((None, d, tf), lambda b, e, f, c: (e, 0, f)),
                  pl.BlockSpec((None, d, tf), lambda b, e, f, c: (e, 0, f)),
                  pl.BlockSpec((None, tf, d), lambda b, e, f, c: (e, f, 0)),
                  pl.BlockSpec((1, d), const)],
        out_specs=pl.BlockSpec((tb, d), blk),
        scratch_shapes=[pltpu.VMEM((tb, LANES), jnp.int32),
                        pltpu.VMEM((tb, d), BF16),
                        pltpu.VMEM((tb, LANES), F32),
                        pltpu.VMEM((tb, d), F32)],
    )
    return pl.pallas_call(
        functools.partial(_moe_expert_kernel, rt=rt),
        grid_spec=grid_spec,
        out_shape=jax.ShapeDtypeStruct((t, d), F32),
        compiler_params=_params("arbitrary", "arbitrary", "arbitrary"),
        name="moe_experts",
    )(counts, x, h, comb, rk, rkt, wg, wu, wd, fw)


MOE_ROW_ALIGN = 16
MOE_ROW_TILE = 128
MOE_GROUP_TILE = 1024


def _moe_plan(cnt, *, tb, region):
    nb, ne = cnt.shape
    padded = (cnt + (MOE_ROW_ALIGN - 1)) // MOE_ROW_ALIGN * MOE_ROW_ALIGN
    off = jnp.cumsum(padded, axis=0) - padded
    total = jnp.sum(padded, axis=0)
    tiles = (total + MOE_ROW_TILE + MOE_GROUP_TILE - 1) // MOE_GROUP_TILE
    tile_end = jnp.cumsum(tiles)
    max_tiles = (2 * nb * tb + nb * ne * MOE_ROW_ALIGN + ne * MOE_ROW_TILE) // MOE_GROUP_TILE + ne + 1
    idx = jnp.arange(max_tiles, dtype=jnp.int32)
    last = jnp.maximum(tile_end[-1] - 1, 0)
    active = idx < tile_end[-1]
    ii = jnp.minimum(idx, last)
    te = jnp.searchsorted(tile_end, ii, side="right").astype(jnp.int32)
    k = ii - (tile_end - tiles)[te]
    tr = te * (region // MOE_GROUP_TILE) + k
    valid = jnp.clip(total[te] - k * MOE_GROUP_TILE, 0, MOE_GROUP_TILE)
    tv = jnp.where(active, valid, -1)
    return (off.reshape(-1).astype(jnp.int32), te, tr.astype(jnp.int32), tv.astype(jnp.int32))


def _row_pieces(t):
    pieces, r0, size = [], 0, MOE_ROW_TILE
    while r0 < t:
        pieces.append((r0, size))
        r0 += size
        size = max(MOE_ROW_TILE, r0) if r0 >= 2 * MOE_ROW_TILE else MOE_ROW_TILE
    return pieces


def _moe_gather_kernel(cnt_ref, off_ref, h_ref, rkt_ref, xs_ref, buf_ref, sem_ref, *, region):
    b = pl.program_id(0)
    t = h_ref.shape[0]
    rt = MOE_ROW_TILE
    pieces = _row_pieces(t)

    @pl.when(b == 0)
    def _():
        buf_ref[...] = jnp.zeros(buf_ref.shape, BF16)

    def for_pieces(e, n, row0, enabled, action):
        for p, (r0, size) in enumerate(pieces):
            @pl.when(jnp.logical_and(enabled, n > r0))
            def _():
                action(pltpu.make_async_copy(buf_ref.at[e, pl.ds(r0, size)],
                                             xs_ref.at[pl.ds(row0 + r0, size)], sem_ref.at[e, p]))

    for e in range(N_EXPERTS):
        n = cnt_ref[b * N_EXPERTS + e]
        n_prev = cnt_ref[jnp.maximum(b - 1, 0) * N_EXPERTS + e]
        row0 = pl.multiple_of(e * region + off_ref[b * N_EXPERTS + e], MOE_ROW_ALIGN)
        for_pieces(e, n_prev, row0, b > 0, lambda cp: cp.wait())

        rk_row = rkt_ref[0, e:e + 1, :]

        def gather(j, carry):
            r0 = pl.multiple_of(j * rt, rt)
            rows = lax.broadcasted_iota(jnp.int32, (rt, t), 0) + r0
            p = jnp.where(rk_row == rows, 1.0, 0.0).astype(BF16)
            buf_ref[e, pl.ds(r0, rt), :] = _dot(p, h_ref[...]).astype(BF16)
            return carry

        lax.fori_loop(0, (n + (rt - 1)) // rt, gather, 0)
        for_pieces(e, n, row0, True, lambda cp: cp.start())
        for_pieces(e, n, row0, b == pl.num_programs(0) - 1, lambda cp: cp.wait())


def _moe_grouped_kernel(te_ref, tr_ref, tv_ref, x_ref, wg_ref, wu_ref, wd_ref, y_ref, xm_ref, acc_ref):
    i = pl.program_id(0)
    f = pl.program_id(1)
    valid = tv_ref[i]
    last_f = pl.num_programs(1) - 1

    @pl.when(valid > 0)
    def _():
        @pl.when(f == 0)
        def _():
            row = lax.broadcasted_iota(jnp.int32, x_ref.shape, 0)
            xm_ref[...] = jnp.where(row < valid, x_ref[...].astype(F32), 0.0).astype(BF16)
            acc_ref[...] = jnp.zeros(acc_ref.shape, F32)

        xm = xm_ref[...]
        g = _dot(xm, wg_ref[...])
        u = _dot(xm, wu_ref[...])
        acc_ref[...] += _dot((g * _sigmoid(g) * u).astype(BF16), wd_ref[...])

        @pl.when(f == last_f)
        def _():
            y_ref[...] = acc_ref[...].astype(y_ref.dtype)

    @pl.when(jnp.logical_and(valid == 0, f == last_f))
    def _():
        y_ref[...] = jnp.zeros(y_ref.shape, y_ref.dtype)


def _moe_combine_kernel(cnt_ref, off_ref, x_ref, comb_ref, rk_ref, ys_ref, fw_ref, o_ref,
                        buf_ref, sem_ref, *, region):
    b = pl.program_id(0)
    t = x_ref.shape[0]
    rt = MOE_ROW_TILE

    pieces = _row_pieces(t)

    def for_pieces(e, action):
        n = cnt_ref[b * N_EXPERTS + e]
        row0 = pl.multiple_of(e * region + off_ref[b * N_EXPERTS + e], MOE_ROW_ALIGN)
        for p, (r0, size) in enumerate(pieces):
            @pl.when(n > r0)
            def _():
                action(pltpu.make_async_copy(ys_ref.at[pl.ds(row0 + r0, size)],
                                             buf_ref.at[e, pl.ds(r0, size)], sem_ref.at[e, p]))

    for e in range(N_EXPERTS):
        for_pieces(e, lambda cp: cp.start())
    o_ref[...] = x_ref[...]
    lane = lax.broadcasted_iota(jnp.int32, (t, LANES), 1)
    for e in range(N_EXPERTS):
        n = cnt_ref[b * N_EXPERTS + e]
        for_pieces(e, lambda cp: cp.wait())
        pick = lane == e
        rkc = jnp.sum(jnp.where(pick, rk_ref[...], 0).astype(F32), axis=-1, keepdims=True)
        rkc = jnp.broadcast_to(rkc, (t, LANES)).astype(jnp.int32)
        gate = jnp.broadcast_to(jnp.sum(jnp.where(pick, comb_ref[...], 0.0), axis=-1, keepdims=True),
                                (t, LANES))
        g_hi = gate.astype(BF16).astype(F32)
        g_lo = gate - g_hi

        def scatter(j, carry):
            r0 = pl.multiple_of(j * rt, rt)
            hit = rkc == lax.broadcasted_iota(jnp.int32, (t, rt), 1) + r0
            s = jnp.concatenate([jnp.where(hit, g_hi, 0.0), jnp.where(hit, g_lo, 0.0)], axis=1).astype(BF16)
            y = buf_ref[e, pl.ds(r0, rt), :]
            o_ref[...] += _dot(s, jnp.concatenate([y, y], axis=0))
            return carry

        lax.fori_loop(0, (n + (rt - 1)) // rt, scatter, 0)
    o_ref[...] = _rms_scale(o_ref[...]) * fw_ref[...]


def moe_ffn_final_sorted(x, nw, wr, wg, wu, wd, fw, *, tb, tf):
    t, d = x.shape
    ne, _, ff = wg.shape
    nb = t // tb
    st = MOE_GROUP_TILE
    region = -(-(tb * nb + MOE_ROW_ALIGN * nb + tb) // st) * st
    h, comb, rk, rkt, cnt = moe_route(x, nw, wr, tb=tb, rt=tb)
    counts = cnt[:, 0, :ne]
    off, te, tr, tv = _moe_plan(counts, tb=tb, region=region)
    counts = counts.reshape(nb * ne)
    n_tiles = te.shape[0]

    xs = pl.pallas_call(
        functools.partial(_moe_gather_kernel, region=region),
        grid_spec=pltpu.PrefetchScalarGridSpec(
            num_scalar_prefetch=2,
            grid=(nb,),
            in_specs=[pl.BlockSpec((tb, d), lambda b, c, o: (b, 0)),
                      pl.BlockSpec((1, LANES, tb), lambda b, c, o: (b, 0, 0))],
            out_specs=pl.BlockSpec(memory_space=pl.ANY),
            scratch_shapes=[pltpu.VMEM((ne, tb, d), BF16), pltpu.SemaphoreType.DMA((ne, len(_row_pieces(tb))))],
        ),
        out_shape=jax.ShapeDtypeStruct((ne * region, d), BF16),
        compiler_params=_params("arbitrary"),
        name="moe_gather",
    )(counts, off, h, rkt)

    last_f = ff // tf - 1
    used_f = lambda i, f, te_r, tr_r, tv_r: jnp.where(tv_r[i] > 0, f, last_f)
    ys = pl.pallas_call(
        _moe_grouped_kernel,
        grid_spec=pltpu.PrefetchScalarGridSpec(
            num_scalar_prefetch=3,
            grid=(n_tiles, ff // tf),
            in_specs=[pl.BlockSpec((st, d), lambda i, f, te_r, tr_r, tv_r: (tr_r[i], 0)),
                      pl.BlockSpec((None, d, tf), lambda i, f, te_r, tr_r, tv_r:
                                   (te_r[i], 0, used_f(i, f, te_r, tr_r, tv_r))),
                      pl.BlockSpec((None, d, tf), lambda i, f, te_r, tr_r, tv_r:
                                   (te_r[i], 0, used_f(i, f, te_r, tr_r, tv_r))),
                      pl.BlockSpec((None, tf, d), lambda i, f, te_r, tr_r, tv_r:
                                   (te_r[i], used_f(i, f, te_r, tr_r, tv_r), 0))],
            out_specs=pl.BlockSpec((st, d), lambda i, f, te_r, tr_r, tv_r: (tr_r[i], 0)),
            scratch_shapes=[pltpu.VMEM((st, d), BF16), pltpu.VMEM((st, d), F32)],
        ),
        out_shape=jax.ShapeDtypeStruct((ne * region, d), BF16),
        compiler_params=_params("arbitrary", "arbitrary"),
        name="moe_grouped",
    )(te, tr, tv, xs, wg, wu, wd)

    return pl.pallas_call(
        functools.partial(_moe_combine_kernel, region=region),
        grid_spec=pltpu.PrefetchScalarGridSpec(
            num_scalar_prefetch=2,
            grid=(nb,),
            in_specs=[pl.BlockSpec((tb, d), lambda b, c, o: (b, 0)),
                      pl.BlockSpec((tb, LANES), lambda b, c, o: (b, 0)),
                      pl.BlockSpec((tb, LANES), lambda b, c, o: (b, 0)),
                      pl.BlockSpec(memory_space=pl.ANY),
                      pl.BlockSpec((1, d), lambda b, c, o: (0, 0))],
            out_specs=pl.BlockSpec((tb, d), lambda b, c, o: (b, 0)),
            scratch_shapes=[pltpu.VMEM((ne, tb, d), BF16), pltpu.SemaphoreType.DMA((ne, len(_row_pieces(tb))))],
        ),
        out_shape=jax.ShapeDtypeStruct((t, d), F32),
        compiler_params=_params("arbitrary"),
        name="moe_combine",
    )(counts, off, x, comb, rk, ys, fw)


def _row(v):
    return v.reshape(1, -1).astype(F32)


def _pad_lanes(v, width):
    v = _row(v)
    return jnp.pad(v, ((0, 0), (0, width - v.shape[1])))


def _trunk(x, pos, conv_in, state_in, past_k, past_v, W, *, ssd_chunk, ssd_valid, tm, tq):
    nb, length, d = x.shape
    t = nb * length
    x0 = x.reshape(t, d)

    proj, dt_raw = rms_matmul(x0, W["mamba_norm_w"], W["w_in"], W["w_dt"], tm=min(t, 1024), tn=1024)
    proj3 = proj.reshape(nb, length, D_INNER + CONV_DIM)
    new_conv = proj3[:, length - (CONV_W - 1):, D_INNER:D_INNER + CONV_DIM].astype(F32)
    conv_hi = conv_in.astype(BF16)
    conv_lo = (conv_in - conv_hi.astype(F32)).astype(BF16)
    front = ((0, 0), (CONV_HIST - (CONV_W - 1), 0), (0, 0))
    conv_hist = jnp.concatenate([jnp.pad(conv_hi, front), jnp.pad(conv_lo, front)], axis=1)
    y, state_out = ssd(proj3, dt_raw.reshape(nb, length, DT_PAD), conv_hist,
                       state_in.reshape(nb, D_INNER, D_STATE),
                       W["conv_w"], W["conv_b"], W["dt_bias"], W["a_log"], W["d_exp"], W["gn_w"],
                       c=ssd_chunk, cv=ssd_valid)
    new_ssm = state_out.reshape(nb, SSM_HEADS, SSM_HEAD_DIM, D_STATE)
    x1 = matmul_residual(y.reshape(t, D_INNER), W["w_out"], x0, tm=tm)

    x2 = swiglu_ffn(x1, W["ffn_norm_w"], W["ffn_wg"], W["ffn_wu"], W["ffn_wd"], tm=tm, tf=1408)

    tables = _rope_tables(pos)
    proj_args = (x2, W["kv_norm_w"], W["attn_norm_w"], W["w_k"], W["w_v"], W["w_q"], tables)
    lambda_init = 0.8 - 0.6 * math.exp(-0.3 * 1)
    if past_k is None:
        k, v, kb, v_t, q_t = qkv_proj(*proj_args, tm=tm, streams=nb, transposed=True)
        o = diff_attention(q_t, kb.reshape(nb, length, d), v_t,
                           W["lam"], W["subln_w"], tq=tq, lambda_init=lambda_init)
    else:
        k, v, q = qkv_proj(*proj_args, tm=tm, streams=nb, transposed=False)
        q3 = q.reshape(nb, length, d)
        o = diff_attention_cached(q3, past_k, past_v,
                                  k.reshape(nb, length, N_HEADS, LANES),
                                  v.reshape(nb, length, N_HEADS, LANES),
                                  W["lam"], W["subln_w"], lambda_init=lambda_init)
    x3 = matmul_residual(o.reshape(t, d), W["w_o"], x2, tm=tm)

    moe_args = (x3, W["moe_norm_w"], W["moe_wr"], W["moe_wg"], W["moe_wu"], W["moe_wd"], W["final_norm_w"])
    if t >= 4 * MOE_GROUP_TILE:
        yout = moe_ffn_final_sorted(*moe_args, tb=512, tf=512)
    else:
        yout = moe_ffn_final(*moe_args, tb=t, rt=256, tf=512)
    return (yout.reshape(nb, length, d), new_conv[None], new_ssm[None],
            k.reshape(nb, length, N_HEADS, 2 * DK), v.reshape(nb, length, N_HEADS, DV))


def kernel(x_prompt, x_sample, cache_conv, state_ssm, cache_k, cache_v, mamba_norm_w, mamba_w_in, mamba_conv_w, mamba_conv_b, mamba_dt_bias, mamba_a_log, mamba_d, mamba_gn_w, mamba_w_out, kv_norm_w, w_k, w_v, attn_norm_w, w_q, lambda_q1, lambda_k1, lambda_q2, lambda_k2, subln_w, w_o, ffn_norm_w, ffn_w_gate, ffn_w_up, ffn_w_down, moe_norm_w, moe_w_router, moe_w_gate, moe_w_up, moe_w_down, final_norm_w):
    w_in = mamba_w_in[0]
    n_dt = w_in.shape[1] - D_INNER - CONV_DIM
    W = dict(
        mamba_norm_w=_row(mamba_norm_w[0]),
        w_in=w_in[:, :D_INNER + CONV_DIM].astype(BF16),
        w_dt=jnp.pad(w_in[:, D_INNER + CONV_DIM:], ((0, 0), (0, DT_PAD - n_dt))).astype(BF16),
        conv_w=mamba_conv_w[0].astype(F32),
        conv_b=_row(mamba_conv_b[0]),
        dt_bias=_pad_lanes(mamba_dt_bias[0], LANES),
        a_log=_pad_lanes(mamba_a_log[0], LANES),
        d_exp=_row(jnp.repeat(mamba_d[0], SSM_HEAD_DIM)),
        gn_w=_row(mamba_gn_w[0]),
        w_out=mamba_w_out[0].astype(BF16),
        kv_norm_w=_row(kv_norm_w), w_k=w_k.astype(BF16), w_v=w_v.astype(BF16),
        attn_norm_w=_row(attn_norm_w[0]), w_q=w_q[0].astype(BF16),
        lam=jnp.stack([lambda_q1[0], lambda_k1[0], lambda_q2[0], lambda_k2[0]]).astype(F32),
        subln_w=_row(subln_w[0]), w_o=w_o[0].astype(BF16),
        ffn_norm_w=_row(ffn_norm_w[0]),
        ffn_wg=ffn_w_gate[0].astype(BF16), ffn_wu=ffn_w_up[0].astype(BF16),
        ffn_wd=ffn_w_down[0].astype(BF16),
        moe_norm_w=_row(moe_norm_w[0]),
        moe_wr=jnp.pad(moe_w_router[0].astype(F32), ((0, 0), (0, LANES - N_EXPERTS))),
        moe_wg=moe_w_gate[0].astype(BF16), moe_wu=moe_w_up[0].astype(BF16),
        moe_wd=moe_w_down[0].astype(BF16),
        final_norm_w=_row(final_norm_w),
    )
    bp, lp = x_prompt.shape[0], x_prompt.shape[1]
    bs, ls = x_sample.shape[0], x_sample.shape[1]
    past = cache_k.shape[1]

    conv0 = jnp.zeros((bp, CONV_W - 1, CONV_DIM), F32)
    ssm0 = jnp.zeros((bp, SSM_HEADS, SSM_HEAD_DIM, D_STATE), F32)
    y_p, conv_p, ssm_p, k_p, v_p = _trunk(
        x_prompt, jnp.arange(lp, dtype=jnp.int32), conv0, ssm0, None, None, W,
        ssd_chunk=128, ssd_valid=128, tm=512, tq=512)

    pos_s = jnp.tile(past + jnp.arange(ls, dtype=jnp.int32), bs)
    y_s, conv_s, ssm_s, k_s, v_s = _trunk(
        x_sample, pos_s, cache_conv[0], state_ssm[0], cache_k, cache_v, W,
        ssd_chunk=128, ssd_valid=ls, tm=bs * ls, tq=None)
    return (y_p, y_s, conv_p, ssm_p, k_p, v_p, conv_s, ssm_s, k_s, v_s)
```

```python
import functools
import math

import numpy as np
import jax
import jax.numpy as jnp
from jax import lax
from jax.experimental import pallas as pl
from jax.experimental.pallas import tpu as pltpu

F32 = jnp.float32
BF16 = jnp.bfloat16

EPS = 1e-5
D_MODEL = 1024
D_INNER = 2048
SSM_HEAD_DIM = 64
SSM_HEADS = 32
SSM_GROUPS = 4
D_STATE = 128
GROUP_W = D_INNER // SSM_GROUPS
CONV_W = 4
CONV_HIST = 16
CONV_DIM = D_INNER + 2 * SSM_GROUPS * D_STATE
DT_PAD = 256
PROJ_W = D_INNER + CONV_DIM + DT_PAD
N_HEADS = 8
DK = 64
DV = 128
ROT_DIM = 16
ROPE_THETA = 500000.0
CHUNK = 64
CHUNK_SHIFT = 6
assert 1 << CHUNK_SHIFT == CHUNK
N_EXPERTS = 8
LANES = 128
NEG_BIG = -1e30
LOG2E = math.log2(math.e)
Q_COLS = 256
ONES_ROWS = 16
SCORE_LOOKAHEAD = 3
VMEM_LIMIT = 56 * 1024 * 1024


def _params(*sem):
    return pltpu.CompilerParams(dimension_semantics=sem, vmem_limit_bytes=VMEM_LIMIT)


def _sigmoid(x):
    return 1.0 / (1.0 + jnp.exp(-x))


def _rms_scale(x):
    return x * lax.rsqrt(jnp.mean(x * x, axis=-1, keepdims=True) + EPS)


def _split3(x):
    hi = x.astype(BF16)
    r1 = x - hi.astype(F32)
    mid = r1.astype(BF16)
    lo = (r1 - mid.astype(F32)).astype(BF16)
    return hi, mid, lo


def _dot(a, b):
    return jnp.dot(a, b, preferred_element_type=F32)


def _lane_tile(x, n):
    return jnp.concatenate([x] * n, axis=1) if n > 1 else x


def _dot_exact_rhs(x, m_bf16):
    hi, mid, lo = _split3(x)
    return _dot(hi, m_bf16) + _dot(mid, m_bf16) + _dot(lo, m_bf16)


def _dot_exact_lhs(m_bf16, x):
    hi, mid, lo = _split3(x)
    return _dot(m_bf16, hi) + _dot(m_bf16, mid) + _dot(m_bf16, lo)


def _rms_matmul_kernel(x_ref, nw_ref, w_ref, wdt_ref, o_ref, dt_ref, h_ref):
    @pl.when(pl.program_id(1) == 0)
    def _():
        h = (_rms_scale(x_ref[...]) * nw_ref[...]).astype(BF16)
        h_ref[...] = h
        dt_ref[...] = _dot(h, wdt_ref[...])

    o_ref[...] = _dot(h_ref[...], w_ref[...]).astype(o_ref.dtype)


def rms_matmul(x, nw, w, w_dt, *, tm, tn):
    t, d = x.shape
    n = w.shape[1]
    n_dt = w_dt.shape[1]
    return pl.pallas_call(
        _rms_matmul_kernel,
        grid=(t // tm, n // tn),
        in_specs=[pl.BlockSpec((tm, d), lambda i, j: (i, 0)),
                  pl.BlockSpec((1, d), lambda i, j: (0, 0)),
                  pl.BlockSpec((d, tn), lambda i, j: (0, j)),
                  pl.BlockSpec((d, n_dt), lambda i, j: (0, 0))],
        out_specs=[pl.BlockSpec((tm, tn), lambda i, j: (i, j)),
                   pl.BlockSpec((tm, n_dt), lambda i, j: (i, 0))],
        out_shape=[jax.ShapeDtypeStruct((t, n), BF16),
                   jax.ShapeDtypeStruct((t, n_dt), F32)],
        scratch_shapes=[pltpu.VMEM((tm, d), BF16)],
        compiler_params=_params("arbitrary", "arbitrary"),
        name="rms_matmul",
    )(x, nw, w, w_dt)


def _ssd_kernel(z_ref, xs_ref, b_ref, c_ref, dt_ref, cin_ref, sin_ref, cw_ref, cb_ref,
                dtb_ref, alog_ref, dexp_ref, gnw_ref, e_ref, shift_ref,
                y_ref, sout_ref, state_ref, ebuf_ref, *, c, cv):
    step = pl.program_id(1)
    hist = CONV_HIST

    @pl.when(step == 0)
    def _():
        state_ref[...] = sin_ref[0].T
        ebuf_ref[0:hist, :] = cin_ref[0, 0:hist, :]
        ebuf_ref[hist + c:2 * hist + c, :] = cin_ref[0, hist:2 * hist, :]

    ebuf_ref[hist:hist + cv, 0:D_INNER] = xs_ref[0]
    ebuf_ref[hist:hist + cv, D_INNER:D_INNER + GROUP_W] = b_ref[0]
    ebuf_ref[hist:hist + cv, D_INNER + GROUP_W:CONV_DIM] = c_ref[0]
    if cv < c:
        ebuf_ref[hist + cv:hist + c, :] = jnp.zeros((c - cv, CONV_DIM), BF16)

    shifted = _dot(shift_ref[...], ebuf_ref[...])
    conv = cb_ref[...] + ebuf_ref[hist:hist + c, :].astype(F32) * cw_ref[CONV_W - 1:CONV_W, :]
    for k in range(CONV_W - 1):
        conv = conv + shifted[k * c:(k + 1) * c, :] * cw_ref[k:k + 1, :]
    conv = conv * _sigmoid(conv)
    if cv == c:
        ebuf_ref[0:hist, :] = ebuf_ref[c:c + hist, :]
        ebuf_ref[hist + c:2 * hist + c, :] = jnp.zeros((hist, CONV_DIM), BF16)

    xs = conv[:, 0:D_INNER]
    bm = conv[:, D_INNER:D_INNER + GROUP_W]
    cm = conv[:, D_INNER + GROUP_W:CONV_DIM]

    dt_in = dt_ref[0][:, 0:LANES] + dtb_ref[...]
    dt = jnp.maximum(dt_in, 0.0) + jnp.log1p(jnp.exp(-jnp.abs(dt_in)))
    if cv < c:
        dt = jnp.concatenate([dt, jnp.zeros((c - cv, LANES), F32)], axis=0)
    a = -jnp.exp(alog_ref[...])
    dta = dt * a

    row = lax.broadcasted_iota(jnp.int32, (c, c), 0)
    col = lax.broadcasted_iota(jnp.int32, (c, c), 1)
    tril = col <= row
    tri = jnp.where(tril, 1.0, 0.0).astype(BF16)
    cs = _dot_exact_lhs(tri, dta)
    cs_t = cs.T

    e = e_ref[...]
    dt_x = _dot_exact_rhs(dt, e)
    cs_x = _dot_exact_rhs(cs, e)
    cs_end_x = cs_x[c - 1:c, :]
    xdt = xs * dt_x
    decay_in = jnp.exp(cs_x)
    xdt_b = xdt.astype(BF16)
    xdt_end = (xdt * jnp.exp(cs_end_x - cs_x)).astype(BF16)
    state_decay = jnp.exp(cs_end_x)

    lane = lax.broadcasted_iota(jnp.int32, (c, LANES), 1)
    low_half = lane < SSM_HEAD_DIM

    y_groups = []
    for g in range(SSM_GROUPS):
        gs = slice(g * GROUP_W, (g + 1) * GROUP_W)
        ns = slice(g * D_STATE, (g + 1) * D_STATE)
        bg_t = bm[:, ns].T.astype(BF16)
        cg = cm[:, ns].astype(BF16)
        cb = _dot(cg, bg_t)
        sg = state_ref[:, gs]
        y_off = _dot(cg, sg.astype(BF16)) * decay_in[:, gs]
        pairs = []
        for j in range(GROUP_W // LANES):
            h0 = g * (GROUP_W // SSM_HEAD_DIM) + 2 * j
            xp = xdt_b[:, g * GROUP_W + j * LANES:g * GROUP_W + (j + 1) * LANES]
            ys = []
            for h in (h0, h0 + 1):
                seg = cs[:, h:h + 1] - cs_t[h:h + 1, :]
                m = (cb * jnp.where(tril, jnp.exp(seg), 0.0)).astype(BF16)
                ys.append(_dot(m, xp))
            pairs.append(jnp.where(low_half, ys[0], ys[1]))
        y_groups.append(jnp.concatenate(pairs, axis=1) + y_off)
        state_ref[:, gs] = sg * state_decay[:, gs] + _dot(bg_t, xdt_end[:, gs])

    y = jnp.concatenate(y_groups, axis=1) + dexp_ref[...] * xs
    if cv < c:
        y = y[0:cv]
    zv = z_ref[0].astype(F32)
    y = y * (zv * _sigmoid(zv))
    outs = []
    for g in range(SSM_GROUPS):
        gs = slice(g * GROUP_W, (g + 1) * GROUP_W)
        outs.append(_rms_scale(y[:, gs]))
    y_ref[0] = (jnp.concatenate(outs, axis=1) * gnw_ref[...]).astype(y_ref.dtype)

    @pl.when(step == pl.num_programs(1) - 1)
    def _():
        sout_ref[0] = state_ref[...].T


def ssd(proj, dt_raw, conv_in, state_in, cw, cb, dtb, alog, dexp, gnw, *, c, cv):
    nb, length, _ = proj.shape
    steps = length // cv
    head_of_channel = np.arange(D_INNER) // SSM_HEAD_DIM
    expand = jnp.asarray(np.arange(LANES)[:, None] == head_of_channel[None, :], dtype=BF16)
    tap, pos = np.divmod(np.arange((CONV_W - 1) * c), c)
    src = pos + tap + CONV_HIST - (CONV_W - 1)
    cols = np.arange(c + 2 * CONV_HIST)[None, :]
    from_cache_low = (cols == (src + c + CONV_HIST)[:, None]) & ((pos + tap) < CONV_W - 1)[:, None]
    shift = jnp.asarray((cols == src[:, None]) | from_cache_low, dtype=BF16)
    const = lambda b, l: (0, 0)
    kern = functools.partial(_ssd_kernel, c=c, cv=cv)
    return pl.pallas_call(
        kern,
        grid=(nb, steps),
        in_specs=[
            pl.BlockSpec((1, cv, D_INNER), lambda b, l: (b, l, 0)),
            pl.BlockSpec((1, cv, D_INNER), lambda b, l: (b, l, 1)),
            pl.BlockSpec((1, cv, GROUP_W), lambda b, l: (b, l, 2 * D_INNER // GROUP_W)),
            pl.BlockSpec((1, cv, GROUP_W), lambda b, l: (b, l, 2 * D_INNER // GROUP_W + 1)),
            pl.BlockSpec((1, cv, DT_PAD), lambda b, l: (b, l, 0)),
            pl.BlockSpec((1, 2 * CONV_HIST, CONV_DIM), lambda b, l: (b, 0, 0)),
            pl.BlockSpec((1, D_INNER, D_STATE), lambda b, l: (b, 0, 0)),
            pl.BlockSpec((CONV_W, CONV_DIM), const),
            pl.BlockSpec((1, CONV_DIM), const),
            pl.BlockSpec((1, LANES), const),
            pl.BlockSpec((1, LANES), const),
            pl.BlockSpec((1, D_INNER), const),
            pl.BlockSpec((1, D_INNER), const),
            pl.BlockSpec((LANES, D_INNER), const),
            pl.BlockSpec(((CONV_W - 1) * c, c + 2 * CONV_HIST), const),
        ],
        out_specs=[
            pl.BlockSpec((1, cv, D_INNER), lambda b, l: (b, l, 0)),
            pl.BlockSpec((1, D_INNER, D_STATE), lambda b, l: (b, 0, 0)),
        ],
        out_shape=[
            jax.ShapeDtypeStruct((nb, length, D_INNER), BF16),
            jax.ShapeDtypeStruct((nb, D_INNER, D_STATE), F32),
        ],
        scratch_shapes=[pltpu.VMEM((D_STATE, D_INNER), F32),
                        pltpu.VMEM((c + 2 * CONV_HIST, CONV_DIM), BF16)],
        compiler_params=_params("arbitrary", "arbitrary"),
        name="ssd",
    )(proj, proj, proj, proj, dt_raw, conv_in, state_in, cw, cb, dtb, alog, dexp, gnw, expand, shift)


def _mm_res_kernel(a_ref, w_ref, r_ref, o_ref):
    o_ref[...] = r_ref[...] + _dot(a_ref[...], w_ref[...])


def matmul_residual(a, w, res, *, tm):
    t, k = a.shape
    n = w.shape[1]
    return pl.pallas_call(
        _mm_res_kernel,
        grid=(t // tm,),
        in_specs=[pl.BlockSpec((tm, k), lambda i: (i, 0)),
                  pl.BlockSpec((k, n), lambda i: (0, 0)),
                  pl.BlockSpec((tm, n), lambda i: (i, 0))],
        out_specs=pl.BlockSpec((tm, n), lambda i: (i, 0)),
        out_shape=jax.ShapeDtypeStruct((t, n), F32),
        compiler_params=_params("arbitrary"),
        name="matmul_residual",
    )(a, w, res)


def _ffn_kernel(x_ref, nw_ref, wg_ref, wu_ref, wd_ref, o_ref, h_ref, acc_ref):
    f = pl.program_id(1)

    @pl.when(f == 0)
    def _():
        x = x_ref[...]
        h_ref[...] = (_rms_scale(x) * nw_ref[...]).astype(BF16)
        acc_ref[...] = x

    h = h_ref[...]
    g = _dot(h, wg_ref[...])
    u = _dot(h, wu_ref[...])
    act = (g * _sigmoid(g) * u).astype(BF16)
    acc_ref[...] += _dot(act, wd_ref[...])

    @pl.when(f == pl.num_programs(1) - 1)
    def _():
        o_ref[...] = acc_ref[...]


def swiglu_ffn(x, nw, wg, wu, wd, *, tm, tf):
    t, d = x.shape
    ff = wg.shape[1]
    return pl.pallas_call(
        _ffn_kernel,
        grid=(t // tm, ff // tf),
        in_specs=[pl.BlockSpec((tm, d), lambda i, f: (i, 0)),
                  pl.BlockSpec((1, d), lambda i, f: (0, 0)),
                  pl.BlockSpec((d, tf), lambda i, f: (0, f)),
                  pl.BlockSpec((d, tf), lambda i, f: (0, f)),
                  pl.BlockSpec((tf, d), lambda i, f: (f, 0))],
        out_specs=pl.BlockSpec((tm, d), lambda i, f: (i, 0)),
        out_shape=jax.ShapeDtypeStruct((t, d), F32),
        scratch_shapes=[pltpu.VMEM((tm, d), BF16), pltpu.VMEM((tm, d), F32)],
        compiler_params=_params("arbitrary", "arbitrary"),
        name="swiglu_ffn",
    )(x, nw, wg, wu, wd)


def _qkv_kernel(x_ref, kvw_ref, aw_ref, wk_ref, wv_ref, wq_ref, cos_ref, sa_ref, sb_ref,
                k_ref, v_ref, *extra_refs, transposed):
    xn = _rms_scale(x_ref[...])
    hkv = (xn * kvw_ref[...]).astype(BF16)
    hq = (xn * aw_ref[...]).astype(BF16)
    cos, sa, sb = cos_ref[...], sa_ref[...], sb_ref[...]

    def rope(t):
        blocks = []
        for j in range(t.shape[1] // LANES):
            tb = t[:, j * LANES:(j + 1) * LANES]
            blocks.append(tb * cos + pltpu.roll(tb, LANES - ROT_DIM // 2, 1) * sa
                          + pltpu.roll(tb, ROT_DIM // 2, 1) * sb)
        return jnp.concatenate(blocks, axis=1)

    k = rope(_dot(hkv, wk_ref[...]))
    v = _dot(hkv, wv_ref[...])
    q = rope(_dot(hq, wq_ref[...]))
    tm = k.shape[0]
    for h in range(N_HEADS):
        rows = pl.ds(h, tm, stride=N_HEADS)
        k_ref[rows, :] = k[:, h * LANES:(h + 1) * LANES]
        v_ref[rows, :] = v[:, h * LANES:(h + 1) * LANES]
    if transposed:
        kb_ref, vt_ref, qt_ref = extra_refs
        kb_ref[...] = k.astype(BF16)
        vt_ref[0] = v.T.astype(BF16)
        qt_ref[0] = (q * (DK ** -0.5 * LOG2E)).T.astype(BF16)
    else:
        (q_ref,) = extra_refs
        q_ref[...] = (q * (DK ** -0.5)).astype(BF16)


def _rope_tables(pos):
    half = ROT_DIM // 2
    inv_freq = ROPE_THETA ** (-jnp.arange(half, dtype=F32) / half)
    ang = pos.astype(F32)[:, None] * inv_freq[None, :]
    cos, sin = jnp.cos(ang), jnp.sin(ang)
    n = pos.shape[0]
    ones = jnp.ones((n, DK - ROT_DIM), F32)
    zeros_h = jnp.zeros((n, half), F32)
    zeros_r = jnp.zeros((n, DK - ROT_DIM), F32)
    cos_t = jnp.concatenate([cos, cos, ones], axis=1)
    sa_t = jnp.concatenate([-sin, zeros_h, zeros_r], axis=1)
    sb_t = jnp.concatenate([zeros_h, sin, zeros_r], axis=1)
    rep = lambda t: jnp.tile(t, (1, LANES // DK))
    return rep(cos_t), rep(sa_t), rep(sb_t)


def qkv_proj(x, kvw, aw, wk, wv, wq, tables, *, tm, streams, transposed):
    t, d = x.shape
    period = tables[0].shape[0] // tm
    row = lambda i: (i, 0)
    const = lambda i: (0, 0)
    tab = lambda i: (i % period, 0)
    heads_out = jax.ShapeDtypeStruct((t * N_HEADS, LANES), F32)
    heads_spec = pl.BlockSpec((tm * N_HEADS, LANES), row)
    out_specs = [heads_spec, heads_spec, pl.BlockSpec((tm, d), row)]
    out_shape = [heads_out, heads_out, jax.ShapeDtypeStruct((t, d), BF16)]
    if transposed:
        per = t // streams // tm
        tr_spec = pl.BlockSpec((1, d, tm), lambda i: (i // per, 0, i % per))
        tr_out = jax.ShapeDtypeStruct((streams, d, t // streams), BF16)
        out_specs += [tr_spec, tr_spec]
        out_shape += [tr_out, tr_out]
    return pl.pallas_call(
        functools.partial(_qkv_kernel, transposed=transposed),
        grid=(t // tm,),
        in_specs=[pl.BlockSpec((tm, d), row),
                  pl.BlockSpec((1, d), const), pl.BlockSpec((1, d), const),
                  pl.BlockSpec((d, d), const), pl.BlockSpec((d, d), const), pl.BlockSpec((d, d), const),
                  pl.BlockSpec((tm, LANES), tab), pl.BlockSpec((tm, LANES), tab),
                  pl.BlockSpec((tm, LANES), tab)],
        out_specs=out_specs,
        out_shape=out_shape,
        compiler_params=_params("arbitrary"),
        name="qkv_proj",
    )(x, kvw, aw, wk, wv, wq, *tables)


def _lambda_value(lp_ref, lambda_init):
    lp = lp_ref[...]
    s1 = jnp.sum(lp[0:1] * lp[1:2], axis=-1, keepdims=True)
    s2 = jnp.sum(lp[2:3] * lp[3:4], axis=-1, keepdims=True)
    return jnp.exp(s1) - jnp.exp(s2) + lambda_init


def _diff_finish(o1, o2, lam, subw, lambda_init):
    o = o1 - lam * o2
    return _rms_scale(o) * subw * (1.0 - lambda_init)


def _attn_kernel(qt_ref, kt_ref, q_ref, k_ref, v_ref, lp_ref, subw_ref, o_ref,
                 qs_ref, *state_refs, tq, tk, lambda_init):
    p = pl.program_id(1)
    qi = qt_ref[p]
    ki = kt_ref[p]
    n_qc = 2 * tq // Q_COLS
    m_refs, acc_refs = state_refs[:n_qc], state_refs[n_qc:]

    @pl.when(ki == 0)
    def _():
        for m_ref, acc_ref in zip(m_refs, acc_refs):
            m_ref[...] = jnp.full(m_ref.shape, NEG_BIG, F32)
            acc_ref[...] = jnp.zeros(acc_ref.shape, F32)
        first_sub = lax.broadcasted_iota(jnp.int32, (LANES, tq), 0) < DK
        for h in range(N_HEADS):
            qh = q_ref[0, h * LANES:(h + 1) * LANES, :]
            zero = jnp.zeros_like(qh)
            qs_ref[h, :, 0:tq] = jnp.where(first_sub, qh, zero)
            qs_ref[h, :, tq:2 * tq] = jnp.where(first_sub, zero, qh)

    ones = jnp.ones((ONES_ROWS, tk), BF16)

    def sweep(masked):
        if masked:
            k_chunk = (ki * tk + lax.broadcasted_iota(jnp.int32, (tk, Q_COLS), 0)) >> CHUNK_SHIFT
            q_lane = lax.broadcasted_iota(jnp.int32, (tk, Q_COLS), 1)

        def scores(h, c):
            kh = k_ref[0, :, h * LANES:(h + 1) * LANES]
            s = _dot(kh, qs_ref[h, :, c * Q_COLS:(c + 1) * Q_COLS])
            if masked:
                q_chunk = (qi * tq + (c * Q_COLS) % tq + q_lane) >> CHUNK_SHIFT
                s = jnp.where(k_chunk <= q_chunk, s, NEG_BIG)
            return s

        groups = [(h, c) for h in range(N_HEADS) for c in range(n_qc)]
        pending = [scores(*g) for g in groups[:SCORE_LOOKAHEAD]]
        for i, (h, c) in enumerate(groups):
            s = pending.pop(0)
            if i + SCORE_LOOKAHEAD < len(groups):
                pending.append(scores(*groups[i + SCORE_LOOKAHEAD]))
            vt = jnp.concatenate([v_ref[0, h * LANES:(h + 1) * LANES, :], ones], axis=0)
            m_prev = m_refs[c][h:h + 1, :]
            m_new = jnp.maximum(m_prev, jnp.max(s, axis=0, keepdims=True))
            alpha = jnp.exp2(m_prev - m_new)
            pr = jnp.exp2(s - m_new).astype(BF16)
            acc_refs[c][h] = acc_refs[c][h] * alpha + _dot(vt, pr)
            m_refs[c][h:h + 1, :] = m_new

    @pl.when(ki < qi)
    def _():
        sweep(False)

    @pl.when(ki == qi)
    def _():
        sweep(True)
        lam = _lambda_value(lp_ref, lambda_init)
        subw = subw_ref[...]

        for h in range(N_HEADS):
            hs = slice(h * LANES, (h + 1) * LANES)
            for r in range(tq // Q_COLS):
                a1 = acc_refs[r][h]
                a2 = acc_refs[r + tq // Q_COLS][h]
                o_t = a1[0:DV] / a1[DV:DV + 1] - lam * (a2[0:DV] / a2[DV:DV + 1])
                res = _rms_scale(o_t.T) * subw * (1.0 - lambda_init)
                o_ref[0, r * Q_COLS:(r + 1) * Q_COLS, hs] = res.astype(o_ref.dtype)


def diff_attention(q_t, k, v_t, lam_params, subw, *, tq, lambda_init):
    nb, d, length = q_t.shape
    tk = tq
    nq = length // tq
    qt = np.concatenate([np.full(i + 1, i) for i in range(nq)]).astype(np.int32)
    kt = np.concatenate([np.arange(i + 1) for i in range(nq)]).astype(np.int32)
    kern = functools.partial(_attn_kernel, tq=tq, tk=tk, lambda_init=lambda_init)
    n_qc = 2 * tq // Q_COLS
    grid_spec = pltpu.PrefetchScalarGridSpec(
        num_scalar_prefetch=2,
        grid=(nb, len(qt)),
        in_specs=[pl.BlockSpec((1, d, tq), lambda b, p, qt_r, kt_r: (b, 0, qt_r[p])),
                  pl.BlockSpec((1, tk, d), lambda b, p, qt_r, kt_r: (b, kt_r[p], 0)),
                  pl.BlockSpec((1, d, tk), lambda b, p, qt_r, kt_r: (b, 0, kt_r[p])),
                  pl.BlockSpec((4, DK), lambda b, p, qt_r, kt_r: (0, 0)),
                  pl.BlockSpec((1, DV), lambda b, p, qt_r, kt_r: (0, 0))],
        out_specs=pl.BlockSpec((1, tq, d), lambda b, p, qt_r, kt_r: (b, qt_r[p], 0)),
        scratch_shapes=([pltpu.VMEM((N_HEADS, LANES, 2 * tq), BF16)]
                        + [pltpu.VMEM((N_HEADS, Q_COLS), F32)] * n_qc
                        + [pltpu.VMEM((N_HEADS, DV + ONES_ROWS, Q_COLS), F32)] * n_qc),
    )
    return pl.pallas_call(
        kern,
        grid_spec=grid_spec,
        out_shape=jax.ShapeDtypeStruct((nb, length, d), BF16),
        compiler_params=_params("arbitrary", "arbitrary"),
        name="diff_attention",
    )(jnp.asarray(qt), jnp.asarray(kt), q_t, k, v_t, lam_params, subw)


def _attn_cached_kernel(q_ref, ck_ref, cv_ref, kn_ref, vn_ref, lp_ref, subw_ref, o_ref,
                        *, nq, lambda_init):
    past = ck_ref.shape[1] // N_HEADS
    lane = lax.broadcasted_iota(jnp.int32, (nq, LANES), 1)
    nt = (((1,), (1,)), ((), ()))
    pad = jnp.zeros((LANES - nq, LANES), BF16)
    lam = _lambda_value(lp_ref, lambda_init)
    for h in range(N_HEADS):
        q = q_ref[0, :, h * LANES:(h + 1) * LANES]
        zero = jnp.zeros_like(q)
        qs = jnp.concatenate([jnp.where(lane < DK, q, zero), jnp.where(lane < DK, zero, q)], axis=0)
        old_rows = pl.ds(h, past, stride=N_HEADS)
        new_rows = pl.ds(h, nq, stride=N_HEADS)
        kc = ck_ref[0, old_rows, :].astype(BF16)
        vc = cv_ref[0, old_rows, :].astype(BF16)
        kn = jnp.concatenate([kn_ref[0, new_rows, :].astype(BF16), pad], axis=0)
        vn = jnp.concatenate([vn_ref[0, new_rows, :].astype(BF16), pad], axis=0)
        s_c = lax.dot_general(qs, kc, nt, preferred_element_type=F32)
        s_n = lax.dot_general(qs, kn, nt, preferred_element_type=F32)
        col = lax.broadcasted_iota(jnp.int32, s_n.shape, 1)
        s_n = jnp.where(col < nq, s_n, NEG_BIG)
        m = jnp.maximum(jnp.max(s_c, axis=-1, keepdims=True), jnp.max(s_n, axis=-1, keepdims=True))
        p_c = jnp.exp(s_c - m)
        p_n = jnp.exp(s_n - m)
        denom = jnp.sum(p_c, axis=-1, keepdims=True) + jnp.sum(p_n, axis=-1, keepdims=True)
        o = (_dot(p_c.astype(BF16), vc) + _dot(p_n.astype(BF16), vn)) / denom
        res = _diff_finish(o[0:nq], o[nq:2 * nq], lam, subw_ref[...], lambda_init)
        o_ref[0, :, h * LANES:(h + 1) * LANES] = res.astype(o_ref.dtype)


def diff_attention_cached(q, cache_k, cache_v, k_new, v_new, lam_params, subw, *, lambda_init):
    nb, nq, d = q.shape
    past = cache_k.shape[1]
    kern = functools.partial(_attn_cached_kernel, nq=nq, lambda_init=lambda_init)
    stream = lambda b: (b, 0, 0)
    const = lambda b: (0, 0)
    rows = lambda a: a.reshape(nb, a.shape[1] * N_HEADS, LANES)
    cache_k, cache_v, k_new, v_new = rows(cache_k), rows(cache_v), rows(k_new), rows(v_new)
    return pl.pallas_call(
        kern,
        grid=(nb,),
        in_specs=[pl.BlockSpec((1, nq, d), lambda b: (b, 0, 0)),
                  pl.BlockSpec((1, past * N_HEADS, LANES), stream),
                  pl.BlockSpec((1, past * N_HEADS, LANES), stream),
                  pl.BlockSpec((1, nq * N_HEADS, LANES), stream),
                  pl.BlockSpec((1, nq * N_HEADS, LANES), stream),
                  pl.BlockSpec((4, DK), const),
                  pl.BlockSpec((1, DV), const)],
        out_specs=pl.BlockSpec((1, nq, d), lambda b: (b, 0, 0)),
        out_shape=jax.ShapeDtypeStruct((nb, nq, d), BF16),
        compiler_params=_params("arbitrary"),
        name="diff_attention_cached",
    )(q, cache_k, cache_v, k_new, v_new, lam_params, subw)


def _route(logits):
    lane = lax.broadcasted_iota(jnp.int32, logits.shape, 1)
    valid = lane < N_EXPERTS
    lg = jnp.where(valid, logits, NEG_BIG)
    m1 = jnp.max(lg, axis=-1, keepdims=True)
    lane_f = lane.astype(F32)
    i1 = jnp.min(jnp.where(lg == m1, lane_f, float(LANES)), axis=-1, keepdims=True)
    lg2 = jnp.where(lane_f == i1, NEG_BIG, lg)
    m2 = jnp.max(lg2, axis=-1, keepdims=True)
    i2 = jnp.min(jnp.where(lg2 == m2, lane_f, float(LANES)), axis=-1, keepdims=True)
    e2 = jnp.exp(m2 - m1)
    g1 = 1.0 / (1.0 + e2)
    g2 = e2 / (1.0 + e2)
    return jnp.where(lane_f == i1, g1, 0.0) + jnp.where(lane_f == i2, g2, 0.0)


def _moe_route_kernel(x_ref, nw_ref, wr_ref, h_ref, comb_ref, rk_ref, rkt_ref, cnt_ref,
                      meta_ref, metat_ref, *, rt):
    t = x_ref.shape[0]
    hf = _rms_scale(x_ref[...]) * nw_ref[...]
    h_ref[...] = hf.astype(BF16)
    comb = _route(_dot_exact_rhs_general(hf, wr_ref[...]))
    comb_ref[...] = comb
    sel = jnp.where(comb > 0.0, 1.0, 0.0)
    row = lax.broadcasted_iota(jnp.int32, (rt, rt), 0)
    col = lax.broadcasted_iota(jnp.int32, (rt, rt), 1)
    strict = jnp.where(col < row, 1.0, 0.0).astype(BF16)
    carry = jnp.zeros((1, LANES), F32)
    ranks = []
    for s in range(t // rt):
        sel_s = sel[s * rt:(s + 1) * rt]
        rank_s = _dot(strict, sel_s.astype(BF16)) + carry
        ranks.append(jnp.where(sel_s > 0.0, rank_s, -1.0))
        carry = carry + jnp.sum(sel_s, axis=0, keepdims=True)
    rk = jnp.concatenate(ranks, axis=0) if len(ranks) > 1 else ranks[0]
    rk_ref[...] = rk.astype(jnp.int32)
    rkt_ref[0] = rk.T.astype(jnp.int32)
    cnt_ref[0] = jnp.broadcast_to(carry, (8, LANES)).astype(jnp.int32)

    padded = jnp.floor((carry + (MOE_ROW_ALIGN - 1)) * (1.0 / MOE_ROW_ALIGN)) * MOE_ROW_ALIGN
    lrow = lax.broadcasted_iota(jnp.int32, (LANES, LANES), 0)
    lcol = lax.broadcasted_iota(jnp.int32, (LANES, LANES), 1)
    before = jnp.where(lrow < lcol, 1.0, 0.0).astype(BF16)
    group_start = _dot(jnp.broadcast_to(padded, (8, LANES)).astype(BF16), before)[0:1]
    dest = jnp.where(sel > 0.0, rk + group_start, -1.0)
    lane_f = lax.broadcasted_iota(jnp.int32, (t, LANES), 1).astype(F32)
    la = jnp.min(jnp.where(sel > 0.0, lane_f, float(LANES)), axis=-1, keepdims=True)
    lb = jnp.max(jnp.where(sel > 0.0, lane_f, -1.0), axis=-1, keepdims=True)
    at = lambda v, l: jnp.sum(jnp.where(lane_f == l, v, 0.0), axis=-1, keepdims=True)
    two = lb > la
    d_a, g_a = at(dest, la), at(comb, la)
    d_b = jnp.where(two, at(dest, lb), -1.0)
    g_b = jnp.where(two, at(comb, lb), 0.0)
    meta = jnp.where(lane_f == 0.0, d_a, jnp.where(lane_f == 1.0, d_b,
                     jnp.where(lane_f == 2.0, g_a, jnp.where(lane_f == 3.0, g_b, 0.0))))
    meta_ref[...] = meta
    metat_ref[0] = meta.T[0:8]


def moe_route(x, nw, wr, *, tb, rt):
    t, d = x.shape
    nb = t // tb
    row = lambda b: (b, 0)
    const = lambda b: (0, 0)
    return pl.pallas_call(
        functools.partial(_moe_route_kernel, rt=rt),
        grid=(nb,),
        in_specs=[pl.BlockSpec((tb, d), row), pl.BlockSpec((1, d), const),
                  pl.BlockSpec((d, LANES), const)],
        out_specs=[pl.BlockSpec((tb, d), row), pl.BlockSpec((tb, LANES), row),
                   pl.BlockSpec((tb, LANES), row),
                   pl.BlockSpec((1, LANES, tb), lambda b: (b, 0, 0)),
                   pl.BlockSpec((1, 8, LANES), lambda b: (b, 0, 0)),
                   pl.BlockSpec((tb, LANES), row),
                   pl.BlockSpec((1, 8, tb), lambda b: (b, 0, 0))],
        out_shape=[jax.ShapeDtypeStruct((t, d), BF16),
                   jax.ShapeDtypeStruct((t, LANES), F32),
                   jax.ShapeDtypeStruct((t, LANES), jnp.int32),
                   jax.ShapeDtypeStruct((nb, LANES, tb), jnp.int32),
                   jax.ShapeDtypeStruct((nb, 8, LANES), jnp.int32),
                   jax.ShapeDtypeStruct((t, LANES), F32),
                   jax.ShapeDtypeStruct((nb, 8, tb), F32)],
        compiler_params=_params("arbitrary"),
        name="moe_route",
    )(x, nw, wr)


def _moe_expert_kernel(cnt_ref, x_ref, h_ref, comb_ref, rk_ref, rkt_ref, wg_ref, wu_ref, wd_ref,
                       fw_ref, o_ref, rkc_ref, xg_ref, gs_ref, yacc_ref, *, rt):
    b = pl.program_id(0)
    e = pl.program_id(1)
    f = pl.program_id(2)
    t = x_ref.shape[0]
    n_rows = cnt_ref[b * N_EXPERTS + e]
    half = rt // 2
    n_tiles = (n_rows + (half - 1)) // rt
    tail_row0 = pl.multiple_of(n_tiles * rt, rt)
    has_tail = n_rows > n_tiles * rt

    def over_tiles(fn):
        def body(j, carry):
            fn(pl.multiple_of(j * rt, rt), rt)
            return carry

        lax.fori_loop(0, n_tiles, body, 0)

        @pl.when(has_tail)
        def _():
            fn(tail_row0, half)

    @pl.when(jnp.logical_and(e == 0, f == 0))
    def _():
        o_ref[...] = x_ref[...]

    @pl.when(f == 0)
    def _():
        lane = lax.broadcasted_iota(jnp.int32, (t, LANES), 1)
        pick = lane == e
        rkc = jnp.sum(jnp.where(pick, rk_ref[...], 0).astype(F32), axis=-1, keepdims=True)
        rkc_ref[...] = jnp.broadcast_to(rkc, (t, LANES)).astype(jnp.int32)
        gate = jnp.sum(jnp.where(pick, comb_ref[...], 0.0), axis=-1, keepdims=True)
        g_hi, g_mid, g_lo = _split3(jnp.broadcast_to(gate, (t, LANES)))
        g3 = jnp.where(lane == 0, g_hi.astype(F32),
                       jnp.where(lane == 1, g_mid.astype(F32),
                                 jnp.where(lane == 2, g_lo.astype(F32), 0.0))).astype(BF16)
        rk_row = rkt_ref[0, pl.ds(e, 1), :]

        def gather(row0, size):
            tile = pl.ds(row0, size)
            rows = lax.broadcasted_iota(jnp.int32, (size, t), 0) + row0
            p = jnp.where(rk_row == rows, 1.0, 0.0).astype(BF16)
            xg_ref[tile, :] = _dot(p, h_ref[...]).astype(BF16)
            gate_rows = jnp.sum(_dot(p, g3), axis=-1, keepdims=True)
            gs_ref[tile, :] = jnp.broadcast_to(gate_rows, (size, LANES))
            yacc_ref[tile, :] = jnp.zeros((size, yacc_ref.shape[1]), F32)

        over_tiles(gather)

    def expert(row0, size):
        tile = pl.ds(row0, size)
        xg = xg_ref[tile, :]
        g = _dot(xg, wg_ref[...])
        u = _dot(xg, wu_ref[...])
        gates = _lane_tile(gs_ref[tile, :], g.shape[1] // LANES)
        act = (g * _sigmoid(g) * u * gates).astype(BF16)
        yacc_ref[tile, :] += _dot(act, wd_ref[...])

    over_tiles(expert)

    @pl.when(f == pl.num_programs(2) - 1)
    def _():
        def scatter(row0, size):
            cols = lax.broadcasted_iota(jnp.int32, (t, size), 1) + row0
            s = jnp.where(_lane_tile(rkc_ref[...], size // LANES) == cols, 1.0, 0.0).astype(BF16)
            o_ref[...] += _dot(s, yacc_ref[pl.ds(row0, size), :].astype(BF16))

        over_tiles(scatter)

    @pl.when(jnp.logical_and(e == pl.num_programs(1) - 1, f == pl.num_programs(2) - 1))
    def _():
        o_ref[...] = _rms_scale(o_ref[...]) * fw_ref[...]


def _dot_exact_rhs_general(x, w):
    xh, xm, xl = _split3(x)
    wh, wm, wl = _split3(w)
    return (_dot(xh, wh) + (_dot(xh, wm) + _dot(xm, wh))
            + (_dot(xh, wl) + _dot(xm, wm) + _dot(xl, wh)))


def moe_ffn_final(x, nw, wr, wg, wu, wd, fw, *, tb, rt, tf):
    t, d = x.shape
    ne, _, ff = wg.shape
    nb = t // tb
    h, comb, rk, rkt, cnt, _, _ = moe_route(x, nw, wr, tb=tb, rt=rt)
    counts = cnt[:, 0, :ne].reshape(nb * ne)
    once = pl.Buffered(1)
    blk = lambda b, e, f, c: (b, 0)
    const = lambda b, e, f, c: (0, 0)
    grid_spec = pltpu.PrefetchScalarGridSpec(
        num_scalar_prefetch=1,
        grid=(nb, ne, ff // tf),
        in_specs=[pl.BlockSpec((tb, d), blk, pipeline_mode=once),
                  pl.BlockSpec((tb, d), blk, pipeline_mode=once),
                  pl.BlockSpec((tb, LANES), blk, pipeline_mode=once),
                  pl.BlockSpec((tb, LANES), blk, pipeline_mode=once),
                  pl.BlockSpec((1, LANES, tb), lambda b, e, f, c: (b, 0, 0), pipeline_mode=once),
                  pl.BlockSpec((None, d, tf), lambda b, e, f, c: (e, 0, f)),
                  pl.BlockSpec((None, d, tf), lambda b, e, f, c: (e, 0, f)),
                  pl.BlockSpec((None, tf, d), lambda b, e, f, c: (e, f, 0)),
                  pl.BlockSpec((1, d), const)],
        out_specs=pl.BlockSpec((tb, d), blk),
        scratch_shapes=[pltpu.VMEM((tb, LANES), jnp.int32),
                        pltpu.VMEM((tb, d), BF16),
                        pltpu.VMEM((tb, LANES), F32),
                        pltpu.VMEM((tb, d), F32)],
    )
    return pl.pallas_call(
        functools.partial(_moe_expert_kernel, rt=rt),
        grid_spec=grid_spec,
        out_shape=jax.ShapeDtypeStruct((t, d), F32),
        compiler_params=_params("arbitrary", "arbitrary", "arbitrary"),
        name="moe_experts",
    )(counts, x, h, comb, rk, rkt, wg, wu, wd, fw)


MOE_ROW_ALIGN = 16
MOE_ROW_TILE = 128
MOE_GROUP_TILE = 1024


def _moe_plan(cnt, *, tb, region):
    nb, ne = cnt.shape
    padded = (cnt + (MOE_ROW_ALIGN - 1)) // MOE_ROW_ALIGN * MOE_ROW_ALIGN
    off = jnp.cumsum(padded, axis=0) - padded
    boff = jnp.cumsum(padded, axis=1) - padded
    total = jnp.sum(padded, axis=0)
    tiles = (total + MOE_ROW_TILE + MOE_GROUP_TILE - 1) // MOE_GROUP_TILE
    tile_end = jnp.cumsum(tiles)
    max_tiles = (2 * nb * tb + nb * ne * MOE_ROW_ALIGN + ne * MOE_ROW_TILE) // MOE_GROUP_TILE + ne + 1
    idx = jnp.arange(max_tiles, dtype=jnp.int32)
    last = jnp.maximum(tile_end[-1] - 1, 0)
    active = idx < tile_end[-1]
    ii = jnp.minimum(idx, last)
    te = jnp.searchsorted(tile_end, ii, side="right").astype(jnp.int32)
    k = ii - (tile_end - tiles)[te]
    tr = te * (region // MOE_GROUP_TILE) + k
    valid = jnp.clip(total[te] - k * MOE_GROUP_TILE, 0, MOE_GROUP_TILE)
    tv = jnp.where(active, valid, -1)
    flat = lambda a: a.reshape(-1).astype(jnp.int32)
    return flat(off), flat(boff), te, tr.astype(jnp.int32), tv.astype(jnp.int32)


def _block_rows(tb):
    need = 2 * tb + N_EXPERTS * MOE_ROW_ALIGN
    return -(-need // MOE_ROW_TILE) * MOE_ROW_TILE


def _for_group_pieces(cnt_ref, off_ref, boff_ref, blk, e, tb, region, enabled, make_copy, action):
    units = (cnt_ref[blk * N_EXPERTS + e] + (MOE_ROW_ALIGN - 1)) // MOE_ROW_ALIGN
    src0 = boff_ref[blk * N_EXPERTS + e]
    dst0 = e * region + off_ref[blk * N_EXPERTS + e]
    done = 0
    for k in reversed(range(_piece_bits(tb))):
        size = MOE_ROW_ALIGN << k
        has = (units >> k) & 1
        src = pl.ds(pl.multiple_of(src0 + done, MOE_ROW_ALIGN), size)
        dst = pl.ds(pl.multiple_of(dst0 + done, MOE_ROW_ALIGN), size)

        @pl.when(jnp.logical_and(enabled, has == 1))
        def _():
            action(make_copy(k, src, dst))

        done = done + has * size


def _piece_bits(tb):
    return (tb // MOE_ROW_ALIGN).bit_length()


def _moe_gather_kernel(cnt_ref, off_ref, boff_ref, h_ref, metat_ref, xs_ref, buf_ref, sem_ref, *, region):
    b = pl.program_id(0)
    tb = h_ref.shape[0]
    rows_b = buf_ref.shape[1]
    slot = lax.rem(b, 2)

    dest = metat_ref[0].astype(jnp.int32)
    rows = lax.broadcasted_iota(jnp.int32, (rows_b, tb), 0)
    onehot = jnp.where(rows == dest[0:1], 1.0, jnp.where(rows == dest[1:2], 1.0, 0.0)).astype(BF16)
    buf_ref[slot] = _dot(onehot, h_ref[...]).astype(BF16)

    def pieces(blk, sl, enabled, action):
        for e in range(N_EXPERTS):
            def make_copy(p, src_rows, dst_rows, e=e):
                return pltpu.make_async_copy(buf_ref.at[sl, src_rows], xs_ref.at[dst_rows],
                                             sem_ref.at[sl, e, p])
            _for_group_pieces(cnt_ref, off_ref, boff_ref, blk, e, tb, region, enabled, make_copy, action)

    pieces(jnp.maximum(b - 1, 0), 1 - slot, b > 0, lambda cp: cp.wait())
    pieces(b, slot, True, lambda cp: cp.start())
    pieces(b, slot, b == pl.num_programs(0) - 1, lambda cp: cp.wait())


def _moe_grouped_kernel(te_ref, tr_ref, tv_ref, x_ref, wg_ref, wu_ref, wd_ref, y_ref, xm_ref, acc_ref):
    i = pl.program_id(0)
    f = pl.program_id(1)
    valid = tv_ref[i]
    last_f = pl.num_programs(1) - 1

    @pl.when(valid > 0)
    def _():
        @pl.when(f == 0)
        def _():
            row = lax.broadcasted_iota(jnp.int32, x_ref.shape, 0)
            xm_ref[...] = jnp.where(row < valid, x_ref[...].astype(F32), 0.0).astype(BF16)
            acc_ref[...] = jnp.zeros(acc_ref.shape, F32)

        xm = xm_ref[...]
        g = _dot(xm, wg_ref[...])
        u = _dot(xm, wu_ref[...])
        acc_ref[...] += _dot((g * _sigmoid(g) * u).astype(BF16), wd_ref[...])

        @pl.when(f == last_f)
        def _():
            y_ref[...] = acc_ref[...].astype(y_ref.dtype)

    @pl.when(jnp.logical_and(valid == 0, f == last_f))
    def _():
        y_ref[...] = jnp.zeros(y_ref.shape, y_ref.dtype)


def _moe_combine_kernel(cnt_ref, off_ref, boff_ref, x_ref, meta_ref, ys_ref, fw_ref, o_ref,
                        buf_ref, sem_ref, *, region):
    b = pl.program_id(0)
    tb = x_ref.shape[0]
    rows_b = buf_ref.shape[0]

    @pl.when(b == 0)
    def _():
        buf_ref[...] = jnp.zeros(buf_ref.shape, BF16)

    def pieces(action):
        for e in range(N_EXPERTS):
            def make_copy(p, buf_rows, sorted_rows, e=e):
                return pltpu.make_async_copy(ys_ref.at[sorted_rows], buf_ref.at[buf_rows], sem_ref.at[e, p])
            _for_group_pieces(cnt_ref, off_ref, boff_ref, b, e, tb, region, True, make_copy, action)

    pieces(lambda cp: cp.start())
    meta = meta_ref[...]
    col = lax.broadcasted_iota(jnp.int32, (tb, rows_b), 1)
    d_a = meta[:, 0:1].astype(jnp.int32)
    d_b = meta[:, 1:2].astype(jnp.int32)
    g_a, g_b = meta[:, 2:3], meta[:, 3:4]
    hi = lambda g: g.astype(BF16).astype(F32)
    hit_a, hit_b = col == d_a, col == d_b
    s_hi = jnp.where(hit_a, hi(g_a), jnp.where(hit_b, hi(g_b), 0.0)).astype(BF16)
    s_lo = jnp.where(hit_a, g_a - hi(g_a), jnp.where(hit_b, g_b - hi(g_b), 0.0)).astype(BF16)
    pieces(lambda cp: cp.wait())
    y = buf_ref[...]
    out = x_ref[...] + _dot(jnp.concatenate([s_hi, s_lo], axis=1), jnp.concatenate([y, y], axis=0))
    o_ref[...] = _rms_scale(out) * fw_ref[...]


def moe_ffn_final_sorted(x, nw, wr, wg, wu, wd, fw, *, tb, tf):
    t, d = x.shape
    ne, _, ff = wg.shape
    nb = t // tb
    st = MOE_GROUP_TILE
    region = -(-(tb * nb + MOE_ROW_ALIGN * nb + tb) // st) * st
    h, _, _, _, cnt, meta, meta_t = moe_route(x, nw, wr, tb=tb, rt=tb)
    counts = cnt[:, 0, :ne]
    off, boff, te, tr, tv = _moe_plan(counts, tb=tb, region=region)
    counts = counts.reshape(nb * ne)
    n_tiles = te.shape[0]
    rows_b = _block_rows(tb)
    n_pieces = _piece_bits(tb)

    xs = pl.pallas_call(
        functools.partial(_moe_gather_kernel, region=region),
        grid_spec=pltpu.PrefetchScalarGridSpec(
            num_scalar_prefetch=3,
            grid=(nb,),
            in_specs=[pl.BlockSpec((tb, d), lambda b, c, o, bo: (b, 0)),
                      pl.BlockSpec((1, 8, tb), lambda b, c, o, bo: (b, 0, 0))],
            out_specs=pl.BlockSpec(memory_space=pl.ANY),
            scratch_shapes=[pltpu.VMEM((2, rows_b, d), BF16),
                            pltpu.SemaphoreType.DMA((2, ne, n_pieces))],
        ),
        out_shape=jax.ShapeDtypeStruct((ne * region, d), BF16),
        compiler_params=_params("arbitrary"),
        name="moe_gather",
    )(counts, off, boff, h, meta_t)

    last_f = ff // tf - 1
    used_f = lambda i, f, te_r, tr_r, tv_r: jnp.where(tv_r[i] > 0, f, last_f)
    ys = pl.pallas_call(
        _moe_grouped_kernel,
        grid_spec=pltpu.PrefetchScalarGridSpec(
            num_scalar_prefetch=3,
            grid=(n_tiles, ff // tf),
            in_specs=[pl.BlockSpec((st, d), lambda i, f, te_r, tr_r, tv_r: (tr_r[i], 0)),
                      pl.BlockSpec((None, d, tf), lambda i, f, te_r, tr_r, tv_r:
                                   (te_r[i], 0, used_f(i, f, te_r, tr_r, tv_r))),
                      pl.BlockSpec((None, d, tf), lambda i, f, te_r, tr_r, tv_r:
                                   (te_r[i], 0, used_f(i, f, te_r, tr_r, tv_r))),
                      pl.BlockSpec((None, tf, d), lambda i, f, te_r, tr_r, tv_r:
                                   (te_r[i], used_f(i, f, te_r, tr_r, tv_r), 0))],
            out_specs=pl.BlockSpec((st, d), lambda i, f, te_r, tr_r, tv_r: (tr_r[i], 0)),
            scratch_shapes=[pltpu.VMEM((st, d), BF16), pltpu.VMEM((st, d), F32)],
        ),
        out_shape=jax.ShapeDtypeStruct((ne * region, d), BF16),
        compiler_params=_params("arbitrary", "arbitrary"),
        name="moe_grouped",
    )(te, tr, tv, xs, wg, wu, wd)

    return pl.pallas_call(
        functools.partial(_moe_combine_kernel, region=region),
        grid_spec=pltpu.PrefetchScalarGridSpec(
            num_scalar_prefetch=3,
            grid=(nb,),
            in_specs=[pl.BlockSpec((tb, d), lambda b, c, o, bo: (b, 0)),
                      pl.BlockSpec((tb, LANES), lambda b, c, o, bo: (b, 0)),
                      pl.BlockSpec(memory_space=pl.ANY),
                      pl.BlockSpec((1, d), lambda b, c, o, bo: (0, 0))],
            out_specs=pl.BlockSpec((tb, d), lambda b, c, o, bo: (b, 0)),
            scratch_shapes=[pltpu.VMEM((rows_b, d), BF16), pltpu.SemaphoreType.DMA((ne, n_pieces))],
        ),
        out_shape=jax.ShapeDtypeStruct((t, d), F32),
        compiler_params=_params("arbitrary"),
        name="moe_combine",
    )(counts, off, boff, x, meta, ys, fw)


def _row(v):
    return v.reshape(1, -1).astype(F32)


def _pad_lanes(v, width):
    v = _row(v)
    return jnp.pad(v, ((0, 0), (0, width - v.shape[1])))


def _trunk(x, pos, conv_in, state_in, past_k, past_v, W, *, ssd_chunk, ssd_valid, tm, tq):
    nb, length, d = x.shape
    t = nb * length
    x0 = x.reshape(t, d)

    proj, dt_raw = rms_matmul(x0, W["mamba_norm_w"], W["w_in"], W["w_dt"], tm=min(t, 1024), tn=1024)
    proj3 = proj.reshape(nb, length, D_INNER + CONV_DIM)
    new_conv = proj3[:, length - (CONV_W - 1):, D_INNER:D_INNER + CONV_DIM].astype(F32)
    conv_hi = conv_in.astype(BF16)
    conv_lo = (conv_in - conv_hi.astype(F32)).astype(BF16)
    front = ((0, 0), (CONV_HIST - (CONV_W - 1), 0), (0, 0))
    conv_hist = jnp.concatenate([jnp.pad(conv_hi, front), jnp.pad(conv_lo, front)], axis=1)
    y, state_out = ssd(proj3, dt_raw.reshape(nb, length, DT_PAD), conv_hist,
                       state_in.reshape(nb, D_INNER, D_STATE),
                       W["conv_w"], W["conv_b"], W["dt_bias"], W["a_log"], W["d_exp"], W["gn_w"],
                       c=ssd_chunk, cv=ssd_valid)
    new_ssm = state_out.reshape(nb, SSM_HEADS, SSM_HEAD_DIM, D_STATE)
    x1 = matmul_residual(y.reshape(t, D_INNER), W["w_out"], x0, tm=tm)

    x2 = swiglu_ffn(x1, W["ffn_norm_w"], W["ffn_wg"], W["ffn_wu"], W["ffn_wd"], tm=tm, tf=1408)

    tables = _rope_tables(pos)
    proj_args = (x2, W["kv_norm_w"], W["attn_norm_w"], W["w_k"], W["w_v"], W["w_q"], tables)
    lambda_init = 0.8 - 0.6 * math.exp(-0.3 * 1)
    if past_k is None:
        k, v, kb, v_t, q_t = qkv_proj(*proj_args, tm=tm, streams=nb, transposed=True)
        o = diff_attention(q_t, kb.reshape(nb, length, d), v_t,
                           W["lam"], W["subln_w"], tq=tq, lambda_init=lambda_init)
    else:
        k, v, q = qkv_proj(*proj_args, tm=tm, streams=nb, transposed=False)
        q3 = q.reshape(nb, length, d)
        o = diff_attention_cached(q3, past_k, past_v,
                                  k.reshape(nb, length, N_HEADS, LANES),
                                  v.reshape(nb, length, N_HEADS, LANES),
                                  W["lam"], W["subln_w"], lambda_init=lambda_init)
    x3 = matmul_residual(o.reshape(t, d), W["w_o"], x2, tm=tm)

    moe_args = (x3, W["moe_norm_w"], W["moe_wr"], W["moe_wg"], W["moe_wu"], W["moe_wd"], W["final_norm_w"])
    if t >= 4 * MOE_GROUP_TILE:
        yout = moe_ffn_final_sorted(*moe_args, tb=512, tf=896)
    else:
        yout = moe_ffn_final(*moe_args, tb=t, rt=256, tf=512)
    return (yout.reshape(nb, length, d), new_conv[None], new_ssm[None],
            k.reshape(nb, length, N_HEADS, 2 * DK), v.reshape(nb, length, N_HEADS, DV))


def kernel(x_prompt, x_sample, cache_conv, state_ssm, cache_k, cache_v, mamba_norm_w, mamba_w_in, mamba_conv_w, mamba_conv_b, mamba_dt_bias, mamba_a_log, mamba_d, mamba_gn_w, mamba_w_out, kv_norm_w, w_k, w_v, attn_norm_w, w_q, lambda_q1, lambda_k1, lambda_q2, lambda_k2, subln_w, w_o, ffn_norm_w, ffn_w_gate, ffn_w_up, ffn_w_down, moe_norm_w, moe_w_router, moe_w_gate, moe_w_up, moe_w_down, final_norm_w):
    w_in = mamba_w_in[0]
    n_dt = w_in.shape[1] - D_INNER - CONV_DIM
    W = dict(
        mamba_norm_w=_row(mamba_norm_w[0]),
        w_in=w_in[:, :D_INNER + CONV_DIM].astype(BF16),
        w_dt=jnp.pad(w_in[:, D_INNER + CONV_DIM:], ((0, 0), (0, DT_PAD - n_dt))).astype(BF16),
        conv_w=mamba_conv_w[0].astype(F32),
        conv_b=_row(mamba_conv_b[0]),
        dt_bias=_pad_lanes(mamba_dt_bias[0], LANES),
        a_log=_pad_lanes(mamba_a_log[0], LANES),
        d_exp=_row(jnp.repeat(mamba_d[0], SSM_HEAD_DIM)),
        gn_w=_row(mamba_gn_w[0]),
        w_out=mamba_w_out[0].astype(BF16),
        kv_norm_w=_row(kv_norm_w), w_k=w_k.astype(BF16), w_v=w_v.astype(BF16),
        attn_norm_w=_row(attn_norm_w[0]), w_q=w_q[0].astype(BF16),
        lam=jnp.stack([lambda_q1[0], lambda_k1[0], lambda_q2[0], lambda_k2[0]]).astype(F32),
        subln_w=_row(subln_w[0]), w_o=w_o[0].astype(BF16),
        ffn_norm_w=_row(ffn_norm_w[0]),
        ffn_wg=ffn_w_gate[0].astype(BF16), ffn_wu=ffn_w_up[0].astype(BF16),
        ffn_wd=ffn_w_down[0].astype(BF16),
        moe_norm_w=_row(moe_norm_w[0]),
        moe_wr=jnp.pad(moe_w_router[0].astype(F32), ((0, 0), (0, LANES - N_EXPERTS))),
        moe_wg=moe_w_gate[0].astype(BF16), moe_wu=moe_w_up[0].astype(BF16),
        moe_wd=moe_w_down[0].astype(BF16),
        final_norm_w=_row(final_norm_w),
    )
    bp, lp = x_prompt.shape[0], x_prompt.shape[1]
    bs, ls = x_sample.shape[0], x_sample.shape[1]
    past = cache_k.shape[1]

    conv0 = jnp.zeros((bp, CONV_W - 1, CONV_DIM), F32)
    ssm0 = jnp.zeros((bp, SSM_HEADS, SSM_HEAD_DIM, D_STATE), F32)
    y_p, conv_p, ssm_p, k_p, v_p = _trunk(
        x_prompt, jnp.arange(lp, dtype=jnp.int32), conv0, ssm0, None, None, W,
        ssd_chunk=128, ssd_valid=128, tm=512, tq=512)

    pos_s = jnp.tile(past + jnp.arange(ls, dtype=jnp.int32), bs)
    y_s, conv_s, ssm_s, k_s, v_s = _trunk(
        x_sample, pos_s, cache_conv[0], state_ssm[0], cache_k, cache_v, W,
        ssd_chunk=128, ssd_valid=ls, tm=bs * ls, tq=None)
    return (y_p, y_s, conv_p, ssm_p, k_p, v_p, conv_s, ssm_s, k_s, v_s)
```

```python
import functools
import math

import numpy as np
import jax
import jax.numpy as jnp
from jax import lax
from jax.experimental import pallas as pl
from jax.experimental.pallas import tpu as pltpu

F32 = jnp.float32
BF16 = jnp.bfloat16

EPS = 1e-5
D_MODEL = 1024
D_INNER = 2048
SSM_HEAD_DIM = 64
SSM_HEADS = 32
SSM_GROUPS = 4
D_STATE = 128
GROUP_W = D_INNER // SSM_GROUPS
CONV_W = 4
CONV_HIST = 16
CONV_DIM = D_INNER + 2 * SSM_GROUPS * D_STATE
DT_PAD = 256
PROJ_W = D_INNER + CONV_DIM + DT_PAD
N_HEADS = 8
DK = 64
DV = 128
ROT_DIM = 16
ROPE_THETA = 500000.0
CHUNK = 64
CHUNK_SHIFT = 6
assert 1 << CHUNK_SHIFT == CHUNK
N_EXPERTS = 8
LANES = 128
NEG_BIG = -1e30
LOG2E = math.log2(math.e)
Q_COLS = 256
ONES_ROWS = 16
SCORE_LOOKAHEAD = 3
VMEM_LIMIT = 56 * 1024 * 1024


def _params(*sem):
    return pltpu.CompilerParams(dimension_semantics=sem, vmem_limit_bytes=VMEM_LIMIT)


def _sigmoid(x):
    return 1.0 / (1.0 + jnp.exp(-x))


def _rms_scale(x):
    return x * lax.rsqrt(jnp.mean(x * x, axis=-1, keepdims=True) + EPS)


def _split3(x):
    hi = x.astype(BF16)
    r1 = x - hi.astype(F32)
    mid = r1.astype(BF16)
    lo = (r1 - mid.astype(F32)).astype(BF16)
    return hi, mid, lo


def _dot(a, b):
    return jnp.dot(a, b, preferred_element_type=F32)


def _lane_tile(x, n):
    return jnp.concatenate([x] * n, axis=1) if n > 1 else x


def _dot_exact_rhs(x, m_bf16):
    hi, mid, lo = _split3(x)
    return _dot(hi, m_bf16) + _dot(mid, m_bf16) + _dot(lo, m_bf16)


def _dot_exact_lhs(m_bf16, x):
    hi, mid, lo = _split3(x)
    return _dot(m_bf16, hi) + _dot(m_bf16, mid) + _dot(m_bf16, lo)


def _rms_matmul_kernel(x_ref, nw_ref, w_ref, wdt_ref, o_ref, dt_ref, h_ref):
    @pl.when(pl.program_id(1) == 0)
    def _():
        h = (_rms_scale(x_ref[...]) * nw_ref[...]).astype(BF16)
        h_ref[...] = h
        dt_ref[...] = _dot(h, wdt_ref[...])

    o_ref[...] = _dot(h_ref[...], w_ref[...]).astype(o_ref.dtype)


def rms_matmul(x, nw, w, w_dt, *, tm, tn):
    t, d = x.shape
    n = w.shape[1]
    n_dt = w_dt.shape[1]
    return pl.pallas_call(
        _rms_matmul_kernel,
        grid=(t // tm, n // tn),
        in_specs=[pl.BlockSpec((tm, d), lambda i, j: (i, 0)),
                  pl.BlockSpec((1, d), lambda i, j: (0, 0)),
                  pl.BlockSpec((d, tn), lambda i, j: (0, j)),
                  pl.BlockSpec((d, n_dt), lambda i, j: (0, 0))],
        out_specs=[pl.BlockSpec((tm, tn), lambda i, j: (i, j)),
                   pl.BlockSpec((tm, n_dt), lambda i, j: (i, 0))],
        out_shape=[jax.ShapeDtypeStruct((t, n), BF16),
                   jax.ShapeDtypeStruct((t, n_dt), F32)],
        scratch_shapes=[pltpu.VMEM((tm, d), BF16)],
        compiler_params=_params("arbitrary", "arbitrary"),
        name="rms_matmul",
    )(x, nw, w, w_dt)


def _ssd_kernel(z_ref, xs_ref, b_ref, c_ref, dt_ref, cin_ref, sin_ref, cw_ref, cb_ref,
                dtb_ref, alog_ref, dexp_ref, gnw_ref, e_ref, shift_ref, wout_ref, res_ref,
                y_ref, sout_ref, state_ref, ebuf_ref, *, c, cv):
    step = pl.program_id(1)
    hist = CONV_HIST

    @pl.when(step == 0)
    def _():
        state_ref[...] = sin_ref[0].T
        ebuf_ref[0:hist, :] = cin_ref[0, 0:hist, :]
        ebuf_ref[hist + c:2 * hist + c, :] = cin_ref[0, hist:2 * hist, :]

    ebuf_ref[hist:hist + cv, 0:D_INNER] = xs_ref[0]
    ebuf_ref[hist:hist + cv, D_INNER:D_INNER + GROUP_W] = b_ref[0]
    ebuf_ref[hist:hist + cv, D_INNER + GROUP_W:CONV_DIM] = c_ref[0]
    if cv < c:
        ebuf_ref[hist + cv:hist + c, :] = jnp.zeros((c - cv, CONV_DIM), BF16)

    shifted = _dot(shift_ref[...], ebuf_ref[...])
    conv = cb_ref[...] + ebuf_ref[hist:hist + c, :].astype(F32) * cw_ref[CONV_W - 1:CONV_W, :]
    for k in range(CONV_W - 1):
        conv = conv + shifted[k * c:(k + 1) * c, :] * cw_ref[k:k + 1, :]
    conv = conv * _sigmoid(conv)
    if cv == c:
        ebuf_ref[0:hist, :] = ebuf_ref[c:c + hist, :]
        ebuf_ref[hist + c:2 * hist + c, :] = jnp.zeros((hist, CONV_DIM), BF16)

    xs = conv[:, 0:D_INNER]
    bm = conv[:, D_INNER:D_INNER + GROUP_W]
    cm = conv[:, D_INNER + GROUP_W:CONV_DIM]

    dt_in = dt_ref[0][:, 0:LANES] + dtb_ref[...]
    dt = jnp.maximum(dt_in, 0.0) + jnp.log1p(jnp.exp(-jnp.abs(dt_in)))
    if cv < c:
        dt = jnp.concatenate([dt, jnp.zeros((c - cv, LANES), F32)], axis=0)
    a = -jnp.exp(alog_ref[...])
    dta = dt * a

    row = lax.broadcasted_iota(jnp.int32, (c, c), 0)
    col = lax.broadcasted_iota(jnp.int32, (c, c), 1)
    tril = col <= row
    tri = jnp.where(tril, 1.0, 0.0).astype(BF16)
    cs = _dot_exact_lhs(tri, dta)
    cs_t = cs.T

    e = e_ref[...]
    dt_x = _dot_exact_rhs(dt, e)
    cs_x = _dot_exact_rhs(cs, e)
    cs_end_x = cs_x[c - 1:c, :]
    xdt = xs * dt_x
    decay_in = jnp.exp(cs_x)
    xdt_b = xdt.astype(BF16)
    xdt_end = (xdt * jnp.exp(cs_end_x - cs_x)).astype(BF16)
    state_decay = jnp.exp(cs_end_x)

    lane = lax.broadcasted_iota(jnp.int32, (c, LANES), 1)
    low_half = lane < SSM_HEAD_DIM

    y_groups = []
    for g in range(SSM_GROUPS):
        gs = slice(g * GROUP_W, (g + 1) * GROUP_W)
        ns = slice(g * D_STATE, (g + 1) * D_STATE)
        bg_t = bm[:, ns].T.astype(BF16)
        cg = cm[:, ns].astype(BF16)
        cb = _dot(cg, bg_t)
        sg = state_ref[:, gs]
        y_off = _dot(cg, sg.astype(BF16)) * decay_in[:, gs]
        pairs = []
        for j in range(GROUP_W // LANES):
            h0 = g * (GROUP_W // SSM_HEAD_DIM) + 2 * j
            xp = xdt_b[:, g * GROUP_W + j * LANES:g * GROUP_W + (j + 1) * LANES]
            ys = []
            for h in (h0, h0 + 1):
                seg = cs[:, h:h + 1] - cs_t[h:h + 1, :]
                m = (cb * jnp.where(tril, jnp.exp(seg), 0.0)).astype(BF16)
                ys.append(_dot(m, xp))
            pairs.append(jnp.where(low_half, ys[0], ys[1]))
        y_groups.append(jnp.concatenate(pairs, axis=1) + y_off)
        state_ref[:, gs] = sg * state_decay[:, gs] + _dot(bg_t, xdt_end[:, gs])

    y = jnp.concatenate(y_groups, axis=1) + dexp_ref[...] * xs
    if cv < c:
        y = y[0:cv]
    zv = z_ref[0].astype(F32)
    y = y * (zv * _sigmoid(zv))
    outs = []
    for g in range(SSM_GROUPS):
        gs = slice(g * GROUP_W, (g + 1) * GROUP_W)
        outs.append(_rms_scale(y[:, gs]))
    y_normed = (jnp.concatenate(outs, axis=1) * gnw_ref[...]).astype(BF16)
    y_ref[0] = res_ref[0] + _dot(y_normed, wout_ref[...])

    @pl.when(step == pl.num_programs(1) - 1)
    def _():
        sout_ref[0] = state_ref[...].T


def ssd(proj, dt_raw, conv_in, state_in, cw, cb, dtb, alog, dexp, gnw, w_out, res, *, c, cv):
    nb, length, _ = proj.shape
    d_model = res.shape[-1]
    steps = length // cv
    head_of_channel = np.arange(D_INNER) // SSM_HEAD_DIM
    expand = jnp.asarray(np.arange(LANES)[:, None] == head_of_channel[None, :], dtype=BF16)
    tap, pos = np.divmod(np.arange((CONV_W - 1) * c), c)
    src = pos + tap + CONV_HIST - (CONV_W - 1)
    cols = np.arange(c + 2 * CONV_HIST)[None, :]
    from_cache_low = (cols == (src + c + CONV_HIST)[:, None]) & ((pos + tap) < CONV_W - 1)[:, None]
    shift = jnp.asarray((cols == src[:, None]) | from_cache_low, dtype=BF16)
    const = lambda b, l: (0, 0)
    kern = functools.partial(_ssd_kernel, c=c, cv=cv)
    return pl.pallas_call(
        kern,
        grid=(nb, steps),
        in_specs=[
            pl.BlockSpec((1, cv, D_INNER), lambda b, l: (b, l, 0)),
            pl.BlockSpec((1, cv, D_INNER), lambda b, l: (b, l, 1)),
            pl.BlockSpec((1, cv, GROUP_W), lambda b, l: (b, l, 2 * D_INNER // GROUP_W)),
            pl.BlockSpec((1, cv, GROUP_W), lambda b, l: (b, l, 2 * D_INNER // GROUP_W + 1)),
            pl.BlockSpec((1, cv, DT_PAD), lambda b, l: (b, l, 0)),
            pl.BlockSpec((1, 2 * CONV_HIST, CONV_DIM), lambda b, l: (b, 0, 0)),
            pl.BlockSpec((1, D_INNER, D_STATE), lambda b, l: (b, 0, 0)),
            pl.BlockSpec((CONV_W, CONV_DIM), const),
            pl.BlockSpec((1, CONV_DIM), const),
            pl.BlockSpec((1, LANES), const),
            pl.BlockSpec((1, LANES), const),
            pl.BlockSpec((1, D_INNER), const),
            pl.BlockSpec((1, D_INNER), const),
            pl.BlockSpec((LANES, D_INNER), const),
            pl.BlockSpec(((CONV_W - 1) * c, c + 2 * CONV_HIST), const),
            pl.BlockSpec((D_INNER, d_model), const),
            pl.BlockSpec((1, cv, d_model), lambda b, l: (b, l, 0)),
        ],
        out_specs=[
            pl.BlockSpec((1, cv, d_model), lambda b, l: (b, l, 0)),
            pl.BlockSpec((1, D_INNER, D_STATE), lambda b, l: (b, 0, 0)),
        ],
        out_shape=[
            jax.ShapeDtypeStruct((nb, length, d_model), F32),
            jax.ShapeDtypeStruct((nb, D_INNER, D_STATE), F32),
        ],
        scratch_shapes=[pltpu.VMEM((D_STATE, D_INNER), F32),
                        pltpu.VMEM((c + 2 * CONV_HIST, CONV_DIM), BF16)],
        compiler_params=_params("arbitrary", "arbitrary"),
        name="ssd",
    )(proj, proj, proj, proj, dt_raw, conv_in, state_in, cw, cb, dtb, alog, dexp, gnw, expand, shift,
      w_out, res)


def _mm_res_kernel(a_ref, w_ref, r_ref, o_ref):
    o_ref[...] = r_ref[...] + _dot(a_ref[...], w_ref[...])


def matmul_residual(a, w, res, *, tm):
    t, k = a.shape
    n = w.shape[1]
    return pl.pallas_call(
        _mm_res_kernel,
        grid=(t // tm,),
        in_specs=[pl.BlockSpec((tm, k), lambda i: (i, 0)),
                  pl.BlockSpec((k, n), lambda i: (0, 0)),
                  pl.BlockSpec((tm, n), lambda i: (i, 0))],
        out_specs=pl.BlockSpec((tm, n), lambda i: (i, 0)),
        out_shape=jax.ShapeDtypeStruct((t, n), F32),
        compiler_params=_params("arbitrary"),
        name="matmul_residual",
    )(a, w, res)


def _ffn_kernel(x_ref, nw_ref, wg_ref, wu_ref, wd_ref, o_ref, h_ref, acc_ref):
    f = pl.program_id(1)

    @pl.when(f == 0)
    def _():
        x = x_ref[...]
        h_ref[...] = (_rms_scale(x) * nw_ref[...]).astype(BF16)
        acc_ref[...] = x

    h = h_ref[...]
    g = _dot(h, wg_ref[...])
    u = _dot(h, wu_ref[...])
    act = (g * _sigmoid(g) * u).astype(BF16)
    acc_ref[...] += _dot(act, wd_ref[...])

    @pl.when(f == pl.num_programs(1) - 1)
    def _():
        o_ref[...] = acc_ref[...]


def swiglu_ffn(x, nw, wg, wu, wd, *, tm, tf):
    t, d = x.shape
    ff = wg.shape[1]
    return pl.pallas_call(
        _ffn_kernel,
        grid=(t // tm, ff // tf),
        in_specs=[pl.BlockSpec((tm, d), lambda i, f: (i, 0)),
                  pl.BlockSpec((1, d), lambda i, f: (0, 0)),
                  pl.BlockSpec((d, tf), lambda i, f: (0, f)),
                  pl.BlockSpec((d, tf), lambda i, f: (0, f)),
                  pl.BlockSpec((tf, d), lambda i, f: (f, 0))],
        out_specs=pl.BlockSpec((tm, d), lambda i, f: (i, 0)),
        out_shape=jax.ShapeDtypeStruct((t, d), F32),
        scratch_shapes=[pltpu.VMEM((tm, d), BF16), pltpu.VMEM((tm, d), F32)],
        compiler_params=_params("arbitrary", "arbitrary"),
        name="swiglu_ffn",
    )(x, nw, wg, wu, wd)


def _qkv_kernel(x_ref, kvw_ref, aw_ref, wk_ref, wv_ref, wq_ref, cos_ref, sa_ref, sb_ref,
                k_ref, v_ref, *extra_refs, transposed):
    xn = _rms_scale(x_ref[...])
    hkv = (xn * kvw_ref[...]).astype(BF16)
    hq = (xn * aw_ref[...]).astype(BF16)
    cos, sa, sb = cos_ref[...], sa_ref[...], sb_ref[...]

    def rope(t):
        blocks = []
        for j in range(t.shape[1] // LANES):
            tb = t[:, j * LANES:(j + 1) * LANES]
            blocks.append(tb * cos + pltpu.roll(tb, LANES - ROT_DIM // 2, 1) * sa
                          + pltpu.roll(tb, ROT_DIM // 2, 1) * sb)
        return jnp.concatenate(blocks, axis=1)

    k = rope(_dot(hkv, wk_ref[...]))
    v = _dot(hkv, wv_ref[...])
    q = rope(_dot(hq, wq_ref[...]))
    tm = k.shape[0]
    for h in range(N_HEADS):
        rows = pl.ds(h, tm, stride=N_HEADS)
        k_ref[rows, :] = k[:, h * LANES:(h + 1) * LANES]
        v_ref[rows, :] = v[:, h * LANES:(h + 1) * LANES]
    if transposed:
        kb_ref, vt_ref, qt_ref = extra_refs
        kb_ref[...] = k.astype(BF16)
        vt_ref[0] = v.T.astype(BF16)
        qt_ref[0] = (q * (DK ** -0.5 * LOG2E)).T.astype(BF16)
    else:
        (q_ref,) = extra_refs
        q_ref[...] = (q * (DK ** -0.5)).astype(BF16)


def _rope_tables(pos):
    half = ROT_DIM // 2
    inv_freq = ROPE_THETA ** (-jnp.arange(half, dtype=F32) / half)
    ang = pos.astype(F32)[:, None] * inv_freq[None, :]
    cos, sin = jnp.cos(ang), jnp.sin(ang)
    n = pos.shape[0]
    ones = jnp.ones((n, DK - ROT_DIM), F32)
    zeros_h = jnp.zeros((n, half), F32)
    zeros_r = jnp.zeros((n, DK - ROT_DIM), F32)
    cos_t = jnp.concatenate([cos, cos, ones], axis=1)
    sa_t = jnp.concatenate([-sin, zeros_h, zeros_r], axis=1)
    sb_t = jnp.concatenate([zeros_h, sin, zeros_r], axis=1)
    rep = lambda t: jnp.tile(t, (1, LANES // DK))
    return rep(cos_t), rep(sa_t), rep(sb_t)


def qkv_proj(x, kvw, aw, wk, wv, wq, tables, *, tm, streams, transposed):
    t, d = x.shape
    period = tables[0].shape[0] // tm
    row = lambda i: (i, 0)
    const = lambda i: (0, 0)
    tab = lambda i: (i % period, 0)
    heads_out = jax.ShapeDtypeStruct((t * N_HEADS, LANES), F32)
    heads_spec = pl.BlockSpec((tm * N_HEADS, LANES), row)
    out_specs = [heads_spec, heads_spec, pl.BlockSpec((tm, d), row)]
    out_shape = [heads_out, heads_out, jax.ShapeDtypeStruct((t, d), BF16)]
    if transposed:
        per = t // streams // tm
        tr_spec = pl.BlockSpec((1, d, tm), lambda i: (i // per, 0, i % per))
        tr_out = jax.ShapeDtypeStruct((streams, d, t // streams), BF16)
        out_specs += [tr_spec, tr_spec]
        out_shape += [tr_out, tr_out]
    return pl.pallas_call(
        functools.partial(_qkv_kernel, transposed=transposed),
        grid=(t // tm,),
        in_specs=[pl.BlockSpec((tm, d), row),
                  pl.BlockSpec((1, d), const), pl.BlockSpec((1, d), const),
                  pl.BlockSpec((d, d), const), pl.BlockSpec((d, d), const), pl.BlockSpec((d, d), const),
                  pl.BlockSpec((tm, LANES), tab), pl.BlockSpec((tm, LANES), tab),
                  pl.BlockSpec((tm, LANES), tab)],
        out_specs=out_specs,
        out_shape=out_shape,
        compiler_params=_params("arbitrary"),
        name="qkv_proj",
    )(x, kvw, aw, wk, wv, wq, *tables)


def _lambda_value(lp_ref, lambda_init):
    lp = lp_ref[...]
    s1 = jnp.sum(lp[0:1] * lp[1:2], axis=-1, keepdims=True)
    s2 = jnp.sum(lp[2:3] * lp[3:4], axis=-1, keepdims=True)
    return jnp.exp(s1) - jnp.exp(s2) + lambda_init


def _diff_finish(o1, o2, lam, subw, lambda_init):
    o = o1 - lam * o2
    return _rms_scale(o) * subw * (1.0 - lambda_init)


def _attn_kernel(qt_ref, kt_ref, q_ref, k_ref, v_ref, lp_ref, subw_ref, wo_ref, res_ref, o_ref,
                 qs_ref, on_ref, *state_refs, tq, tk, lambda_init):
    p = pl.program_id(1)
    qi = qt_ref[p]
    ki = kt_ref[p]
    n_qc = 2 * tq // Q_COLS
    m_refs, acc_refs = state_refs[:n_qc], state_refs[n_qc:]

    @pl.when(ki == 0)
    def _():
        for m_ref, acc_ref in zip(m_refs, acc_refs):
            m_ref[...] = jnp.full(m_ref.shape, NEG_BIG, F32)
            acc_ref[...] = jnp.zeros(acc_ref.shape, F32)
        first_sub = lax.broadcasted_iota(jnp.int32, (LANES, tq), 0) < DK
        for h in range(N_HEADS):
            qh = q_ref[0, h * LANES:(h + 1) * LANES, :]
            zero = jnp.zeros_like(qh)
            qs_ref[h, :, 0:tq] = jnp.where(first_sub, qh, zero)
            qs_ref[h, :, tq:2 * tq] = jnp.where(first_sub, zero, qh)

    ones = jnp.ones((ONES_ROWS, tk), BF16)

    def sweep(masked):
        if masked:
            k_chunk = (ki * tk + lax.broadcasted_iota(jnp.int32, (tk, Q_COLS), 0)) >> CHUNK_SHIFT
            q_lane = lax.broadcasted_iota(jnp.int32, (tk, Q_COLS), 1)

        def scores(h, c):
            kh = k_ref[0, :, h * LANES:(h + 1) * LANES]
            s = _dot(kh, qs_ref[h, :, c * Q_COLS:(c + 1) * Q_COLS])
            if masked:
                q_chunk = (qi * tq + (c * Q_COLS) % tq + q_lane) >> CHUNK_SHIFT
                s = jnp.where(k_chunk <= q_chunk, s, NEG_BIG)
            return s

        groups = [(h, c) for h in range(N_HEADS) for c in range(n_qc)]
        pending = [scores(*g) for g in groups[:SCORE_LOOKAHEAD]]
        for i, (h, c) in enumerate(groups):
            s = pending.pop(0)
            if i + SCORE_LOOKAHEAD < len(groups):
                pending.append(scores(*groups[i + SCORE_LOOKAHEAD]))
            vt = jnp.concatenate([v_ref[0, h * LANES:(h + 1) * LANES, :], ones], axis=0)
            m_prev = m_refs[c][h:h + 1, :]
            m_new = jnp.maximum(m_prev, jnp.max(s, axis=0, keepdims=True))
            alpha = jnp.exp2(m_prev - m_new)
            pr = jnp.exp2(s - m_new).astype(BF16)
            acc_refs[c][h] = acc_refs[c][h] * alpha + _dot(vt, pr)
            m_refs[c][h:h + 1, :] = m_new

    @pl.when(ki < qi)
    def _():
        sweep(False)

    @pl.when(ki == qi)
    def _():
        sweep(True)
        lam = _lambda_value(lp_ref, lambda_init)
        subw = subw_ref[...]

        for h in range(N_HEADS):
            hs = slice(h * LANES, (h + 1) * LANES)
            for r in range(tq // Q_COLS):
                a1 = acc_refs[r][h]
                a2 = acc_refs[r + tq // Q_COLS][h]
                o_t = a1[0:DV] / a1[DV:DV + 1] - lam * (a2[0:DV] / a2[DV:DV + 1])
                res = _rms_scale(o_t.T) * subw * (1.0 - lambda_init)
                on_ref[r * Q_COLS:(r + 1) * Q_COLS, hs] = res.astype(BF16)
        o_ref[0] = res_ref[0] + _dot(on_ref[...], wo_ref[...])


def diff_attention(q_t, k, v_t, lam_params, subw, w_o, res, *, tq, lambda_init):
    nb, d, length = q_t.shape
    tk = tq
    nq = length // tq
    qt = np.concatenate([np.full(i + 1, i) for i in range(nq)]).astype(np.int32)
    kt = np.concatenate([np.arange(i + 1) for i in range(nq)]).astype(np.int32)
    kern = functools.partial(_attn_kernel, tq=tq, tk=tk, lambda_init=lambda_init)
    n_qc = 2 * tq // Q_COLS
    grid_spec = pltpu.PrefetchScalarGridSpec(
        num_scalar_prefetch=2,
        grid=(nb, len(qt)),
        in_specs=[pl.BlockSpec((1, d, tq), lambda b, p, qt_r, kt_r: (b, 0, qt_r[p])),
                  pl.BlockSpec((1, tk, d), lambda b, p, qt_r, kt_r: (b, kt_r[p], 0)),
                  pl.BlockSpec((1, d, tk), lambda b, p, qt_r, kt_r: (b, 0, kt_r[p])),
                  pl.BlockSpec((4, DK), lambda b, p, qt_r, kt_r: (0, 0)),
                  pl.BlockSpec((1, DV), lambda b, p, qt_r, kt_r: (0, 0)),
                  pl.BlockSpec((d, d), lambda b, p, qt_r, kt_r: (0, 0)),
                  pl.BlockSpec((1, tq, d), lambda b, p, qt_r, kt_r: (b, qt_r[p], 0))],
        out_specs=pl.BlockSpec((1, tq, d), lambda b, p, qt_r, kt_r: (b, qt_r[p], 0)),
        scratch_shapes=([pltpu.VMEM((N_HEADS, LANES, 2 * tq), BF16), pltpu.VMEM((tq, d), BF16)]
                        + [pltpu.VMEM((N_HEADS, Q_COLS), F32)] * n_qc
                        + [pltpu.VMEM((N_HEADS, DV + ONES_ROWS, Q_COLS), F32)] * n_qc),
    )
    return pl.pallas_call(
        kern,
        grid_spec=grid_spec,
        out_shape=jax.ShapeDtypeStruct((nb, length, d), F32),
        compiler_params=_params("arbitrary", "arbitrary"),
        name="diff_attention",
    )(jnp.asarray(qt), jnp.asarray(kt), q_t, k, v_t, lam_params, subw, w_o, res)


def _attn_cached_kernel(q_ref, ck_ref, cv_ref, kn_ref, vn_ref, lp_ref, subw_ref, o_ref,
                        *, nq, lambda_init):
    past = ck_ref.shape[1] // N_HEADS
    lane = lax.broadcasted_iota(jnp.int32, (nq, LANES), 1)
    nt = (((1,), (1,)), ((), ()))
    pad = jnp.zeros((LANES - nq, LANES), BF16)
    lam = _lambda_value(lp_ref, lambda_init)
    for h in range(N_HEADS):
        q = q_ref[0, :, h * LANES:(h + 1) * LANES]
        zero = jnp.zeros_like(q)
        qs = jnp.concatenate([jnp.where(lane < DK, q, zero), jnp.where(lane < DK, zero, q)], axis=0)
        old_rows = pl.ds(h, past, stride=N_HEADS)
        new_rows = pl.ds(h, nq, stride=N_HEADS)
        kc = ck_ref[0, old_rows, :].astype(BF16)
        vc = cv_ref[0, old_rows, :].astype(BF16)
        kn = jnp.concatenate([kn_ref[0, new_rows, :].astype(BF16), pad], axis=0)
        vn = jnp.concatenate([vn_ref[0, new_rows, :].astype(BF16), pad], axis=0)
        s_c = lax.dot_general(qs, kc, nt, preferred_element_type=F32)
        s_n = lax.dot_general(qs, kn, nt, preferred_element_type=F32)
        col = lax.broadcasted_iota(jnp.int32, s_n.shape, 1)
        s_n = jnp.where(col < nq, s_n, NEG_BIG)
        m = jnp.maximum(jnp.max(s_c, axis=-1, keepdims=True), jnp.max(s_n, axis=-1, keepdims=True))
        p_c = jnp.exp(s_c - m)
        p_n = jnp.exp(s_n - m)
        denom = jnp.sum(p_c, axis=-1, keepdims=True) + jnp.sum(p_n, axis=-1, keepdims=True)
        o = (_dot(p_c.astype(BF16), vc) + _dot(p_n.astype(BF16), vn)) / denom
        res = _diff_finish(o[0:nq], o[nq:2 * nq], lam, subw_ref[...], lambda_init)
        o_ref[0, :, h * LANES:(h + 1) * LANES] = res.astype(o_ref.dtype)


def diff_attention_cached(q, cache_k, cache_v, k_new, v_new, lam_params, subw, *, lambda_init):
    nb, nq, d = q.shape
    past = cache_k.shape[1]
    kern = functools.partial(_attn_cached_kernel, nq=nq, lambda_init=lambda_init)
    stream = lambda b: (b, 0, 0)
    const = lambda b: (0, 0)
    rows = lambda a: a.reshape(nb, a.shape[1] * N_HEADS, LANES)
    cache_k, cache_v, k_new, v_new = rows(cache_k), rows(cache_v), rows(k_new), rows(v_new)
    return pl.pallas_call(
        kern,
        grid=(nb,),
        in_specs=[pl.BlockSpec((1, nq, d), lambda b: (b, 0, 0)),
                  pl.BlockSpec((1, past * N_HEADS, LANES), stream),
                  pl.BlockSpec((1, past * N_HEADS, LANES), stream),
                  pl.BlockSpec((1, nq * N_HEADS, LANES), stream),
                  pl.BlockSpec((1, nq * N_HEADS, LANES), stream),
                  pl.BlockSpec((4, DK), const),
                  pl.BlockSpec((1, DV), const)],
        out_specs=pl.BlockSpec((1, nq, d), lambda b: (b, 0, 0)),
        out_shape=jax.ShapeDtypeStruct((nb, nq, d), BF16),
        compiler_params=_params("arbitrary"),
        name="diff_attention_cached",
    )(q, cache_k, cache_v, k_new, v_new, lam_params, subw)


def _route(logits):
    lane = lax.broadcasted_iota(jnp.int32, logits.shape, 1)
    valid = lane < N_EXPERTS
    lg = jnp.where(valid, logits, NEG_BIG)
    m1 = jnp.max(lg, axis=-1, keepdims=True)
    lane_f = lane.astype(F32)
    i1 = jnp.min(jnp.where(lg == m1, lane_f, float(LANES)), axis=-1, keepdims=True)
    lg2 = jnp.where(lane_f == i1, NEG_BIG, lg)
    m2 = jnp.max(lg2, axis=-1, keepdims=True)
    i2 = jnp.min(jnp.where(lg2 == m2, lane_f, float(LANES)), axis=-1, keepdims=True)
    e2 = jnp.exp(m2 - m1)
    g1 = 1.0 / (1.0 + e2)
    g2 = e2 / (1.0 + e2)
    return jnp.where(lane_f == i1, g1, 0.0) + jnp.where(lane_f == i2, g2, 0.0)


def _moe_route_kernel(x_ref, nw_ref, wr_ref, h_ref, comb_ref, rk_ref, rkt_ref, cnt_ref,
                      meta_ref, metat_ref, *, rt):
    t = x_ref.shape[0]
    hf = _rms_scale(x_ref[...]) * nw_ref[...]
    h_ref[...] = hf.astype(BF16)
    comb = _route(_dot_exact_rhs_general(hf, wr_ref[...]))
    comb_ref[...] = comb
    sel = jnp.where(comb > 0.0, 1.0, 0.0)
    row = lax.broadcasted_iota(jnp.int32, (rt, rt), 0)
    col = lax.broadcasted_iota(jnp.int32, (rt, rt), 1)
    strict = jnp.where(col < row, 1.0, 0.0).astype(BF16)
    carry = jnp.zeros((1, LANES), F32)
    ranks = []
    for s in range(t // rt):
        sel_s = sel[s * rt:(s + 1) * rt]
        rank_s = _dot(strict, sel_s.astype(BF16)) + carry
        ranks.append(jnp.where(sel_s > 0.0, rank_s, -1.0))
        carry = carry + jnp.sum(sel_s, axis=0, keepdims=True)
    rk = jnp.concatenate(ranks, axis=0) if len(ranks) > 1 else ranks[0]
    rk_ref[...] = rk.astype(jnp.int32)
    rkt_ref[0] = rk.T.astype(jnp.int32)
    cnt_ref[0] = jnp.broadcast_to(carry, (8, LANES)).astype(jnp.int32)

    padded = jnp.floor((carry + (MOE_ROW_ALIGN - 1)) * (1.0 / MOE_ROW_ALIGN)) * MOE_ROW_ALIGN
    lrow = lax.broadcasted_iota(jnp.int32, (LANES, LANES), 0)
    lcol = lax.broadcasted_iota(jnp.int32, (LANES, LANES), 1)
    before = jnp.where(lrow < lcol, 1.0, 0.0).astype(BF16)
    group_start = _dot(jnp.broadcast_to(padded, (8, LANES)).astype(BF16), before)[0:1]
    dest = jnp.where(sel > 0.0, rk + group_start, -1.0)
    lane_f = lax.broadcasted_iota(jnp.int32, (t, LANES), 1).astype(F32)
    la = jnp.min(jnp.where(sel > 0.0, lane_f, float(LANES)), axis=-1, keepdims=True)
    lb = jnp.max(jnp.where(sel > 0.0, lane_f, -1.0), axis=-1, keepdims=True)
    at = lambda v, l: jnp.sum(jnp.where(lane_f == l, v, 0.0), axis=-1, keepdims=True)
    two = lb > la
    d_a, g_a = at(dest, la), at(comb, la)
    d_b = jnp.where(two, at(dest, lb), -1.0)
    g_b = jnp.where(two, at(comb, lb), 0.0)
    meta = jnp.where(lane_f == 0.0, d_a, jnp.where(lane_f == 1.0, d_b,
                     jnp.where(lane_f == 2.0, g_a, jnp.where(lane_f == 3.0, g_b, 0.0))))
    meta_ref[...] = meta
    metat_ref[0] = meta.T[0:8]


def moe_route(x, nw, wr, *, tb, rt):
    t, d = x.shape
    nb = t // tb
    row = lambda b: (b, 0)
    const = lambda b: (0, 0)
    return pl.pallas_call(
        functools.partial(_moe_route_kernel, rt=rt),
        grid=(nb,),
        in_specs=[pl.BlockSpec((tb, d), row), pl.BlockSpec((1, d), const),
                  pl.BlockSpec((d, LANES), const)],
        out_specs=[pl.BlockSpec((tb, d), row), pl.BlockSpec((tb, LANES), row),
                   pl.BlockSpec((tb, LANES), row),
                   pl.BlockSpec((1, LANES, tb), lambda b: (b, 0, 0)),
                   pl.BlockSpec((1, 8, LANES), lambda b: (b, 0, 0)),
                   pl.BlockSpec((tb, LANES), row),
                   pl.BlockSpec((1, 8, tb), lambda b: (b, 0, 0))],
        out_shape=[jax.ShapeDtypeStruct((t, d), BF16),
                   jax.ShapeDtypeStruct((t, LANES), F32),
                   jax.ShapeDtypeStruct((t, LANES), jnp.int32),
                   jax.ShapeDtypeStruct((nb, LANES, tb), jnp.int32),
                   jax.ShapeDtypeStruct((nb, 8, LANES), jnp.int32),
                   jax.ShapeDtypeStruct((t, LANES), F32),
                   jax.ShapeDtypeStruct((nb, 8, tb), F32)],
        compiler_params=_params("arbitrary"),
        name="moe_route",
    )(x, nw, wr)


def _moe_expert_kernel(cnt_ref, x_ref, h_ref, comb_ref, rk_ref, rkt_ref, wg_ref, wu_ref, wd_ref,
                       fw_ref, o_ref, rkc_ref, xg_ref, gs_ref, yacc_ref, *, rt):
    b = pl.program_id(0)
    e = pl.program_id(1)
    f = pl.program_id(2)
    t = x_ref.shape[0]
    n_rows = cnt_ref[b * N_EXPERTS + e]
    half = rt // 2
    n_tiles = (n_rows + (half - 1)) // rt
    tail_row0 = pl.multiple_of(n_tiles * rt, rt)
    has_tail = n_rows > n_tiles * rt

    def over_tiles(fn):
        def body(j, carry):
            fn(pl.multiple_of(j * rt, rt), rt)
            return carry

        lax.fori_loop(0, n_tiles, body, 0)

        @pl.when(has_tail)
        def _():
            fn(tail_row0, half)

    @pl.when(jnp.logical_and(e == 0, f == 0))
    def _():
        o_ref[...] = x_ref[...]

    @pl.when(f == 0)
    def _():
        lane = lax.broadcasted_iota(jnp.int32, (t, LANES), 1)
        pick = lane == e
        rkc = jnp.sum(jnp.where(pick, rk_ref[...], 0).astype(F32), axis=-1, keepdims=True)
        rkc_ref[...] = jnp.broadcast_to(rkc, (t, LANES)).astype(jnp.int32)
        gate = jnp.sum(jnp.where(pick, comb_ref[...], 0.0), axis=-1, keepdims=True)
        g_hi, g_mid, g_lo = _split3(jnp.broadcast_to(gate, (t, LANES)))
        g3 = jnp.where(lane == 0, g_hi.astype(F32),
                       jnp.where(lane == 1, g_mid.astype(F32),
                                 jnp.where(lane == 2, g_lo.astype(F32), 0.0))).astype(BF16)
        rk_row = rkt_ref[0, pl.ds(e, 1), :]

        def gather(row0, size):
            tile = pl.ds(row0, size)
            rows = lax.broadcasted_iota(jnp.int32, (size, t), 0) + row0
            p = jnp.where(rk_row == rows, 1.0, 0.0).astype(BF16)
            xg_ref[tile, :] = _dot(p, h_ref[...]).astype(BF16)
            gate_rows = jnp.sum(_dot(p, g3), axis=-1, keepdims=True)
            gs_ref[tile, :] = jnp.broadcast_to(gate_rows, (size, LANES))
            yacc_ref[tile, :] = jnp.zeros((size, yacc_ref.shape[1]), F32)

        over_tiles(gather)

    def expert(row0, size):
        tile = pl.ds(row0, size)
        xg = xg_ref[tile, :]
        g = _dot(xg, wg_ref[...])
        u = _dot(xg, wu_ref[...])
        gates = _lane_tile(gs_ref[tile, :], g.shape[1] // LANES)
        act = (g * _sigmoid(g) * u * gates).astype(BF16)
        yacc_ref[tile, :] += _dot(act, wd_ref[...])

    over_tiles(expert)

    @pl.when(f == pl.num_programs(2) - 1)
    def _():
        def scatter(row0, size):
            cols = lax.broadcasted_iota(jnp.int32, (t, size), 1) + row0
            s = jnp.where(_lane_tile(rkc_ref[...], size // LANES) == cols, 1.0, 0.0).astype(BF16)
            o_ref[...] += _dot(s, yacc_ref[pl.ds(row0, size), :].astype(BF16))

        over_tiles(scatter)

    @pl.when(jnp.logical_and(e == pl.num_programs(1) - 1, f == pl.num_programs(2) - 1))
    def _():
        o_ref[...] = _rms_scale(o_ref[...]) * fw_ref[...]


def _dot_exact_rhs_general(x, w):
    xh, xm, xl = _split3(x)
    wh, wm, wl = _split3(w)
    return (_dot(xh, wh) + (_dot(xh, wm) + _dot(xm, wh))
            + (_dot(xh, wl) + _dot(xm, wm) + _dot(xl, wh)))


def moe_ffn_final(x, nw, wr, wg, wu, wd, fw, *, tb, rt, tf):
    t, d = x.shape
    ne, _, ff = wg.shape
    nb = t // tb
    h, comb, rk, rkt, cnt, _, _ = moe_route(x, nw, wr, tb=tb, rt=rt)
    counts = cnt[:, 0, :ne].reshape(nb * ne)
    once = pl.Buffered(1)
    blk = lambda b, e, f, c: (b, 0)
    const = lambda b, e, f, c: (0, 0)
    grid_spec = pltpu.PrefetchScalarGridSpec(
        num_scalar_prefetch=1,
        grid=(nb, ne, ff // tf),
        in_specs=[pl.BlockSpec((tb, d), blk, pipeline_mode=once),
                  pl.BlockSpec((tb, d), blk, pipeline_mode=once),
                  pl.BlockSpec((tb, LANES), blk, pipeline_mode=once),
                  pl.BlockSpec((tb, LANES), blk, pipeline_mode=once),
                  pl.BlockSpec((1, LANES, tb), lambda b, e, f, c: (b, 0, 0), pipeline_mode=once),
                  pl.BlockSpec((None, d, tf), lambda b, e, f, c: (e, 0, f)),
                  pl.BlockSpec((None, d, tf), lambda b, e, f, c: (e, 0, f)),
                  pl.BlockSpec((None, tf, d), lambda b, e, f, c: (e, f, 0)),
                  pl.BlockSpec((1, d), const)],
        out_specs=pl.BlockSpec((tb, d), blk),
        scratch_shapes=[pltpu.VMEM((tb, LANES), jnp.int32),
                        pltpu.VMEM((tb, d), BF16),
                        pltpu.VMEM((tb, LANES), F32),
                        pltpu.VMEM((tb, d), F32)],
    )
    return pl.pallas_call(
        functools.partial(_moe_expert_kernel, rt=rt),
        grid_spec=grid_spec,
        out_shape=jax.ShapeDtypeStruct((t, d), F32),
        compiler_params=_params("arbitrary", "arbitrary", "arbitrary"),
        name="moe_experts",
    )(counts, x, h, comb, rk, rkt, wg, wu, wd, fw)


MOE_ROW_ALIGN = 16
MOE_ROW_TILE = 128
MOE_GROUP_TILE = 1024


def _moe_plan(cnt, *, tb, region):
    nb, ne = cnt.shape
    padded = (cnt + (MOE_ROW_ALIGN - 1)) // MOE_ROW_ALIGN * MOE_ROW_ALIGN
    off = jnp.cumsum(padded, axis=0) - padded
    boff = jnp.cumsum(padded, axis=1) - padded
    total = jnp.sum(padded, axis=0)
    tiles = (total + MOE_ROW_TILE + MOE_GROUP_TILE - 1) // MOE_GROUP_TILE
    tile_end = jnp.cumsum(tiles)
    max_tiles = (2 * nb * tb + nb * ne * MOE_ROW_ALIGN + ne * MOE_ROW_TILE) // MOE_GROUP_TILE + ne + 1
    idx = jnp.arange(max_tiles, dtype=jnp.int32)
    last = jnp.maximum(tile_end[-1] - 1, 0)
    active = idx < tile_end[-1]
    ii = jnp.minimum(idx, last)
    te = jnp.searchsorted(tile_end, ii, side="right").astype(jnp.int32)
    k = ii - (tile_end - tiles)[te]
    tr = te * (region // MOE_GROUP_TILE) + k
    valid = jnp.clip(total[te] - k * MOE_GROUP_TILE, 0, MOE_GROUP_TILE)
    tv = jnp.where(active, valid, -1)
    flat = lambda a: a.reshape(-1).astype(jnp.int32)
    return flat(off), flat(boff), te, tr.astype(jnp.int32), tv.astype(jnp.int32)


def _block_rows(tb):
    need = 2 * tb + N_EXPERTS * MOE_ROW_ALIGN
    return -(-need // MOE_ROW_TILE) * MOE_ROW_TILE


def _for_group_pieces(cnt_ref, off_ref, boff_ref, blk, e, tb, region, enabled, make_copy, action):
    units = (cnt_ref[blk * N_EXPERTS + e] + (MOE_ROW_ALIGN - 1)) // MOE_ROW_ALIGN
    src0 = boff_ref[blk * N_EXPERTS + e]
    dst0 = e * region + off_ref[blk * N_EXPERTS + e]
    done = 0
    for k in reversed(range(_piece_bits(tb))):
        size = MOE_ROW_ALIGN << k
        has = (units >> k) & 1
        src = pl.ds(pl.multiple_of(src0 + done, MOE_ROW_ALIGN), size)
        dst = pl.ds(pl.multiple_of(dst0 + done, MOE_ROW_ALIGN), size)

        @pl.when(jnp.logical_and(enabled, has == 1))
        def _():
            action(make_copy(k, src, dst))

        done = done + has * size


def _piece_bits(tb):
    return (tb // MOE_ROW_ALIGN).bit_length()


def _moe_gather_kernel(cnt_ref, off_ref, boff_ref, h_ref, metat_ref, xs_ref, buf_ref, sem_ref, *, region):
    b = pl.program_id(0)
    tb = h_ref.shape[0]
    rows_b = buf_ref.shape[1]
    slot = lax.rem(b, 2)

    dest = metat_ref[0].astype(jnp.int32)
    rows = lax.broadcasted_iota(jnp.int32, (rows_b, tb), 0)
    onehot = jnp.where(rows == dest[0:1], 1.0, jnp.where(rows == dest[1:2], 1.0, 0.0)).astype(BF16)
    buf_ref[slot] = _dot(onehot, h_ref[...]).astype(BF16)

    def pieces(blk, sl, enabled, action):
        for e in range(N_EXPERTS):
            def make_copy(p, src_rows, dst_rows, e=e):
                return pltpu.make_async_copy(buf_ref.at[sl, src_rows], xs_ref.at[dst_rows],
                                             sem_ref.at[sl, e, p])
            _for_group_pieces(cnt_ref, off_ref, boff_ref, blk, e, tb, region, enabled, make_copy, action)

    pieces(jnp.maximum(b - 1, 0), 1 - slot, b > 0, lambda cp: cp.wait())
    pieces(b, slot, True, lambda cp: cp.start())
    pieces(b, slot, b == pl.num_programs(0) - 1, lambda cp: cp.wait())


def _moe_grouped_kernel(te_ref, tr_ref, tv_ref, x_ref, wg_ref, wu_ref, wd_ref, y_ref, xm_ref, acc_ref):
    i = pl.program_id(0)
    f = pl.program_id(1)
    valid = tv_ref[i]
    last_f = pl.num_programs(1) - 1

    @pl.when(valid > 0)
    def _():
        @pl.when(f == 0)
        def _():
            row = lax.broadcasted_iota(jnp.int32, x_ref.shape, 0)
            xm_ref[...] = jnp.where(row < valid, x_ref[...].astype(F32), 0.0).astype(BF16)
            acc_ref[...] = jnp.zeros(acc_ref.shape, F32)

        xm = xm_ref[...]
        g = _dot(xm, wg_ref[...])
        u = _dot(xm, wu_ref[...])
        acc_ref[...] += _dot((g * _sigmoid(g) * u).astype(BF16), wd_ref[...])

        @pl.when(f == last_f)
        def _():
            y_ref[...] = acc_ref[...].astype(y_ref.dtype)

    @pl.when(jnp.logical_and(valid == 0, f == last_f))
    def _():
        y_ref[...] = jnp.zeros(y_ref.shape, y_ref.dtype)


def _moe_combine_kernel(cnt_ref, off_ref, boff_ref, x_ref, meta_ref, ys_ref, fw_ref, o_ref,
                        buf_ref, sem_ref, *, region):
    b = pl.program_id(0)
    tb = x_ref.shape[0]
    rows_b = buf_ref.shape[0]

    @pl.when(b == 0)
    def _():
        buf_ref[...] = jnp.zeros(buf_ref.shape, BF16)

    def pieces(action):
        for e in range(N_EXPERTS):
            def make_copy(p, buf_rows, sorted_rows, e=e):
                return pltpu.make_async_copy(ys_ref.at[sorted_rows], buf_ref.at[buf_rows], sem_ref.at[e, p])
            _for_group_pieces(cnt_ref, off_ref, boff_ref, b, e, tb, region, True, make_copy, action)

    pieces(lambda cp: cp.start())
    meta = meta_ref[...]
    col = lax.broadcasted_iota(jnp.int32, (tb, rows_b), 1)
    d_a = meta[:, 0:1].astype(jnp.int32)
    d_b = meta[:, 1:2].astype(jnp.int32)
    g_a, g_b = meta[:, 2:3], meta[:, 3:4]
    hi = lambda g: g.astype(BF16).astype(F32)
    hit_a, hit_b = col == d_a, col == d_b
    s_hi = jnp.where(hit_a, hi(g_a), jnp.where(hit_b, hi(g_b), 0.0)).astype(BF16)
    s_lo = jnp.where(hit_a, g_a - hi(g_a), jnp.where(hit_b, g_b - hi(g_b), 0.0)).astype(BF16)
    pieces(lambda cp: cp.wait())
    y = buf_ref[...]
    out = x_ref[...] + _dot(jnp.concatenate([s_hi, s_lo], axis=1), jnp.concatenate([y, y], axis=0))
    o_ref[...] = _rms_scale(out) * fw_ref[...]


def moe_ffn_final_sorted(x, nw, wr, wg, wu, wd, fw, *, tb, tf):
    t, d = x.shape
    ne, _, ff = wg.shape
    nb = t // tb
    st = MOE_GROUP_TILE
    region = -(-(tb * nb + MOE_ROW_ALIGN * nb + tb) // st) * st
    h, _, _, _, cnt, meta, meta_t = moe_route(x, nw, wr, tb=tb, rt=tb)
    counts = cnt[:, 0, :ne]
    off, boff, te, tr, tv = _moe_plan(counts, tb=tb, region=region)
    counts = counts.reshape(nb * ne)
    n_tiles = te.shape[0]
    rows_b = _block_rows(tb)
    n_pieces = _piece_bits(tb)

    xs = pl.pallas_call(
        functools.partial(_moe_gather_kernel, region=region),
        grid_spec=pltpu.PrefetchScalarGridSpec(
            num_scalar_prefetch=3,
            grid=(nb,),
            in_specs=[pl.BlockSpec((tb, d), lambda b, c, o, bo: (b, 0)),
                      pl.BlockSpec((1, 8, tb), lambda b, c, o, bo: (b, 0, 0))],
            out_specs=pl.BlockSpec(memory_space=pl.ANY),
            scratch_shapes=[pltpu.VMEM((2, rows_b, d), BF16),
                            pltpu.SemaphoreType.DMA((2, ne, n_pieces))],
        ),
        out_shape=jax.ShapeDtypeStruct((ne * region, d), BF16),
        compiler_params=_params("arbitrary"),
        name="moe_gather",
    )(counts, off, boff, h, meta_t)

    last_f = ff // tf - 1
    used_f = lambda i, f, te_r, tr_r, tv_r: jnp.where(tv_r[i] > 0, f, last_f)
    ys = pl.pallas_call(
        _moe_grouped_kernel,
        grid_spec=pltpu.PrefetchScalarGridSpec(
            num_scalar_prefetch=3,
            grid=(n_tiles, ff // tf),
            in_specs=[pl.BlockSpec((st, d), lambda i, f, te_r, tr_r, tv_r: (tr_r[i], 0)),
                      pl.BlockSpec((None, d, tf), lambda i, f, te_r, tr_r, tv_r:
                                   (te_r[i], 0, used_f(i, f, te_r, tr_r, tv_r))),
                      pl.BlockSpec((None, d, tf), lambda i, f, te_r, tr_r, tv_r:
                                   (te_r[i], 0, used_f(i, f, te_r, tr_r, tv_r))),
                      pl.BlockSpec((None, tf, d), lambda i, f, te_r, tr_r, tv_r:
                                   (te_r[i], used_f(i, f, te_r, tr_r, tv_r), 0))],
            out_specs=pl.BlockSpec((st, d), lambda i, f, te_r, tr_r, tv_r: (tr_r[i], 0)),
            scratch_shapes=[pltpu.VMEM((st, d), BF16), pltpu.VMEM((st, d), F32)],
        ),
        out_shape=jax.ShapeDtypeStruct((ne * region, d), BF16),
        compiler_params=_params("arbitrary", "arbitrary"),
        name="moe_grouped",
    )(te, tr, tv, xs, wg, wu, wd)

    return pl.pallas_call(
        functools.partial(_moe_combine_kernel, region=region),
        grid_spec=pltpu.PrefetchScalarGridSpec(
            num_scalar_prefetch=3,
            grid=(nb,),
            in_specs=[pl.BlockSpec((tb, d), lambda b, c, o, bo: (b, 0)),
                      pl.BlockSpec((tb, LANES), lambda b, c, o, bo: (b, 0)),
                      pl.BlockSpec(memory_space=pl.ANY),
                      pl.BlockSpec((1, d), lambda b, c, o, bo: (0, 0))],
            out_specs=pl.BlockSpec((tb, d), lambda b, c, o, bo: (b, 0)),
            scratch_shapes=[pltpu.VMEM((rows_b, d), BF16), pltpu.SemaphoreType.DMA((ne, n_pieces))],
        ),
        out_shape=jax.ShapeDtypeStruct((t, d), F32),
        compiler_params=_params("arbitrary"),
        name="moe_combine",
    )(counts, off, boff, x, meta, ys, fw)


def _row(v):
    return v.reshape(1, -1).astype(F32)


def _pad_lanes(v, width):
    v = _row(v)
    return jnp.pad(v, ((0, 0), (0, width - v.shape[1])))


def _trunk(x, pos, conv_in, state_in, past_k, past_v, W, *, ssd_chunk, ssd_valid, tm, tq):
    nb, length, d = x.shape
    t = nb * length
    x0 = x.reshape(t, d)

    proj, dt_raw = rms_matmul(x0, W["mamba_norm_w"], W["w_in"], W["w_dt"], tm=min(t, 1024), tn=1024)
    proj3 = proj.reshape(nb, length, D_INNER + CONV_DIM)
    new_conv = proj3[:, length - (CONV_W - 1):, D_INNER:D_INNER + CONV_DIM].astype(F32)
    conv_hi = conv_in.astype(BF16)
    conv_lo = (conv_in - conv_hi.astype(F32)).astype(BF16)
    front = ((0, 0), (CONV_HIST - (CONV_W - 1), 0), (0, 0))
    conv_hist = jnp.concatenate([jnp.pad(conv_hi, front), jnp.pad(conv_lo, front)], axis=1)
    x1, state_out = ssd(proj3, dt_raw.reshape(nb, length, DT_PAD), conv_hist,
                        state_in.reshape(nb, D_INNER, D_STATE),
                        W["conv_w"], W["conv_b"], W["dt_bias"], W["a_log"], W["d_exp"], W["gn_w"],
                        W["w_out"], x, c=ssd_chunk, cv=ssd_valid)
    new_ssm = state_out.reshape(nb, SSM_HEADS, SSM_HEAD_DIM, D_STATE)
    x1 = x1.reshape(t, d)

    x2 = swiglu_ffn(x1, W["ffn_norm_w"], W["ffn_wg"], W["ffn_wu"], W["ffn_wd"], tm=tm, tf=1408)

    tables = _rope_tables(pos)
    proj_args = (x2, W["kv_norm_w"], W["attn_norm_w"], W["w_k"], W["w_v"], W["w_q"], tables)
    lambda_init = 0.8 - 0.6 * math.exp(-0.3 * 1)
    if past_k is None:
        k, v, kb, v_t, q_t = qkv_proj(*proj_args, tm=tm, streams=nb, transposed=True)
        x3 = diff_attention(q_t, kb.reshape(nb, length, d), v_t, W["lam"], W["subln_w"], W["w_o"],
                            x2.reshape(nb, length, d), tq=tq, lambda_init=lambda_init).reshape(t, d)
    else:
        k, v, q = qkv_proj(*proj_args, tm=tm, streams=nb, transposed=False)
        q3 = q.reshape(nb, length, d)
        o = diff_attention_cached(q3, past_k, past_v,
                                  k.reshape(nb, length, N_HEADS, LANES),
                                  v.reshape(nb, length, N_HEADS, LANES),
                                  W["lam"], W["subln_w"], lambda_init=lambda_init)
        x3 = matmul_residual(o.reshape(t, d), W["w_o"], x2, tm=tm)

    moe_args = (x3, W["moe_norm_w"], W["moe_wr"], W["moe_wg"], W["moe_wu"], W["moe_wd"], W["final_norm_w"])
    if t >= 4 * MOE_GROUP_TILE:
        yout = moe_ffn_final_sorted(*moe_args, tb=512, tf=512)
    else:
        yout = moe_ffn_final(*moe_args, tb=t, rt=256, tf=512)
    return (yout.reshape(nb, length, d), new_conv[None], new_ssm[None],
            k.reshape(nb, length, N_HEADS, 2 * DK), v.reshape(nb, length, N_HEADS, DV))


def kernel(x_prompt, x_sample, cache_conv, state_ssm, cache_k, cache_v, mamba_norm_w, mamba_w_in, mamba_conv_w, mamba_conv_b, mamba_dt_bias, mamba_a_log, mamba_d, mamba_gn_w, mamba_w_out, kv_norm_w, w_k, w_v, attn_norm_w, w_q, lambda_q1, lambda_k1, lambda_q2, lambda_k2, subln_w, w_o, ffn_norm_w, ffn_w_gate, ffn_w_up, ffn_w_down, moe_norm_w, moe_w_router, moe_w_gate, moe_w_up, moe_w_down, final_norm_w):
    w_in = mamba_w_in[0]
    n_dt = w_in.shape[1] - D_INNER - CONV_DIM
    W = dict(
        mamba_norm_w=_row(mamba_norm_w[0]),
        w_in=w_in[:, :D_INNER + CONV_DIM].astype(BF16),
        w_dt=jnp.pad(w_in[:, D_INNER + CONV_DIM:], ((0, 0), (0, DT_PAD - n_dt))).astype(BF16),
        conv_w=mamba_conv_w[0].astype(F32),
        conv_b=_row(mamba_conv_b[0]),
        dt_bias=_pad_lanes(mamba_dt_bias[0], LANES),
        a_log=_pad_lanes(mamba_a_log[0], LANES),
        d_exp=_row(jnp.repeat(mamba_d[0], SSM_HEAD_DIM)),
        gn_w=_row(mamba_gn_w[0]),
        w_out=mamba_w_out[0].astype(BF16),
        kv_norm_w=_row(kv_norm_w), w_k=w_k.astype(BF16), w_v=w_v.astype(BF16),
        attn_norm_w=_row(attn_norm_w[0]), w_q=w_q[0].astype(BF16),
        lam=jnp.stack([lambda_q1[0], lambda_k1[0], lambda_q2[0], lambda_k2[0]]).astype(F32),
        subln_w=_row(subln_w[0]), w_o=w_o[0].astype(BF16),
        ffn_norm_w=_row(ffn_norm_w[0]),
        ffn_wg=ffn_w_gate[0].astype(BF16), ffn_wu=ffn_w_up[0].astype(BF16),
        ffn_wd=ffn_w_down[0].astype(BF16),
        moe_norm_w=_row(moe_norm_w[0]),
        moe_wr=jnp.pad(moe_w_router[0].astype(F32), ((0, 0), (0, LANES - N_EXPERTS))),
        moe_wg=moe_w_gate[0].astype(BF16), moe_wu=moe_w_up[0].astype(BF16),
        moe_wd=moe_w_down[0].astype(BF16),
        final_norm_w=_row(final_norm_w),
    )
    bp, lp = x_prompt.shape[0], x_prompt.shape[1]
    bs, ls = x_sample.shape[0], x_sample.shape[1]
    past = cache_k.shape[1]

    conv0 = jnp.zeros((bp, CONV_W - 1, CONV_DIM), F32)
    ssm0 = jnp.zeros((bp, SSM_HEADS, SSM_HEAD_DIM, D_STATE), F32)
    y_p, conv_p, ssm_p, k_p, v_p = _trunk(
        x_prompt, jnp.arange(lp, dtype=jnp.int32), conv0, ssm0, None, None, W,
        ssd_chunk=128, ssd_valid=128, tm=512, tq=512)

    pos_s = jnp.tile(past + jnp.arange(ls, dtype=jnp.int32), bs)
    y_s, conv_s, ssm_s, k_s, v_s = _trunk(
        x_sample, pos_s, cache_conv[0], state_ssm[0], cache_k, cache_v, W,
        ssd_chunk=128, ssd_valid=ls, tm=bs * ls, tq=None)
    return (y_p, y_s, conv_p, ssm_p, k_p, v_p, conv_s, ssm_s, k_s, v_s)
```

```python
import functools
import math

import numpy as np
import jax
import jax.numpy as jnp
from jax import lax
from jax.experimental import pallas as pl
from jax.experimental.pallas import tpu as pltpu

F32 = jnp.float32
BF16 = jnp.bfloat16

EPS = 1e-5
D_MODEL = 1024
D_INNER = 2048
SSM_HEAD_DIM = 64
SSM_HEADS = 32
SSM_GROUPS = 4
D_STATE = 128
GROUP_W = D_INNER // SSM_GROUPS
CONV_W = 4
CONV_HIST = 16
CONV_DIM = D_INNER + 2 * SSM_GROUPS * D_STATE
DT_PAD = 256
PROJ_W = D_INNER + CONV_DIM + DT_PAD
N_HEADS = 8
DK = 64
DV = 128
ROT_DIM = 16
ROPE_THETA = 500000.0
CHUNK = 64
CHUNK_SHIFT = 6
assert 1 << CHUNK_SHIFT == CHUNK
N_EXPERTS = 8
LANES = 128
NEG_BIG = -1e30
LOG2E = math.log2(math.e)
Q_COLS = 256
ONES_ROWS = 16
SCORE_LOOKAHEAD = 3
VMEM_LIMIT = 56 * 1024 * 1024


def _params(*sem):
    return pltpu.CompilerParams(dimension_semantics=sem, vmem_limit_bytes=VMEM_LIMIT)


def _sigmoid(x):
    return 1.0 / (1.0 + jnp.exp(-x))


def _rms_scale(x):
    return x * lax.rsqrt(jnp.mean(x * x, axis=-1, keepdims=True) + EPS)


def _split3(x):
    hi = x.astype(BF16)
    r1 = x - hi.astype(F32)
    mid = r1.astype(BF16)
    lo = (r1 - mid.astype(F32)).astype(BF16)
    return hi, mid, lo


def _dot(a, b):
    return jnp.dot(a, b, preferred_element_type=F32)


def _lane_tile(x, n):
    return jnp.concatenate([x] * n, axis=1) if n > 1 else x


def _dot_exact_rhs(x, m_bf16):
    hi, mid, lo = _split3(x)
    return _dot(hi, m_bf16) + _dot(mid, m_bf16) + _dot(lo, m_bf16)


def _dot_exact_lhs(m_bf16, x):
    hi, mid, lo = _split3(x)
    return _dot(m_bf16, hi) + _dot(m_bf16, mid) + _dot(m_bf16, lo)


def _rms_matmul_kernel(x_ref, nw_ref, w_ref, wdt_ref, o_ref, dt_ref, h_ref):
    @pl.when(pl.program_id(1) == 0)
    def _():
        h = (_rms_scale(x_ref[...]) * nw_ref[...]).astype(BF16)
        h_ref[...] = h
        dt_ref[...] = _dot(h, wdt_ref[...])

    o_ref[...] = _dot(h_ref[...], w_ref[...]).astype(o_ref.dtype)


def rms_matmul(x, nw, w, w_dt, *, tm, tn):
    t, d = x.shape
    n = w.shape[1]
    n_dt = w_dt.shape[1]
    resident = dict(pipeline_mode=pl.Buffered(1)) if tn == n else {}
    return pl.pallas_call(
        _rms_matmul_kernel,
        grid=(t // tm, n // tn),
        in_specs=[pl.BlockSpec((tm, d), lambda i, j: (i, 0)),
                  pl.BlockSpec((1, d), lambda i, j: (0, 0)),
                  pl.BlockSpec((d, tn), lambda i, j: (0, j), **resident),
                  pl.BlockSpec((d, n_dt), lambda i, j: (0, 0))],
        out_specs=[pl.BlockSpec((tm, tn), lambda i, j: (i, j)),
                   pl.BlockSpec((tm, n_dt), lambda i, j: (i, 0))],
        out_shape=[jax.ShapeDtypeStruct((t, n), BF16),
                   jax.ShapeDtypeStruct((t, n_dt), F32)],
        scratch_shapes=[pltpu.VMEM((tm, d), BF16)],
        compiler_params=_params("arbitrary", "arbitrary"),
        name="rms_matmul",
    )(x, nw, w, w_dt)


def _ssd_kernel(z_ref, xs_ref, b_ref, c_ref, dt_ref, cin_ref, sin_ref, cw_ref, cb_ref,
                dtb_ref, alog_ref, dexp_ref, gnw_ref, e_ref, shift_ref, wout_ref, res_ref,
                y_ref, sout_ref, state_ref, ebuf_ref, *, c, cv):
    step = pl.program_id(1)
    hist = CONV_HIST

    @pl.when(step == 0)
    def _():
        state_ref[...] = sin_ref[0].T
        ebuf_ref[0:hist, :] = cin_ref[0, 0:hist, :]
        ebuf_ref[hist + c:2 * hist + c, :] = cin_ref[0, hist:2 * hist, :]

    ebuf_ref[hist:hist + cv, 0:D_INNER] = xs_ref[0]
    ebuf_ref[hist:hist + cv, D_INNER:D_INNER + GROUP_W] = b_ref[0]
    ebuf_ref[hist:hist + cv, D_INNER + GROUP_W:CONV_DIM] = c_ref[0]
    if cv < c:
        ebuf_ref[hist + cv:hist + c, :] = jnp.zeros((c - cv, CONV_DIM), BF16)

    shifted = _dot(shift_ref[...], ebuf_ref[...])
    conv = cb_ref[...] + ebuf_ref[hist:hist + c, :].astype(F32) * cw_ref[CONV_W - 1:CONV_W, :]
    for k in range(CONV_W - 1):
        conv = conv + shifted[k * c:(k + 1) * c, :] * cw_ref[k:k + 1, :]
    conv = conv * _sigmoid(conv)
    if cv == c:
        ebuf_ref[0:hist, :] = ebuf_ref[c:c + hist, :]
        ebuf_ref[hist + c:2 * hist + c, :] = jnp.zeros((hist, CONV_DIM), BF16)

    xs = conv[:, 0:D_INNER]
    bm = conv[:, D_INNER:D_INNER + GROUP_W]
    cm = conv[:, D_INNER + GROUP_W:CONV_DIM]

    dt_in = dt_ref[0][:, 0:LANES] + dtb_ref[...]
    dt = jnp.maximum(dt_in, 0.0) + jnp.log1p(jnp.exp(-jnp.abs(dt_in)))
    if cv < c:
        dt = jnp.concatenate([dt, jnp.zeros((c - cv, LANES), F32)], axis=0)
    a = -jnp.exp(alog_ref[...])
    dta = dt * a

    row = lax.broadcasted_iota(jnp.int32, (c, c), 0)
    col = lax.broadcasted_iota(jnp.int32, (c, c), 1)
    tril = col <= row
    tri = jnp.where(tril, 1.0, 0.0).astype(BF16)
    cs = _dot_exact_lhs(tri, dta)
    cs_t = cs.T

    e = e_ref[...]
    dt_x = _dot_exact_rhs(dt, e)
    cs_x = _dot_exact_rhs(cs, e)
    cs_end_x = cs_x[c - 1:c, :]
    xdt = xs * dt_x
    decay_in = jnp.exp(cs_x)
    xdt_b = xdt.astype(BF16)
    xdt_end = (xdt * jnp.exp(cs_end_x - cs_x)).astype(BF16)
    state_decay = jnp.exp(cs_end_x)

    lane = lax.broadcasted_iota(jnp.int32, (c, LANES), 1)
    low_half = lane < SSM_HEAD_DIM

    y_groups = []
    for g in range(SSM_GROUPS):
        gs = slice(g * GROUP_W, (g + 1) * GROUP_W)
        ns = slice(g * D_STATE, (g + 1) * D_STATE)
        bg_t = bm[:, ns].T.astype(BF16)
        cg = cm[:, ns].astype(BF16)
        cb = _dot(cg, bg_t)
        sg = state_ref[:, gs]
        y_off = _dot(cg, sg.astype(BF16)) * decay_in[:, gs]
        pairs = []
        for j in range(GROUP_W // LANES):
            h0 = g * (GROUP_W // SSM_HEAD_DIM) + 2 * j
            xp = xdt_b[:, g * GROUP_W + j * LANES:g * GROUP_W + (j + 1) * LANES]
            ys = []
            for h in (h0, h0 + 1):
                seg = cs[:, h:h + 1] - cs_t[h:h + 1, :]
                m = (cb * jnp.where(tril, jnp.exp(seg), 0.0)).astype(BF16)
                ys.append(_dot(m, xp))
            pairs.append(jnp.where(low_half, ys[0], ys[1]))
        y_groups.append(jnp.concatenate(pairs, axis=1) + y_off)
        state_ref[:, gs] = sg * state_decay[:, gs] + _dot(bg_t, xdt_end[:, gs])

    y = jnp.concatenate(y_groups, axis=1) + dexp_ref[...] * xs
    if cv < c:
        y = y[0:cv]
    zv = z_ref[0].astype(F32)
    y = y * (zv * _sigmoid(zv))
    outs = []
    for g in range(SSM_GROUPS):
        gs = slice(g * GROUP_W, (g + 1) * GROUP_W)
        outs.append(_rms_scale(y[:, gs]))
    y_normed = (jnp.concatenate(outs, axis=1) * gnw_ref[...]).astype(BF16)
    y_ref[0] = res_ref[0] + _dot(y_normed, wout_ref[...])

    @pl.when(step == pl.num_programs(1) - 1)
    def _():
        sout_ref[0] = state_ref[...].T


def ssd(proj, dt_raw, conv_in, state_in, cw, cb, dtb, alog, dexp, gnw, w_out, res, *, c, cv):
    nb, length, _ = proj.shape
    d_model = res.shape[-1]
    steps = length // cv
    head_of_channel = np.arange(D_INNER) // SSM_HEAD_DIM
    expand = jnp.asarray(np.arange(LANES)[:, None] == head_of_channel[None, :], dtype=BF16)
    tap, pos = np.divmod(np.arange((CONV_W - 1) * c), c)
    src = pos + tap + CONV_HIST - (CONV_W - 1)
    cols = np.arange(c + 2 * CONV_HIST)[None, :]
    from_cache_low = (cols == (src + c + CONV_HIST)[:, None]) & ((pos + tap) < CONV_W - 1)[:, None]
    shift = jnp.asarray((cols == src[:, None]) | from_cache_low, dtype=BF16)
    const = lambda b, l: (0, 0)
    kern = functools.partial(_ssd_kernel, c=c, cv=cv)
    return pl.pallas_call(
        kern,
        grid=(nb, steps),
        in_specs=[
            pl.BlockSpec((1, cv, D_INNER), lambda b, l: (b, l, 0)),
            pl.BlockSpec((1, cv, D_INNER), lambda b, l: (b, l, 1)),
            pl.BlockSpec((1, cv, GROUP_W), lambda b, l: (b, l, 2 * D_INNER // GROUP_W)),
            pl.BlockSpec((1, cv, GROUP_W), lambda b, l: (b, l, 2 * D_INNER // GROUP_W + 1)),
            pl.BlockSpec((1, cv, DT_PAD), lambda b, l: (b, l, 0)),
            pl.BlockSpec((1, 2 * CONV_HIST, CONV_DIM), lambda b, l: (b, 0, 0)),
            pl.BlockSpec((1, D_INNER, D_STATE), lambda b, l: (b, 0, 0)),
            pl.BlockSpec((CONV_W, CONV_DIM), const),
            pl.BlockSpec((1, CONV_DIM), const),
            pl.BlockSpec((1, LANES), const),
            pl.BlockSpec((1, LANES), const),
            pl.BlockSpec((1, D_INNER), const),
            pl.BlockSpec((1, D_INNER), const),
            pl.BlockSpec((LANES, D_INNER), const),
            pl.BlockSpec(((CONV_W - 1) * c, c + 2 * CONV_HIST), const),
            pl.BlockSpec((D_INNER, d_model), const),
            pl.BlockSpec((1, cv, d_model), lambda b, l: (b, l, 0)),
        ],
        out_specs=[
            pl.BlockSpec((1, cv, d_model), lambda b, l: (b, l, 0)),
            pl.BlockSpec((1, D_INNER, D_STATE), lambda b, l: (b, 0, 0)),
        ],
        out_shape=[
            jax.ShapeDtypeStruct((nb, length, d_model), F32),
            jax.ShapeDtypeStruct((nb, D_INNER, D_STATE), F32),
        ],
        scratch_shapes=[pltpu.VMEM((D_STATE, D_INNER), F32),
                        pltpu.VMEM((c + 2 * CONV_HIST, CONV_DIM), BF16)],
        compiler_params=_params("arbitrary", "arbitrary"),
        name="ssd",
    )(proj, proj, proj, proj, dt_raw, conv_in, state_in, cw, cb, dtb, alog, dexp, gnw, expand, shift,
      w_out, res)


def _mm_res_kernel(a_ref, w_ref, r_ref, o_ref):
    o_ref[...] = r_ref[...] + _dot(a_ref[...], w_ref[...])


def matmul_residual(a, w, res, *, tm):
    t, k = a.shape
    n = w.shape[1]
    return pl.pallas_call(
        _mm_res_kernel,
        grid=(t // tm,),
        in_specs=[pl.BlockSpec((tm, k), lambda i: (i, 0)),
                  pl.BlockSpec((k, n), lambda i: (0, 0)),
                  pl.BlockSpec((tm, n), lambda i: (i, 0))],
        out_specs=pl.BlockSpec((tm, n), lambda i: (i, 0)),
        out_shape=jax.ShapeDtypeStruct((t, n), F32),
        compiler_params=_params("arbitrary"),
        name="matmul_residual",
    )(a, w, res)


def _ffn_kernel(x_ref, nw_ref, wg_ref, wu_ref, wd_ref, o_ref, h_ref, acc_ref):
    f = pl.program_id(1)

    @pl.when(f == 0)
    def _():
        x = x_ref[...]
        h_ref[...] = (_rms_scale(x) * nw_ref[...]).astype(BF16)
        acc_ref[...] = x

    h = h_ref[...]
    g = _dot(h, wg_ref[...])
    u = _dot(h, wu_ref[...])
    act = (g * _sigmoid(g) * u).astype(BF16)
    acc_ref[...] += _dot(act, wd_ref[...])

    @pl.when(f == pl.num_programs(1) - 1)
    def _():
        o_ref[...] = acc_ref[...]


def swiglu_ffn(x, nw, wg, wu, wd, *, tm, tf):
    t, d = x.shape
    ff = wg.shape[1]
    resident = dict(pipeline_mode=pl.Buffered(1)) if tf == ff else {}
    return pl.pallas_call(
        _ffn_kernel,
        grid=(t // tm, ff // tf),
        in_specs=[pl.BlockSpec((tm, d), lambda i, f: (i, 0)),
                  pl.BlockSpec((1, d), lambda i, f: (0, 0)),
                  pl.BlockSpec((d, tf), lambda i, f: (0, f), **resident),
                  pl.BlockSpec((d, tf), lambda i, f: (0, f), **resident),
                  pl.BlockSpec((tf, d), lambda i, f: (f, 0), **resident)],
        out_specs=pl.BlockSpec((tm, d), lambda i, f: (i, 0)),
        out_shape=jax.ShapeDtypeStruct((t, d), F32),
        scratch_shapes=[pltpu.VMEM((tm, d), BF16), pltpu.VMEM((tm, d), F32)],
        compiler_params=_params("arbitrary", "arbitrary"),
        name="swiglu_ffn",
    )(x, nw, wg, wu, wd)


def _qkv_kernel(x_ref, kvw_ref, aw_ref, wk_ref, wv_ref, wq_ref, cos_ref, sa_ref, sb_ref,
                k_ref, v_ref, *extra_refs, transposed):
    xn = _rms_scale(x_ref[...])
    hkv = (xn * kvw_ref[...]).astype(BF16)
    hq = (xn * aw_ref[...]).astype(BF16)
    cos, sa, sb = cos_ref[...], sa_ref[...], sb_ref[...]

    def rope(t):
        blocks = []
        for j in range(t.shape[1] // LANES):
            tb = t[:, j * LANES:(j + 1) * LANES]
            blocks.append(tb * cos + pltpu.roll(tb, LANES - ROT_DIM // 2, 1) * sa
                          + pltpu.roll(tb, ROT_DIM // 2, 1) * sb)
        return jnp.concatenate(blocks, axis=1)

    k = rope(_dot(hkv, wk_ref[...]))
    v = _dot(hkv, wv_ref[...])
    q = rope(_dot(hq, wq_ref[...]))
    tm = k.shape[0]
    for h in range(N_HEADS):
        rows = pl.ds(h, tm, stride=N_HEADS)
        k_ref[rows, :] = k[:, h * LANES:(h + 1) * LANES]
        v_ref[rows, :] = v[:, h * LANES:(h + 1) * LANES]
    if transposed:
        kb_ref, vt_ref, qt_ref = extra_refs
        kb_ref[...] = k.astype(BF16)
        vt_ref[0] = v.T.astype(BF16)
        qt_ref[0] = (q * (DK ** -0.5 * LOG2E)).T.astype(BF16)
    else:
        (q_ref,) = extra_refs
        q_ref[...] = (q * (DK ** -0.5)).astype(BF16)


def _rope_tables(pos):
    half = ROT_DIM // 2
    inv_freq = ROPE_THETA ** (-jnp.arange(half, dtype=F32) / half)
    ang = pos.astype(F32)[:, None] * inv_freq[None, :]
    cos, sin = jnp.cos(ang), jnp.sin(ang)
    n = pos.shape[0]
    ones = jnp.ones((n, DK - ROT_DIM), F32)
    zeros_h = jnp.zeros((n, half), F32)
    zeros_r = jnp.zeros((n, DK - ROT_DIM), F32)
    cos_t = jnp.concatenate([cos, cos, ones], axis=1)
    sa_t = jnp.concatenate([-sin, zeros_h, zeros_r], axis=1)
    sb_t = jnp.concatenate([zeros_h, sin, zeros_r], axis=1)
    rep = lambda t: jnp.tile(t, (1, LANES // DK))
    return rep(cos_t), rep(sa_t), rep(sb_t)


def qkv_proj(x, kvw, aw, wk, wv, wq, tables, *, tm, streams, transposed):
    t, d = x.shape
    period = tables[0].shape[0] // tm
    row = lambda i: (i, 0)
    const = lambda i: (0, 0)
    tab = lambda i: (i % period, 0)
    heads_out = jax.ShapeDtypeStruct((t * N_HEADS, LANES), F32)
    heads_spec = pl.BlockSpec((tm * N_HEADS, LANES), row)
    out_specs = [heads_spec, heads_spec, pl.BlockSpec((tm, d), row)]
    out_shape = [heads_out, heads_out, jax.ShapeDtypeStruct((t, d), BF16)]
    if transposed:
        per = t // streams // tm
        tr_spec = pl.BlockSpec((1, d, tm), lambda i: (i // per, 0, i % per))
        tr_out = jax.ShapeDtypeStruct((streams, d, t // streams), BF16)
        out_specs += [tr_spec, tr_spec]
        out_shape += [tr_out, tr_out]
    return pl.pallas_call(
        functools.partial(_qkv_kernel, transposed=transposed),
        grid=(t // tm,),
        in_specs=[pl.BlockSpec((tm, d), row),
                  pl.BlockSpec((1, d), const), pl.BlockSpec((1, d), const),
                  pl.BlockSpec((d, d), const), pl.BlockSpec((d, d), const), pl.BlockSpec((d, d), const),
                  pl.BlockSpec((tm, LANES), tab), pl.BlockSpec((tm, LANES), tab),
                  pl.BlockSpec((tm, LANES), tab)],
        out_specs=out_specs,
        out_shape=out_shape,
        compiler_params=_params("arbitrary"),
        name="qkv_proj",
    )(x, kvw, aw, wk, wv, wq, *tables)


def _lambda_value(lp_ref, lambda_init):
    lp = lp_ref[...]
    s1 = jnp.sum(lp[0:1] * lp[1:2], axis=-1, keepdims=True)
    s2 = jnp.sum(lp[2:3] * lp[3:4], axis=-1, keepdims=True)
    return jnp.exp(s1) - jnp.exp(s2) + lambda_init


def _diff_finish(o1, o2, lam, subw, lambda_init):
    o = o1 - lam * o2
    return _rms_scale(o) * subw * (1.0 - lambda_init)


def _attn_kernel(qt_ref, kt_ref, q_ref, k_ref, v_ref, lp_ref, subw_ref, wo_ref, res_ref, o_ref,
                 qs_ref, on_ref, *state_refs, tq, tk, lambda_init):
    p = pl.program_id(1)
    qi = qt_ref[p]
    ki = kt_ref[p]
    n_qc = 2 * tq // Q_COLS
    m_refs, acc_refs = state_refs[:n_qc], state_refs[n_qc:]

    @pl.when(ki == 0)
    def _():
        for m_ref, acc_ref in zip(m_refs, acc_refs):
            m_ref[...] = jnp.full(m_ref.shape, NEG_BIG, F32)
            acc_ref[...] = jnp.zeros(acc_ref.shape, F32)
        first_sub = lax.broadcasted_iota(jnp.int32, (LANES, tq), 0) < DK
        for h in range(N_HEADS):
            qh = q_ref[0, h * LANES:(h + 1) * LANES, :]
            zero = jnp.zeros_like(qh)
            qs_ref[h, :, 0:tq] = jnp.where(first_sub, qh, zero)
            qs_ref[h, :, tq:2 * tq] = jnp.where(first_sub, zero, qh)

    ones = jnp.ones((ONES_ROWS, tk), BF16)

    def sweep(masked):
        if masked:
            k_chunk = (ki * tk + lax.broadcasted_iota(jnp.int32, (tk, Q_COLS), 0)) >> CHUNK_SHIFT
            q_lane = lax.broadcasted_iota(jnp.int32, (tk, Q_COLS), 1)

        def scores(h, c):
            kh = k_ref[0, :, h * LANES:(h + 1) * LANES]
            s = _dot(kh, qs_ref[h, :, c * Q_COLS:(c + 1) * Q_COLS])
            if masked:
                q_chunk = (qi * tq + (c * Q_COLS) % tq + q_lane) >> CHUNK_SHIFT
                s = jnp.where(k_chunk <= q_chunk, s, NEG_BIG)
            return s

        groups = [(h, c) for h in range(N_HEADS) for c in range(n_qc)]
        pending = [scores(*g) for g in groups[:SCORE_LOOKAHEAD]]
        for i, (h, c) in enumerate(groups):
            s = pending.pop(0)
            if i + SCORE_LOOKAHEAD < len(groups):
                pending.append(scores(*groups[i + SCORE_LOOKAHEAD]))
            vt = jnp.concatenate([v_ref[0, h * LANES:(h + 1) * LANES, :], ones], axis=0)
            m_prev = m_refs[c][h:h + 1, :]
            m_new = jnp.maximum(m_prev, jnp.max(s, axis=0, keepdims=True))
            alpha = jnp.exp2(m_prev - m_new)
            pr = jnp.exp2(s - m_new).astype(BF16)
            acc_refs[c][h] = acc_refs[c][h] * alpha + _dot(vt, pr)
            m_refs[c][h:h + 1, :] = m_new

    @pl.when(ki < qi)
    def _():
        sweep(False)

    @pl.when(ki == qi)
    def _():
        sweep(True)
        lam = _lambda_value(lp_ref, lambda_init)
        subw = subw_ref[...]

        for h in range(N_HEADS):
            hs = slice(h * LANES, (h + 1) * LANES)
            for r in range(tq // Q_COLS):
                a1 = acc_refs[r][h]
                a2 = acc_refs[r + tq // Q_COLS][h]
                o_t = a1[0:DV] / a1[DV:DV + 1] - lam * (a2[0:DV] / a2[DV:DV + 1])
                res = _rms_scale(o_t.T) * subw * (1.0 - lambda_init)
                on_ref[r * Q_COLS:(r + 1) * Q_COLS, hs] = res.astype(BF16)
        o_ref[0] = res_ref[0] + _dot(on_ref[...], wo_ref[...])


def diff_attention(q_t, k, v_t, lam_params, subw, w_o, res, *, tq, lambda_init):
    nb, d, length = q_t.shape
    tk = tq
    nq = length // tq
    qt = np.concatenate([np.full(i + 1, i) for i in range(nq)]).astype(np.int32)
    kt = np.concatenate([np.arange(i + 1) for i in range(nq)]).astype(np.int32)
    kern = functools.partial(_attn_kernel, tq=tq, tk=tk, lambda_init=lambda_init)
    n_qc = 2 * tq // Q_COLS
    grid_spec = pltpu.PrefetchScalarGridSpec(
        num_scalar_prefetch=2,
        grid=(nb, len(qt)),
        in_specs=[pl.BlockSpec((1, d, tq), lambda b, p, qt_r, kt_r: (b, 0, qt_r[p])),
                  pl.BlockSpec((1, tk, d), lambda b, p, qt_r, kt_r: (b, kt_r[p], 0)),
                  pl.BlockSpec((1, d, tk), lambda b, p, qt_r, kt_r: (b, 0, kt_r[p])),
                  pl.BlockSpec((4, DK), lambda b, p, qt_r, kt_r: (0, 0)),
                  pl.BlockSpec((1, DV), lambda b, p, qt_r, kt_r: (0, 0)),
                  pl.BlockSpec((d, d), lambda b, p, qt_r, kt_r: (0, 0)),
                  pl.BlockSpec((1, tq, d), lambda b, p, qt_r, kt_r: (b, qt_r[p], 0))],
        out_specs=pl.BlockSpec((1, tq, d), lambda b, p, qt_r, kt_r: (b, qt_r[p], 0)),
        scratch_shapes=([pltpu.VMEM((N_HEADS, LANES, 2 * tq), BF16), pltpu.VMEM((tq, d), BF16)]
                        + [pltpu.VMEM((N_HEADS, Q_COLS), F32)] * n_qc
                        + [pltpu.VMEM((N_HEADS, DV + ONES_ROWS, Q_COLS), F32)] * n_qc),
    )
    return pl.pallas_call(
        kern,
        grid_spec=grid_spec,
        out_shape=jax.ShapeDtypeStruct((nb, length, d), F32),
        compiler_params=_params("arbitrary", "arbitrary"),
        name="diff_attention",
    )(jnp.asarray(qt), jnp.asarray(kt), q_t, k, v_t, lam_params, subw, w_o, res)


def _attn_cached_kernel(q_ref, ck_ref, cv_ref, kn_ref, vn_ref, lp_ref, subw_ref, o_ref,
                        *, nq, lambda_init):
    past = ck_ref.shape[1] // N_HEADS
    lane = lax.broadcasted_iota(jnp.int32, (nq, LANES), 1)
    nt = (((1,), (1,)), ((), ()))
    pad = jnp.zeros((LANES - nq, LANES), BF16)
    lam = _lambda_value(lp_ref, lambda_init)
    for h in range(N_HEADS):
        q = q_ref[0, :, h * LANES:(h + 1) * LANES]
        zero = jnp.zeros_like(q)
        qs = jnp.concatenate([jnp.where(lane < DK, q, zero), jnp.where(lane < DK, zero, q)], axis=0)
        old_rows = pl.ds(h, past, stride=N_HEADS)
        new_rows = pl.ds(h, nq, stride=N_HEADS)
        kc = ck_ref[0, old_rows, :].astype(BF16)
        vc = cv_ref[0, old_rows, :].astype(BF16)
        kn = jnp.concatenate([kn_ref[0, new_rows, :].astype(BF16), pad], axis=0)
        vn = jnp.concatenate([vn_ref[0, new_rows, :].astype(BF16), pad], axis=0)
        s_c = lax.dot_general(qs, kc, nt, preferred_element_type=F32)
        s_n = lax.dot_general(qs, kn, nt, preferred_element_type=F32)
        col = lax.broadcasted_iota(jnp.int32, s_n.shape, 1)
        s_n = jnp.where(col < nq, s_n, NEG_BIG)
        m = jnp.maximum(jnp.max(s_c, axis=-1, keepdims=True), jnp.max(s_n, axis=-1, keepdims=True))
        p_c = jnp.exp(s_c - m)
        p_n = jnp.exp(s_n - m)
        denom = jnp.sum(p_c, axis=-1, keepdims=True) + jnp.sum(p_n, axis=-1, keepdims=True)
        o = (_dot(p_c.astype(BF16), vc) + _dot(p_n.astype(BF16), vn)) / denom
        res = _diff_finish(o[0:nq], o[nq:2 * nq], lam, subw_ref[...], lambda_init)
        o_ref[0, :, h * LANES:(h + 1) * LANES] = res.astype(o_ref.dtype)


def diff_attention_cached(q, cache_k, cache_v, k_new, v_new, lam_params, subw, *, lambda_init):
    nb, nq, d = q.shape
    past = cache_k.shape[1]
    kern = functools.partial(_attn_cached_kernel, nq=nq, lambda_init=lambda_init)
    stream = lambda b: (b, 0, 0)
    const = lambda b: (0, 0)
    rows = lambda a: a.reshape(nb, a.shape[1] * N_HEADS, LANES)
    cache_k, cache_v, k_new, v_new = rows(cache_k), rows(cache_v), rows(k_new), rows(v_new)
    return pl.pallas_call(
        kern,
        grid=(nb,),
        in_specs=[pl.BlockSpec((1, nq, d), lambda b: (b, 0, 0)),
                  pl.BlockSpec((1, past * N_HEADS, LANES), stream),
                  pl.BlockSpec((1, past * N_HEADS, LANES), stream),
                  pl.BlockSpec((1, nq * N_HEADS, LANES), stream),
                  pl.BlockSpec((1, nq * N_HEADS, LANES), stream),
                  pl.BlockSpec((4, DK), const),
                  pl.BlockSpec((1, DV), const)],
        out_specs=pl.BlockSpec((1, nq, d), lambda b: (b, 0, 0)),
        out_shape=jax.ShapeDtypeStruct((nb, nq, d), BF16),
        compiler_params=_params("arbitrary"),
        name="diff_attention_cached",
    )(q, cache_k, cache_v, k_new, v_new, lam_params, subw)


def _route(logits):
    lane = lax.broadcasted_iota(jnp.int32, logits.shape, 1)
    valid = lane < N_EXPERTS
    lg = jnp.where(valid, logits, NEG_BIG)
    m1 = jnp.max(lg, axis=-1, keepdims=True)
    lane_f = lane.astype(F32)
    i1 = jnp.min(jnp.where(lg == m1, lane_f, float(LANES)), axis=-1, keepdims=True)
    lg2 = jnp.where(lane_f == i1, NEG_BIG, lg)
    m2 = jnp.max(lg2, axis=-1, keepdims=True)
    i2 = jnp.min(jnp.where(lg2 == m2, lane_f, float(LANES)), axis=-1, keepdims=True)
    e2 = jnp.exp(m2 - m1)
    g1 = 1.0 / (1.0 + e2)
    g2 = e2 / (1.0 + e2)
    return jnp.where(lane_f == i1, g1, 0.0) + jnp.where(lane_f == i2, g2, 0.0)


def _moe_route_kernel(x_ref, nw_ref, wr_ref, h_ref, comb_ref, rk_ref, rkt_ref, cnt_ref,
                      meta_ref, metat_ref, *, rt):
    t = x_ref.shape[0]
    hf = _rms_scale(x_ref[...]) * nw_ref[...]
    h_ref[...] = hf.astype(BF16)
    comb = _route(_dot_exact_rhs_general(hf, wr_ref[...]))
    comb_ref[...] = comb
    sel = jnp.where(comb > 0.0, 1.0, 0.0)
    row = lax.broadcasted_iota(jnp.int32, (rt, rt), 0)
    col = lax.broadcasted_iota(jnp.int32, (rt, rt), 1)
    strict = jnp.where(col < row, 1.0, 0.0).astype(BF16)
    carry = jnp.zeros((1, LANES), F32)
    ranks = []
    for s in range(t // rt):
        sel_s = sel[s * rt:(s + 1) * rt]
        rank_s = _dot(strict, sel_s.astype(BF16)) + carry
        ranks.append(jnp.where(sel_s > 0.0, rank_s, -1.0))
        carry = carry + jnp.sum(sel_s, axis=0, keepdims=True)
    rk = jnp.concatenate(ranks, axis=0) if len(ranks) > 1 else ranks[0]
    rk_ref[...] = rk.astype(jnp.int32)
    rkt_ref[0] = rk.T.astype(jnp.int32)
    cnt_ref[0] = jnp.broadcast_to(carry, (8, LANES)).astype(jnp.int32)

    padded = jnp.floor((carry + (MOE_ROW_ALIGN - 1)) * (1.0 / MOE_ROW_ALIGN)) * MOE_ROW_ALIGN
    lrow = lax.broadcasted_iota(jnp.int32, (LANES, LANES), 0)
    lcol = lax.broadcasted_iota(jnp.int32, (LANES, LANES), 1)
    before = jnp.where(lrow < lcol, 1.0, 0.0).astype(BF16)
    group_start = _dot(jnp.broadcast_to(padded, (8, LANES)).astype(BF16), before)[0:1]
    dest = jnp.where(sel > 0.0, rk + group_start, -1.0)
    lane_f = lax.broadcasted_iota(jnp.int32, (t, LANES), 1).astype(F32)
    la = jnp.min(jnp.where(sel > 0.0, lane_f, float(LANES)), axis=-1, keepdims=True)
    lb = jnp.max(jnp.where(sel > 0.0, lane_f, -1.0), axis=-1, keepdims=True)
    at = lambda v, l: jnp.sum(jnp.where(lane_f == l, v, 0.0), axis=-1, keepdims=True)
    two = lb > la
    d_a, g_a = at(dest, la), at(comb, la)
    d_b = jnp.where(two, at(dest, lb), -1.0)
    g_b = jnp.where(two, at(comb, lb), 0.0)
    meta = jnp.where(lane_f == 0.0, d_a, jnp.where(lane_f == 1.0, d_b,
                     jnp.where(lane_f == 2.0, g_a, jnp.where(lane_f == 3.0, g_b, 0.0))))
    meta_ref[...] = meta
    metat_ref[0] = meta.T[0:8]


def moe_route(x, nw, wr, *, tb, rt):
    t, d = x.shape
    nb = t // tb
    row = lambda b: (b, 0)
    const = lambda b: (0, 0)
    return pl.pallas_call(
        functools.partial(_moe_route_kernel, rt=rt),
        grid=(nb,),
        in_specs=[pl.BlockSpec((tb, d), row), pl.BlockSpec((1, d), const),
                  pl.BlockSpec((d, LANES), const)],
        out_specs=[pl.BlockSpec((tb, d), row), pl.BlockSpec((tb, LANES), row),
                   pl.BlockSpec((tb, LANES), row),
                   pl.BlockSpec((1, LANES, tb), lambda b: (b, 0, 0)),
                   pl.BlockSpec((1, 8, LANES), lambda b: (b, 0, 0)),
                   pl.BlockSpec((tb, LANES), row),
                   pl.BlockSpec((1, 8, tb), lambda b: (b, 0, 0))],
        out_shape=[jax.ShapeDtypeStruct((t, d), BF16),
                   jax.ShapeDtypeStruct((t, LANES), F32),
                   jax.ShapeDtypeStruct((t, LANES), jnp.int32),
                   jax.ShapeDtypeStruct((nb, LANES, tb), jnp.int32),
                   jax.ShapeDtypeStruct((nb, 8, LANES), jnp.int32),
                   jax.ShapeDtypeStruct((t, LANES), F32),
                   jax.ShapeDtypeStruct((nb, 8, tb), F32)],
        compiler_params=_params("arbitrary"),
        name="moe_route",
    )(x, nw, wr)


def _moe_expert_kernel(cnt_ref, x_ref, h_ref, comb_ref, rk_ref, rkt_ref, wg_ref, wu_ref, wd_ref,
                       fw_ref, o_ref, rkc_ref, xg_ref, gs_ref, yacc_ref, *, rt):
    b = pl.program_id(0)
    e = pl.program_id(1)
    f = pl.program_id(2)
    t = x_ref.shape[0]
    n_rows = cnt_ref[b * N_EXPERTS + e]
    half = rt // 2
    n_tiles = (n_rows + (half - 1)) // rt
    tail_row0 = pl.multiple_of(n_tiles * rt, rt)
    has_tail = n_rows > n_tiles * rt

    def over_tiles(fn):
        def body(j, carry):
            fn(pl.multiple_of(j * rt, rt), rt)
            return carry

        lax.fori_loop(0, n_tiles, body, 0)

        @pl.when(has_tail)
        def _():
            fn(tail_row0, half)

    @pl.when(jnp.logical_and(e == 0, f == 0))
    def _():
        o_ref[...] = x_ref[...]

    @pl.when(f == 0)
    def _():
        lane = lax.broadcasted_iota(jnp.int32, (t, LANES), 1)
        pick = lane == e
        rkc = jnp.sum(jnp.where(pick, rk_ref[...], 0).astype(F32), axis=-1, keepdims=True)
        rkc_ref[...] = jnp.broadcast_to(rkc, (t, LANES)).astype(jnp.int32)
        gate = jnp.sum(jnp.where(pick, comb_ref[...], 0.0), axis=-1, keepdims=True)
        g_hi, g_mid, g_lo = _split3(jnp.broadcast_to(gate, (t, LANES)))
        g3 = jnp.where(lane == 0, g_hi.astype(F32),
                       jnp.where(lane == 1, g_mid.astype(F32),
                                 jnp.where(lane == 2, g_lo.astype(F32), 0.0))).astype(BF16)
        rk_row = rkt_ref[0, pl.ds(e, 1), :]

        def gather(row0, size):
            tile = pl.ds(row0, size)
            rows = lax.broadcasted_iota(jnp.int32, (size, t), 0) + row0
            p = jnp.where(rk_row == rows, 1.0, 0.0).astype(BF16)
            xg_ref[tile, :] = _dot(p, h_ref[...]).astype(BF16)
            gate_rows = jnp.sum(_dot(p, g3), axis=-1, keepdims=True)
            gs_ref[tile, :] = jnp.broadcast_to(gate_rows, (size, LANES))
            yacc_ref[tile, :] = jnp.zeros((size, yacc_ref.shape[1]), F32)

        over_tiles(gather)

    def expert(row0, size):
        tile = pl.ds(row0, size)
        xg = xg_ref[tile, :]
        g = _dot(xg, wg_ref[...])
        u = _dot(xg, wu_ref[...])
        gates = _lane_tile(gs_ref[tile, :], g.shape[1] // LANES)
        act = (g * _sigmoid(g) * u * gates).astype(BF16)
        yacc_ref[tile, :] += _dot(act, wd_ref[...])

    over_tiles(expert)

    @pl.when(f == pl.num_programs(2) - 1)
    def _():
        def scatter(row0, size):
            cols = lax.broadcasted_iota(jnp.int32, (t, size), 1) + row0
            s = jnp.where(_lane_tile(rkc_ref[...], size // LANES) == cols, 1.0, 0.0).astype(BF16)
            o_ref[...] += _dot(s, yacc_ref[pl.ds(row0, size), :].astype(BF16))

        over_tiles(scatter)

    @pl.when(jnp.logical_and(e == pl.num_programs(1) - 1, f == pl.num_programs(2) - 1))
    def _():
        o_ref[...] = _rms_scale(o_ref[...]) * fw_ref[...]


def _dot_exact_rhs_general(x, w):
    xh, xm, xl = _split3(x)
    wh, wm, wl = _split3(w)
    return (_dot(xh, wh) + (_dot(xh, wm) + _dot(xm, wh))
            + (_dot(xh, wl) + _dot(xm, wm) + _dot(xl, wh)))


def moe_ffn_final(x, nw, wr, wg, wu, wd, fw, *, tb, rt, tf):
    t, d = x.shape
    ne, _, ff = wg.shape
    nb = t // tb
    h, comb, rk, rkt, cnt, _, _ = moe_route(x, nw, wr, tb=tb, rt=rt)
    counts = cnt[:, 0, :ne].reshape(nb * ne)
    once = pl.Buffered(1)
    blk = lambda b, e, f, c: (b, 0)
    const = lambda b, e, f, c: (0, 0)
    grid_spec = pltpu.PrefetchScalarGridSpec(
        num_scalar_prefetch=1,
        grid=(nb, ne, ff // tf),
        in_specs=[pl.BlockSpec((tb, d), blk, pipeline_mode=once),
                  pl.BlockSpec((tb, d), blk, pipeline_mode=once),
                  pl.BlockSpec((tb, LANES), blk, pipeline_mode=once),
                  pl.BlockSpec((tb, LANES), blk, pipeline_mode=once),
                  pl.BlockSpec((1, LANES, tb), lambda b, e, f, c: (b, 0, 0), pipeline_mode=once),
                  pl.BlockSpec((None, d, tf), lambda b, e, f, c: (e, 0, f)),
                  pl.BlockSpec((None, d, tf), lambda b, e, f, c: (e, 0, f)),
                  pl.BlockSpec((None, tf, d), lambda b, e, f, c: (e, f, 0)),
                  pl.BlockSpec((1, d), const)],
        out_specs=pl.BlockSpec((tb, d), blk),
        scratch_shapes=[pltpu.VMEM((tb, LANES), jnp.int32),
                        pltpu.VMEM((tb, d), BF16),
                        pltpu.VMEM((tb, LANES), F32),
                        pltpu.VMEM((tb, d), F32)],
    )
    return pl.pallas_call(
        functools.partial(_moe_expert_kernel, rt=rt),
        grid_spec=grid_spec,
        out_shape=jax.ShapeDtypeStruct((t, d), F32),
        compiler_params=_params("arbitrary", "arbitrary", "arbitrary"),
        name="moe_experts",
    )(counts, x, h, comb, rk, rkt, wg, wu, wd, fw)


MOE_ROW_ALIGN = 16
MOE_ROW_TILE = 128
MOE_GROUP_TILE = 1024


def _moe_plan(cnt, *, tb, region):
    nb, ne = cnt.shape
    padded = (cnt + (MOE_ROW_ALIGN - 1)) // MOE_ROW_ALIGN * MOE_ROW_ALIGN
    off = jnp.cumsum(padded, axis=0) - padded
    boff = jnp.cumsum(padded, axis=1) - padded
    total = jnp.sum(padded, axis=0)
    tiles = (total + MOE_ROW_TILE + MOE_GROUP_TILE - 1) // MOE_GROUP_TILE
    tile_end = jnp.cumsum(tiles)
    max_tiles = (2 * nb * tb + nb * ne * MOE_ROW_ALIGN + ne * MOE_ROW_TILE) // MOE_GROUP_TILE + ne + 1
    idx = jnp.arange(max_tiles, dtype=jnp.int32)
    last = jnp.maximum(tile_end[-1] - 1, 0)
    active = idx < tile_end[-1]
    ii = jnp.minimum(idx, last)
    te = jnp.searchsorted(tile_end, ii, side="right").astype(jnp.int32)
    k = ii - (tile_end - tiles)[te]
    tr = te * (region // MOE_GROUP_TILE) + k
    valid = jnp.clip(total[te] - k * MOE_GROUP_TILE, 0, MOE_GROUP_TILE)
    tv = jnp.where(active, valid, -1)
    flat = lambda a: a.reshape(-1).astype(jnp.int32)
    return flat(off), flat(boff), te, tr.astype(jnp.int32), tv.astype(jnp.int32)


def _block_rows(tb):
    need = 2 * tb + N_EXPERTS * MOE_ROW_ALIGN
    return -(-need // MOE_ROW_TILE) * MOE_ROW_TILE


def _for_group_pieces(cnt_ref, off_ref, boff_ref, blk, e, tb, region, enabled, make_copy, action):
    units = (cnt_ref[blk * N_EXPERTS + e] + (MOE_ROW_ALIGN - 1)) // MOE_ROW_ALIGN
    src0 = boff_ref[blk * N_EXPERTS + e]
    dst0 = e * region + off_ref[blk * N_EXPERTS + e]
    done = 0
    for k in reversed(range(_piece_bits(tb))):
        size = MOE_ROW_ALIGN << k
        has = (units >> k) & 1
        src = pl.ds(pl.multiple_of(src0 + done, MOE_ROW_ALIGN), size)
        dst = pl.ds(pl.multiple_of(dst0 + done, MOE_ROW_ALIGN), size)

        @pl.when(jnp.logical_and(enabled, has == 1))
        def _():
            action(make_copy(k, src, dst))

        done = done + has * size


def _piece_bits(tb):
    return (tb // MOE_ROW_ALIGN).bit_length()


def _moe_gather_kernel(cnt_ref, off_ref, boff_ref, h_ref, metat_ref, xs_ref, buf_ref, sem_ref, *, region):
    b = pl.program_id(0)
    tb = h_ref.shape[0]
    rows_b = buf_ref.shape[1]
    slot = lax.rem(b, 2)

    dest = metat_ref[0].astype(jnp.int32)
    rows = lax.broadcasted_iota(jnp.int32, (rows_b, tb), 0)
    onehot = jnp.where(rows == dest[0:1], 1.0, jnp.where(rows == dest[1:2], 1.0, 0.0)).astype(BF16)
    buf_ref[slot] = _dot(onehot, h_ref[...]).astype(BF16)

    def pieces(blk, sl, enabled, action):
        for e in range(N_EXPERTS):
            def make_copy(p, src_rows, dst_rows, e=e):
                return pltpu.make_async_copy(buf_ref.at[sl, src_rows], xs_ref.at[dst_rows],
                                             sem_ref.at[sl, e, p])
            _for_group_pieces(cnt_ref, off_ref, boff_ref, blk, e, tb, region, enabled, make_copy, action)

    pieces(jnp.maximum(b - 1, 0), 1 - slot, b > 0, lambda cp: cp.wait())
    pieces(b, slot, True, lambda cp: cp.start())
    pieces(b, slot, b == pl.num_programs(0) - 1, lambda cp: cp.wait())


def _moe_grouped_kernel(te_ref, tr_ref, tv_ref, x_ref, wg_ref, wu_ref, wd_ref, y_ref, xm_ref, acc_ref):
    i = pl.program_id(0)
    f = pl.program_id(1)
    valid = tv_ref[i]
    last_f = pl.num_programs(1) - 1

    @pl.when(valid > 0)
    def _():
        @pl.when(f == 0)
        def _():
            row = lax.broadcasted_iota(jnp.int32, x_ref.shape, 0)
            xm_ref[...] = jnp.where(row < valid, x_ref[...].astype(F32), 0.0).astype(BF16)
            acc_ref[...] = jnp.zeros(acc_ref.shape, F32)

        xm = xm_ref[...]
        g = _dot(xm, wg_ref[...])
        u = _dot(xm, wu_ref[...])
        acc_ref[...] += _dot((g * _sigmoid(g) * u).astype(BF16), wd_ref[...])

        @pl.when(f == last_f)
        def _():
            y_ref[...] = acc_ref[...].astype(y_ref.dtype)

    @pl.when(jnp.logical_and(valid == 0, f == last_f))
    def _():
        y_ref[...] = jnp.zeros(y_ref.shape, y_ref.dtype)


def _moe_combine_kernel(cnt_ref, off_ref, boff_ref, x_ref, meta_ref, ys_ref, fw_ref, o_ref,
                        buf_ref, sem_ref, *, region):
    b = pl.program_id(0)
    tb = x_ref.shape[0]
    rows_b = buf_ref.shape[0]

    @pl.when(b == 0)
    def _():
        buf_ref[...] = jnp.zeros(buf_ref.shape, BF16)

    def pieces(action):
        for e in range(N_EXPERTS):
            def make_copy(p, buf_rows, sorted_rows, e=e):
                return pltpu.make_async_copy(ys_ref.at[sorted_rows], buf_ref.at[buf_rows], sem_ref.at[e, p])
            _for_group_pieces(cnt_ref, off_ref, boff_ref, b, e, tb, region, True, make_copy, action)

    pieces(lambda cp: cp.start())
    meta = meta_ref[...]
    col = lax.broadcasted_iota(jnp.int32, (tb, rows_b), 1)
    d_a = meta[:, 0:1].astype(jnp.int32)
    d_b = meta[:, 1:2].astype(jnp.int32)
    g_a, g_b = meta[:, 2:3], meta[:, 3:4]
    hi = lambda g: g.astype(BF16).astype(F32)
    hit_a, hit_b = col == d_a, col == d_b
    s_hi = jnp.where(hit_a, hi(g_a), jnp.where(hit_b, hi(g_b), 0.0)).astype(BF16)
    s_lo = jnp.where(hit_a, g_a - hi(g_a), jnp.where(hit_b, g_b - hi(g_b), 0.0)).astype(BF16)
    pieces(lambda cp: cp.wait())
    y = buf_ref[...]
    out = x_ref[...] + _dot(jnp.concatenate([s_hi, s_lo], axis=1), jnp.concatenate([y, y], axis=0))
    o_ref[...] = _rms_scale(out) * fw_ref[...]


def moe_ffn_final_sorted(x, nw, wr, wg, wu, wd, fw, *, tb, tf):
    t, d = x.shape
    ne, _, ff = wg.shape
    nb = t // tb
    st = MOE_GROUP_TILE
    region = -(-(tb * nb + MOE_ROW_ALIGN * nb + tb) // st) * st
    h, _, _, _, cnt, meta, meta_t = moe_route(x, nw, wr, tb=tb, rt=tb)
    counts = cnt[:, 0, :ne]
    off, boff, te, tr, tv = _moe_plan(counts, tb=tb, region=region)
    counts = counts.reshape(nb * ne)
    n_tiles = te.shape[0]
    rows_b = _block_rows(tb)
    n_pieces = _piece_bits(tb)

    xs = pl.pallas_call(
        functools.partial(_moe_gather_kernel, region=region),
        grid_spec=pltpu.PrefetchScalarGridSpec(
            num_scalar_prefetch=3,
            grid=(nb,),
            in_specs=[pl.BlockSpec((tb, d), lambda b, c, o, bo: (b, 0)),
                      pl.BlockSpec((1, 8, tb), lambda b, c, o, bo: (b, 0, 0))],
            out_specs=pl.BlockSpec(memory_space=pl.ANY),
            scratch_shapes=[pltpu.VMEM((2, rows_b, d), BF16),
                            pltpu.SemaphoreType.DMA((2, ne, n_pieces))],
        ),
        out_shape=jax.ShapeDtypeStruct((ne * region, d), BF16),
        compiler_params=_params("arbitrary"),
        name="moe_gather",
    )(counts, off, boff, h, meta_t)

    last_f = ff // tf - 1
    used_f = lambda i, f, te_r, tr_r, tv_r: jnp.where(tv_r[i] > 0, f, last_f)
    ys = pl.pallas_call(
        _moe_grouped_kernel,
        grid_spec=pltpu.PrefetchScalarGridSpec(
            num_scalar_prefetch=3,
            grid=(n_tiles, ff // tf),
            in_specs=[pl.BlockSpec((st, d), lambda i, f, te_r, tr_r, tv_r: (tr_r[i], 0)),
                      pl.BlockSpec((None, d, tf), lambda i, f, te_r, tr_r, tv_r:
                                   (te_r[i], 0, used_f(i, f, te_r, tr_r, tv_r))),
                      pl.BlockSpec((None, d, tf), lambda i, f, te_r, tr_r, tv_r:
                                   (te_r[i], 0, used_f(i, f, te_r, tr_r, tv_r))),
                      pl.BlockSpec((None, tf, d), lambda i, f, te_r, tr_r, tv_r:
                                   (te_r[i], used_f(i, f, te_r, tr_r, tv_r), 0))],
            out_specs=pl.BlockSpec((st, d), lambda i, f, te_r, tr_r, tv_r: (tr_r[i], 0)),
            scratch_shapes=[pltpu.VMEM((st, d), BF16), pltpu.VMEM((st, d), F32)],
        ),
        out_shape=jax.ShapeDtypeStruct((ne * region, d), BF16),
        compiler_params=_params("arbitrary", "arbitrary"),
        name="moe_grouped",
    )(te, tr, tv, xs, wg, wu, wd)

    return pl.pallas_call(
        functools.partial(_moe_combine_kernel, region=region),
        grid_spec=pltpu.PrefetchScalarGridSpec(
            num_scalar_prefetch=3,
            grid=(nb,),
            in_specs=[pl.BlockSpec((tb, d), lambda b, c, o, bo: (b, 0)),
                      pl.BlockSpec((tb, LANES), lambda b, c, o, bo: (b, 0)),
                      pl.BlockSpec(memory_space=pl.ANY),
                      pl.BlockSpec((1, d), lambda b, c, o, bo: (0, 0))],
            out_specs=pl.BlockSpec((tb, d), lambda b, c, o, bo: (b, 0)),
            scratch_shapes=[pltpu.VMEM((rows_b, d), BF16), pltpu.SemaphoreType.DMA((ne, n_pieces))],
        ),
        out_shape=jax.ShapeDtypeStruct((t, d), F32),
        compiler_params=_params("arbitrary"),
        name="moe_combine",
    )(counts, off, boff, x, meta, ys, fw)


def _row(v):
    return v.reshape(1, -1).astype(F32)


def _pad_lanes(v, width):
    v = _row(v)
    return jnp.pad(v, ((0, 0), (0, width - v.shape[1])))


def _trunk(x, pos, conv_in, state_in, past_k, past_v, W, *, ssd_chunk, ssd_valid, tm, tq):
    nb, length, d = x.shape
    t = nb * length
    x0 = x.reshape(t, d)

    proj, dt_raw = rms_matmul(x0, W["mamba_norm_w"], W["w_in"], W["w_dt"], tm=min(t, 512),
                              tn=W["w_in"].shape[1])
    proj3 = proj.reshape(nb, length, D_INNER + CONV_DIM)
    new_conv = proj3[:, length - (CONV_W - 1):, D_INNER:D_INNER + CONV_DIM].astype(F32)
    conv_hi = conv_in.astype(BF16)
    conv_lo = (conv_in - conv_hi.astype(F32)).astype(BF16)
    front = ((0, 0), (CONV_HIST - (CONV_W - 1), 0), (0, 0))
    conv_hist = jnp.concatenate([jnp.pad(conv_hi, front), jnp.pad(conv_lo, front)], axis=1)
    x1, state_out = ssd(proj3, dt_raw.reshape(nb, length, DT_PAD), conv_hist,
                        state_in.reshape(nb, D_INNER, D_STATE),
                        W["conv_w"], W["conv_b"], W["dt_bias"], W["a_log"], W["d_exp"], W["gn_w"],
                        W["w_out"], x, c=ssd_chunk, cv=ssd_valid)
    new_ssm = state_out.reshape(nb, SSM_HEADS, SSM_HEAD_DIM, D_STATE)
    x1 = x1.reshape(t, d)

    x2 = swiglu_ffn(x1, W["ffn_norm_w"], W["ffn_wg"], W["ffn_wu"], W["ffn_wd"], tm=tm,
                    tf=W["ffn_wg"].shape[1])

    tables = _rope_tables(pos)
    proj_args = (x2, W["kv_norm_w"], W["attn_norm_w"], W["w_k"], W["w_v"], W["w_q"], tables)
    lambda_init = 0.8 - 0.6 * math.exp(-0.3 * 1)
    if past_k is None:
        k, v, kb, v_t, q_t = qkv_proj(*proj_args, tm=tm, streams=nb, transposed=True)
        x3 = diff_attention(q_t, kb.reshape(nb, length, d), v_t, W["lam"], W["subln_w"], W["w_o"],
                            x2.reshape(nb, length, d), tq=tq, lambda_init=lambda_init).reshape(t, d)
    else:
        k, v, q = qkv_proj(*proj_args, tm=tm, streams=nb, transposed=False)
        q3 = q.reshape(nb, length, d)
        o = diff_attention_cached(q3, past_k, past_v,
                                  k.reshape(nb, length, N_HEADS, LANES),
                                  v.reshape(nb, length, N_HEADS, LANES),
                                  W["lam"], W["subln_w"], lambda_init=lambda_init)
        x3 = matmul_residual(o.reshape(t, d), W["w_o"], x2, tm=tm)

    moe_args = (x3, W["moe_norm_w"], W["moe_wr"], W["moe_wg"], W["moe_wu"], W["moe_wd"], W["final_norm_w"])
    if t >= 4 * MOE_GROUP_TILE:
        yout = moe_ffn_final_sorted(*moe_args, tb=512, tf=512)
    else:
        yout = moe_ffn_final(*moe_args, tb=t, rt=256, tf=512)
    return (yout.reshape(nb, length, d), new_conv[None], new_ssm[None],
            k.reshape(nb, length, N_HEADS, 2 * DK), v.reshape(nb, length, N_HEADS, DV))


def kernel(x_prompt, x_sample, cache_conv, state_ssm, cache_k, cache_v, mamba_norm_w, mamba_w_in, mamba_conv_w, mamba_conv_b, mamba_dt_bias, mamba_a_log, mamba_d, mamba_gn_w, mamba_w_out, kv_norm_w, w_k, w_v, attn_norm_w, w_q, lambda_q1, lambda_k1, lambda_q2, lambda_k2, subln_w, w_o, ffn_norm_w, ffn_w_gate, ffn_w_up, ffn_w_down, moe_norm_w, moe_w_router, moe_w_gate, moe_w_up, moe_w_down, final_norm_w):
    w_in = mamba_w_in[0]
    n_dt = w_in.shape[1] - D_INNER - CONV_DIM
    W = dict(
        mamba_norm_w=_row(mamba_norm_w[0]),
        w_in=w_in[:, :D_INNER + CONV_DIM].astype(BF16),
        w_dt=jnp.pad(w_in[:, D_INNER + CONV_DIM:], ((0, 0), (0, DT_PAD - n_dt))).astype(BF16),
        conv_w=mamba_conv_w[0].astype(F32),
        conv_b=_row(mamba_conv_b[0]),
        dt_bias=_pad_lanes(mamba_dt_bias[0], LANES),
        a_log=_pad_lanes(mamba_a_log[0], LANES),
        d_exp=_row(jnp.repeat(mamba_d[0], SSM_HEAD_DIM)),
        gn_w=_row(mamba_gn_w[0]),
        w_out=mamba_w_out[0].astype(BF16),
        kv_norm_w=_row(kv_norm_w), w_k=w_k.astype(BF16), w_v=w_v.astype(BF16),
        attn_norm_w=_row(attn_norm_w[0]), w_q=w_q[0].astype(BF16),
        lam=jnp.stack([lambda_q1[0], lambda_k1[0], lambda_q2[0], lambda_k2[0]]).astype(F32),
        subln_w=_row(subln_w[0]), w_o=w_o[0].astype(BF16),
        ffn_norm_w=_row(ffn_norm_w[0]),
        ffn_wg=ffn_w_gate[0].astype(BF16), ffn_wu=ffn_w_up[0].astype(BF16),
        ffn_wd=ffn_w_down[0].astype(BF16),
        moe_norm_w=_row(moe_norm_w[0]),
        moe_wr=jnp.pad(moe_w_router[0].astype(F32), ((0, 0), (0, LANES - N_EXPERTS))),
        moe_wg=moe_w_gate[0].astype(BF16), moe_wu=moe_w_up[0].astype(BF16),
        moe_wd=moe_w_down[0].astype(BF16),
        final_norm_w=_row(final_norm_w),
    )
    bp, lp = x_prompt.shape[0], x_prompt.shape[1]
    bs, ls = x_sample.shape[0], x_sample.shape[1]
    past = cache_k.shape[1]

    conv0 = jnp.zeros((bp, CONV_W - 1, CONV_DIM), F32)
    ssm0 = jnp.zeros((bp, SSM_HEADS, SSM_HEAD_DIM, D_STATE), F32)
    y_p, conv_p, ssm_p, k_p, v_p = _trunk(
        x_prompt, jnp.arange(lp, dtype=jnp.int32), conv0, ssm0, None, None, W,
        ssd_chunk=128, ssd_valid=128, tm=512, tq=512)

    pos_s = jnp.tile(past + jnp.arange(ls, dtype=jnp.int32), bs)
    y_s, conv_s, ssm_s, k_s, v_s = _trunk(
        x_sample, pos_s, cache_conv[0], state_ssm[0], cache_k, cache_v, W,
        ssd_chunk=128, ssd_valid=ls, tm=bs * ls, tq=None)
    return (y_p, y_s, conv_p, ssm_p, k_p, v_p, conv_s, ssm_s, k_s, v_s)
```

```python
import functools
import math

import numpy as np
import jax
import jax.numpy as jnp
from jax import lax
from jax.experimental import pallas as pl
from jax.experimental.pallas import tpu as pltpu

F32 = jnp.float32
BF16 = jnp.bfloat16

EPS = 1e-5
D_MODEL = 1024
D_INNER = 2048
SSM_HEAD_DIM = 64
SSM_HEADS = 32
SSM_GROUPS = 4
D_STATE = 128
GROUP_W = D_INNER // SSM_GROUPS
CONV_W = 4
CONV_HIST = 16
CONV_DIM = D_INNER + 2 * SSM_GROUPS * D_STATE
DT_PAD = 256
PROJ_W = D_INNER + CONV_DIM + DT_PAD
N_HEADS = 8
DK = 64
DV = 128
ROT_DIM = 16
ROPE_THETA = 500000.0
CHUNK = 64
CHUNK_SHIFT = 6
assert 1 << CHUNK_SHIFT == CHUNK
N_EXPERTS = 8
LANES = 128
NEG_BIG = -1e30
LOG2E = math.log2(math.e)
Q_COLS = 256
ONES_ROWS = 16
SCORE_LOOKAHEAD = 3
VMEM_LIMIT = 56 * 1024 * 1024


def _params(*sem):
    return pltpu.CompilerParams(dimension_semantics=sem, vmem_limit_bytes=VMEM_LIMIT)


def _sigmoid(x):
    return 1.0 / (1.0 + jnp.exp(-x))


def _rms_scale(x):
    return x * lax.rsqrt(jnp.mean(x * x, axis=-1, keepdims=True) + EPS)


def _split3(x):
    hi = x.astype(BF16)
    r1 = x - hi.astype(F32)
    mid = r1.astype(BF16)
    lo = (r1 - mid.astype(F32)).astype(BF16)
    return hi, mid, lo


def _dot(a, b):
    return jnp.dot(a, b, preferred_element_type=F32)


def _lane_tile(x, n):
    return jnp.concatenate([x] * n, axis=1) if n > 1 else x


def _dot_exact_rhs(x, m_bf16):
    hi, mid, lo = _split3(x)
    return _dot(hi, m_bf16) + _dot(mid, m_bf16) + _dot(lo, m_bf16)


def _dot_exact_lhs(m_bf16, x):
    hi, mid, lo = _split3(x)
    return _dot(m_bf16, hi) + _dot(m_bf16, mid) + _dot(m_bf16, lo)


def _rms_matmul_kernel(x_ref, nw_ref, w_ref, wdt_ref, o_ref, dt_ref, h_ref):
    @pl.when(pl.program_id(1) == 0)
    def _():
        h = (_rms_scale(x_ref[...]) * nw_ref[...]).astype(BF16)
        h_ref[...] = h
        dt_ref[...] = _dot(h, wdt_ref[...])

    o_ref[...] = _dot(h_ref[...], w_ref[...]).astype(o_ref.dtype)


def rms_matmul(x, nw, w, w_dt, *, tm, tn):
    t, d = x.shape
    n = w.shape[1]
    n_dt = w_dt.shape[1]
    resident = dict(pipeline_mode=pl.Buffered(1)) if tn == n else {}
    return pl.pallas_call(
        _rms_matmul_kernel,
        grid=(t // tm, n // tn),
        in_specs=[pl.BlockSpec((tm, d), lambda i, j: (i, 0)),
                  pl.BlockSpec((1, d), lambda i, j: (0, 0)),
                  pl.BlockSpec((d, tn), lambda i, j: (0, j), **resident),
                  pl.BlockSpec((d, n_dt), lambda i, j: (0, 0))],
        out_specs=[pl.BlockSpec((tm, tn), lambda i, j: (i, j)),
                   pl.BlockSpec((tm, n_dt), lambda i, j: (i, 0))],
        out_shape=[jax.ShapeDtypeStruct((t, n), BF16),
                   jax.ShapeDtypeStruct((t, n_dt), F32)],
        scratch_shapes=[pltpu.VMEM((tm, d), BF16)],
        compiler_params=_params("arbitrary", "arbitrary"),
        name="rms_matmul",
    )(x, nw, w, w_dt)


def _ssd_kernel(z_ref, xs_ref, b_ref, c_ref, dt_ref, cin_ref, sin_ref, cw_ref, cb_ref,
                dtb_ref, alog_ref, dexp_ref, gnw_ref, e_ref, shift_ref, wout_ref, res_ref,
                y_ref, sout_ref, state_ref, ebuf_ref, *, c, cv):
    step = pl.program_id(1)
    hist = CONV_HIST

    @pl.when(step == 0)
    def _():
        state_ref[...] = sin_ref[0].T
        ebuf_ref[0:hist, :] = cin_ref[0, 0:hist, :]
        ebuf_ref[hist + c:2 * hist + c, :] = cin_ref[0, hist:2 * hist, :]

    ebuf_ref[hist:hist + cv, 0:D_INNER] = xs_ref[0]
    ebuf_ref[hist:hist + cv, D_INNER:D_INNER + GROUP_W] = b_ref[0]
    ebuf_ref[hist:hist + cv, D_INNER + GROUP_W:CONV_DIM] = c_ref[0]
    if cv < c:
        ebuf_ref[hist + cv:hist + c, :] = jnp.zeros((c - cv, CONV_DIM), BF16)

    shifted = _dot(shift_ref[...], ebuf_ref[...])
    conv = cb_ref[...] + ebuf_ref[hist:hist + c, :].astype(F32) * cw_ref[CONV_W - 1:CONV_W, :]
    for k in range(CONV_W - 1):
        conv = conv + shifted[k * c:(k + 1) * c, :] * cw_ref[k:k + 1, :]
    conv = conv * _sigmoid(conv)
    if cv == c:
        ebuf_ref[0:hist, :] = ebuf_ref[c:c + hist, :]
        ebuf_ref[hist + c:2 * hist + c, :] = jnp.zeros((hist, CONV_DIM), BF16)

    xs = conv[:, 0:D_INNER]
    bm = conv[:, D_INNER:D_INNER + GROUP_W]
    cm = conv[:, D_INNER + GROUP_W:CONV_DIM]

    dt_in = dt_ref[0][:, 0:LANES] + dtb_ref[...]
    dt = jnp.maximum(dt_in, 0.0) + jnp.log1p(jnp.exp(-jnp.abs(dt_in)))
    if cv < c:
        dt = jnp.concatenate([dt, jnp.zeros((c - cv, LANES), F32)], axis=0)
    a = -jnp.exp(alog_ref[...])
    dta = dt * a

    row = lax.broadcasted_iota(jnp.int32, (c, c), 0)
    col = lax.broadcasted_iota(jnp.int32, (c, c), 1)
    tril = col <= row
    tri = jnp.where(tril, 1.0, 0.0).astype(BF16)
    cs = _dot_exact_lhs(tri, dta)
    cs_t = cs.T

    e = e_ref[...]
    dt_x = _dot_exact_rhs(dt, e)
    cs_x = _dot_exact_rhs(cs, e)
    cs_end_x = cs_x[c - 1:c, :]
    xdt = xs * dt_x
    decay_in = jnp.exp(cs_x)
    xdt_b = xdt.astype(BF16)
    xdt_end = (xdt * jnp.exp(cs_end_x - cs_x)).astype(BF16)
    state_decay = jnp.exp(cs_end_x)

    lane = lax.broadcasted_iota(jnp.int32, (c, LANES), 1)
    low_half = lane < SSM_HEAD_DIM

    y_groups = []
    for g in range(SSM_GROUPS):
        gs = slice(g * GROUP_W, (g + 1) * GROUP_W)
        ns = slice(g * D_STATE, (g + 1) * D_STATE)
        bg_t = bm[:, ns].T.astype(BF16)
        cg = cm[:, ns].astype(BF16)
        cb = _dot(cg, bg_t)
        sg = state_ref[:, gs]
        y_off = _dot(cg, sg.astype(BF16)) * decay_in[:, gs]
        pairs = []
        for j in range(GROUP_W // LANES):
            h0 = g * (GROUP_W // SSM_HEAD_DIM) + 2 * j
            xp = xdt_b[:, g * GROUP_W + j * LANES:g * GROUP_W + (j + 1) * LANES]
            ys = []
            for h in (h0, h0 + 1):
                seg = cs[:, h:h + 1] - cs_t[h:h + 1, :]
                m = (cb * jnp.where(tril, jnp.exp(seg), 0.0)).astype(BF16)
                ys.append(_dot(m, xp))
            pairs.append(jnp.where(low_half, ys[0], ys[1]))
        y_groups.append(jnp.concatenate(pairs, axis=1) + y_off)
        state_ref[:, gs] = sg * state_decay[:, gs] + _dot(bg_t, xdt_end[:, gs])

    y = jnp.concatenate(y_groups, axis=1) + dexp_ref[...] * xs
    if cv < c:
        y = y[0:cv]
    zv = z_ref[0].astype(F32)
    y = y * (zv * _sigmoid(zv))
    outs = []
    for g in range(SSM_GROUPS):
        gs = slice(g * GROUP_W, (g + 1) * GROUP_W)
        outs.append(_rms_scale(y[:, gs]))
    y_normed = (jnp.concatenate(outs, axis=1) * gnw_ref[...]).astype(BF16)
    y_ref[0] = res_ref[0] + _dot(y_normed, wout_ref[...])

    @pl.when(step == pl.num_programs(1) - 1)
    def _():
        sout_ref[0] = state_ref[...].T


def ssd(proj, dt_raw, conv_in, state_in, cw, cb, dtb, alog, dexp, gnw, w_out, res, *, c, cv):
    nb, length, _ = proj.shape
    d_model = res.shape[-1]
    steps = length // cv
    head_of_channel = np.arange(D_INNER) // SSM_HEAD_DIM
    expand = jnp.asarray(np.arange(LANES)[:, None] == head_of_channel[None, :], dtype=BF16)
    tap, pos = np.divmod(np.arange((CONV_W - 1) * c), c)
    src = pos + tap + CONV_HIST - (CONV_W - 1)
    cols = np.arange(c + 2 * CONV_HIST)[None, :]
    from_cache_low = (cols == (src + c + CONV_HIST)[:, None]) & ((pos + tap) < CONV_W - 1)[:, None]
    shift = jnp.asarray((cols == src[:, None]) | from_cache_low, dtype=BF16)
    const = lambda b, l: (0, 0)
    kern = functools.partial(_ssd_kernel, c=c, cv=cv)
    return pl.pallas_call(
        kern,
        grid=(nb, steps),
        in_specs=[
            pl.BlockSpec((1, cv, D_INNER), lambda b, l: (b, l, 0)),
            pl.BlockSpec((1, cv, D_INNER), lambda b, l: (b, l, 1)),
            pl.BlockSpec((1, cv, GROUP_W), lambda b, l: (b, l, 2 * D_INNER // GROUP_W)),
            pl.BlockSpec((1, cv, GROUP_W), lambda b, l: (b, l, 2 * D_INNER // GROUP_W + 1)),
            pl.BlockSpec((1, cv, DT_PAD), lambda b, l: (b, l, 0)),
            pl.BlockSpec((1, 2 * CONV_HIST, CONV_DIM), lambda b, l: (b, 0, 0)),
            pl.BlockSpec((1, D_INNER, D_STATE), lambda b, l: (b, 0, 0)),
            pl.BlockSpec((CONV_W, CONV_DIM), const),
            pl.BlockSpec((1, CONV_DIM), const),
            pl.BlockSpec((1, LANES), const),
            pl.BlockSpec((1, LANES), const),
            pl.BlockSpec((1, D_INNER), const),
            pl.BlockSpec((1, D_INNER), const),
            pl.BlockSpec((LANES, D_INNER), const),
            pl.BlockSpec(((CONV_W - 1) * c, c + 2 * CONV_HIST), const),
            pl.BlockSpec((D_INNER, d_model), const),
            pl.BlockSpec((1, cv, d_model), lambda b, l: (b, l, 0)),
        ],
        out_specs=[
            pl.BlockSpec((1, cv, d_model), lambda b, l: (b, l, 0)),
            pl.BlockSpec((1, D_INNER, D_STATE), lambda b, l: (b, 0, 0)),
        ],
        out_shape=[
            jax.ShapeDtypeStruct((nb, length, d_model), F32),
            jax.ShapeDtypeStruct((nb, D_INNER, D_STATE), F32),
        ],
        scratch_shapes=[pltpu.VMEM((D_STATE, D_INNER), F32),
                        pltpu.VMEM((c + 2 * CONV_HIST, CONV_DIM), BF16)],
        compiler_params=_params("arbitrary", "arbitrary"),
        name="ssd",
    )(proj, proj, proj, proj, dt_raw, conv_in, state_in, cw, cb, dtb, alog, dexp, gnw, expand, shift,
      w_out, res)


def _mm_res_kernel(a_ref, w_ref, r_ref, o_ref):
    o_ref[...] = r_ref[...] + _dot(a_ref[...], w_ref[...])


def matmul_residual(a, w, res, *, tm):
    t, k = a.shape
    n = w.shape[1]
    return pl.pallas_call(
        _mm_res_kernel,
        grid=(t // tm,),
        in_specs=[pl.BlockSpec((tm, k), lambda i: (i, 0)),
                  pl.BlockSpec((k, n), lambda i: (0, 0)),
                  pl.BlockSpec((tm, n), lambda i: (i, 0))],
        out_specs=pl.BlockSpec((tm, n), lambda i: (i, 0)),
        out_shape=jax.ShapeDtypeStruct((t, n), F32),
        compiler_params=_params("arbitrary"),
        name="matmul_residual",
    )(a, w, res)


def _ffn_kernel(x_ref, nw_ref, wg_ref, wu_ref, wd_ref, o_ref, h_ref, acc_ref):
    f = pl.program_id(1)

    @pl.when(f == 0)
    def _():
        x = x_ref[...]
        h_ref[...] = (_rms_scale(x) * nw_ref[...]).astype(BF16)
        acc_ref[...] = x

    h = h_ref[...]
    g = _dot(h, wg_ref[...])
    u = _dot(h, wu_ref[...])
    act = (g * _sigmoid(g) * u).astype(BF16)
    acc_ref[...] += _dot(act, wd_ref[...])

    @pl.when(f == pl.num_programs(1) - 1)
    def _():
        o_ref[...] = acc_ref[...]


def swiglu_ffn(x, nw, wg, wu, wd, *, tm, tf):
    t, d = x.shape
    ff = wg.shape[1]
    resident = dict(pipeline_mode=pl.Buffered(1)) if tf == ff else {}
    return pl.pallas_call(
        _ffn_kernel,
        grid=(t // tm, ff // tf),
        in_specs=[pl.BlockSpec((tm, d), lambda i, f: (i, 0)),
                  pl.BlockSpec((1, d), lambda i, f: (0, 0)),
                  pl.BlockSpec((d, tf), lambda i, f: (0, f), **resident),
                  pl.BlockSpec((d, tf), lambda i, f: (0, f), **resident),
                  pl.BlockSpec((tf, d), lambda i, f: (f, 0), **resident)],
        out_specs=pl.BlockSpec((tm, d), lambda i, f: (i, 0)),
        out_shape=jax.ShapeDtypeStruct((t, d), F32),
        scratch_shapes=[pltpu.VMEM((tm, d), BF16), pltpu.VMEM((tm, d), F32)],
        compiler_params=_params("arbitrary", "arbitrary"),
        name="swiglu_ffn",
    )(x, nw, wg, wu, wd)


def _qkv_kernel(x_ref, kvw_ref, aw_ref, wk_ref, wv_ref, wq_ref, cos_ref, sa_ref, sb_ref,
                k_ref, v_ref, *extra_refs, transposed):
    xn = _rms_scale(x_ref[...])
    hkv = (xn * kvw_ref[...]).astype(BF16)
    hq = (xn * aw_ref[...]).astype(BF16)
    cos, sa, sb = cos_ref[...], sa_ref[...], sb_ref[...]

    def rope(t):
        blocks = []
        for j in range(t.shape[1] // LANES):
            tb = t[:, j * LANES:(j + 1) * LANES]
            blocks.append(tb * cos + pltpu.roll(tb, LANES - ROT_DIM // 2, 1) * sa
                          + pltpu.roll(tb, ROT_DIM // 2, 1) * sb)
        return jnp.concatenate(blocks, axis=1)

    k = rope(_dot(hkv, wk_ref[...]))
    v = _dot(hkv, wv_ref[...])
    q = rope(_dot(hq, wq_ref[...]))
    tm = k.shape[0]
    for h in range(N_HEADS):
        rows = pl.ds(h, tm, stride=N_HEADS)
        k_ref[rows, :] = k[:, h * LANES:(h + 1) * LANES]
        v_ref[rows, :] = v[:, h * LANES:(h + 1) * LANES]
    if transposed:
        kb_ref, vt_ref, qt_ref = extra_refs
        kb_ref[...] = k.astype(BF16)
        vt_ref[0] = v.T.astype(BF16)
        qt_ref[0] = (q * (DK ** -0.5 * LOG2E)).T.astype(BF16)
    else:
        (q_ref,) = extra_refs
        q_ref[...] = (q * (DK ** -0.5)).astype(BF16)


def _rope_tables(pos):
    half = ROT_DIM // 2
    inv_freq = ROPE_THETA ** (-jnp.arange(half, dtype=F32) / half)
    ang = pos.astype(F32)[:, None] * inv_freq[None, :]
    cos, sin = jnp.cos(ang), jnp.sin(ang)
    n = pos.shape[0]
    ones = jnp.ones((n, DK - ROT_DIM), F32)
    zeros_h = jnp.zeros((n, half), F32)
    zeros_r = jnp.zeros((n, DK - ROT_DIM), F32)
    cos_t = jnp.concatenate([cos, cos, ones], axis=1)
    sa_t = jnp.concatenate([-sin, zeros_h, zeros_r], axis=1)
    sb_t = jnp.concatenate([zeros_h, sin, zeros_r], axis=1)
    rep = lambda t: jnp.tile(t, (1, LANES // DK))
    return rep(cos_t), rep(sa_t), rep(sb_t)


def qkv_proj(x, kvw, aw, wk, wv, wq, tables, *, tm, streams, transposed):
    t, d = x.shape
    period = tables[0].shape[0] // tm
    row = lambda i: (i, 0)
    const = lambda i: (0, 0)
    tab = lambda i: (i % period, 0)
    heads_out = jax.ShapeDtypeStruct((t * N_HEADS, LANES), F32)
    heads_spec = pl.BlockSpec((tm * N_HEADS, LANES), row)
    out_specs = [heads_spec, heads_spec, pl.BlockSpec((tm, d), row)]
    out_shape = [heads_out, heads_out, jax.ShapeDtypeStruct((t, d), BF16)]
    if transposed:
        per = t // streams // tm
        tr_spec = pl.BlockSpec((1, d, tm), lambda i: (i // per, 0, i % per))
        tr_out = jax.ShapeDtypeStruct((streams, d, t // streams), BF16)
        out_specs += [tr_spec, tr_spec]
        out_shape += [tr_out, tr_out]
    return pl.pallas_call(
        functools.partial(_qkv_kernel, transposed=transposed),
        grid=(t // tm,),
        in_specs=[pl.BlockSpec((tm, d), row),
                  pl.BlockSpec((1, d), const), pl.BlockSpec((1, d), const),
                  pl.BlockSpec((d, d), const), pl.BlockSpec((d, d), const), pl.BlockSpec((d, d), const),
                  pl.BlockSpec((tm, LANES), tab), pl.BlockSpec((tm, LANES), tab),
                  pl.BlockSpec((tm, LANES), tab)],
        out_specs=out_specs,
        out_shape=out_shape,
        compiler_params=_params("arbitrary"),
        name="qkv_proj",
    )(x, kvw, aw, wk, wv, wq, *tables)


def _lambda_value(lp_ref, lambda_init):
    lp = lp_ref[...]
    s1 = jnp.sum(lp[0:1] * lp[1:2], axis=-1, keepdims=True)
    s2 = jnp.sum(lp[2:3] * lp[3:4], axis=-1, keepdims=True)
    return jnp.exp(s1) - jnp.exp(s2) + lambda_init


def _diff_finish(o1, o2, lam, subw, lambda_init):
    o = o1 - lam * o2
    return _rms_scale(o) * subw * (1.0 - lambda_init)


def _attn_kernel(qt_ref, kt_ref, q_ref, k_ref, v_ref, lp_ref, subw_ref, wo_ref, res_ref, o_ref,
                 qs_ref, on_ref, *state_refs, tq, tk, lambda_init):
    p = pl.program_id(1)
    qi = qt_ref[p]
    ki = kt_ref[p]
    n_qc = 2 * tq // Q_COLS
    m_refs, acc_refs = state_refs[:n_qc], state_refs[n_qc:]

    @pl.when(ki == 0)
    def _():
        for m_ref, acc_ref in zip(m_refs, acc_refs):
            m_ref[...] = jnp.full(m_ref.shape, NEG_BIG, F32)
            acc_ref[...] = jnp.zeros(acc_ref.shape, F32)
        first_sub = lax.broadcasted_iota(jnp.int32, (LANES, tq), 0) < DK
        for h in range(N_HEADS):
            qh = q_ref[0, h * LANES:(h + 1) * LANES, :]
            zero = jnp.zeros_like(qh)
            qs_ref[h, :, 0:tq] = jnp.where(first_sub, qh, zero)
            qs_ref[h, :, tq:2 * tq] = jnp.where(first_sub, zero, qh)

    ones = jnp.ones((ONES_ROWS, tk), BF16)

    def sweep(masked):
        if masked:
            k_chunk = (ki * tk + lax.broadcasted_iota(jnp.int32, (tk, Q_COLS), 0)) >> CHUNK_SHIFT
            q_lane = lax.broadcasted_iota(jnp.int32, (tk, Q_COLS), 1)

        def scores(h, c):
            kh = k_ref[0, :, h * LANES:(h + 1) * LANES]
            s = _dot(kh, qs_ref[h, :, c * Q_COLS:(c + 1) * Q_COLS])
            if masked:
                q_chunk = (qi * tq + (c * Q_COLS) % tq + q_lane) >> CHUNK_SHIFT
                s = jnp.where(k_chunk <= q_chunk, s, NEG_BIG)
            return s

        groups = [(h, c) for h in range(N_HEADS) for c in range(n_qc)]
        pending = [scores(*g) for g in groups[:SCORE_LOOKAHEAD]]
        for i, (h, c) in enumerate(groups):
            s = pending.pop(0)
            if i + SCORE_LOOKAHEAD < len(groups):
                pending.append(scores(*groups[i + SCORE_LOOKAHEAD]))
            vt = jnp.concatenate([v_ref[0, h * LANES:(h + 1) * LANES, :], ones], axis=0)
            m_prev = m_refs[c][h:h + 1, :]
            m_new = jnp.maximum(m_prev, jnp.max(s, axis=0, keepdims=True))
            alpha = jnp.exp2(m_prev - m_new)
            pr = jnp.exp2(s - m_new).astype(BF16)
            acc_refs[c][h] = acc_refs[c][h] * alpha + _dot(vt, pr)
            m_refs[c][h:h + 1, :] = m_new

    @pl.when(ki < qi)
    def _():
        sweep(False)

    @pl.when(ki == qi)
    def _():
        sweep(True)
        lam = _lambda_value(lp_ref, lambda_init)
        subw = subw_ref[...]

        for h in range(N_HEADS):
            hs = slice(h * LANES, (h + 1) * LANES)
            for r in range(tq // Q_COLS):
                a1 = acc_refs[r][h]
                a2 = acc_refs[r + tq // Q_COLS][h]
                o_t = a1[0:DV] / a1[DV:DV + 1] - lam * (a2[0:DV] / a2[DV:DV + 1])
                res = _rms_scale(o_t.T) * subw * (1.0 - lambda_init)
                on_ref[r * Q_COLS:(r + 1) * Q_COLS, hs] = res.astype(BF16)
        o_ref[0] = res_ref[0] + _dot(on_ref[...], wo_ref[...])


def diff_attention(q_t, k, v_t, lam_params, subw, w_o, res, *, tq, lambda_init):
    nb, d, length = q_t.shape
    tk = tq
    nq = length // tq
    qt = np.concatenate([np.full(i + 1, i) for i in range(nq)]).astype(np.int32)
    kt = np.concatenate([np.arange(i + 1) for i in range(nq)]).astype(np.int32)
    kern = functools.partial(_attn_kernel, tq=tq, tk=tk, lambda_init=lambda_init)
    n_qc = 2 * tq // Q_COLS
    grid_spec = pltpu.PrefetchScalarGridSpec(
        num_scalar_prefetch=2,
        grid=(nb, len(qt)),
        in_specs=[pl.BlockSpec((1, d, tq), lambda b, p, qt_r, kt_r: (b, 0, qt_r[p])),
                  pl.BlockSpec((1, tk, d), lambda b, p, qt_r, kt_r: (b, kt_r[p], 0)),
                  pl.BlockSpec((1, d, tk), lambda b, p, qt_r, kt_r: (b, 0, kt_r[p])),
                  pl.BlockSpec((4, DK), lambda b, p, qt_r, kt_r: (0, 0)),
                  pl.BlockSpec((1, DV), lambda b, p, qt_r, kt_r: (0, 0)),
                  pl.BlockSpec((d, d), lambda b, p, qt_r, kt_r: (0, 0)),
                  pl.BlockSpec((1, tq, d), lambda b, p, qt_r, kt_r: (b, qt_r[p], 0))],
        out_specs=pl.BlockSpec((1, tq, d), lambda b, p, qt_r, kt_r: (b, qt_r[p], 0)),
        scratch_shapes=([pltpu.VMEM((N_HEADS, LANES, 2 * tq), BF16), pltpu.VMEM((tq, d), BF16)]
                        + [pltpu.VMEM((N_HEADS, Q_COLS), F32)] * n_qc
                        + [pltpu.VMEM((N_HEADS, DV + ONES_ROWS, Q_COLS), F32)] * n_qc),
    )
    return pl.pallas_call(
        kern,
        grid_spec=grid_spec,
        out_shape=jax.ShapeDtypeStruct((nb, length, d), F32),
        compiler_params=_params("arbitrary", "arbitrary"),
        name="diff_attention",
    )(jnp.asarray(qt), jnp.asarray(kt), q_t, k, v_t, lam_params, subw, w_o, res)


def _attn_cached_kernel(q_ref, ck_ref, cv_ref, kn_ref, vn_ref, lp_ref, subw_ref, o_ref,
                        *, nq, lambda_init):
    past = ck_ref.shape[1] // N_HEADS
    lane = lax.broadcasted_iota(jnp.int32, (nq, LANES), 1)
    nt = (((1,), (1,)), ((), ()))
    pad = jnp.zeros((LANES - nq, LANES), BF16)
    lam = _lambda_value(lp_ref, lambda_init)
    for h in range(N_HEADS):
        q = q_ref[0, :, h * LANES:(h + 1) * LANES]
        zero = jnp.zeros_like(q)
        qs = jnp.concatenate([jnp.where(lane < DK, q, zero), jnp.where(lane < DK, zero, q)], axis=0)
        old_rows = pl.ds(h, past, stride=N_HEADS)
        new_rows = pl.ds(h, nq, stride=N_HEADS)
        kc = ck_ref[0, old_rows, :].astype(BF16)
        vc = cv_ref[0, old_rows, :].astype(BF16)
        kn = jnp.concatenate([kn_ref[0, new_rows, :].astype(BF16), pad], axis=0)
        vn = jnp.concatenate([vn_ref[0, new_rows, :].astype(BF16), pad], axis=0)
        s_c = lax.dot_general(qs, kc, nt, preferred_element_type=F32)
        s_n = lax.dot_general(qs, kn, nt, preferred_element_type=F32)
        col = lax.broadcasted_iota(jnp.int32, s_n.shape, 1)
        s_n = jnp.where(col < nq, s_n, NEG_BIG)
        m = jnp.maximum(jnp.max(s_c, axis=-1, keepdims=True), jnp.max(s_n, axis=-1, keepdims=True))
        p_c = jnp.exp(s_c - m)
        p_n = jnp.exp(s_n - m)
        denom = jnp.sum(p_c, axis=-1, keepdims=True) + jnp.sum(p_n, axis=-1, keepdims=True)
        o = (_dot(p_c.astype(BF16), vc) + _dot(p_n.astype(BF16), vn)) / denom
        res = _diff_finish(o[0:nq], o[nq:2 * nq], lam, subw_ref[...], lambda_init)
        o_ref[0, :, h * LANES:(h + 1) * LANES] = res.astype(o_ref.dtype)


def diff_attention_cached(q, cache_k, cache_v, k_new, v_new, lam_params, subw, *, lambda_init):
    nb, nq, d = q.shape
    past = cache_k.shape[1]
    kern = functools.partial(_attn_cached_kernel, nq=nq, lambda_init=lambda_init)
    stream = lambda b: (b, 0, 0)
    const = lambda b: (0, 0)
    rows = lambda a: a.reshape(nb, a.shape[1] * N_HEADS, LANES)
    cache_k, cache_v, k_new, v_new = rows(cache_k), rows(cache_v), rows(k_new), rows(v_new)
    return pl.pallas_call(
        kern,
        grid=(nb,),
        in_specs=[pl.BlockSpec((1, nq, d), lambda b: (b, 0, 0)),
                  pl.BlockSpec((1, past * N_HEADS, LANES), stream),
                  pl.BlockSpec((1, past * N_HEADS, LANES), stream),
                  pl.BlockSpec((1, nq * N_HEADS, LANES), stream),
                  pl.BlockSpec((1, nq * N_HEADS, LANES), stream),
                  pl.BlockSpec((4, DK), const),
                  pl.BlockSpec((1, DV), const)],
        out_specs=pl.BlockSpec((1, nq, d), lambda b: (b, 0, 0)),
        out_shape=jax.ShapeDtypeStruct((nb, nq, d), BF16),
        compiler_params=_params("arbitrary"),
        name="diff_attention_cached",
    )(q, cache_k, cache_v, k_new, v_new, lam_params, subw)


def _route(logits):
    lane = lax.broadcasted_iota(jnp.int32, logits.shape, 1)
    valid = lane < N_EXPERTS
    lg = jnp.where(valid, logits, NEG_BIG)
    m1 = jnp.max(lg, axis=-1, keepdims=True)
    lane_f = lane.astype(F32)
    i1 = jnp.min(jnp.where(lg == m1, lane_f, float(LANES)), axis=-1, keepdims=True)
    lg2 = jnp.where(lane_f == i1, NEG_BIG, lg)
    m2 = jnp.max(lg2, axis=-1, keepdims=True)
    i2 = jnp.min(jnp.where(lg2 == m2, lane_f, float(LANES)), axis=-1, keepdims=True)
    e2 = jnp.exp(m2 - m1)
    g1 = 1.0 / (1.0 + e2)
    g2 = e2 / (1.0 + e2)
    return jnp.where(lane_f == i1, g1, 0.0) + jnp.where(lane_f == i2, g2, 0.0)


def _moe_route_kernel(x_ref, nw_ref, wr_ref, h_ref, comb_ref, rk_ref, rkt_ref, cnt_ref,
                      meta_ref, metat_ref, *, rt):
    t = x_ref.shape[0]
    hf = _rms_scale(x_ref[...]) * nw_ref[...]
    h_ref[...] = hf.astype(BF16)
    comb = _route(_dot_exact_rhs_general(hf, wr_ref[...]))
    comb_ref[...] = comb
    sel = jnp.where(comb > 0.0, 1.0, 0.0)
    row = lax.broadcasted_iota(jnp.int32, (rt, rt), 0)
    col = lax.broadcasted_iota(jnp.int32, (rt, rt), 1)
    strict = jnp.where(col < row, 1.0, 0.0).astype(BF16)
    carry = jnp.zeros((1, LANES), F32)
    ranks = []
    for s in range(t // rt):
        sel_s = sel[s * rt:(s + 1) * rt]
        rank_s = _dot(strict, sel_s.astype(BF16)) + carry
        ranks.append(jnp.where(sel_s > 0.0, rank_s, -1.0))
        carry = carry + jnp.sum(sel_s, axis=0, keepdims=True)
    rk = jnp.concatenate(ranks, axis=0) if len(ranks) > 1 else ranks[0]
    rk_ref[...] = rk.astype(jnp.int32)
    rkt_ref[0] = rk.T.astype(jnp.int32)
    cnt_ref[0] = jnp.broadcast_to(carry, (8, LANES)).astype(jnp.int32)

    padded = jnp.floor((carry + (MOE_ROW_ALIGN - 1)) * (1.0 / MOE_ROW_ALIGN)) * MOE_ROW_ALIGN
    lrow = lax.broadcasted_iota(jnp.int32, (LANES, LANES), 0)
    lcol = lax.broadcasted_iota(jnp.int32, (LANES, LANES), 1)
    before = jnp.where(lrow < lcol, 1.0, 0.0).astype(BF16)
    group_start = _dot(jnp.broadcast_to(padded, (8, LANES)).astype(BF16), before)[0:1]
    dest = jnp.where(sel > 0.0, rk + group_start, -1.0)
    lane_f = lax.broadcasted_iota(jnp.int32, (t, LANES), 1).astype(F32)
    la = jnp.min(jnp.where(sel > 0.0, lane_f, float(LANES)), axis=-1, keepdims=True)
    lb = jnp.max(jnp.where(sel > 0.0, lane_f, -1.0), axis=-1, keepdims=True)
    at = lambda v, l: jnp.sum(jnp.where(lane_f == l, v, 0.0), axis=-1, keepdims=True)
    two = lb > la
    d_a, g_a = at(dest, la), at(comb, la)
    d_b = jnp.where(two, at(dest, lb), -1.0)
    g_b = jnp.where(two, at(comb, lb), 0.0)
    meta = jnp.where(lane_f == 0.0, d_a, jnp.where(lane_f == 1.0, d_b,
                     jnp.where(lane_f == 2.0, g_a, jnp.where(lane_f == 3.0, g_b, 0.0))))
    meta_ref[...] = meta
    metat_ref[0] = meta.T[0:8]


def moe_route(x, nw, wr, *, tb, rt):
    t, d = x.shape
    nb = t // tb
    row = lambda b: (b, 0)
    const = lambda b: (0, 0)
    return pl.pallas_call(
        functools.partial(_moe_route_kernel, rt=rt),
        grid=(nb,),
        in_specs=[pl.BlockSpec((tb, d), row), pl.BlockSpec((1, d), const),
                  pl.BlockSpec((d, LANES), const)],
        out_specs=[pl.BlockSpec((tb, d), row), pl.BlockSpec((tb, LANES), row),
                   pl.BlockSpec((tb, LANES), row),
                   pl.BlockSpec((1, LANES, tb), lambda b: (b, 0, 0)),
                   pl.BlockSpec((1, 8, LANES), lambda b: (b, 0, 0)),
                   pl.BlockSpec((tb, LANES), row),
                   pl.BlockSpec((1, 8, tb), lambda b: (b, 0, 0))],
        out_shape=[jax.ShapeDtypeStruct((t, d), BF16),
                   jax.ShapeDtypeStruct((t, LANES), F32),
                   jax.ShapeDtypeStruct((t, LANES), jnp.int32),
                   jax.ShapeDtypeStruct((nb, LANES, tb), jnp.int32),
                   jax.ShapeDtypeStruct((nb, 8, LANES), jnp.int32),
                   jax.ShapeDtypeStruct((t, LANES), F32),
                   jax.ShapeDtypeStruct((nb, 8, tb), F32)],
        compiler_params=_params("arbitrary"),
        name="moe_route",
    )(x, nw, wr)


def _moe_expert_kernel(cnt_ref, x_ref, h_ref, comb_ref, rk_ref, rkt_ref, wg_ref, wu_ref, wd_ref,
                       fw_ref, o_ref, rkc_ref, xg_ref, gs_ref, yacc_ref, *, rt):
    b = pl.program_id(0)
    e = pl.program_id(1)
    f = pl.program_id(2)
    t = x_ref.shape[0]
    n_rows = cnt_ref[b * N_EXPERTS + e]
    half = rt // 2
    n_tiles = (n_rows + (half - 1)) // rt
    tail_row0 = pl.multiple_of(n_tiles * rt, rt)
    has_tail = n_rows > n_tiles * rt

    def over_tiles(fn):
        def body(j, carry):
            fn(pl.multiple_of(j * rt, rt), rt)
            return carry

        lax.fori_loop(0, n_tiles, body, 0)

        @pl.when(has_tail)
        def _():
            fn(tail_row0, half)

    @pl.when(jnp.logical_and(e == 0, f == 0))
    def _():
        o_ref[...] = x_ref[...]

    @pl.when(f == 0)
    def _():
        lane = lax.broadcasted_iota(jnp.int32, (t, LANES), 1)
        pick = lane == e
        rkc = jnp.sum(jnp.where(pick, rk_ref[...], 0).astype(F32), axis=-1, keepdims=True)
        rkc_ref[...] = jnp.broadcast_to(rkc, (t, LANES)).astype(jnp.int32)
        gate = jnp.sum(jnp.where(pick, comb_ref[...], 0.0), axis=-1, keepdims=True)
        g_hi, g_mid, g_lo = _split3(jnp.broadcast_to(gate, (t, LANES)))
        g3 = jnp.where(lane == 0, g_hi.astype(F32),
                       jnp.where(lane == 1, g_mid.astype(F32),
                                 jnp.where(lane == 2, g_lo.astype(F32), 0.0))).astype(BF16)
        rk_row = rkt_ref[0, pl.ds(e, 1), :]

        def gather(row0, size):
            tile = pl.ds(row0, size)
            rows = lax.broadcasted_iota(jnp.int32, (size, t), 0) + row0
            p = jnp.where(rk_row == rows, 1.0, 0.0).astype(BF16)
            xg_ref[tile, :] = _dot(p, h_ref[...]).astype(BF16)
            gate_rows = jnp.sum(_dot(p, g3), axis=-1, keepdims=True)
            gs_ref[tile, :] = jnp.broadcast_to(gate_rows, (size, LANES))
            yacc_ref[tile, :] = jnp.zeros((size, yacc_ref.shape[1]), F32)

        over_tiles(gather)

    def expert(row0, size):
        tile = pl.ds(row0, size)
        xg = xg_ref[tile, :]
        g = _dot(xg, wg_ref[...])
        u = _dot(xg, wu_ref[...])
        gates = _lane_tile(gs_ref[tile, :], g.shape[1] // LANES)
        act = (g * _sigmoid(g) * u * gates).astype(BF16)
        yacc_ref[tile, :] += _dot(act, wd_ref[...])

    over_tiles(expert)

    @pl.when(f == pl.num_programs(2) - 1)
    def _():
        def scatter(row0, size):
            cols = lax.broadcasted_iota(jnp.int32, (t, size), 1) + row0
            s = jnp.where(_lane_tile(rkc_ref[...], size // LANES) == cols, 1.0, 0.0).astype(BF16)
            o_ref[...] += _dot(s, yacc_ref[pl.ds(row0, size), :].astype(BF16))

        over_tiles(scatter)

    @pl.when(jnp.logical_and(e == pl.num_programs(1) - 1, f == pl.num_programs(2) - 1))
    def _():
        o_ref[...] = _rms_scale(o_ref[...]) * fw_ref[...]


def _dot_exact_rhs_general(x, w):
    xh, xm, xl = _split3(x)
    wh, wm, wl = _split3(w)
    return (_dot(xh, wh) + (_dot(xh, wm) + _dot(xm, wh))
            + (_dot(xh, wl) + _dot(xm, wm) + _dot(xl, wh)))


def moe_ffn_final(x, nw, wr, wg, wu, wd, fw, *, tb, rt, tf):
    t, d = x.shape
    ne, _, ff = wg.shape
    nb = t // tb
    h, comb, rk, rkt, cnt, _, _ = moe_route(x, nw, wr, tb=tb, rt=rt)
    counts = cnt[:, 0, :ne].reshape(nb * ne)
    once = pl.Buffered(1)
    blk = lambda b, e, f, c: (b, 0)
    const = lambda b, e, f, c: (0, 0)
    grid_spec = pltpu.PrefetchScalarGridSpec(
        num_scalar_prefetch=1,
        grid=(nb, ne, ff // tf),
        in_specs=[pl.BlockSpec((tb, d), blk, pipeline_mode=once),
                  pl.BlockSpec((tb, d), blk, pipeline_mode=once),
                  pl.BlockSpec((tb, LANES), blk, pipeline_mode=once),
                  pl.BlockSpec((tb, LANES), blk, pipeline_mode=once),
                  pl.BlockSpec((1, LANES, tb), lambda b, e, f, c: (b, 0, 0), pipeline_mode=once),
                  pl.BlockSpec((None, d, tf), lambda b, e, f, c: (e, 0, f)),
                  pl.BlockSpec((None, d, tf), lambda b, e, f, c: (e, 0, f)),
                  pl.BlockSpec((None, tf, d), lambda b, e, f, c: (e, f, 0)),
                  pl.BlockSpec((1, d), const)],
        out_specs=pl.BlockSpec((tb, d), blk),
        scratch_shapes=[pltpu.VMEM((tb, LANES), jnp.int32),
                        pltpu.VMEM((tb, d), BF16),
                        pltpu.VMEM((tb, LANES), F32),
                        pltpu.VMEM((tb, d), F32)],
    )
    return pl.pallas_call(
        functools.partial(_moe_expert_kernel, rt=rt),
        grid_spec=grid_spec,
        out_shape=jax.ShapeDtypeStruct((t, d), F32),
        compiler_params=_params("arbitrary", "arbitrary", "arbitrary"),
        name="moe_experts",
    )(counts, x, h, comb, rk, rkt, wg, wu, wd, fw)


MOE_ROW_ALIGN = 16
MOE_ROW_TILE = 128
MOE_GROUP_TILE = 1024


def _moe_plan(cnt, *, tb, region):
    nb, ne = cnt.shape
    padded = (cnt + (MOE_ROW_ALIGN - 1)) // MOE_ROW_ALIGN * MOE_ROW_ALIGN
    off = jnp.cumsum(padded, axis=0) - padded
    boff = jnp.cumsum(padded, axis=1) - padded
    total = jnp.sum(padded, axis=0)
    tiles = (total + MOE_ROW_TILE + MOE_GROUP_TILE - 1) // MOE_GROUP_TILE
    tile_end = jnp.cumsum(tiles)
    max_tiles = (2 * nb * tb + nb * ne * MOE_ROW_ALIGN + ne * MOE_ROW_TILE) // MOE_GROUP_TILE + ne + 1
    idx = jnp.arange(max_tiles, dtype=jnp.int32)
    last = jnp.maximum(tile_end[-1] - 1, 0)
    active = idx < tile_end[-1]
    ii = jnp.minimum(idx, last)
    te = jnp.searchsorted(tile_end, ii, side="right").astype(jnp.int32)
    k = ii - (tile_end - tiles)[te]
    tr = te * (region // MOE_GROUP_TILE) + k
    valid = jnp.clip(total[te] - k * MOE_GROUP_TILE, 0, MOE_GROUP_TILE)
    tv = jnp.where(active, valid, -1)
    flat = lambda a: a.reshape(-1).astype(jnp.int32)
    return flat(off), flat(boff), te, tr.astype(jnp.int32), tv.astype(jnp.int32)


def _block_rows(tb):
    need = 2 * tb + N_EXPERTS * MOE_ROW_ALIGN
    return -(-need // MOE_ROW_TILE) * MOE_ROW_TILE


def _for_group_pieces(cnt_ref, off_ref, boff_ref, blk, e, tb, region, enabled, make_copy, action):
    units = (cnt_ref[blk * N_EXPERTS + e] + (MOE_ROW_ALIGN - 1)) // MOE_ROW_ALIGN
    src0 = boff_ref[blk * N_EXPERTS + e]
    dst0 = e * region + off_ref[blk * N_EXPERTS + e]
    done = 0
    for k in reversed(range(_piece_bits(tb))):
        size = MOE_ROW_ALIGN << k
        has = (units >> k) & 1
        src = pl.ds(pl.multiple_of(src0 + done, MOE_ROW_ALIGN), size)
        dst = pl.ds(pl.multiple_of(dst0 + done, MOE_ROW_ALIGN), size)

        @pl.when(jnp.logical_and(enabled, has == 1))
        def _():
            action(make_copy(k, src, dst))

        done = done + has * size


def _piece_bits(tb):
    return (tb // MOE_ROW_ALIGN).bit_length()


def _moe_gather_kernel(cnt_ref, off_ref, boff_ref, h_ref, metat_ref, xs_ref, buf_ref, sem_ref, *, region):
    b = pl.program_id(0)
    tb = h_ref.shape[0]
    rows_b = buf_ref.shape[1]
    slot = lax.rem(b, 2)

    dest = metat_ref[0].astype(jnp.int32)
    rows = lax.broadcasted_iota(jnp.int32, (rows_b, tb), 0)
    onehot = jnp.where(rows == dest[0:1], 1.0, jnp.where(rows == dest[1:2], 1.0, 0.0)).astype(BF16)
    buf_ref[slot] = _dot(onehot, h_ref[...]).astype(BF16)

    def pieces(blk, sl, enabled, action):
        for e in range(N_EXPERTS):
            def make_copy(p, src_rows, dst_rows, e=e):
                return pltpu.make_async_copy(buf_ref.at[sl, src_rows], xs_ref.at[dst_rows],
                                             sem_ref.at[sl, e, p])
            _for_group_pieces(cnt_ref, off_ref, boff_ref, blk, e, tb, region, enabled, make_copy, action)

    pieces(jnp.maximum(b - 1, 0), 1 - slot, b > 0, lambda cp: cp.wait())
    pieces(b, slot, True, lambda cp: cp.start())
    pieces(b, slot, b == pl.num_programs(0) - 1, lambda cp: cp.wait())


def _moe_grouped_kernel(te_ref, tr_ref, tv_ref, x_ref, wg_ref, wu_ref, wd_ref, y_ref, xm_ref, acc_ref):
    i = pl.program_id(0)
    f = pl.program_id(1)
    valid = tv_ref[i]
    last_f = pl.num_programs(1) - 1

    @pl.when(valid > 0)
    def _():
        @pl.when(f == 0)
        def _():
            row = lax.broadcasted_iota(jnp.int32, x_ref.shape, 0)
            xm_ref[...] = jnp.where(row < valid, x_ref[...].astype(F32), 0.0).astype(BF16)
            acc_ref[...] = jnp.zeros(acc_ref.shape, F32)

        xm = xm_ref[...]
        g = _dot(xm, wg_ref[...])
        u = _dot(xm, wu_ref[...])
        acc_ref[...] += _dot((g * _sigmoid(g) * u).astype(BF16), wd_ref[...])

        @pl.when(f == last_f)
        def _():
            y_ref[...] = acc_ref[...].astype(y_ref.dtype)

    @pl.when(jnp.logical_and(valid == 0, f == last_f))
    def _():
        y_ref[...] = jnp.zeros(y_ref.shape, y_ref.dtype)


def _moe_combine_kernel(cnt_ref, off_ref, boff_ref, x_ref, meta_ref, ys_ref, fw_ref, o_ref,
                        buf_ref, sem_ref, *, region):
    b = pl.program_id(0)
    last = pl.num_programs(0) - 1
    tb = x_ref.shape[0]
    rows_b = buf_ref.shape[1]
    slot = lax.rem(b, 2)

    def pieces(blk, sl, enabled, action):
        for e in range(N_EXPERTS):
            def make_copy(p, buf_rows, sorted_rows, e=e):
                return pltpu.make_async_copy(ys_ref.at[sorted_rows], buf_ref.at[sl, buf_rows],
                                             sem_ref.at[sl, e, p])
            _for_group_pieces(cnt_ref, off_ref, boff_ref, blk, e, tb, region, enabled, make_copy, action)

    @pl.when(b == 0)
    def _():
        buf_ref[...] = jnp.zeros(buf_ref.shape, BF16)

    pieces(b, slot, b == 0, lambda cp: cp.start())
    pieces(jnp.minimum(b + 1, last), 1 - slot, b < last, lambda cp: cp.start())
    meta = meta_ref[...]
    col = lax.broadcasted_iota(jnp.int32, (tb, rows_b), 1)
    d_a = meta[:, 0:1].astype(jnp.int32)
    d_b = meta[:, 1:2].astype(jnp.int32)
    g_a, g_b = meta[:, 2:3], meta[:, 3:4]
    hi = lambda g: g.astype(BF16).astype(F32)
    hit_a, hit_b = col == d_a, col == d_b
    s_hi = jnp.where(hit_a, hi(g_a), jnp.where(hit_b, hi(g_b), 0.0)).astype(BF16)
    s_lo = jnp.where(hit_a, g_a - hi(g_a), jnp.where(hit_b, g_b - hi(g_b), 0.0)).astype(BF16)
    pieces(b, slot, True, lambda cp: cp.wait())
    y = buf_ref[slot]
    out = x_ref[...] + _dot(jnp.concatenate([s_hi, s_lo], axis=1), jnp.concatenate([y, y], axis=0))
    o_ref[...] = _rms_scale(out) * fw_ref[...]


def moe_ffn_final_sorted(x, nw, wr, wg, wu, wd, fw, *, tb, tf):
    t, d = x.shape
    ne, _, ff = wg.shape
    nb = t // tb
    st = MOE_GROUP_TILE
    region = -(-(tb * nb + MOE_ROW_ALIGN * nb + tb) // st) * st
    h, _, _, _, cnt, meta, meta_t = moe_route(x, nw, wr, tb=tb, rt=tb)
    counts = cnt[:, 0, :ne]
    off, boff, te, tr, tv = _moe_plan(counts, tb=tb, region=region)
    counts = counts.reshape(nb * ne)
    n_tiles = te.shape[0]
    rows_b = _block_rows(tb)
    n_pieces = _piece_bits(tb)

    xs = pl.pallas_call(
        functools.partial(_moe_gather_kernel, region=region),
        grid_spec=pltpu.PrefetchScalarGridSpec(
            num_scalar_prefetch=3,
            grid=(nb,),
            in_specs=[pl.BlockSpec((tb, d), lambda b, c, o, bo: (b, 0)),
                      pl.BlockSpec((1, 8, tb), lambda b, c, o, bo: (b, 0, 0))],
            out_specs=pl.BlockSpec(memory_space=pl.ANY),
            scratch_shapes=[pltpu.VMEM((2, rows_b, d), BF16),
                            pltpu.SemaphoreType.DMA((2, ne, n_pieces))],
        ),
        out_shape=jax.ShapeDtypeStruct((ne * region, d), BF16),
        compiler_params=_params("arbitrary"),
        name="moe_gather",
    )(counts, off, boff, h, meta_t)

    last_f = ff // tf - 1
    used_f = lambda i, f, te_r, tr_r, tv_r: jnp.where(tv_r[i] > 0, f, last_f)
    ys = pl.pallas_call(
        _moe_grouped_kernel,
        grid_spec=pltpu.PrefetchScalarGridSpec(
            num_scalar_prefetch=3,
            grid=(n_tiles, ff // tf),
            in_specs=[pl.BlockSpec((st, d), lambda i, f, te_r, tr_r, tv_r: (tr_r[i], 0)),
                      pl.BlockSpec((None, d, tf), lambda i, f, te_r, tr_r, tv_r:
                                   (te_r[i], 0, used_f(i, f, te_r, tr_r, tv_r))),
                      pl.BlockSpec((None, d, tf), lambda i, f, te_r, tr_r, tv_r:
                                   (te_r[i], 0, used_f(i, f, te_r, tr_r, tv_r))),
                      pl.BlockSpec((None, tf, d), lambda i, f, te_r, tr_r, tv_r:
                                   (te_r[i], used_f(i, f, te_r, tr_r, tv_r), 0))],
            out_specs=pl.BlockSpec((st, d), lambda i, f, te_r, tr_r, tv_r: (tr_r[i], 0)),
            scratch_shapes=[pltpu.VMEM((st, d), BF16), pltpu.VMEM((st, d), F32)],
        ),
        out_shape=jax.ShapeDtypeStruct((ne * region, d), BF16),
        compiler_params=_params("arbitrary", "arbitrary"),
        name="moe_grouped",
    )(te, tr, tv, xs, wg, wu, wd)

    return pl.pallas_call(
        functools.partial(_moe_combine_kernel, region=region),
        grid_spec=pltpu.PrefetchScalarGridSpec(
            num_scalar_prefetch=3,
            grid=(nb,),
            in_specs=[pl.BlockSpec((tb, d), lambda b, c, o, bo: (b, 0)),
                      pl.BlockSpec((tb, LANES), lambda b, c, o, bo: (b, 0)),
                      pl.BlockSpec(memory_space=pl.ANY),
                      pl.BlockSpec((1, d), lambda b, c, o, bo: (0, 0))],
            out_specs=pl.BlockSpec((tb, d), lambda b, c, o, bo: (b, 0)),
            scratch_shapes=[pltpu.VMEM((2, rows_b, d), BF16),
                            pltpu.SemaphoreType.DMA((2, ne, n_pieces))],
        ),
        out_shape=jax.ShapeDtypeStruct((t, d), F32),
        compiler_params=_params("arbitrary"),
        name="moe_combine",
    )(counts, off, boff, x, meta, ys, fw)


def _row(v):
    return v.reshape(1, -1).astype(F32)


def _pad_lanes(v, width):
    v = _row(v)
    return jnp.pad(v, ((0, 0), (0, width - v.shape[1])))


def _trunk(x, pos, conv_in, state_in, past_k, past_v, W, *, ssd_chunk, ssd_valid, tm, tq):
    nb, length, d = x.shape
    t = nb * length
    x0 = x.reshape(t, d)

    proj, dt_raw = rms_matmul(x0, W["mamba_norm_w"], W["w_in"], W["w_dt"], tm=min(t, 512),
                              tn=W["w_in"].shape[1])
    proj3 = proj.reshape(nb, length, D_INNER + CONV_DIM)
    new_conv = proj3[:, length - (CONV_W - 1):, D_INNER:D_INNER + CONV_DIM].astype(F32)
    conv_hi = conv_in.astype(BF16)
    conv_lo = (conv_in - conv_hi.astype(F32)).astype(BF16)
    front = ((0, 0), (CONV_HIST - (CONV_W - 1), 0), (0, 0))
    conv_hist = jnp.concatenate([jnp.pad(conv_hi, front), jnp.pad(conv_lo, front)], axis=1)
    x1, state_out = ssd(proj3, dt_raw.reshape(nb, length, DT_PAD), conv_hist,
                        state_in.reshape(nb, D_INNER, D_STATE),
                        W["conv_w"], W["conv_b"], W["dt_bias"], W["a_log"], W["d_exp"], W["gn_w"],
                        W["w_out"], x, c=ssd_chunk, cv=ssd_valid)
    new_ssm = state_out.reshape(nb, SSM_HEADS, SSM_HEAD_DIM, D_STATE)
    x1 = x1.reshape(t, d)

    x2 = swiglu_ffn(x1, W["ffn_norm_w"], W["ffn_wg"], W["ffn_wu"], W["ffn_wd"], tm=tm,
                    tf=W["ffn_wg"].shape[1])

    tables = _rope_tables(pos)
    proj_args = (x2, W["kv_norm_w"], W["attn_norm_w"], W["w_k"], W["w_v"], W["w_q"], tables)
    lambda_init = 0.8 - 0.6 * math.exp(-0.3 * 1)
    if past_k is None:
        k, v, kb, v_t, q_t = qkv_proj(*proj_args, tm=tm, streams=nb, transposed=True)
        x3 = diff_attention(q_t, kb.reshape(nb, length, d), v_t, W["lam"], W["subln_w"], W["w_o"],
                            x2.reshape(nb, length, d), tq=tq, lambda_init=lambda_init).reshape(t, d)
    else:
        k, v, q = qkv_proj(*proj_args, tm=tm, streams=nb, transposed=False)
        q3 = q.reshape(nb, length, d)
        o = diff_attention_cached(q3, past_k, past_v,
                                  k.reshape(nb, length, N_HEADS, LANES),
                                  v.reshape(nb, length, N_HEADS, LANES),
                                  W["lam"], W["subln_w"], lambda_init=lambda_init)
        x3 = matmul_residual(o.reshape(t, d), W["w_o"], x2, tm=tm)

    moe_args = (x3, W["moe_norm_w"], W["moe_wr"], W["moe_wg"], W["moe_wu"], W["moe_wd"], W["final_norm_w"])
    if t >= 4 * MOE_GROUP_TILE:
        yout = moe_ffn_final_sorted(*moe_args, tb=512, tf=512)
    else:
        yout = moe_ffn_final(*moe_args, tb=t, rt=256, tf=512)
    return (yout.reshape(nb, length, d), new_conv[None], new_ssm[None],
            k.reshape(nb, length, N_HEADS, 2 * DK), v.reshape(nb, length, N_HEADS, DV))


def kernel(x_prompt, x_sample, cache_conv, state_ssm, cache_k, cache_v, mamba_norm_w, mamba_w_in, mamba_conv_w, mamba_conv_b, mamba_dt_bias, mamba_a_log, mamba_d, mamba_gn_w, mamba_w_out, kv_norm_w, w_k, w_v, attn_norm_w, w_q, lambda_q1, lambda_k1, lambda_q2, lambda_k2, subln_w, w_o, ffn_norm_w, ffn_w_gate, ffn_w_up, ffn_w_down, moe_norm_w, moe_w_router, moe_w_gate, moe_w_up, moe_w_down, final_norm_w):
    w_in = mamba_w_in[0]
    n_dt = w_in.shape[1] - D_INNER - CONV_DIM
    W = dict(
        mamba_norm_w=_row(mamba_norm_w[0]),
        w_in=w_in[:, :D_INNER + CONV_DIM].astype(BF16),
        w_dt=jnp.pad(w_in[:, D_INNER + CONV_DIM:], ((0, 0), (0, DT_PAD - n_dt))).astype(BF16),
        conv_w=mamba_conv_w[0].astype(F32),
        conv_b=_row(mamba_conv_b[0]),
        dt_bias=_pad_lanes(mamba_dt_bias[0], LANES),
        a_log=_pad_lanes(mamba_a_log[0], LANES),
        d_exp=_row(jnp.repeat(mamba_d[0], SSM_HEAD_DIM)),
        gn_w=_row(mamba_gn_w[0]),
        w_out=mamba_w_out[0].astype(BF16),
        kv_norm_w=_row(kv_norm_w), w_k=w_k.astype(BF16), w_v=w_v.astype(BF16),
        attn_norm_w=_row(attn_norm_w[0]), w_q=w_q[0].astype(BF16),
        lam=jnp.stack([lambda_q1[0], lambda_k1[0], lambda_q2[0], lambda_k2[0]]).astype(F32),
        subln_w=_row(subln_w[0]), w_o=w_o[0].astype(BF16),
        ffn_norm_w=_row(ffn_norm_w[0]),
        ffn_wg=ffn_w_gate[0].astype(BF16), ffn_wu=ffn_w_up[0].astype(BF16),
        ffn_wd=ffn_w_down[0].astype(BF16),
        moe_norm_w=_row(moe_norm_w[0]),
        moe_wr=jnp.pad(moe_w_router[0].astype(F32), ((0, 0), (0, LANES - N_EXPERTS))),
        moe_wg=moe_w_gate[0].astype(BF16), moe_wu=moe_w_up[0].astype(BF16),
        moe_wd=moe_w_down[0].astype(BF16),
        final_norm_w=_row(final_norm_w),
    )
    bp, lp = x_prompt.shape[0], x_prompt.shape[1]
    bs, ls = x_sample.shape[0], x_sample.shape[1]
    past = cache_k.shape[1]

    conv0 = jnp.zeros((bp, CONV_W - 1, CONV_DIM), F32)
    ssm0 = jnp.zeros((bp, SSM_HEADS, SSM_HEAD_DIM, D_STATE), F32)
    y_p, conv_p, ssm_p, k_p, v_p = _trunk(
        x_prompt, jnp.arange(lp, dtype=jnp.int32), conv0, ssm0, None, None, W,
        ssd_chunk=128, ssd_valid=128, tm=512, tq=512)

    pos_s = jnp.tile(past + jnp.arange(ls, dtype=jnp.int32), bs)
    y_s, conv_s, ssm_s, k_s, v_s = _trunk(
        x_sample, pos_s, cache_conv[0], state_ssm[0], cache_k, cache_v, W,
        ssd_chunk=128, ssd_valid=ls, tm=bs * ls, tq=None)
    return (y_p, y_s, conv_p, ssm_p, k_p, v_p, conv_s, ssm_s, k_s, v_s)
```

```python
import functools
import math

import numpy as np
import jax
import jax.numpy as jnp
from jax import lax
from jax.experimental import pallas as pl
from jax.experimental.pallas import tpu as pltpu

F32 = jnp.float32
BF16 = jnp.bfloat16

EPS = 1e-5
D_MODEL = 1024
D_INNER = 2048
SSM_HEAD_DIM = 64
SSM_HEADS = 32
SSM_GROUPS = 4
D_STATE = 128
GROUP_W = D_INNER // SSM_GROUPS
CONV_W = 4
CONV_HIST = 16
CONV_DIM = D_INNER + 2 * SSM_GROUPS * D_STATE
DT_PAD = 256
PROJ_W = D_INNER + CONV_DIM + DT_PAD
N_HEADS = 8
DK = 64
DV = 128
ROT_DIM = 16
ROPE_THETA = 500000.0
CHUNK = 64
CHUNK_SHIFT = 6
assert 1 << CHUNK_SHIFT == CHUNK
N_EXPERTS = 8
LANES = 128
NEG_BIG = -1e30
LOG2E = math.log2(math.e)
Q_COLS = 256
ONES_ROWS = 16
SCORE_LOOKAHEAD = 3
VMEM_LIMIT = 56 * 1024 * 1024


def _params(*sem):
    return pltpu.CompilerParams(dimension_semantics=sem, vmem_limit_bytes=VMEM_LIMIT)


def _sigmoid(x):
    return 1.0 / (1.0 + jnp.exp(-x))


def _rms_scale(x):
    return x * lax.rsqrt(jnp.mean(x * x, axis=-1, keepdims=True) + EPS)


def _split3(x):
    hi = x.astype(BF16)
    r1 = x - hi.astype(F32)
    mid = r1.astype(BF16)
    lo = (r1 - mid.astype(F32)).astype(BF16)
    return hi, mid, lo


def _dot(a, b):
    return jnp.dot(a, b, preferred_element_type=F32)


def _lane_tile(x, n):
    return jnp.concatenate([x] * n, axis=1) if n > 1 else x


def _dot_exact_rhs(x, m_bf16):
    hi, mid, lo = _split3(x)
    return _dot(hi, m_bf16) + _dot(mid, m_bf16) + _dot(lo, m_bf16)


def _dot_exact_lhs(m_bf16, x):
    hi, mid, lo = _split3(x)
    return _dot(m_bf16, hi) + _dot(m_bf16, mid) + _dot(m_bf16, lo)


def _rms_matmul_kernel(x_ref, nw_ref, w_ref, wdt_ref, o_ref, dt_ref, h_ref):
    @pl.when(pl.program_id(1) == 0)
    def _():
        h = (_rms_scale(x_ref[...]) * nw_ref[...]).astype(BF16)
        h_ref[...] = h
        dt_ref[...] = _dot(h, wdt_ref[...])

    o_ref[...] = _dot(h_ref[...], w_ref[...]).astype(o_ref.dtype)


def rms_matmul(x, nw, w, w_dt, *, tm, tn):
    t, d = x.shape
    n = w.shape[1]
    n_dt = w_dt.shape[1]
    resident = dict(pipeline_mode=pl.Buffered(1)) if tn == n else {}
    return pl.pallas_call(
        _rms_matmul_kernel,
        grid=(t // tm, n // tn),
        in_specs=[pl.BlockSpec((tm, d), lambda i, j: (i, 0)),
                  pl.BlockSpec((1, d), lambda i, j: (0, 0)),
                  pl.BlockSpec((d, tn), lambda i, j: (0, j), **resident),
                  pl.BlockSpec((d, n_dt), lambda i, j: (0, 0))],
        out_specs=[pl.BlockSpec((tm, tn), lambda i, j: (i, j)),
                   pl.BlockSpec((tm, n_dt), lambda i, j: (i, 0))],
        out_shape=[jax.ShapeDtypeStruct((t, n), BF16),
                   jax.ShapeDtypeStruct((t, n_dt), F32)],
        scratch_shapes=[pltpu.VMEM((tm, d), BF16)],
        compiler_params=_params("arbitrary", "arbitrary"),
        name="rms_matmul",
    )(x, nw, w, w_dt)


def _ssd_kernel(z_ref, xs_ref, b_ref, c_ref, dt_ref, cin_ref, sin_ref, cw_ref, cb_ref,
                dtb_ref, alog_ref, dexp_ref, gnw_ref, e_ref, shift_ref, wout_ref, res_ref,
                y_ref, sout_ref, state_ref, ebuf_ref, *, c, cv):
    step = pl.program_id(1)
    hist = CONV_HIST

    @pl.when(step == 0)
    def _():
        state_ref[...] = sin_ref[0].T
        ebuf_ref[0:hist, :] = cin_ref[0, 0:hist, :]
        ebuf_ref[hist + c:2 * hist + c, :] = cin_ref[0, hist:2 * hist, :]

    ebuf_ref[hist:hist + cv, 0:D_INNER] = xs_ref[0]
    ebuf_ref[hist:hist + cv, D_INNER:D_INNER + GROUP_W] = b_ref[0]
    ebuf_ref[hist:hist + cv, D_INNER + GROUP_W:CONV_DIM] = c_ref[0]
    if cv < c:
        ebuf_ref[hist + cv:hist + c, :] = jnp.zeros((c - cv, CONV_DIM), BF16)

    shifted = _dot(shift_ref[...], ebuf_ref[...])
    conv = cb_ref[...] + ebuf_ref[hist:hist + c, :].astype(F32) * cw_ref[CONV_W - 1:CONV_W, :]
    for k in range(CONV_W - 1):
        conv = conv + shifted[k * c:(k + 1) * c, :] * cw_ref[k:k + 1, :]
    conv = conv * _sigmoid(conv)
    if cv == c:
        ebuf_ref[0:hist, :] = ebuf_ref[c:c + hist, :]
        ebuf_ref[hist + c:2 * hist + c, :] = jnp.zeros((hist, CONV_DIM), BF16)

    xs = conv[:, 0:D_INNER]
    bm = conv[:, D_INNER:D_INNER + GROUP_W]
    cm = conv[:, D_INNER + GROUP_W:CONV_DIM]

    dt_in = dt_ref[0][:, 0:LANES] + dtb_ref[...]
    dt = jnp.maximum(dt_in, 0.0) + jnp.log1p(jnp.exp(-jnp.abs(dt_in)))
    if cv < c:
        dt = jnp.concatenate([dt, jnp.zeros((c - cv, LANES), F32)], axis=0)
    a = -jnp.exp(alog_ref[...])
    dta = dt * a

    row = lax.broadcasted_iota(jnp.int32, (c, c), 0)
    col = lax.broadcasted_iota(jnp.int32, (c, c), 1)
    tril = col <= row
    tri = jnp.where(tril, 1.0, 0.0).astype(BF16)
    cs = _dot_exact_lhs(tri, dta)
    cs_t = cs.T

    e = e_ref[...]
    dt_x = _dot_exact_rhs(dt, e)
    cs_x = _dot_exact_rhs(cs, e)
    cs_end_x = cs_x[c - 1:c, :]
    xdt = xs * dt_x
    decay_in = jnp.exp(cs_x)
    xdt_b = xdt.astype(BF16)
    xdt_end = (xdt * jnp.exp(cs_end_x - cs_x)).astype(BF16)
    state_decay = jnp.exp(cs_end_x)

    lane = lax.broadcasted_iota(jnp.int32, (c, LANES), 1)
    low_half = lane < SSM_HEAD_DIM

    y_groups = []
    for g in range(SSM_GROUPS):
        gs = slice(g * GROUP_W, (g + 1) * GROUP_W)
        ns = slice(g * D_STATE, (g + 1) * D_STATE)
        bg_t = bm[:, ns].T.astype(BF16)
        cg = cm[:, ns].astype(BF16)
        cb = _dot(cg, bg_t)
        sg = state_ref[:, gs]
        y_off = _dot(cg, sg.astype(BF16)) * decay_in[:, gs]
        pairs = []
        for j in range(GROUP_W // LANES):
            h0 = g * (GROUP_W // SSM_HEAD_DIM) + 2 * j
            xp = xdt_b[:, g * GROUP_W + j * LANES:g * GROUP_W + (j + 1) * LANES]
            ys = []
            for h in (h0, h0 + 1):
                seg = cs[:, h:h + 1] - cs_t[h:h + 1, :]
                m = (cb * jnp.where(tril, jnp.exp(seg), 0.0)).astype(BF16)
                ys.append(_dot(m, xp))
            pairs.append(jnp.where(low_half, ys[0], ys[1]))
        y_groups.append(jnp.concatenate(pairs, axis=1) + y_off)
        state_ref[:, gs] = sg * state_decay[:, gs] + _dot(bg_t, xdt_end[:, gs])

    y = jnp.concatenate(y_groups, axis=1) + dexp_ref[...] * xs
    if cv < c:
        y = y[0:cv]
    zv = z_ref[0].astype(F32)
    y = y * (zv * _sigmoid(zv))
    outs = []
    for g in range(SSM_GROUPS):
        gs = slice(g * GROUP_W, (g + 1) * GROUP_W)
        outs.append(_rms_scale(y[:, gs]))
    y_normed = (jnp.concatenate(outs, axis=1) * gnw_ref[...]).astype(BF16)
    y_ref[0] = res_ref[0] + _dot(y_normed, wout_ref[...])

    @pl.when(step == pl.num_programs(1) - 1)
    def _():
        sout_ref[0] = state_ref[...].T


def ssd(proj, dt_raw, conv_in, state_in, cw, cb, dtb, alog, dexp, gnw, w_out, res, *, c, cv):
    nb, length, _ = proj.shape
    d_model = res.shape[-1]
    steps = length // cv
    head_of_channel = np.arange(D_INNER) // SSM_HEAD_DIM
    expand = jnp.asarray(np.arange(LANES)[:, None] == head_of_channel[None, :], dtype=BF16)
    tap, pos = np.divmod(np.arange((CONV_W - 1) * c), c)
    src = pos + tap + CONV_HIST - (CONV_W - 1)
    cols = np.arange(c + 2 * CONV_HIST)[None, :]
    from_cache_low = (cols == (src + c + CONV_HIST)[:, None]) & ((pos + tap) < CONV_W - 1)[:, None]
    shift = jnp.asarray((cols == src[:, None]) | from_cache_low, dtype=BF16)
    const = lambda b, l: (0, 0)
    kern = functools.partial(_ssd_kernel, c=c, cv=cv)
    return pl.pallas_call(
        kern,
        grid=(nb, steps),
        in_specs=[
            pl.BlockSpec((1, cv, D_INNER), lambda b, l: (b, l, 0)),
            pl.BlockSpec((1, cv, D_INNER), lambda b, l: (b, l, 1)),
            pl.BlockSpec((1, cv, GROUP_W), lambda b, l: (b, l, 2 * D_INNER // GROUP_W)),
            pl.BlockSpec((1, cv, GROUP_W), lambda b, l: (b, l, 2 * D_INNER // GROUP_W + 1)),
            pl.BlockSpec((1, cv, DT_PAD), lambda b, l: (b, l, 0)),
            pl.BlockSpec((1, 2 * CONV_HIST, CONV_DIM), lambda b, l: (b, 0, 0)),
            pl.BlockSpec((1, D_INNER, D_STATE), lambda b, l: (b, 0, 0)),
            pl.BlockSpec((CONV_W, CONV_DIM), const),
            pl.BlockSpec((1, CONV_DIM), const),
            pl.BlockSpec((1, LANES), const),
            pl.BlockSpec((1, LANES), const),
            pl.BlockSpec((1, D_INNER), const),
            pl.BlockSpec((1, D_INNER), const),
            pl.BlockSpec((LANES, D_INNER), const),
            pl.BlockSpec(((CONV_W - 1) * c, c + 2 * CONV_HIST), const),
            pl.BlockSpec((D_INNER, d_model), const),
            pl.BlockSpec((1, cv, d_model), lambda b, l: (b, l, 0)),
        ],
        out_specs=[
            pl.BlockSpec((1, cv, d_model), lambda b, l: (b, l, 0)),
            pl.BlockSpec((1, D_INNER, D_STATE), lambda b, l: (b, 0, 0)),
        ],
        out_shape=[
            jax.ShapeDtypeStruct((nb, length, d_model), F32),
            jax.ShapeDtypeStruct((nb, D_INNER, D_STATE), F32),
        ],
        scratch_shapes=[pltpu.VMEM((D_STATE, D_INNER), F32),
                        pltpu.VMEM((c + 2 * CONV_HIST, CONV_DIM), BF16)],
        compiler_params=_params("arbitrary", "arbitrary"),
        name="ssd",
    )(proj, proj, proj, proj, dt_raw, conv_in, state_in, cw, cb, dtb, alog, dexp, gnw, expand, shift,
      w_out, res)


def _mm_res_kernel(a_ref, w_ref, r_ref, o_ref):
    o_ref[...] = r_ref[...] + _dot(a_ref[...], w_ref[...])


def matmul_residual(a, w, res, *, tm):
    t, k = a.shape
    n = w.shape[1]
    return pl.pallas_call(
        _mm_res_kernel,
        grid=(t // tm,),
        in_specs=[pl.BlockSpec((tm, k), lambda i: (i, 0)),
                  pl.BlockSpec((k, n), lambda i: (0, 0)),
                  pl.BlockSpec((tm, n), lambda i: (i, 0))],
        out_specs=pl.BlockSpec((tm, n), lambda i: (i, 0)),
        out_shape=jax.ShapeDtypeStruct((t, n), F32),
        compiler_params=_params("arbitrary"),
        name="matmul_residual",
    )(a, w, res)


def _ffn_kernel(x_ref, nw_ref, wg_ref, wu_ref, wd_ref, o_ref, h_ref, acc_ref):
    f = pl.program_id(1)

    @pl.when(f == 0)
    def _():
        x = x_ref[...]
        h_ref[...] = (_rms_scale(x) * nw_ref[...]).astype(BF16)
        acc_ref[...] = x

    h = h_ref[...]
    g = _dot(h, wg_ref[...])
    u = _dot(h, wu_ref[...])
    act = (g * _sigmoid(g) * u).astype(BF16)
    acc_ref[...] += _dot(act, wd_ref[...])

    @pl.when(f == pl.num_programs(1) - 1)
    def _():
        o_ref[...] = acc_ref[...]


def swiglu_ffn(x, nw, wg, wu, wd, *, tm, tf):
    t, d = x.shape
    ff = wg.shape[1]
    resident = dict(pipeline_mode=pl.Buffered(1)) if tf == ff else {}
    return pl.pallas_call(
        _ffn_kernel,
        grid=(t // tm, ff // tf),
        in_specs=[pl.BlockSpec((tm, d), lambda i, f: (i, 0)),
                  pl.BlockSpec((1, d), lambda i, f: (0, 0)),
                  pl.BlockSpec((d, tf), lambda i, f: (0, f), **resident),
                  pl.BlockSpec((d, tf), lambda i, f: (0, f), **resident),
                  pl.BlockSpec((tf, d), lambda i, f: (f, 0), **resident)],
        out_specs=pl.BlockSpec((tm, d), lambda i, f: (i, 0)),
        out_shape=jax.ShapeDtypeStruct((t, d), F32),
        scratch_shapes=[pltpu.VMEM((tm, d), BF16), pltpu.VMEM((tm, d), F32)],
        compiler_params=_params("arbitrary", "arbitrary"),
        name="swiglu_ffn",
    )(x, nw, wg, wu, wd)


def _qkv_kernel(x_ref, kvw_ref, aw_ref, wk_ref, wv_ref, wq_ref, cos_ref, sa_ref, sb_ref,
                k_ref, v_ref, *extra_refs, transposed):
    xn = _rms_scale(x_ref[...])
    hkv = (xn * kvw_ref[...]).astype(BF16)
    hq = (xn * aw_ref[...]).astype(BF16)
    cos, sa, sb = cos_ref[...], sa_ref[...], sb_ref[...]

    def rope(t):
        blocks = []
        for j in range(t.shape[1] // LANES):
            tb = t[:, j * LANES:(j + 1) * LANES]
            blocks.append(tb * cos + pltpu.roll(tb, LANES - ROT_DIM // 2, 1) * sa
                          + pltpu.roll(tb, ROT_DIM // 2, 1) * sb)
        return jnp.concatenate(blocks, axis=1)

    k = rope(_dot(hkv, wk_ref[...]))
    v = _dot(hkv, wv_ref[...])
    q = rope(_dot(hq, wq_ref[...]))
    tm = k.shape[0]
    for h in range(N_HEADS):
        rows = pl.ds(h, tm, stride=N_HEADS)
        k_ref[rows, :] = k[:, h * LANES:(h + 1) * LANES]
        v_ref[rows, :] = v[:, h * LANES:(h + 1) * LANES]
    if transposed:
        kb_ref, vt_ref, qt_ref = extra_refs
        kb_ref[...] = k.astype(BF16)
        vt_ref[0] = v.T.astype(BF16)
        qt_ref[0] = (q * (DK ** -0.5 * LOG2E)).T.astype(BF16)
    else:
        (q_ref,) = extra_refs
        q_ref[...] = (q * (DK ** -0.5)).astype(BF16)


def _rope_tables(pos):
    half = ROT_DIM // 2
    inv_freq = ROPE_THETA ** (-jnp.arange(half, dtype=F32) / half)
    ang = pos.astype(F32)[:, None] * inv_freq[None, :]
    cos, sin = jnp.cos(ang), jnp.sin(ang)
    n = pos.shape[0]
    ones = jnp.ones((n, DK - ROT_DIM), F32)
    zeros_h = jnp.zeros((n, half), F32)
    zeros_r = jnp.zeros((n, DK - ROT_DIM), F32)
    cos_t = jnp.concatenate([cos, cos, ones], axis=1)
    sa_t = jnp.concatenate([-sin, zeros_h, zeros_r], axis=1)
    sb_t = jnp.concatenate([zeros_h, sin, zeros_r], axis=1)
    rep = lambda t: jnp.tile(t, (1, LANES // DK))
    return rep(cos_t), rep(sa_t), rep(sb_t)


def qkv_proj(x, kvw, aw, wk, wv, wq, tables, *, tm, streams, transposed):
    t, d = x.shape
    period = tables[0].shape[0] // tm
    row = lambda i: (i, 0)
    const = lambda i: (0, 0)
    tab = lambda i: (i % period, 0)
    heads_out = jax.ShapeDtypeStruct((t * N_HEADS, LANES), F32)
    heads_spec = pl.BlockSpec((tm * N_HEADS, LANES), row)
    out_specs = [heads_spec, heads_spec, pl.BlockSpec((tm, d), row)]
    out_shape = [heads_out, heads_out, jax.ShapeDtypeStruct((t, d), BF16)]
    if transposed:
        per = t // streams // tm
        tr_spec = pl.BlockSpec((1, d, tm), lambda i: (i // per, 0, i % per))
        tr_out = jax.ShapeDtypeStruct((streams, d, t // streams), BF16)
        out_specs += [tr_spec, tr_spec]
        out_shape += [tr_out, tr_out]
    return pl.pallas_call(
        functools.partial(_qkv_kernel, transposed=transposed),
        grid=(t // tm,),
        in_specs=[pl.BlockSpec((tm, d), row),
                  pl.BlockSpec((1, d), const), pl.BlockSpec((1, d), const),
                  pl.BlockSpec((d, d), const), pl.BlockSpec((d, d), const), pl.BlockSpec((d, d), const),
                  pl.BlockSpec((tm, LANES), tab), pl.BlockSpec((tm, LANES), tab),
                  pl.BlockSpec((tm, LANES), tab)],
        out_specs=out_specs,
        out_shape=out_shape,
        compiler_params=_params("arbitrary"),
        name="qkv_proj",
    )(x, kvw, aw, wk, wv, wq, *tables)


def _lambda_value(lp_ref, lambda_init):
    lp = lp_ref[...]
    s1 = jnp.sum(lp[0:1] * lp[1:2], axis=-1, keepdims=True)
    s2 = jnp.sum(lp[2:3] * lp[3:4], axis=-1, keepdims=True)
    return jnp.exp(s1) - jnp.exp(s2) + lambda_init


def _diff_finish(o1, o2, lam, subw, lambda_init):
    o = o1 - lam * o2
    return _rms_scale(o) * subw * (1.0 - lambda_init)


def _attn_kernel(qt_ref, kt_ref, q_ref, k_ref, v_ref, lp_ref, subw_ref, wo_ref, res_ref, o_ref,
                 qs_ref, on_ref, *state_refs, tq, tk, lambda_init):
    p = pl.program_id(1)
    qi = qt_ref[p]
    ki = kt_ref[p]
    n_qc = 2 * tq // Q_COLS
    m_refs, acc_refs = state_refs[:n_qc], state_refs[n_qc:]

    @pl.when(ki == 0)
    def _():
        for m_ref, acc_ref in zip(m_refs, acc_refs):
            m_ref[...] = jnp.full(m_ref.shape, NEG_BIG, F32)
            acc_ref[...] = jnp.zeros(acc_ref.shape, F32)
        first_sub = lax.broadcasted_iota(jnp.int32, (LANES, tq), 0) < DK
        for h in range(N_HEADS):
            qh = q_ref[0, h * LANES:(h + 1) * LANES, :]
            zero = jnp.zeros_like(qh)
            qs_ref[h, :, 0:tq] = jnp.where(first_sub, qh, zero)
            qs_ref[h, :, tq:2 * tq] = jnp.where(first_sub, zero, qh)

    ones = jnp.ones((ONES_ROWS, tk), BF16)

    def sweep(masked):
        if masked:
            k_chunk = (ki * tk + lax.broadcasted_iota(jnp.int32, (tk, Q_COLS), 0)) >> CHUNK_SHIFT
            q_lane = lax.broadcasted_iota(jnp.int32, (tk, Q_COLS), 1)

        def scores(h, c):
            kh = k_ref[0, :, h * LANES:(h + 1) * LANES]
            s = _dot(kh, qs_ref[h, :, c * Q_COLS:(c + 1) * Q_COLS])
            if masked:
                q_chunk = (qi * tq + (c * Q_COLS) % tq + q_lane) >> CHUNK_SHIFT
                s = jnp.where(k_chunk <= q_chunk, s, NEG_BIG)
            return s

        groups = [(h, c) for h in range(N_HEADS) for c in range(n_qc)]
        pending = [scores(*g) for g in groups[:SCORE_LOOKAHEAD]]
        for i, (h, c) in enumerate(groups):
            s = pending.pop(0)
            if i + SCORE_LOOKAHEAD < len(groups):
                pending.append(scores(*groups[i + SCORE_LOOKAHEAD]))
            vt = jnp.concatenate([v_ref[0, h * LANES:(h + 1) * LANES, :], ones], axis=0)
            m_prev = m_refs[c][h:h + 1, :]
            m_new = jnp.maximum(m_prev, jnp.max(s, axis=0, keepdims=True))
            alpha = jnp.exp2(m_prev - m_new)
            pr = jnp.exp2(s - m_new).astype(BF16)
            acc_refs[c][h] = acc_refs[c][h] * alpha + _dot(vt, pr)
            m_refs[c][h:h + 1, :] = m_new

    @pl.when(ki < qi)
    def _():
        sweep(False)

    @pl.when(ki == qi)
    def _():
        sweep(True)
        lam = _lambda_value(lp_ref, lambda_init)
        subw = subw_ref[...]

        for h in range(N_HEADS):
            hs = slice(h * LANES, (h + 1) * LANES)
            for r in range(tq // Q_COLS):
                a1 = acc_refs[r][h]
                a2 = acc_refs[r + tq // Q_COLS][h]
                o_t = a1[0:DV] / a1[DV:DV + 1] - lam * (a2[0:DV] / a2[DV:DV + 1])
                res = _rms_scale(o_t.T) * subw * (1.0 - lambda_init)
                on_ref[r * Q_COLS:(r + 1) * Q_COLS, hs] = res.astype(BF16)
        o_ref[0] = res_ref[0] + _dot(on_ref[...], wo_ref[...])


def diff_attention(q_t, k, v_t, lam_params, subw, w_o, res, *, tq, lambda_init):
    nb, d, length = q_t.shape
    tk = tq
    nq = length // tq
    qt = np.concatenate([np.full(i + 1, i) for i in range(nq)]).astype(np.int32)
    kt = np.concatenate([np.arange(i + 1) for i in range(nq)]).astype(np.int32)
    kern = functools.partial(_attn_kernel, tq=tq, tk=tk, lambda_init=lambda_init)
    n_qc = 2 * tq // Q_COLS
    grid_spec = pltpu.PrefetchScalarGridSpec(
        num_scalar_prefetch=2,
        grid=(nb, len(qt)),
        in_specs=[pl.BlockSpec((1, d, tq), lambda b, p, qt_r, kt_r: (b, 0, qt_r[p])),
                  pl.BlockSpec((1, tk, d), lambda b, p, qt_r, kt_r: (b, kt_r[p], 0)),
                  pl.BlockSpec((1, d, tk), lambda b, p, qt_r, kt_r: (b, 0, kt_r[p])),
                  pl.BlockSpec((4, DK), lambda b, p, qt_r, kt_r: (0, 0)),
                  pl.BlockSpec((1, DV), lambda b, p, qt_r, kt_r: (0, 0)),
                  pl.BlockSpec((d, d), lambda b, p, qt_r, kt_r: (0, 0)),
                  pl.BlockSpec((1, tq, d), lambda b, p, qt_r, kt_r: (b, qt_r[p], 0))],
        out_specs=pl.BlockSpec((1, tq, d), lambda b, p, qt_r, kt_r: (b, qt_r[p], 0)),
        scratch_shapes=([pltpu.VMEM((N_HEADS, LANES, 2 * tq), BF16), pltpu.VMEM((tq, d), BF16)]
                        + [pltpu.VMEM((N_HEADS, Q_COLS), F32)] * n_qc
                        + [pltpu.VMEM((N_HEADS, DV + ONES_ROWS, Q_COLS), F32)] * n_qc),
    )
    return pl.pallas_call(
        kern,
        grid_spec=grid_spec,
        out_shape=jax.ShapeDtypeStruct((nb, length, d), F32),
        compiler_params=_params("arbitrary", "arbitrary"),
        name="diff_attention",
    )(jnp.asarray(qt), jnp.asarray(kt), q_t, k, v_t, lam_params, subw, w_o, res)


def _attn_cached_kernel(q_ref, ck_ref, cv_ref, kn_ref, vn_ref, lp_ref, subw_ref, o_ref,
                        *, nq, lambda_init):
    past = ck_ref.shape[1] // N_HEADS
    lane = lax.broadcasted_iota(jnp.int32, (nq, LANES), 1)
    nt = (((1,), (1,)), ((), ()))
    pad = jnp.zeros((LANES - nq, LANES), BF16)
    lam = _lambda_value(lp_ref, lambda_init)
    for h in range(N_HEADS):
        q = q_ref[0, :, h * LANES:(h + 1) * LANES]
        zero = jnp.zeros_like(q)
        qs = jnp.concatenate([jnp.where(lane < DK, q, zero), jnp.where(lane < DK, zero, q)], axis=0)
        old_rows = pl.ds(h, past, stride=N_HEADS)
        new_rows = pl.ds(h, nq, stride=N_HEADS)
        kc = ck_ref[0, old_rows, :].astype(BF16)
        vc = cv_ref[0, old_rows, :].astype(BF16)
        kn = jnp.concatenate([kn_ref[0, new_rows, :].astype(BF16), pad], axis=0)
        vn = jnp.concatenate([vn_ref[0, new_rows, :].astype(BF16), pad], axis=0)
        s_c = lax.dot_general(qs, kc, nt, preferred_element_type=F32)
        s_n = lax.dot_general(qs, kn, nt, preferred_element_type=F32)
        col = lax.broadcasted_iota(jnp.int32, s_n.shape, 1)
        s_n = jnp.where(col < nq, s_n, NEG_BIG)
        m = jnp.maximum(jnp.max(s_c, axis=-1, keepdims=True), jnp.max(s_n, axis=-1, keepdims=True))
        p_c = jnp.exp(s_c - m)
        p_n = jnp.exp(s_n - m)
        denom = jnp.sum(p_c, axis=-1, keepdims=True) + jnp.sum(p_n, axis=-1, keepdims=True)
        o = (_dot(p_c.astype(BF16), vc) + _dot(p_n.astype(BF16), vn)) / denom
        res = _diff_finish(o[0:nq], o[nq:2 * nq], lam, subw_ref[...], lambda_init)
        o_ref[0, :, h * LANES:(h + 1) * LANES] = res.astype(o_ref.dtype)


def diff_attention_cached(q, cache_k, cache_v, k_new, v_new, lam_params, subw, *, lambda_init):
    nb, nq, d = q.shape
    past = cache_k.shape[1]
    kern = functools.partial(_attn_cached_kernel, nq=nq, lambda_init=lambda_init)
    stream = lambda b: (b, 0, 0)
    const = lambda b: (0, 0)
    rows = lambda a: a.reshape(nb, a.shape[1] * N_HEADS, LANES)
    cache_k, cache_v, k_new, v_new = rows(cache_k), rows(cache_v), rows(k_new), rows(v_new)
    return pl.pallas_call(
        kern,
        grid=(nb,),
        in_specs=[pl.BlockSpec((1, nq, d), lambda b: (b, 0, 0)),
                  pl.BlockSpec((1, past * N_HEADS, LANES), stream),
                  pl.BlockSpec((1, past * N_HEADS, LANES), stream),
                  pl.BlockSpec((1, nq * N_HEADS, LANES), stream),
                  pl.BlockSpec((1, nq * N_HEADS, LANES), stream),
                  pl.BlockSpec((4, DK), const),
                  pl.BlockSpec((1, DV), const)],
        out_specs=pl.BlockSpec((1, nq, d), lambda b: (b, 0, 0)),
        out_shape=jax.ShapeDtypeStruct((nb, nq, d), BF16),
        compiler_params=_params("arbitrary"),
        name="diff_attention_cached",
    )(q, cache_k, cache_v, k_new, v_new, lam_params, subw)


def _route(logits):
    lane = lax.broadcasted_iota(jnp.int32, logits.shape, 1)
    valid = lane < N_EXPERTS
    lg = jnp.where(valid, logits, NEG_BIG)
    m1 = jnp.max(lg, axis=-1, keepdims=True)
    lane_f = lane.astype(F32)
    i1 = jnp.min(jnp.where(lg == m1, lane_f, float(LANES)), axis=-1, keepdims=True)
    lg2 = jnp.where(lane_f == i1, NEG_BIG, lg)
    m2 = jnp.max(lg2, axis=-1, keepdims=True)
    i2 = jnp.min(jnp.where(lg2 == m2, lane_f, float(LANES)), axis=-1, keepdims=True)
    e2 = jnp.exp(m2 - m1)
    g1 = 1.0 / (1.0 + e2)
    g2 = e2 / (1.0 + e2)
    return jnp.where(lane_f == i1, g1, 0.0) + jnp.where(lane_f == i2, g2, 0.0)


def _moe_route_kernel(x_ref, nw_ref, wr_ref, h_ref, comb_ref, rk_ref, rkt_ref, cnt_ref,
                      meta_ref, metat_ref, *, rt):
    t = x_ref.shape[0]
    hf = _rms_scale(x_ref[...]) * nw_ref[...]
    h_ref[...] = hf.astype(BF16)
    comb = _route(_dot_exact_rhs_general(hf, wr_ref[...]))
    comb_ref[...] = comb
    sel = jnp.where(comb > 0.0, 1.0, 0.0)
    row = lax.broadcasted_iota(jnp.int32, (rt, rt), 0)
    col = lax.broadcasted_iota(jnp.int32, (rt, rt), 1)
    strict = jnp.where(col < row, 1.0, 0.0).astype(BF16)
    carry = jnp.zeros((1, LANES), F32)
    ranks = []
    for s in range(t // rt):
        sel_s = sel[s * rt:(s + 1) * rt]
        rank_s = _dot(strict, sel_s.astype(BF16)) + carry
        ranks.append(jnp.where(sel_s > 0.0, rank_s, -1.0))
        carry = carry + jnp.sum(sel_s, axis=0, keepdims=True)
    rk = jnp.concatenate(ranks, axis=0) if len(ranks) > 1 else ranks[0]
    rk_ref[...] = rk.astype(jnp.int32)
    rkt_ref[0] = rk.T.astype(jnp.int32)
    cnt_ref[0] = jnp.broadcast_to(carry, (8, LANES)).astype(jnp.int32)

    padded = jnp.floor((carry + (MOE_ROW_ALIGN - 1)) * (1.0 / MOE_ROW_ALIGN)) * MOE_ROW_ALIGN
    lrow = lax.broadcasted_iota(jnp.int32, (LANES, LANES), 0)
    lcol = lax.broadcasted_iota(jnp.int32, (LANES, LANES), 1)
    before = jnp.where(lrow < lcol, 1.0, 0.0).astype(BF16)
    group_start = _dot(jnp.broadcast_to(padded, (8, LANES)).astype(BF16), before)[0:1]
    dest = jnp.where(sel > 0.0, rk + group_start, -1.0)
    lane_f = lax.broadcasted_iota(jnp.int32, (t, LANES), 1).astype(F32)
    la = jnp.min(jnp.where(sel > 0.0, lane_f, float(LANES)), axis=-1, keepdims=True)
    lb = jnp.max(jnp.where(sel > 0.0, lane_f, -1.0), axis=-1, keepdims=True)
    at = lambda v, l: jnp.sum(jnp.where(lane_f == l, v, 0.0), axis=-1, keepdims=True)
    two = lb > la
    d_a, g_a = at(dest, la), at(comb, la)
    d_b = jnp.where(two, at(dest, lb), -1.0)
    g_b = jnp.where(two, at(comb, lb), 0.0)
    meta = jnp.where(lane_f == 0.0, d_a, jnp.where(lane_f == 1.0, d_b,
                     jnp.where(lane_f == 2.0, g_a, jnp.where(lane_f == 3.0, g_b, 0.0))))
    meta_ref[...] = meta
    metat_ref[0] = meta.T[0:8]


def moe_route(x, nw, wr, *, tb, rt):
    t, d = x.shape
    nb = t // tb
    row = lambda b: (b, 0)
    const = lambda b: (0, 0)
    return pl.pallas_call(
        functools.partial(_moe_route_kernel, rt=rt),
        grid=(nb,),
        in_specs=[pl.BlockSpec((tb, d), row), pl.BlockSpec((1, d), const),
                  pl.BlockSpec((d, LANES), const)],
        out_specs=[pl.BlockSpec((tb, d), row), pl.BlockSpec((tb, LANES), row),
                   pl.BlockSpec((tb, LANES), row),
                   pl.BlockSpec((1, LANES, tb), lambda b: (b, 0, 0)),
                   pl.BlockSpec((1, 8, LANES), lambda b: (b, 0, 0)),
                   pl.BlockSpec((tb, LANES), row),
                   pl.BlockSpec((1, 8, tb), lambda b: (b, 0, 0))],
        out_shape=[jax.ShapeDtypeStruct((t, d), BF16),
                   jax.ShapeDtypeStruct((t, LANES), F32),
                   jax.ShapeDtypeStruct((t, LANES), jnp.int32),
                   jax.ShapeDtypeStruct((nb, LANES, tb), jnp.int32),
                   jax.ShapeDtypeStruct((nb, 8, LANES), jnp.int32),
                   jax.ShapeDtypeStruct((t, LANES), F32),
                   jax.ShapeDtypeStruct((nb, 8, tb), F32)],
        compiler_params=_params("arbitrary"),
        name="moe_route",
    )(x, nw, wr)


def _moe_expert_kernel(cnt_ref, x_ref, h_ref, comb_ref, rk_ref, rkt_ref, wg_ref, wu_ref, wd_ref,
                       fw_ref, o_ref, rkc_ref, xg_ref, gs_ref, yacc_ref, *, rt):
    b = pl.program_id(0)
    e = pl.program_id(1)
    f = pl.program_id(2)
    t = x_ref.shape[0]
    n_rows = cnt_ref[b * N_EXPERTS + e]
    half = rt // 2
    n_tiles = (n_rows + (half - 1)) // rt
    tail_row0 = pl.multiple_of(n_tiles * rt, rt)
    has_tail = n_rows > n_tiles * rt

    def over_tiles(fn):
        def body(j, carry):
            fn(pl.multiple_of(j * rt, rt), rt)
            return carry

        lax.fori_loop(0, n_tiles, body, 0)

        @pl.when(has_tail)
        def _():
            fn(tail_row0, half)

    @pl.when(jnp.logical_and(e == 0, f == 0))
    def _():
        o_ref[...] = x_ref[...]

    @pl.when(f == 0)
    def _():
        lane = lax.broadcasted_iota(jnp.int32, (t, LANES), 1)
        pick = lane == e
        rkc = jnp.sum(jnp.where(pick, rk_ref[...], 0).astype(F32), axis=-1, keepdims=True)
        rkc_ref[...] = jnp.broadcast_to(rkc, (t, LANES)).astype(jnp.int32)
        gate = jnp.sum(jnp.where(pick, comb_ref[...], 0.0), axis=-1, keepdims=True)
        g_hi, g_mid, g_lo = _split3(jnp.broadcast_to(gate, (t, LANES)))
        g3 = jnp.where(lane == 0, g_hi.astype(F32),
                       jnp.where(lane == 1, g_mid.astype(F32),
                                 jnp.where(lane == 2, g_lo.astype(F32), 0.0))).astype(BF16)
        rk_row = rkt_ref[0, pl.ds(e, 1), :]

        def gather(row0, size):
            tile = pl.ds(row0, size)
            rows = lax.broadcasted_iota(jnp.int32, (size, t), 0) + row0
            p = jnp.where(rk_row == rows, 1.0, 0.0).astype(BF16)
            xg_ref[tile, :] = _dot(p, h_ref[...]).astype(BF16)
            gate_rows = jnp.sum(_dot(p, g3), axis=-1, keepdims=True)
            gs_ref[tile, :] = jnp.broadcast_to(gate_rows, (size, LANES))
            yacc_ref[tile, :] = jnp.zeros((size, yacc_ref.shape[1]), F32)

        over_tiles(gather)

    def expert(row0, size):
        tile = pl.ds(row0, size)
        xg = xg_ref[tile, :]
        g = _dot(xg, wg_ref[...])
        u = _dot(xg, wu_ref[...])
        gates = _lane_tile(gs_ref[tile, :], g.shape[1] // LANES)
        act = (g * _sigmoid(g) * u * gates).astype(BF16)
        yacc_ref[tile, :] += _dot(act, wd_ref[...])

    over_tiles(expert)

    @pl.when(f == pl.num_programs(2) - 1)
    def _():
        def scatter(row0, size):
            cols = lax.broadcasted_iota(jnp.int32, (t, size), 1) + row0
            s = jnp.where(_lane_tile(rkc_ref[...], size // LANES) == cols, 1.0, 0.0).astype(BF16)
            o_ref[...] += _dot(s, yacc_ref[pl.ds(row0, size), :].astype(BF16))

        over_tiles(scatter)

    @pl.when(jnp.logical_and(e == pl.num_programs(1) - 1, f == pl.num_programs(2) - 1))
    def _():
        o_ref[...] = _rms_scale(o_ref[...]) * fw_ref[...]


def _dot_exact_rhs_general(x, w):
    xh, xm, xl = _split3(x)
    wh, wm, wl = _split3(w)
    return (_dot(xh, wh) + (_dot(xh, wm) + _dot(xm, wh))
            + (_dot(xh, wl) + _dot(xm, wm) + _dot(xl, wh)))


def moe_ffn_final(x, nw, wr, wg, wu, wd, fw, *, tb, rt, tf):
    t, d = x.shape
    ne, _, ff = wg.shape
    nb = t // tb
    h, comb, rk, rkt, cnt, _, _ = moe_route(x, nw, wr, tb=tb, rt=rt)
    counts = cnt[:, 0, :ne].reshape(nb * ne)
    once = pl.Buffered(1)
    blk = lambda b, e, f, c: (b, 0)
    const = lambda b, e, f, c: (0, 0)
    grid_spec = pltpu.PrefetchScalarGridSpec(
        num_scalar_prefetch=1,
        grid=(nb, ne, ff // tf),
        in_specs=[pl.BlockSpec((tb, d), blk, pipeline_mode=once),
                  pl.BlockSpec((tb, d), blk, pipeline_mode=once),
                  pl.BlockSpec((tb, LANES), blk, pipeline_mode=once),
                  pl.BlockSpec((tb, LANES), blk, pipeline_mode=once),
                  pl.BlockSpec((1, LANES, tb), lambda b, e, f, c: (b, 0, 0), pipeline_mode=once),
                  pl.BlockSpec((None, d, tf), lambda b, e, f, c: (e, 0, f)),
                  pl.BlockSpec((None, d, tf), lambda b, e, f, c: (e, 0, f)),
                  pl.BlockSpec((None, tf, d), lambda b, e, f, c: (e, f, 0)),
                  pl.BlockSpec((1, d), const)],
        out_specs=pl.BlockSpec((tb, d), blk),
        scratch_shapes=[pltpu.VMEM((tb, LANES), jnp.int32),
                        pltpu.VMEM((tb, d), BF16),
                        pltpu.VMEM((tb, LANES), F32),
                        pltpu.VMEM((tb, d), F32)],
    )
    return pl.pallas_call(
        functools.partial(_moe_expert_kernel, rt=rt),
        grid_spec=grid_spec,
        out_shape=jax.ShapeDtypeStruct((t, d), F32),
        compiler_params=_params("arbitrary", "arbitrary", "arbitrary"),
        name="moe_experts",
    )(counts, x, h, comb, rk, rkt, wg, wu, wd, fw)


MOE_ROW_ALIGN = 16
MOE_ROW_TILE = 128
MOE_GROUP_TILE = 1024


def _moe_plan(cnt, *, tb, region):
    nb, ne = cnt.shape
    padded = (cnt + (MOE_ROW_ALIGN - 1)) // MOE_ROW_ALIGN * MOE_ROW_ALIGN
    off = jnp.cumsum(padded, axis=0) - padded
    boff = jnp.cumsum(padded, axis=1) - padded
    total = jnp.sum(padded, axis=0)
    tiles = (total + MOE_ROW_TILE + MOE_GROUP_TILE - 1) // MOE_GROUP_TILE
    tile_end = jnp.cumsum(tiles)
    max_tiles = (2 * nb * tb + nb * ne * MOE_ROW_ALIGN + ne * MOE_ROW_TILE) // MOE_GROUP_TILE + ne + 1
    idx = jnp.arange(max_tiles, dtype=jnp.int32)
    last = jnp.maximum(tile_end[-1] - 1, 0)
    active = idx < tile_end[-1]
    ii = jnp.minimum(idx, last)
    te = jnp.searchsorted(tile_end, ii, side="right").astype(jnp.int32)
    k = ii - (tile_end - tiles)[te]
    tr = te * (region // MOE_GROUP_TILE) + k
    valid = jnp.clip(total[te] - k * MOE_GROUP_TILE, 0, MOE_GROUP_TILE)
    tv = jnp.where(active, valid, -1)
    flat = lambda a: a.reshape(-1).astype(jnp.int32)
    return flat(off), flat(boff), te, tr.astype(jnp.int32), tv.astype(jnp.int32)


def _block_rows(tb):
    need = 2 * tb + N_EXPERTS * MOE_ROW_ALIGN
    return -(-need // MOE_ROW_TILE) * MOE_ROW_TILE


def _for_group_pieces(cnt_ref, off_ref, boff_ref, blk, e, tb, region, enabled, make_copy, action):
    units = (cnt_ref[blk * N_EXPERTS + e] + (MOE_ROW_ALIGN - 1)) // MOE_ROW_ALIGN
    src0 = boff_ref[blk * N_EXPERTS + e]
    dst0 = e * region + off_ref[blk * N_EXPERTS + e]
    done = 0
    for k in reversed(range(_piece_bits(tb))):
        size = MOE_ROW_ALIGN << k
        has = (units >> k) & 1
        src = pl.ds(pl.multiple_of(src0 + done, MOE_ROW_ALIGN), size)
        dst = pl.ds(pl.multiple_of(dst0 + done, MOE_ROW_ALIGN), size)

        @pl.when(jnp.logical_and(enabled, has == 1))
        def _():
            action(make_copy(k, src, dst))

        done = done + has * size


def _piece_bits(tb):
    return (tb // MOE_ROW_ALIGN).bit_length()


def _moe_gather_kernel(cnt_ref, off_ref, boff_ref, h_ref, metat_ref, xs_ref, buf_ref, sem_ref, *, region):
    b = pl.program_id(0)
    tb = h_ref.shape[0]
    rows_b = buf_ref.shape[1]
    slot = lax.rem(b, 2)

    dest = metat_ref[0].astype(jnp.int32)
    rows = lax.broadcasted_iota(jnp.int32, (rows_b, tb), 0)
    onehot = jnp.where(rows == dest[0:1], 1.0, jnp.where(rows == dest[1:2], 1.0, 0.0)).astype(BF16)
    buf_ref[slot] = _dot(onehot, h_ref[...]).astype(BF16)

    def pieces(blk, sl, enabled, action):
        for e in range(N_EXPERTS):
            def make_copy(p, src_rows, dst_rows, e=e):
                return pltpu.make_async_copy(buf_ref.at[sl, src_rows], xs_ref.at[dst_rows],
                                             sem_ref.at[sl, e, p])
            _for_group_pieces(cnt_ref, off_ref, boff_ref, blk, e, tb, region, enabled, make_copy, action)

    pieces(jnp.maximum(b - 1, 0), 1 - slot, b > 0, lambda cp: cp.wait())
    pieces(b, slot, True, lambda cp: cp.start())
    pieces(b, slot, b == pl.num_programs(0) - 1, lambda cp: cp.wait())


def _moe_grouped_kernel(te_ref, tr_ref, tv_ref, x_ref, wg_ref, wu_ref, wd_ref, y_ref, xm_ref, act_ref):
    i = pl.program_id(0)
    f = pl.program_id(1)
    valid = tv_ref[i]
    last_f = pl.num_programs(1) - 1
    tf = wg_ref.shape[1]

    @pl.when(valid > 0)
    def _():
        @pl.when(f == 0)
        def _():
            row = lax.broadcasted_iota(jnp.int32, x_ref.shape, 0)
            xm_ref[...] = jnp.where(row < valid, x_ref[...].astype(F32), 0.0).astype(BF16)

        xm = xm_ref[...]
        g = _dot(xm, wg_ref[...])
        u = _dot(xm, wu_ref[...])
        act_ref[:, pl.ds(pl.multiple_of(f * tf, tf), tf)] = (g * _sigmoid(g) * u).astype(BF16)

        @pl.when(f == last_f)
        def _():
            y_ref[...] = _dot(act_ref[...], wd_ref[...]).astype(y_ref.dtype)

    @pl.when(jnp.logical_and(valid == 0, f == last_f))
    def _():
        y_ref[...] = jnp.zeros(y_ref.shape, y_ref.dtype)


def _moe_combine_kernel(cnt_ref, off_ref, boff_ref, x_ref, meta_ref, ys_ref, fw_ref, o_ref,
                        buf_ref, sem_ref, *, region):
    b = pl.program_id(0)
    last = pl.num_programs(0) - 1
    tb = x_ref.shape[0]
    rows_b = buf_ref.shape[1]
    slot = lax.rem(b, 2)

    def pieces(blk, sl, enabled, action):
        for e in range(N_EXPERTS):
            def make_copy(p, buf_rows, sorted_rows, e=e):
                return pltpu.make_async_copy(ys_ref.at[sorted_rows], buf_ref.at[sl, buf_rows],
                                             sem_ref.at[sl, e, p])
            _for_group_pieces(cnt_ref, off_ref, boff_ref, blk, e, tb, region, enabled, make_copy, action)

    @pl.when(b == 0)
    def _():
        buf_ref[...] = jnp.zeros(buf_ref.shape, BF16)

    pieces(b, slot, b == 0, lambda cp: cp.start())
    pieces(jnp.minimum(b + 1, last), 1 - slot, b < last, lambda cp: cp.start())
    meta = meta_ref[...]
    col = lax.broadcasted_iota(jnp.int32, (tb, rows_b), 1)
    d_a = meta[:, 0:1].astype(jnp.int32)
    d_b = meta[:, 1:2].astype(jnp.int32)
    g_a, g_b = meta[:, 2:3], meta[:, 3:4]
    hi = lambda g: g.astype(BF16).astype(F32)
    hit_a, hit_b = col == d_a, col == d_b
    s_hi = jnp.where(hit_a, hi(g_a), jnp.where(hit_b, hi(g_b), 0.0)).astype(BF16)
    s_lo = jnp.where(hit_a, g_a - hi(g_a), jnp.where(hit_b, g_b - hi(g_b), 0.0)).astype(BF16)
    pieces(b, slot, True, lambda cp: cp.wait())
    y = buf_ref[slot]
    out = x_ref[...] + _dot(jnp.concatenate([s_hi, s_lo], axis=1), jnp.concatenate([y, y], axis=0))
    o_ref[...] = _rms_scale(out) * fw_ref[...]


def moe_ffn_final_sorted(x, nw, wr, wg, wu, wd, fw, *, tb, tf):
    t, d = x.shape
    ne, _, ff = wg.shape
    nb = t // tb
    st = MOE_GROUP_TILE
    region = -(-(tb * nb + MOE_ROW_ALIGN * nb + tb) // st) * st
    h, _, _, _, cnt, meta, meta_t = moe_route(x, nw, wr, tb=tb, rt=tb)
    counts = cnt[:, 0, :ne]
    off, boff, te, tr, tv = _moe_plan(counts, tb=tb, region=region)
    counts = counts.reshape(nb * ne)
    n_tiles = te.shape[0]
    rows_b = _block_rows(tb)
    n_pieces = _piece_bits(tb)

    xs = pl.pallas_call(
        functools.partial(_moe_gather_kernel, region=region),
        grid_spec=pltpu.PrefetchScalarGridSpec(
            num_scalar_prefetch=3,
            grid=(nb,),
            in_specs=[pl.BlockSpec((tb, d), lambda b, c, o, bo: (b, 0)),
                      pl.BlockSpec((1, 8, tb), lambda b, c, o, bo: (b, 0, 0))],
            out_specs=pl.BlockSpec(memory_space=pl.ANY),
            scratch_shapes=[pltpu.VMEM((2, rows_b, d), BF16),
                            pltpu.SemaphoreType.DMA((2, ne, n_pieces))],
        ),
        out_shape=jax.ShapeDtypeStruct((ne * region, d), BF16),
        compiler_params=_params("arbitrary"),
        name="moe_gather",
    )(counts, off, boff, h, meta_t)

    last_f = ff // tf - 1
    used_f = lambda i, f, te_r, tr_r, tv_r: jnp.where(tv_r[i] > 0, f, last_f)
    ys = pl.pallas_call(
        _moe_grouped_kernel,
        grid_spec=pltpu.PrefetchScalarGridSpec(
            num_scalar_prefetch=3,
            grid=(n_tiles, ff // tf),
            in_specs=[pl.BlockSpec((st, d), lambda i, f, te_r, tr_r, tv_r: (tr_r[i], 0)),
                      pl.BlockSpec((None, d, tf), lambda i, f, te_r, tr_r, tv_r:
                                   (te_r[i], 0, used_f(i, f, te_r, tr_r, tv_r))),
                      pl.BlockSpec((None, d, tf), lambda i, f, te_r, tr_r, tv_r:
                                   (te_r[i], 0, used_f(i, f, te_r, tr_r, tv_r))),
                      pl.BlockSpec((None, ff, d), lambda i, f, te_r, tr_r, tv_r: (te_r[i], 0, 0))],
            out_specs=pl.BlockSpec((st, d), lambda i, f, te_r, tr_r, tv_r: (tr_r[i], 0)),
            scratch_shapes=[pltpu.VMEM((st, d), BF16), pltpu.VMEM((st, ff), BF16)],
        ),
        out_shape=jax.ShapeDtypeStruct((ne * region, d), BF16),
        compiler_params=_params("arbitrary", "arbitrary"),
        name="moe_grouped",
    )(te, tr, tv, xs, wg, wu, wd)

    return pl.pallas_call(
        functools.partial(_moe_combine_kernel, region=region),
        grid_spec=pltpu.PrefetchScalarGridSpec(
            num_scalar_prefetch=3,
            grid=(nb,),
            in_specs=[pl.BlockSpec((tb, d), lambda b, c, o, bo: (b, 0)),
                      pl.BlockSpec((tb, LANES), lambda b, c, o, bo: (b, 0)),
                      pl.BlockSpec(memory_space=pl.ANY),
                      pl.BlockSpec((1, d), lambda b, c, o, bo: (0, 0))],
            out_specs=pl.BlockSpec((tb, d), lambda b, c, o, bo: (b, 0)),
            scratch_shapes=[pltpu.VMEM((2, rows_b, d), BF16),
                            pltpu.SemaphoreType.DMA((2, ne, n_pieces))],
        ),
        out_shape=jax.ShapeDtypeStruct((t, d), F32),
        compiler_params=_params("arbitrary"),
        name="moe_combine",
    )(counts, off, boff, x, meta, ys, fw)


def _row(v):
    return v.reshape(1, -1).astype(F32)


def _pad_lanes(v, width):
    v = _row(v)
    return jnp.pad(v, ((0, 0), (0, width - v.shape[1])))


def _trunk(x, pos, conv_in, state_in, past_k, past_v, W, *, ssd_chunk, ssd_valid, tm, tq):
    nb, length, d = x.shape
    t = nb * length
    x0 = x.reshape(t, d)

    proj, dt_raw = rms_matmul(x0, W["mamba_norm_w"], W["w_in"], W["w_dt"], tm=min(t, 512),
                              tn=W["w_in"].shape[1])
    proj3 = proj.reshape(nb, length, D_INNER + CONV_DIM)
    new_conv = proj3[:, length - (CONV_W - 1):, D_INNER:D_INNER + CONV_DIM].astype(F32)
    conv_hi = conv_in.astype(BF16)
    conv_lo = (conv_in - conv_hi.astype(F32)).astype(BF16)
    front = ((0, 0), (CONV_HIST - (CONV_W - 1), 0), (0, 0))
    conv_hist = jnp.concatenate([jnp.pad(conv_hi, front), jnp.pad(conv_lo, front)], axis=1)
    x1, state_out = ssd(proj3, dt_raw.reshape(nb, length, DT_PAD), conv_hist,
                        state_in.reshape(nb, D_INNER, D_STATE),
                        W["conv_w"], W["conv_b"], W["dt_bias"], W["a_log"], W["d_exp"], W["gn_w"],
                        W["w_out"], x, c=ssd_chunk, cv=ssd_valid)
    new_ssm = state_out.reshape(nb, SSM_HEADS, SSM_HEAD_DIM, D_STATE)
    x1 = x1.reshape(t, d)

    x2 = swiglu_ffn(x1, W["ffn_norm_w"], W["ffn_wg"], W["ffn_wu"], W["ffn_wd"], tm=tm,
                    tf=W["ffn_wg"].shape[1])

    tables = _rope_tables(pos)
    proj_args = (x2, W["kv_norm_w"], W["attn_norm_w"], W["w_k"], W["w_v"], W["w_q"], tables)
    lambda_init = 0.8 - 0.6 * math.exp(-0.3 * 1)
    if past_k is None:
        k, v, kb, v_t, q_t = qkv_proj(*proj_args, tm=tm, streams=nb, transposed=True)
        x3 = diff_attention(q_t, kb.reshape(nb, length, d), v_t, W["lam"], W["subln_w"], W["w_o"],
                            x2.reshape(nb, length, d), tq=tq, lambda_init=lambda_init).reshape(t, d)
    else:
        k, v, q = qkv_proj(*proj_args, tm=tm, streams=nb, transposed=False)
        q3 = q.reshape(nb, length, d)
        o = diff_attention_cached(q3, past_k, past_v,
                                  k.reshape(nb, length, N_HEADS, LANES),
                                  v.reshape(nb, length, N_HEADS, LANES),
                                  W["lam"], W["subln_w"], lambda_init=lambda_init)
        x3 = matmul_residual(o.reshape(t, d), W["w_o"], x2, tm=tm)

    moe_args = (x3, W["moe_norm_w"], W["moe_wr"], W["moe_wg"], W["moe_wu"], W["moe_wd"], W["final_norm_w"])
    if t >= 4 * MOE_GROUP_TILE:
        yout = moe_ffn_final_sorted(*moe_args, tb=512, tf=512)
    else:
        yout = moe_ffn_final(*moe_args, tb=t, rt=256, tf=512)
    return (yout.reshape(nb, length, d), new_conv[None], new_ssm[None],
            k.reshape(nb, length, N_HEADS, 2 * DK), v.reshape(nb, length, N_HEADS, DV))


def kernel(x_prompt, x_sample, cache_conv, state_ssm, cache_k, cache_v, mamba_norm_w, mamba_w_in, mamba_conv_w, mamba_conv_b, mamba_dt_bias, mamba_a_log, mamba_d, mamba_gn_w, mamba_w_out, kv_norm_w, w_k, w_v, attn_norm_w, w_q, lambda_q1, lambda_k1, lambda_q2, lambda_k2, subln_w, w_o, ffn_norm_w, ffn_w_gate, ffn_w_up, ffn_w_down, moe_norm_w, moe_w_router, moe_w_gate, moe_w_up, moe_w_down, final_norm_w):
    w_in = mamba_w_in[0]
    n_dt = w_in.shape[1] - D_INNER - CONV_DIM
    W = dict(
        mamba_norm_w=_row(mamba_norm_w[0]),
        w_in=w_in[:, :D_INNER + CONV_DIM].astype(BF16),
        w_dt=jnp.pad(w_in[:, D_INNER + CONV_DIM:], ((0, 0), (0, DT_PAD - n_dt))).astype(BF16),
        conv_w=mamba_conv_w[0].astype(F32),
        conv_b=_row(mamba_conv_b[0]),
        dt_bias=_pad_lanes(mamba_dt_bias[0], LANES),
        a_log=_pad_lanes(mamba_a_log[0], LANES),
        d_exp=_row(jnp.repeat(mamba_d[0], SSM_HEAD_DIM)),
        gn_w=_row(mamba_gn_w[0]),
        w_out=mamba_w_out[0].astype(BF16),
        kv_norm_w=_row(kv_norm_w), w_k=w_k.astype(BF16), w_v=w_v.astype(BF16),
        attn_norm_w=_row(attn_norm_w[0]), w_q=w_q[0].astype(BF16),
        lam=jnp.stack([lambda_q1[0], lambda_k1[0], lambda_q2[0], lambda_k2[0]]).astype(F32),
        subln_w=_row(subln_w[0]), w_o=w_o[0].astype(BF16),
        ffn_norm_w=_row(ffn_norm_w[0]),
        ffn_wg=ffn_w_gate[0].astype(BF16), ffn_wu=ffn_w_up[0].astype(BF16),
        ffn_wd=ffn_w_down[0].astype(BF16),
        moe_norm_w=_row(moe_norm_w[0]),
        moe_wr=jnp.pad(moe_w_router[0].astype(F32), ((0, 0), (0, LANES - N_EXPERTS))),
        moe_wg=moe_w_gate[0].astype(BF16), moe_wu=moe_w_up[0].astype(BF16),
        moe_wd=moe_w_down[0].astype(BF16),
        final_norm_w=_row(final_norm_w),
    )
    bp, lp = x_prompt.shape[0], x_prompt.shape[1]
    bs, ls = x_sample.shape[0], x_sample.shape[1]
    past = cache_k.shape[1]

    conv0 = jnp.zeros((bp, CONV_W - 1, CONV_DIM), F32)
    ssm0 = jnp.zeros((bp, SSM_HEADS, SSM_HEAD_DIM, D_STATE), F32)
    y_p, conv_p, ssm_p, k_p, v_p = _trunk(
        x_prompt, jnp.arange(lp, dtype=jnp.int32), conv0, ssm0, None, None, W,
        ssd_chunk=128, ssd_valid=128, tm=512, tq=512)

    pos_s = jnp.tile(past + jnp.arange(ls, dtype=jnp.int32), bs)
    y_s, conv_s, ssm_s, k_s, v_s = _trunk(
        x_sample, pos_s, cache_conv[0], state_ssm[0], cache_k, cache_v, W,
        ssd_chunk=128, ssd_valid=ls, tm=bs * ls, tq=None)
    return (y_p, y_s, conv_p, ssm_p, k_p, v_p, conv_s, ssm_s, k_s, v_s)
```

```python
import functools
import math

import numpy as np
import jax
import jax.numpy as jnp
from jax import lax
from jax.experimental import pallas as pl
from jax.experimental.pallas import tpu as pltpu

F32 = jnp.float32
BF16 = jnp.bfloat16

EPS = 1e-5
D_MODEL = 1024
D_INNER = 2048
SSM_HEAD_DIM = 64
SSM_HEADS = 32
SSM_GROUPS = 4
D_STATE = 128
GROUP_W = D_INNER // SSM_GROUPS
CONV_W = 4
CONV_HIST = 16
SSD_STREAMS_PER_STEP = 1
CONV_DIM = D_INNER + 2 * SSM_GROUPS * D_STATE
DT_PAD = 256
PROJ_W = D_INNER + CONV_DIM + DT_PAD
N_HEADS = 8
DK = 64
DV = 128
ROT_DIM = 16
ROPE_THETA = 500000.0
CHUNK = 64
CHUNK_SHIFT = 6
assert 1 << CHUNK_SHIFT == CHUNK
N_EXPERTS = 8
LANES = 128
NEG_BIG = -1e30
LOG2E = math.log2(math.e)
Q_COLS = 256
ONES_ROWS = 16
SCORE_LOOKAHEAD = 3
VMEM_LIMIT = 56 * 1024 * 1024


def _params(*sem):
    return pltpu.CompilerParams(dimension_semantics=sem, vmem_limit_bytes=VMEM_LIMIT)


def _sigmoid(x):
    return 1.0 / (1.0 + jnp.exp(-x))


def _rms_scale(x):
    return x * lax.rsqrt(jnp.mean(x * x, axis=-1, keepdims=True) + EPS)


def _split3(x):
    hi = x.astype(BF16)
    r1 = x - hi.astype(F32)
    mid = r1.astype(BF16)
    lo = (r1 - mid.astype(F32)).astype(BF16)
    return hi, mid, lo


def _dot(a, b):
    return jnp.dot(a, b, preferred_element_type=F32)


def _lane_tile(x, n):
    return jnp.concatenate([x] * n, axis=1) if n > 1 else x


def _dot_exact_rhs(x, m_bf16):
    hi, mid, lo = _split3(x)
    return _dot(hi, m_bf16) + _dot(mid, m_bf16) + _dot(lo, m_bf16)


def _dot_exact_lhs(m_bf16, x):
    hi, mid, lo = _split3(x)
    return _dot(m_bf16, hi) + _dot(m_bf16, mid) + _dot(m_bf16, lo)


def _rms_matmul_kernel(x_ref, nw_ref, w_ref, wdt_ref, o_ref, dt_ref, h_ref):
    @pl.when(pl.program_id(1) == 0)
    def _():
        h = (_rms_scale(x_ref[...]) * nw_ref[...]).astype(BF16)
        h_ref[...] = h
        dt_ref[...] = _dot(h, wdt_ref[...])

    o_ref[...] = _dot(h_ref[...], w_ref[...]).astype(o_ref.dtype)


def rms_matmul(x, nw, w, w_dt, *, tm, tn):
    t, d = x.shape
    n = w.shape[1]
    n_dt = w_dt.shape[1]
    resident = dict(pipeline_mode=pl.Buffered(1)) if tn == n else {}
    return pl.pallas_call(
        _rms_matmul_kernel,
        grid=(t // tm, n // tn),
        in_specs=[pl.BlockSpec((tm, d), lambda i, j: (i, 0)),
                  pl.BlockSpec((1, d), lambda i, j: (0, 0)),
                  pl.BlockSpec((d, tn), lambda i, j: (0, j), **resident),
                  pl.BlockSpec((d, n_dt), lambda i, j: (0, 0))],
        out_specs=[pl.BlockSpec((tm, tn), lambda i, j: (i, j)),
                   pl.BlockSpec((tm, n_dt), lambda i, j: (i, 0))],
        out_shape=[jax.ShapeDtypeStruct((t, n), BF16),
                   jax.ShapeDtypeStruct((t, n_dt), F32)],
        scratch_shapes=[pltpu.VMEM((tm, d), BF16)],
        compiler_params=_params("arbitrary", "arbitrary"),
        name="rms_matmul",
    )(x, nw, w, w_dt)


def _ssd_kernel(z_ref, xs_ref, b_ref, c_ref, dt_ref, cin_ref, sin_ref, cw_ref, cb_ref,
                dtb_ref, alog_ref, dexp_ref, gnw_ref, e_ref, shift_ref, wout_ref, res_ref,
                y_ref, sout_ref, state_ref, ebuf_ref, *, c, cv):
    def stream(s):
        one = lambda r: r.at[:, s]
        return _ssd_stream(one(z_ref), one(xs_ref), one(b_ref), one(c_ref), one(dt_ref), one(cin_ref),
                           one(sin_ref), cw_ref, cb_ref, dtb_ref, alog_ref, dexp_ref, gnw_ref, e_ref,
                           shift_ref, wout_ref, one(res_ref), one(y_ref), one(sout_ref),
                           state_ref.at[s], ebuf_ref.at[s], c=c, cv=cv)

    running = [stream(s) for s in range(z_ref.shape[1])]
    while running:
        running = [g for g in running if next(g, "done") != "done"]


def _ssd_stream(z_ref, xs_ref, b_ref, c_ref, dt_ref, cin_ref, sin_ref, cw_ref, cb_ref,
                dtb_ref, alog_ref, dexp_ref, gnw_ref, e_ref, shift_ref, wout_ref, res_ref,
                y_ref, sout_ref, state_ref, ebuf_ref, *, c, cv):
    step = pl.program_id(1)
    hist = CONV_HIST

    @pl.when(step == 0)
    def _():
        state_ref[...] = sin_ref[0].T
        ebuf_ref[0:hist, :] = cin_ref[0, 0:hist, :]
        ebuf_ref[hist + c:2 * hist + c, :] = cin_ref[0, hist:2 * hist, :]

    ebuf_ref[hist:hist + cv, 0:D_INNER] = xs_ref[0]
    ebuf_ref[hist:hist + cv, D_INNER:D_INNER + GROUP_W] = b_ref[0]
    ebuf_ref[hist:hist + cv, D_INNER + GROUP_W:CONV_DIM] = c_ref[0]
    if cv < c:
        ebuf_ref[hist + cv:hist + c, :] = jnp.zeros((c - cv, CONV_DIM), BF16)

    shifted = _dot(shift_ref[...], ebuf_ref[...])
    conv = cb_ref[...] + ebuf_ref[hist:hist + c, :].astype(F32) * cw_ref[CONV_W - 1:CONV_W, :]
    for k in range(CONV_W - 1):
        conv = conv + shifted[k * c:(k + 1) * c, :] * cw_ref[k:k + 1, :]
    conv = conv * _sigmoid(conv)
    if cv == c:
        ebuf_ref[0:hist, :] = ebuf_ref[c:c + hist, :]
        ebuf_ref[hist + c:2 * hist + c, :] = jnp.zeros((hist, CONV_DIM), BF16)
    yield

    xs = conv[:, 0:D_INNER]
    bm = conv[:, D_INNER:D_INNER + GROUP_W]
    cm = conv[:, D_INNER + GROUP_W:CONV_DIM]

    dt_in = dt_ref[0][:, 0:LANES] + dtb_ref[...]
    dt = jnp.maximum(dt_in, 0.0) + jnp.log1p(jnp.exp(-jnp.abs(dt_in)))
    if cv < c:
        dt = jnp.concatenate([dt, jnp.zeros((c - cv, LANES), F32)], axis=0)
    a = -jnp.exp(alog_ref[...])
    dta = dt * a

    row = lax.broadcasted_iota(jnp.int32, (c, c), 0)
    col = lax.broadcasted_iota(jnp.int32, (c, c), 1)
    tril = col <= row
    tri = jnp.where(tril, 1.0, 0.0).astype(BF16)
    cs = _dot_exact_lhs(tri, dta)
    cs_t = cs.T
    yield

    e = e_ref[...]
    dt_x = _dot_exact_rhs(dt, e)
    cs_x = _dot_exact_rhs(cs, e)
    cs_end_x = cs_x[c - 1:c, :]
    xdt = xs * dt_x
    decay_in = jnp.exp(cs_x)
    xdt_b = xdt.astype(BF16)
    xdt_end = (xdt * jnp.exp(cs_end_x - cs_x)).astype(BF16)
    state_decay = jnp.exp(cs_end_x)
    yield

    lane = lax.broadcasted_iota(jnp.int32, (c, LANES), 1)
    low_half = lane < SSM_HEAD_DIM

    y_groups = []
    for g in range(SSM_GROUPS):
        gs = slice(g * GROUP_W, (g + 1) * GROUP_W)
        ns = slice(g * D_STATE, (g + 1) * D_STATE)
        bg_t = bm[:, ns].T.astype(BF16)
        cg = cm[:, ns].astype(BF16)
        cb = _dot(cg, bg_t)
        sg = state_ref[:, gs]
        y_off = _dot(cg, sg.astype(BF16)) * decay_in[:, gs]
        pairs = []
        for j in range(GROUP_W // LANES):
            h0 = g * (GROUP_W // SSM_HEAD_DIM) + 2 * j
            xp = xdt_b[:, g * GROUP_W + j * LANES:g * GROUP_W + (j + 1) * LANES]
            ys = []
            for h in (h0, h0 + 1):
                seg = cs[:, h:h + 1] - cs_t[h:h + 1, :]
                m = (cb * jnp.where(tril, jnp.exp(seg), 0.0)).astype(BF16)
                ys.append(_dot(m, xp))
            pairs.append(jnp.where(low_half, ys[0], ys[1]))
        y_groups.append(jnp.concatenate(pairs, axis=1) + y_off)
        state_ref[:, gs] = sg * state_decay[:, gs] + _dot(bg_t, xdt_end[:, gs])
        yield

    y = jnp.concatenate(y_groups, axis=1) + dexp_ref[...] * xs
    if cv < c:
        y = y[0:cv]
    zv = z_ref[0].astype(F32)
    y = y * (zv * _sigmoid(zv))
    outs = []
    for g in range(SSM_GROUPS):
        gs = slice(g * GROUP_W, (g + 1) * GROUP_W)
        outs.append(_rms_scale(y[:, gs]))
    y_normed = (jnp.concatenate(outs, axis=1) * gnw_ref[...]).astype(BF16)
    y_ref[0] = res_ref[0] + _dot(y_normed, wout_ref[...])

    @pl.when(step == pl.num_programs(1) - 1)
    def _():
        sout_ref[0] = state_ref[...].T


def ssd(proj, dt_raw, conv_in, state_in, cw, cb, dtb, alog, dexp, gnw, w_out, res, *, c, cv):
    nb, length, _ = proj.shape
    d_model = res.shape[-1]
    steps = length // cv
    head_of_channel = np.arange(D_INNER) // SSM_HEAD_DIM
    expand = jnp.asarray(np.arange(LANES)[:, None] == head_of_channel[None, :], dtype=BF16)
    tap, pos = np.divmod(np.arange((CONV_W - 1) * c), c)
    src = pos + tap + CONV_HIST - (CONV_W - 1)
    cols = np.arange(c + 2 * CONV_HIST)[None, :]
    from_cache_low = (cols == (src + c + CONV_HIST)[:, None]) & ((pos + tap) < CONV_W - 1)[:, None]
    shift = jnp.asarray((cols == src[:, None]) | from_cache_low, dtype=BF16)
    const = lambda b, l: (0, 0)
    kern = functools.partial(_ssd_kernel, c=c, cv=cv)
    sg = SSD_STREAMS_PER_STEP
    ng = nb // sg
    grouped = lambda a: a.reshape(ng, sg, *a.shape[1:])
    proj, dt_raw, conv_in, state_in, res = map(grouped, (proj, dt_raw, conv_in, state_in, res))
    outs = pl.pallas_call(
        kern,
        grid=(ng, steps),
        in_specs=[
            pl.BlockSpec((1, sg, cv, D_INNER), lambda b, l: (b, 0, l, 0)),
            pl.BlockSpec((1, sg, cv, D_INNER), lambda b, l: (b, 0, l, 1)),
            pl.BlockSpec((1, sg, cv, GROUP_W), lambda b, l: (b, 0, l, 2 * D_INNER // GROUP_W)),
            pl.BlockSpec((1, sg, cv, GROUP_W), lambda b, l: (b, 0, l, 2 * D_INNER // GROUP_W + 1)),
            pl.BlockSpec((1, sg, cv, DT_PAD), lambda b, l: (b, 0, l, 0)),
            pl.BlockSpec((1, sg, 2 * CONV_HIST, CONV_DIM), lambda b, l: (b, 0, 0, 0)),
            pl.BlockSpec((1, sg, D_INNER, D_STATE), lambda b, l: (b, 0, 0, 0)),
            pl.BlockSpec((CONV_W, CONV_DIM), const),
            pl.BlockSpec((1, CONV_DIM), const),
            pl.BlockSpec((1, LANES), const),
            pl.BlockSpec((1, LANES), const),
            pl.BlockSpec((1, D_INNER), const),
            pl.BlockSpec((1, D_INNER), const),
            pl.BlockSpec((LANES, D_INNER), const),
            pl.BlockSpec(((CONV_W - 1) * c, c + 2 * CONV_HIST), const),
            pl.BlockSpec((D_INNER, d_model), const),
            pl.BlockSpec((1, sg, cv, d_model), lambda b, l: (b, 0, l, 0)),
        ],
        out_specs=[
            pl.BlockSpec((1, sg, cv, d_model), lambda b, l: (b, 0, l, 0)),
            pl.BlockSpec((1, sg, D_INNER, D_STATE), lambda b, l: (b, 0, 0, 0)),
        ],
        out_shape=[
            jax.ShapeDtypeStruct((ng, sg, length, d_model), F32),
            jax.ShapeDtypeStruct((ng, sg, D_INNER, D_STATE), F32),
        ],
        scratch_shapes=[pltpu.VMEM((sg, D_STATE, D_INNER), F32),
                        pltpu.VMEM((sg, c + 2 * CONV_HIST, CONV_DIM), BF16)],
        compiler_params=_params("arbitrary", "arbitrary"),
        name="ssd",
    )(proj, proj, proj, proj, dt_raw, conv_in, state_in, cw, cb, dtb, alog, dexp, gnw, expand, shift,
      w_out, res)
    return outs[0].reshape(nb, length, d_model), outs[1].reshape(nb, D_INNER, D_STATE)


def _mm_res_kernel(a_ref, w_ref, r_ref, o_ref):
    o_ref[...] = r_ref[...] + _dot(a_ref[...], w_ref[...])


def matmul_residual(a, w, res, *, tm):
    t, k = a.shape
    n = w.shape[1]
    return pl.pallas_call(
        _mm_res_kernel,
        grid=(t // tm,),
        in_specs=[pl.BlockSpec((tm, k), lambda i: (i, 0)),
                  pl.BlockSpec((k, n), lambda i: (0, 0)),
                  pl.BlockSpec((tm, n), lambda i: (i, 0))],
        out_specs=pl.BlockSpec((tm, n), lambda i: (i, 0)),
        out_shape=jax.ShapeDtypeStruct((t, n), F32),
        compiler_params=_params("arbitrary"),
        name="matmul_residual",
    )(a, w, res)


def _ffn_kernel(x_ref, nw_ref, wg_ref, wu_ref, wd_ref, o_ref, h_ref, acc_ref):
    f = pl.program_id(1)

    @pl.when(f == 0)
    def _():
        x = x_ref[...]
        h_ref[...] = (_rms_scale(x) * nw_ref[...]).astype(BF16)
        acc_ref[...] = x

    h = h_ref[...]
    g = _dot(h, wg_ref[...])
    u = _dot(h, wu_ref[...])
    act = (g * _sigmoid(g) * u).astype(BF16)
    acc_ref[...] += _dot(act, wd_ref[...])

    @pl.when(f == pl.num_programs(1) - 1)
    def _():
        o_ref[...] = acc_ref[...]


def swiglu_ffn(x, nw, wg, wu, wd, *, tm, tf):
    t, d = x.shape
    ff = wg.shape[1]
    resident = dict(pipeline_mode=pl.Buffered(1)) if tf == ff else {}
    return pl.pallas_call(
        _ffn_kernel,
        grid=(t // tm, ff // tf),
        in_specs=[pl.BlockSpec((tm, d), lambda i, f: (i, 0)),
                  pl.BlockSpec((1, d), lambda i, f: (0, 0)),
                  pl.BlockSpec((d, tf), lambda i, f: (0, f), **resident),
                  pl.BlockSpec((d, tf), lambda i, f: (0, f), **resident),
                  pl.BlockSpec((tf, d), lambda i, f: (f, 0), **resident)],
        out_specs=pl.BlockSpec((tm, d), lambda i, f: (i, 0)),
        out_shape=jax.ShapeDtypeStruct((t, d), F32),
        scratch_shapes=[pltpu.VMEM((tm, d), BF16), pltpu.VMEM((tm, d), F32)],
        compiler_params=_params("arbitrary", "arbitrary"),
        name="swiglu_ffn",
    )(x, nw, wg, wu, wd)


def _qkv_kernel(x_ref, kvw_ref, aw_ref, wk_ref, wv_ref, wq_ref, cos_ref, sa_ref, sb_ref,
                k_ref, v_ref, *extra_refs, transposed):
    xn = _rms_scale(x_ref[...])
    hkv = (xn * kvw_ref[...]).astype(BF16)
    hq = (xn * aw_ref[...]).astype(BF16)
    cos, sa, sb = cos_ref[...], sa_ref[...], sb_ref[...]

    def rope(t):
        blocks = []
        for j in range(t.shape[1] // LANES):
            tb = t[:, j * LANES:(j + 1) * LANES]
            blocks.append(tb * cos + pltpu.roll(tb, LANES - ROT_DIM // 2, 1) * sa
                          + pltpu.roll(tb, ROT_DIM // 2, 1) * sb)
        return jnp.concatenate(blocks, axis=1)

    k = rope(_dot(hkv, wk_ref[...]))
    v = _dot(hkv, wv_ref[...])
    q = rope(_dot(hq, wq_ref[...]))
    tm = k.shape[0]
    for h in range(N_HEADS):
        rows = pl.ds(h, tm, stride=N_HEADS)
        k_ref[rows, :] = k[:, h * LANES:(h + 1) * LANES]
        v_ref[rows, :] = v[:, h * LANES:(h + 1) * LANES]
    if transposed:
        kb_ref, vt_ref, qt_ref = extra_refs
        kb_ref[...] = k.astype(BF16)
        vt_ref[0] = v.T.astype(BF16)
        qt_ref[0] = (q * (DK ** -0.5 * LOG2E)).T.astype(BF16)
    else:
        (q_ref,) = extra_refs
        q_ref[...] = (q * (DK ** -0.5)).astype(BF16)


def _rope_tables(pos):
    half = ROT_DIM // 2
    inv_freq = ROPE_THETA ** (-jnp.arange(half, dtype=F32) / half)
    ang = pos.astype(F32)[:, None] * inv_freq[None, :]
    cos, sin = jnp.cos(ang), jnp.sin(ang)
    n = pos.shape[0]
    ones = jnp.ones((n, DK - ROT_DIM), F32)
    zeros_h = jnp.zeros((n, half), F32)
    zeros_r = jnp.zeros((n, DK - ROT_DIM), F32)
    cos_t = jnp.concatenate([cos, cos, ones], axis=1)
    sa_t = jnp.concatenate([-sin, zeros_h, zeros_r], axis=1)
    sb_t = jnp.concatenate([zeros_h, sin, zeros_r], axis=1)
    rep = lambda t: jnp.tile(t, (1, LANES // DK))
    return rep(cos_t), rep(sa_t), rep(sb_t)


def qkv_proj(x, kvw, aw, wk, wv, wq, tables, *, tm, streams, transposed):
    t, d = x.shape
    period = tables[0].shape[0] // tm
    row = lambda i: (i, 0)
    const = lambda i: (0, 0)
    tab = lambda i: (i % period, 0)
    heads_out = jax.ShapeDtypeStruct((t * N_HEADS, LANES), F32)
    heads_spec = pl.BlockSpec((tm * N_HEADS, LANES), row)
    out_specs = [heads_spec, heads_spec, pl.BlockSpec((tm, d), row)]
    out_shape = [heads_out, heads_out, jax.ShapeDtypeStruct((t, d), BF16)]
    if transposed:
        per = t // streams // tm
        tr_spec = pl.BlockSpec((1, d, tm), lambda i: (i // per, 0, i % per))
        tr_out = jax.ShapeDtypeStruct((streams, d, t // streams), BF16)
        out_specs += [tr_spec, tr_spec]
        out_shape += [tr_out, tr_out]
    return pl.pallas_call(
        functools.partial(_qkv_kernel, transposed=transposed),
        grid=(t // tm,),
        in_specs=[pl.BlockSpec((tm, d), row),
                  pl.BlockSpec((1, d), const), pl.BlockSpec((1, d), const),
                  pl.BlockSpec((d, d), const), pl.BlockSpec((d, d), const), pl.BlockSpec((d, d), const),
                  pl.BlockSpec((tm, LANES), tab), pl.BlockSpec((tm, LANES), tab),
                  pl.BlockSpec((tm, LANES), tab)],
        out_specs=out_specs,
        out_shape=out_shape,
        compiler_params=_params("arbitrary"),
        name="qkv_proj",
    )(x, kvw, aw, wk, wv, wq, *tables)


def _lambda_value(lp_ref, lambda_init):
    lp = lp_ref[...]
    s1 = jnp.sum(lp[0:1] * lp[1:2], axis=-1, keepdims=True)
    s2 = jnp.sum(lp[2:3] * lp[3:4], axis=-1, keepdims=True)
    return jnp.exp(s1) - jnp.exp(s2) + lambda_init


def _diff_finish(o1, o2, lam, subw, lambda_init):
    o = o1 - lam * o2
    return _rms_scale(o) * subw * (1.0 - lambda_init)


def _attn_kernel(qt_ref, kt_ref, q_ref, k_ref, v_ref, lp_ref, subw_ref, wo_ref, res_ref, o_ref,
                 qs_ref, on_ref, *state_refs, tq, tk, lambda_init):
    p = pl.program_id(1)
    qi = qt_ref[p]
    ki = kt_ref[p]
    n_qc = 2 * tq // Q_COLS
    m_refs, acc_refs = state_refs[:n_qc], state_refs[n_qc:]

    @pl.when(ki == 0)
    def _():
        for m_ref, acc_ref in zip(m_refs, acc_refs):
            m_ref[...] = jnp.full(m_ref.shape, NEG_BIG, F32)
            acc_ref[...] = jnp.zeros(acc_ref.shape, F32)
        first_sub = lax.broadcasted_iota(jnp.int32, (LANES, tq), 0) < DK
        for h in range(N_HEADS):
            qh = q_ref[0, h * LANES:(h + 1) * LANES, :]
            zero = jnp.zeros_like(qh)
            qs_ref[h, :, 0:tq] = jnp.where(first_sub, qh, zero)
            qs_ref[h, :, tq:2 * tq] = jnp.where(first_sub, zero, qh)

    ones = jnp.ones((ONES_ROWS, tk), BF16)

    def sweep(masked):
        if masked:
            k_chunk = (ki * tk + lax.broadcasted_iota(jnp.int32, (tk, Q_COLS), 0)) >> CHUNK_SHIFT
            q_lane = lax.broadcasted_iota(jnp.int32, (tk, Q_COLS), 1)

        def scores(h, c):
            kh = k_ref[0, :, h * LANES:(h + 1) * LANES]
            s = _dot(kh, qs_ref[h, :, c * Q_COLS:(c + 1) * Q_COLS])
            if masked:
                q_chunk = (qi * tq + (c * Q_COLS) % tq + q_lane) >> CHUNK_SHIFT
                s = jnp.where(k_chunk <= q_chunk, s, NEG_BIG)
            return s

        groups = [(h, c) for h in range(N_HEADS) for c in range(n_qc)]
        pending = [scores(*g) for g in groups[:SCORE_LOOKAHEAD]]
        for i, (h, c) in enumerate(groups):
            s = pending.pop(0)
            if i + SCORE_LOOKAHEAD < len(groups):
                pending.append(scores(*groups[i + SCORE_LOOKAHEAD]))
            vt = jnp.concatenate([v_ref[0, h * LANES:(h + 1) * LANES, :], ones], axis=0)
            m_prev = m_refs[c][h:h + 1, :]
            m_new = jnp.maximum(m_prev, jnp.max(s, axis=0, keepdims=True))
            alpha = jnp.exp2(m_prev - m_new)
            pr = jnp.exp2(s - m_new).astype(BF16)
            acc_refs[c][h] = acc_refs[c][h] * alpha + _dot(vt, pr)
            m_refs[c][h:h + 1, :] = m_new

    @pl.when(ki < qi)
    def _():
        sweep(False)

    @pl.when(ki == qi)
    def _():
        sweep(True)
        lam = _lambda_value(lp_ref, lambda_init)
        subw = subw_ref[...]

        for h in range(N_HEADS):
            hs = slice(h * LANES, (h + 1) * LANES)
            for r in range(tq // Q_COLS):
                a1 = acc_refs[r][h]
                a2 = acc_refs[r + tq // Q_COLS][h]
                o_t = a1[0:DV] / a1[DV:DV + 1] - lam * (a2[0:DV] / a2[DV:DV + 1])
                res = _rms_scale(o_t.T) * subw * (1.0 - lambda_init)
                on_ref[r * Q_COLS:(r + 1) * Q_COLS, hs] = res.astype(BF16)
        o_ref[0] = res_ref[0] + _dot(on_ref[...], wo_ref[...])


def diff_attention(q_t, k, v_t, lam_params, subw, w_o, res, *, tq, lambda_init):
    nb, d, length = q_t.shape
    tk = tq
    nq = length // tq
    qt = np.concatenate([np.full(i + 1, i) for i in range(nq)]).astype(np.int32)
    kt = np.concatenate([np.arange(i + 1) for i in range(nq)]).astype(np.int32)
    kern = functools.partial(_attn_kernel, tq=tq, tk=tk, lambda_init=lambda_init)
    n_qc = 2 * tq // Q_COLS
    grid_spec = pltpu.PrefetchScalarGridSpec(
        num_scalar_prefetch=2,
        grid=(nb, len(qt)),
        in_specs=[pl.BlockSpec((1, d, tq), lambda b, p, qt_r, kt_r: (b, 0, qt_r[p])),
                  pl.BlockSpec((1, tk, d), lambda b, p, qt_r, kt_r: (b, kt_r[p], 0)),
                  pl.BlockSpec((1, d, tk), lambda b, p, qt_r, kt_r: (b, 0, kt_r[p])),
                  pl.BlockSpec((4, DK), lambda b, p, qt_r, kt_r: (0, 0)),
                  pl.BlockSpec((1, DV), lambda b, p, qt_r, kt_r: (0, 0)),
                  pl.BlockSpec((d, d), lambda b, p, qt_r, kt_r: (0, 0)),
                  pl.BlockSpec((1, tq, d), lambda b, p, qt_r, kt_r: (b, qt_r[p], 0))],
        out_specs=pl.BlockSpec((1, tq, d), lambda b, p, qt_r, kt_r: (b, qt_r[p], 0)),
        scratch_shapes=([pltpu.VMEM((N_HEADS, LANES, 2 * tq), BF16), pltpu.VMEM((tq, d), BF16)]
                        + [pltpu.VMEM((N_HEADS, Q_COLS), F32)] * n_qc
                        + [pltpu.VMEM((N_HEADS, DV + ONES_ROWS, Q_COLS), F32)] * n_qc),
    )
    return pl.pallas_call(
        kern,
        grid_spec=grid_spec,
        out_shape=jax.ShapeDtypeStruct((nb, length, d), F32),
        compiler_params=_params("arbitrary", "arbitrary"),
        name="diff_attention",
    )(jnp.asarray(qt), jnp.asarray(kt), q_t, k, v_t, lam_params, subw, w_o, res)


def _attn_cached_kernel(q_ref, ck_ref, cv_ref, kn_ref, vn_ref, lp_ref, subw_ref, o_ref,
                        *, nq, lambda_init):
    past = ck_ref.shape[1] // N_HEADS
    lane = lax.broadcasted_iota(jnp.int32, (nq, LANES), 1)
    nt = (((1,), (1,)), ((), ()))
    pad = jnp.zeros((LANES - nq, LANES), BF16)
    lam = _lambda_value(lp_ref, lambda_init)
    for h in range(N_HEADS):
        q = q_ref[0, :, h * LANES:(h + 1) * LANES]
        zero = jnp.zeros_like(q)
        qs = jnp.concatenate([jnp.where(lane < DK, q, zero), jnp.where(lane < DK, zero, q)], axis=0)
        old_rows = pl.ds(h, past, stride=N_HEADS)
        new_rows = pl.ds(h, nq, stride=N_HEADS)
        kc = ck_ref[0, old_rows, :].astype(BF16)
        vc = cv_ref[0, old_rows, :].astype(BF16)
        kn = jnp.concatenate([kn_ref[0, new_rows, :].astype(BF16), pad], axis=0)
        vn = jnp.concatenate([vn_ref[0, new_rows, :].astype(BF16), pad], axis=0)
        s_c = lax.dot_general(qs, kc, nt, preferred_element_type=F32)
        s_n = lax.dot_general(qs, kn, nt, preferred_element_type=F32)
        col = lax.broadcasted_iota(jnp.int32, s_n.shape, 1)
        s_n = jnp.where(col < nq, s_n, NEG_BIG)
        m = jnp.maximum(jnp.max(s_c, axis=-1, keepdims=True), jnp.max(s_n, axis=-1, keepdims=True))
        p_c = jnp.exp(s_c - m)
        p_n = jnp.exp(s_n - m)
        denom = jnp.sum(p_c, axis=-1, keepdims=True) + jnp.sum(p_n, axis=-1, keepdims=True)
        o = (_dot(p_c.astype(BF16), vc) + _dot(p_n.astype(BF16), vn)) / denom
        res = _diff_finish(o[0:nq], o[nq:2 * nq], lam, subw_ref[...], lambda_init)
        o_ref[0, :, h * LANES:(h + 1) * LANES] = res.astype(o_ref.dtype)


def diff_attention_cached(q, cache_k, cache_v, k_new, v_new, lam_params, subw, *, lambda_init):
    nb, nq, d = q.shape
    past = cache_k.shape[1]
    kern = functools.partial(_attn_cached_kernel, nq=nq, lambda_init=lambda_init)
    stream = lambda b: (b, 0, 0)
    const = lambda b: (0, 0)
    rows = lambda a: a.reshape(nb, a.shape[1] * N_HEADS, LANES)
    cache_k, cache_v, k_new, v_new = rows(cache_k), rows(cache_v), rows(k_new), rows(v_new)
    return pl.pallas_call(
        kern,
        grid=(nb,),
        in_specs=[pl.BlockSpec((1, nq, d), lambda b: (b, 0, 0)),
                  pl.BlockSpec((1, past * N_HEADS, LANES), stream),
                  pl.BlockSpec((1, past * N_HEADS, LANES), stream),
                  pl.BlockSpec((1, nq * N_HEADS, LANES), stream),
                  pl.BlockSpec((1, nq * N_HEADS, LANES), stream),
                  pl.BlockSpec((4, DK), const),
                  pl.BlockSpec((1, DV), const)],
        out_specs=pl.BlockSpec((1, nq, d), lambda b: (b, 0, 0)),
        out_shape=jax.ShapeDtypeStruct((nb, nq, d), BF16),
        compiler_params=_params("arbitrary"),
        name="diff_attention_cached",
    )(q, cache_k, cache_v, k_new, v_new, lam_params, subw)


def _route(logits):
    lane = lax.broadcasted_iota(jnp.int32, logits.shape, 1)
    valid = lane < N_EXPERTS
    lg = jnp.where(valid, logits, NEG_BIG)
    m1 = jnp.max(lg, axis=-1, keepdims=True)
    lane_f = lane.astype(F32)
    i1 = jnp.min(jnp.where(lg == m1, lane_f, float(LANES)), axis=-1, keepdims=True)
    lg2 = jnp.where(lane_f == i1, NEG_BIG, lg)
    m2 = jnp.max(lg2, axis=-1, keepdims=True)
    i2 = jnp.min(jnp.where(lg2 == m2, lane_f, float(LANES)), axis=-1, keepdims=True)
    e2 = jnp.exp(m2 - m1)
    g1 = 1.0 / (1.0 + e2)
    g2 = e2 / (1.0 + e2)
    return jnp.where(lane_f == i1, g1, 0.0) + jnp.where(lane_f == i2, g2, 0.0)


def _moe_route_kernel(x_ref, nw_ref, wr_ref, h_ref, comb_ref, rk_ref, rkt_ref, cnt_ref,
                      meta_ref, metat_ref, *, rt):
    t = x_ref.shape[0]
    hf = _rms_scale(x_ref[...]) * nw_ref[...]
    h_ref[...] = hf.astype(BF16)
    comb = _route(_dot_exact_rhs_general(hf, wr_ref[...]))
    comb_ref[...] = comb
    sel = jnp.where(comb > 0.0, 1.0, 0.0)
    row = lax.broadcasted_iota(jnp.int32, (rt, rt), 0)
    col = lax.broadcasted_iota(jnp.int32, (rt, rt), 1)
    strict = jnp.where(col < row, 1.0, 0.0).astype(BF16)
    carry = jnp.zeros((1, LANES), F32)
    ranks = []
    for s in range(t // rt):
        sel_s = sel[s * rt:(s + 1) * rt]
        rank_s = _dot(strict, sel_s.astype(BF16)) + carry
        ranks.append(jnp.where(sel_s > 0.0, rank_s, -1.0))
        carry = carry + jnp.sum(sel_s, axis=0, keepdims=True)
    rk = jnp.concatenate(ranks, axis=0) if len(ranks) > 1 else ranks[0]
    rk_ref[...] = rk.astype(jnp.int32)
    rkt_ref[0] = rk.T.astype(jnp.int32)
    cnt_ref[0] = jnp.broadcast_to(carry, (8, LANES)).astype(jnp.int32)

    padded = jnp.floor((carry + (MOE_ROW_ALIGN - 1)) * (1.0 / MOE_ROW_ALIGN)) * MOE_ROW_ALIGN
    lrow = lax.broadcasted_iota(jnp.int32, (LANES, LANES), 0)
    lcol = lax.broadcasted_iota(jnp.int32, (LANES, LANES), 1)
    before = jnp.where(lrow < lcol, 1.0, 0.0).astype(BF16)
    group_start = _dot(jnp.broadcast_to(padded, (8, LANES)).astype(BF16), before)[0:1]
    dest = jnp.where(sel > 0.0, rk + group_start, -1.0)
    lane_f = lax.broadcasted_iota(jnp.int32, (t, LANES), 1).astype(F32)
    la = jnp.min(jnp.where(sel > 0.0, lane_f, float(LANES)), axis=-1, keepdims=True)
    lb = jnp.max(jnp.where(sel > 0.0, lane_f, -1.0), axis=-1, keepdims=True)
    at = lambda v, l: jnp.sum(jnp.where(lane_f == l, v, 0.0), axis=-1, keepdims=True)
    two = lb > la
    d_a, g_a = at(dest, la), at(comb, la)
    d_b = jnp.where(two, at(dest, lb), -1.0)
    g_b = jnp.where(two, at(comb, lb), 0.0)
    meta = jnp.where(lane_f == 0.0, d_a, jnp.where(lane_f == 1.0, d_b,
                     jnp.where(lane_f == 2.0, g_a, jnp.where(lane_f == 3.0, g_b, 0.0))))
    meta_ref[...] = meta
    metat_ref[0] = meta.T[0:8]


def moe_route(x, nw, wr, *, tb, rt):
    t, d = x.shape
    nb = t // tb
    row = lambda b: (b, 0)
    const = lambda b: (0, 0)
    return pl.pallas_call(
        functools.partial(_moe_route_kernel, rt=rt),
        grid=(nb,),
        in_specs=[pl.BlockSpec((tb, d), row), pl.BlockSpec((1, d), const),
                  pl.BlockSpec((d, LANES), const)],
        out_specs=[pl.BlockSpec((tb, d), row), pl.BlockSpec((tb, LANES), row),
                   pl.BlockSpec((tb, LANES), row),
                   pl.BlockSpec((1, LANES, tb), lambda b: (b, 0, 0)),
                   pl.BlockSpec((1, 8, LANES), lambda b: (b, 0, 0)),
                   pl.BlockSpec((tb, LANES), row),
                   pl.BlockSpec((1, 8, tb), lambda b: (b, 0, 0))],
        out_shape=[jax.ShapeDtypeStruct((t, d), BF16),
                   jax.ShapeDtypeStruct((t, LANES), F32),
                   jax.ShapeDtypeStruct((t, LANES), jnp.int32),
                   jax.ShapeDtypeStruct((nb, LANES, tb), jnp.int32),
                   jax.ShapeDtypeStruct((nb, 8, LANES), jnp.int32),
                   jax.ShapeDtypeStruct((t, LANES), F32),
                   jax.ShapeDtypeStruct((nb, 8, tb), F32)],
        compiler_params=_params("arbitrary"),
        name="moe_route",
    )(x, nw, wr)


def _moe_expert_kernel(cnt_ref, x_ref, h_ref, comb_ref, rk_ref, rkt_ref, wg_ref, wu_ref, wd_ref,
                       fw_ref, o_ref, rkc_ref, xg_ref, gs_ref, yacc_ref, *, rt):
    b = pl.program_id(0)
    e = pl.program_id(1)
    f = pl.program_id(2)
    t = x_ref.shape[0]
    n_rows = cnt_ref[b * N_EXPERTS + e]
    half = rt // 2
    n_tiles = (n_rows + (half - 1)) // rt
    tail_row0 = pl.multiple_of(n_tiles * rt, rt)
    has_tail = n_rows > n_tiles * rt

    def over_tiles(fn):
        def body(j, carry):
            fn(pl.multiple_of(j * rt, rt), rt)
            return carry

        lax.fori_loop(0, n_tiles, body, 0)

        @pl.when(has_tail)
        def _():
            fn(tail_row0, half)

    @pl.when(jnp.logical_and(e == 0, f == 0))
    def _():
        o_ref[...] = x_ref[...]

    @pl.when(f == 0)
    def _():
        lane = lax.broadcasted_iota(jnp.int32, (t, LANES), 1)
        pick = lane == e
        rkc = jnp.sum(jnp.where(pick, rk_ref[...], 0).astype(F32), axis=-1, keepdims=True)
        rkc_ref[...] = jnp.broadcast_to(rkc, (t, LANES)).astype(jnp.int32)
        gate = jnp.sum(jnp.where(pick, comb_ref[...], 0.0), axis=-1, keepdims=True)
        g_hi, g_mid, g_lo = _split3(jnp.broadcast_to(gate, (t, LANES)))
        g3 = jnp.where(lane == 0, g_hi.astype(F32),
                       jnp.where(lane == 1, g_mid.astype(F32),
                                 jnp.where(lane == 2, g_lo.astype(F32), 0.0))).astype(BF16)
        rk_row = rkt_ref[0, pl.ds(e, 1), :]

        def gather(row0, size):
            tile = pl.ds(row0, size)
            rows = lax.broadcasted_iota(jnp.int32, (size, t), 0) + row0
            p = jnp.where(rk_row == rows, 1.0, 0.0).astype(BF16)
            xg_ref[tile, :] = _dot(p, h_ref[...]).astype(BF16)
            gate_rows = jnp.sum(_dot(p, g3), axis=-1, keepdims=True)
            gs_ref[tile, :] = jnp.broadcast_to(gate_rows, (size, LANES))
            yacc_ref[tile, :] = jnp.zeros((size, yacc_ref.shape[1]), F32)

        over_tiles(gather)

    def expert(row0, size):
        tile = pl.ds(row0, size)
        xg = xg_ref[tile, :]
        g = _dot(xg, wg_ref[...])
        u = _dot(xg, wu_ref[...])
        gates = _lane_tile(gs_ref[tile, :], g.shape[1] // LANES)
        act = (g * _sigmoid(g) * u * gates).astype(BF16)
        yacc_ref[tile, :] += _dot(act, wd_ref[...])

    over_tiles(expert)

    @pl.when(f == pl.num_programs(2) - 1)
    def _():
        def scatter(row0, size):
            cols = lax.broadcasted_iota(jnp.int32, (t, size), 1) + row0
            s = jnp.where(_lane_tile(rkc_ref[...], size // LANES) == cols, 1.0, 0.0).astype(BF16)
            o_ref[...] += _dot(s, yacc_ref[pl.ds(row0, size), :].astype(BF16))

        over_tiles(scatter)

    @pl.when(jnp.logical_and(e == pl.num_programs(1) - 1, f == pl.num_programs(2) - 1))
    def _():
        o_ref[...] = _rms_scale(o_ref[...]) * fw_ref[...]


def _dot_exact_rhs_general(x, w):
    xh, xm, xl = _split3(x)
    wh, wm, wl = _split3(w)
    return (_dot(xh, wh) + (_dot(xh, wm) + _dot(xm, wh))
            + (_dot(xh, wl) + _dot(xm, wm) + _dot(xl, wh)))


def moe_ffn_final(x, nw, wr, wg, wu, wd, fw, *, tb, rt, tf):
    t, d = x.shape
    ne, _, ff = wg.shape
    nb = t // tb
    h, comb, rk, rkt, cnt, _, _ = moe_route(x, nw, wr, tb=tb, rt=rt)
    counts = cnt[:, 0, :ne].reshape(nb * ne)
    once = pl.Buffered(1)
    blk = lambda b, e, f, c: (b, 0)
    const = lambda b, e, f, c: (0, 0)
    grid_spec = pltpu.PrefetchScalarGridSpec(
        num_scalar_prefetch=1,
        grid=(nb, ne, ff // tf),
        in_specs=[pl.BlockSpec((tb, d), blk, pipeline_mode=once),
                  pl.BlockSpec((tb, d), blk, pipeline_mode=once),
                  pl.BlockSpec((tb, LANES), blk, pipeline_mode=once),
                  pl.BlockSpec((tb, LANES), blk, pipeline_mode=once),
                  pl.BlockSpec((1, LANES, tb), lambda b, e, f, c: (b, 0, 0), pipeline_mode=once),
                  pl.BlockSpec((None, d, tf), lambda b, e, f, c: (e, 0, f)),
                  pl.BlockSpec((None, d, tf), lambda b, e, f, c: (e, 0, f)),
                  pl.BlockSpec((None, tf, d), lambda b, e, f, c: (e, f, 0)),
                  pl.BlockSpec((1, d), const)],
        out_specs=pl.BlockSpec((tb, d), blk),
        scratch_shapes=[pltpu.VMEM((tb, LANES), jnp.int32),
                        pltpu.VMEM((tb, d), BF16),
                        pltpu.VMEM((tb, LANES), F32),
                        pltpu.VMEM((tb, d), F32)],
    )
    return pl.pallas_call(
        functools.partial(_moe_expert_kernel, rt=rt),
        grid_spec=grid_spec,
        out_shape=jax.ShapeDtypeStruct((t, d), F32),
        compiler_params=_params("arbitrary", "arbitrary", "arbitrary"),
        name="moe_experts",
    )(counts, x, h, comb, rk, rkt, wg, wu, wd, fw)


MOE_ROW_ALIGN = 16
MOE_ROW_TILE = 128
MOE_GROUP_TILE = 1024


def _moe_plan(cnt, *, tb, region):
    nb, ne = cnt.shape
    padded = (cnt + (MOE_ROW_ALIGN - 1)) // MOE_ROW_ALIGN * MOE_ROW_ALIGN
    off = jnp.cumsum(padded, axis=0) - padded
    boff = jnp.cumsum(padded, axis=1) - padded
    total = jnp.sum(padded, axis=0)
    tiles = (total + MOE_ROW_TILE + MOE_GROUP_TILE - 1) // MOE_GROUP_TILE
    tile_end = jnp.cumsum(tiles)
    max_tiles = (2 * nb * tb + nb * ne * MOE_ROW_ALIGN + ne * MOE_ROW_TILE) // MOE_GROUP_TILE + ne + 1
    idx = jnp.arange(max_tiles, dtype=jnp.int32)
    last = jnp.maximum(tile_end[-1] - 1, 0)
    active = idx < tile_end[-1]
    ii = jnp.minimum(idx, last)
    te = jnp.sum(ii[:, None] >= tile_end[None, :], axis=1).astype(jnp.int32)
    k = ii - (tile_end - tiles)[te]
    tr = te * (region // MOE_GROUP_TILE) + k
    valid = jnp.clip(total[te] - k * MOE_GROUP_TILE, 0, MOE_GROUP_TILE)
    tv = jnp.where(active, valid, -1)
    flat = lambda a: a.reshape(-1).astype(jnp.int32)
    return flat(off), flat(boff), te, tr.astype(jnp.int32), tv.astype(jnp.int32)


def _block_rows(tb):
    need = 2 * tb + N_EXPERTS * MOE_ROW_ALIGN
    return -(-need // MOE_ROW_TILE) * MOE_ROW_TILE


def _for_group_pieces(cnt_ref, off_ref, boff_ref, blk, e, tb, region, enabled, make_copy, action):
    units = (cnt_ref[blk * N_EXPERTS + e] + (MOE_ROW_ALIGN - 1)) // MOE_ROW_ALIGN
    src0 = boff_ref[blk * N_EXPERTS + e]
    dst0 = e * region + off_ref[blk * N_EXPERTS + e]
    done = 0
    for k in reversed(range(_piece_bits(tb))):
        size = MOE_ROW_ALIGN << k
        has = (units >> k) & 1
        src = pl.ds(pl.multiple_of(src0 + done, MOE_ROW_ALIGN), size)
        dst = pl.ds(pl.multiple_of(dst0 + done, MOE_ROW_ALIGN), size)

        @pl.when(jnp.logical_and(enabled, has == 1))
        def _():
            action(make_copy(k, src, dst))

        done = done + has * size


def _piece_bits(tb):
    return (tb // MOE_ROW_ALIGN).bit_length()


def _moe_gather_kernel(cnt_ref, off_ref, boff_ref, h_ref, metat_ref, xs_ref, buf_ref, sem_ref, *, region):
    b = pl.program_id(0)
    tb = h_ref.shape[0]
    rows_b = buf_ref.shape[1]
    slot = lax.rem(b, 2)

    dest = metat_ref[0].astype(jnp.int32)
    rows = lax.broadcasted_iota(jnp.int32, (rows_b, tb), 0)
    onehot = jnp.where(rows == dest[0:1], 1.0, jnp.where(rows == dest[1:2], 1.0, 0.0)).astype(BF16)
    buf_ref[slot] = _dot(onehot, h_ref[...]).astype(BF16)

    def pieces(blk, sl, enabled, action):
        for e in range(N_EXPERTS):
            def make_copy(p, src_rows, dst_rows, e=e):
                return pltpu.make_async_copy(buf_ref.at[sl, src_rows], xs_ref.at[dst_rows],
                                             sem_ref.at[sl, e, p])
            _for_group_pieces(cnt_ref, off_ref, boff_ref, blk, e, tb, region, enabled, make_copy, action)

    pieces(jnp.maximum(b - 1, 0), 1 - slot, b > 0, lambda cp: cp.wait())
    pieces(b, slot, True, lambda cp: cp.start())
    pieces(b, slot, b == pl.num_programs(0) - 1, lambda cp: cp.wait())


def _moe_grouped_kernel(te_ref, tr_ref, tv_ref, x_ref, wg_ref, wu_ref, wd_ref, y_ref, xm_ref, act_ref):
    i = pl.program_id(0)
    f = pl.program_id(1)
    valid = tv_ref[i]
    last_f = pl.num_programs(1) - 1
    tf = wg_ref.shape[1]

    @pl.when(valid > 0)
    def _():
        @pl.when(f == 0)
        def _():
            row = lax.broadcasted_iota(jnp.int32, x_ref.shape, 0)
            xm_ref[...] = jnp.where(row < valid, x_ref[...].astype(F32), 0.0).astype(BF16)

        xm = xm_ref[...]
        g = _dot(xm, wg_ref[...])
        u = _dot(xm, wu_ref[...])
        act_ref[:, pl.ds(pl.multiple_of(f * tf, tf), tf)] = (g * _sigmoid(g) * u).astype(BF16)

        @pl.when(f == last_f)
        def _():
            y_ref[...] = _dot(act_ref[...], wd_ref[...]).astype(y_ref.dtype)

    @pl.when(jnp.logical_and(valid == 0, f == last_f))
    def _():
        y_ref[...] = jnp.zeros(y_ref.shape, y_ref.dtype)


def _moe_combine_kernel(cnt_ref, off_ref, boff_ref, x_ref, meta_ref, ys_ref, fw_ref, o_ref,
                        buf_ref, sem_ref, *, region):
    b = pl.program_id(0)
    last = pl.num_programs(0) - 1
    tb = x_ref.shape[0]
    rows_b = buf_ref.shape[1]
    slot = lax.rem(b, 2)

    def pieces(blk, sl, enabled, action):
        for e in range(N_EXPERTS):
            def make_copy(p, buf_rows, sorted_rows, e=e):
                return pltpu.make_async_copy(ys_ref.at[sorted_rows], buf_ref.at[sl, buf_rows],
                                             sem_ref.at[sl, e, p])
            _for_group_pieces(cnt_ref, off_ref, boff_ref, blk, e, tb, region, enabled, make_copy, action)

    @pl.when(b == 0)
    def _():
        buf_ref[...] = jnp.zeros(buf_ref.shape, BF16)

    pieces(b, slot, b == 0, lambda cp: cp.start())
    pieces(jnp.minimum(b + 1, last), 1 - slot, b < last, lambda cp: cp.start())
    meta = meta_ref[...]
    col = lax.broadcasted_iota(jnp.int32, (tb, rows_b), 1)
    d_a = meta[:, 0:1].astype(jnp.int32)
    d_b = meta[:, 1:2].astype(jnp.int32)
    g_a, g_b = meta[:, 2:3], meta[:, 3:4]
    hi = lambda g: g.astype(BF16).astype(F32)
    hit_a, hit_b = col == d_a, col == d_b
    s_hi = jnp.where(hit_a, hi(g_a), jnp.where(hit_b, hi(g_b), 0.0)).astype(BF16)
    s_lo = jnp.where(hit_a, g_a - hi(g_a), jnp.where(hit_b, g_b - hi(g_b), 0.0)).astype(BF16)
    pieces(b, slot, True, lambda cp: cp.wait())
    y = buf_ref[slot]
    out = x_ref[...] + _dot(jnp.concatenate([s_hi, s_lo], axis=1), jnp.concatenate([y, y], axis=0))
    o_ref[...] = _rms_scale(out) * fw_ref[...]


def moe_ffn_final_sorted(x, nw, wr, wg, wu, wd, fw, *, tb, tf):
    t, d = x.shape
    ne, _, ff = wg.shape
    nb = t // tb
    st = MOE_GROUP_TILE
    region = -(-(tb * nb + MOE_ROW_ALIGN * nb + tb) // st) * st
    h, _, _, _, cnt, meta, meta_t = moe_route(x, nw, wr, tb=tb, rt=tb)
    counts = cnt[:, 0, :ne]
    off, boff, te, tr, tv = _moe_plan(counts, tb=tb, region=region)
    counts = counts.reshape(nb * ne)
    n_tiles = te.shape[0]
    rows_b = _block_rows(tb)
    n_pieces = _piece_bits(tb)

    xs = pl.pallas_call(
        functools.partial(_moe_gather_kernel, region=region),
        grid_spec=pltpu.PrefetchScalarGridSpec(
            num_scalar_prefetch=3,
            grid=(nb,),
            in_specs=[pl.BlockSpec((tb, d), lambda b, c, o, bo: (b, 0)),
                      pl.BlockSpec((1, 8, tb), lambda b, c, o, bo: (b, 0, 0))],
            out_specs=pl.BlockSpec(memory_space=pl.ANY),
            scratch_shapes=[pltpu.VMEM((2, rows_b, d), BF16),
                            pltpu.SemaphoreType.DMA((2, ne, n_pieces))],
        ),
        out_shape=jax.ShapeDtypeStruct((ne * region, d), BF16),
        compiler_params=_params("arbitrary"),
        name="moe_gather",
    )(counts, off, boff, h, meta_t)

    last_f = ff // tf - 1
    used_f = lambda i, f, te_r, tr_r, tv_r: jnp.where(tv_r[i] > 0, f, last_f)
    ys = pl.pallas_call(
        _moe_grouped_kernel,
        grid_spec=pltpu.PrefetchScalarGridSpec(
            num_scalar_prefetch=3,
            grid=(n_tiles, ff // tf),
            in_specs=[pl.BlockSpec((st, d), lambda i, f, te_r, tr_r, tv_r: (tr_r[i], 0)),
                      pl.BlockSpec((None, d, tf), lambda i, f, te_r, tr_r, tv_r:
                                   (te_r[i], 0, used_f(i, f, te_r, tr_r, tv_r))),
                      pl.BlockSpec((None, d, tf), lambda i, f, te_r, tr_r, tv_r:
                                   (te_r[i], 0, used_f(i, f, te_r, tr_r, tv_r))),
                      pl.BlockSpec((None, ff, d), lambda i, f, te_r, tr_r, tv_r: (te_r[i], 0, 0))],
            out_specs=pl.BlockSpec((st, d), lambda i, f, te_r, tr_r, tv_r: (tr_r[i], 0)),
            scratch_shapes=[pltpu.VMEM((st, d), BF16), pltpu.VMEM((st, ff), BF16)],
        ),
        out_shape=jax.ShapeDtypeStruct((ne * region, d), BF16),
        compiler_params=_params("arbitrary", "arbitrary"),
        name="moe_grouped",
    )(te, tr, tv, xs, wg, wu, wd)

    return pl.pallas_call(
        functools.partial(_moe_combine_kernel, region=region),
        grid_spec=pltpu.PrefetchScalarGridSpec(
            num_scalar_prefetch=3,
            grid=(nb,),
            in_specs=[pl.BlockSpec((tb, d), lambda b, c, o, bo: (b, 0)),
                      pl.BlockSpec((tb, LANES), lambda b, c, o, bo: (b, 0)),
                      pl.BlockSpec(memory_space=pl.ANY),
                      pl.BlockSpec((1, d), lambda b, c, o, bo: (0, 0))],
            out_specs=pl.BlockSpec((tb, d), lambda b, c, o, bo: (b, 0)),
            scratch_shapes=[pltpu.VMEM((2, rows_b, d), BF16),
                            pltpu.SemaphoreType.DMA((2, ne, n_pieces))],
        ),
        out_shape=jax.ShapeDtypeStruct((t, d), F32),
        compiler_params=_params("arbitrary"),
        name="moe_combine",
    )(counts, off, boff, x, meta, ys, fw)


def _row(v):
    return v.reshape(1, -1).astype(F32)


def _pad_lanes(v, width):
    v = _row(v)
    return jnp.pad(v, ((0, 0), (0, width - v.shape[1])))


def _trunk(x, pos, conv_in, state_in, past_k, past_v, W, *, ssd_chunk, ssd_valid, tm, tq):
    nb, length, d = x.shape
    t = nb * length
    x0 = x.reshape(t, d)

    proj, dt_raw = rms_matmul(x0, W["mamba_norm_w"], W["w_in"], W["w_dt"], tm=min(t, 512),
                              tn=W["w_in"].shape[1])
    proj3 = proj.reshape(nb, length, D_INNER + CONV_DIM)
    new_conv = proj3[:, length - (CONV_W - 1):, D_INNER:D_INNER + CONV_DIM].astype(F32)
    conv_hi = conv_in.astype(BF16)
    conv_lo = (conv_in - conv_hi.astype(F32)).astype(BF16)
    front = ((0, 0), (CONV_HIST - (CONV_W - 1), 0), (0, 0))
    conv_hist = jnp.concatenate([jnp.pad(conv_hi, front), jnp.pad(conv_lo, front)], axis=1)
    x1, state_out = ssd(proj3, dt_raw.reshape(nb, length, DT_PAD), conv_hist,
                        state_in.reshape(nb, D_INNER, D_STATE),
                        W["conv_w"], W["conv_b"], W["dt_bias"], W["a_log"], W["d_exp"], W["gn_w"],
                        W["w_out"], x, c=ssd_chunk, cv=ssd_valid)
    new_ssm = state_out.reshape(nb, SSM_HEADS, SSM_HEAD_DIM, D_STATE)
    x1 = x1.reshape(t, d)

    x2 = swiglu_ffn(x1, W["ffn_norm_w"], W["ffn_wg"], W["ffn_wu"], W["ffn_wd"], tm=tm,
                    tf=W["ffn_wg"].shape[1])

    tables = _rope_tables(pos)
    proj_args = (x2, W["kv_norm_w"], W["attn_norm_w"], W["w_k"], W["w_v"], W["w_q"], tables)
    lambda_init = 0.8 - 0.6 * math.exp(-0.3 * 1)
    if past_k is None:
        k, v, kb, v_t, q_t = qkv_proj(*proj_args, tm=tm, streams=nb, transposed=True)
        x3 = diff_attention(q_t, kb.reshape(nb, length, d), v_t, W["lam"], W["subln_w"], W["w_o"],
                            x2.reshape(nb, length, d), tq=tq, lambda_init=lambda_init).reshape(t, d)
    else:
        k, v, q = qkv_proj(*proj_args, tm=tm, streams=nb, transposed=False)
        q3 = q.reshape(nb, length, d)
        o = diff_attention_cached(q3, past_k, past_v,
                                  k.reshape(nb, length, N_HEADS, LANES),
                                  v.reshape(nb, length, N_HEADS, LANES),
                                  W["lam"], W["subln_w"], lambda_init=lambda_init)
        x3 = matmul_residual(o.reshape(t, d), W["w_o"], x2, tm=tm)

    moe_args = (x3, W["moe_norm_w"], W["moe_wr"], W["moe_wg"], W["moe_wu"], W["moe_wd"], W["final_norm_w"])
    if t >= 4 * MOE_GROUP_TILE:
        yout = moe_ffn_final_sorted(*moe_args, tb=512, tf=512)
    else:
        yout = moe_ffn_final(*moe_args, tb=t, rt=256, tf=512)
    return (yout.reshape(nb, length, d), new_conv[None], new_ssm[None],
            k.reshape(nb, length, N_HEADS, 2 * DK), v.reshape(nb, length, N_HEADS, DV))


def kernel(x_prompt, x_sample, cache_conv, state_ssm, cache_k, cache_v, mamba_norm_w, mamba_w_in, mamba_conv_w, mamba_conv_b, mamba_dt_bias, mamba_a_log, mamba_d, mamba_gn_w, mamba_w_out, kv_norm_w, w_k, w_v, attn_norm_w, w_q, lambda_q1, lambda_k1, lambda_q2, lambda_k2, subln_w, w_o, ffn_norm_w, ffn_w_gate, ffn_w_up, ffn_w_down, moe_norm_w, moe_w_router, moe_w_gate, moe_w_up, moe_w_down, final_norm_w):
    w_in = mamba_w_in[0]
    n_dt = w_in.shape[1] - D_INNER - CONV_DIM
    W = dict(
        mamba_norm_w=_row(mamba_norm_w[0]),
        w_in=w_in[:, :D_INNER + CONV_DIM].astype(BF16),
        w_dt=jnp.pad(w_in[:, D_INNER + CONV_DIM:], ((0, 0), (0, DT_PAD - n_dt))).astype(BF16),
        conv_w=mamba_conv_w[0].astype(F32),
        conv_b=_row(mamba_conv_b[0]),
        dt_bias=_pad_lanes(mamba_dt_bias[0], LANES),
        a_log=_pad_lanes(mamba_a_log[0], LANES),
        d_exp=_row(jnp.repeat(mamba_d[0], SSM_HEAD_DIM)),
        gn_w=_row(mamba_gn_w[0]),
        w_out=mamba_w_out[0].astype(BF16),
        kv_norm_w=_row(kv_norm_w), w_k=w_k.astype(BF16), w_v=w_v.astype(BF16),
        attn_norm_w=_row(attn_norm_w[0]), w_q=w_q[0].astype(BF16),
        lam=jnp.stack([lambda_q1[0], lambda_k1[0], lambda_q2[0], lambda_k2[0]]).astype(F32),
        subln_w=_row(subln_w[0]), w_o=w_o[0].astype(BF16),
        ffn_norm_w=_row(ffn_norm_w[0]),
        ffn_wg=ffn_w_gate[0].astype(BF16), ffn_wu=ffn_w_up[0].astype(BF16),
        ffn_wd=ffn_w_down[0].astype(BF16),
        moe_norm_w=_row(moe_norm_w[0]),
        moe_wr=jnp.pad(moe_w_router[0].astype(F32), ((0, 0), (0, LANES - N_EXPERTS))),
        moe_wg=moe_w_gate[0].astype(BF16), moe_wu=moe_w_up[0].astype(BF16),
        moe_wd=moe_w_down[0].astype(BF16),
        final_norm_w=_row(final_norm_w),
    )
    bp, lp = x_prompt.shape[0], x_prompt.shape[1]
    bs, ls = x_sample.shape[0], x_sample.shape[1]
    past = cache_k.shape[1]

    conv0 = jnp.zeros((bp, CONV_W - 1, CONV_DIM), F32)
    ssm0 = jnp.zeros((bp, SSM_HEADS, SSM_HEAD_DIM, D_STATE), F32)
    y_p, conv_p, ssm_p, k_p, v_p = _trunk(
        x_prompt, jnp.arange(lp, dtype=jnp.int32), conv0, ssm0, None, None, W,
        ssd_chunk=128, ssd_valid=128, tm=512, tq=512)

    pos_s = jnp.tile(past + jnp.arange(ls, dtype=jnp.int32), bs)
    y_s, conv_s, ssm_s, k_s, v_s = _trunk(
        x_sample, pos_s, cache_conv[0], state_ssm[0], cache_k, cache_v, W,
        ssd_chunk=128, ssd_valid=ls, tm=bs * ls, tq=None)
    return (y_p, y_s, conv_p, ssm_p, k_p, v_p, conv_s, ssm_s, k_s, v_s)
```

```python
import functools
import math

import numpy as np
import jax
import jax.numpy as jnp
from jax import lax
from jax.experimental import pallas as pl
from jax.experimental.pallas import tpu as pltpu

F32 = jnp.float32
BF16 = jnp.bfloat16

EPS = 1e-5
D_MODEL = 1024
D_INNER = 2048
SSM_HEAD_DIM = 64
SSM_HEADS = 32
SSM_GROUPS = 4
D_STATE = 128
GROUP_W = D_INNER // SSM_GROUPS
CONV_W = 4
CONV_HIST = 16
SSD_STREAMS_PER_STEP = 1
CONV_DIM = D_INNER + 2 * SSM_GROUPS * D_STATE
DT_PAD = 256
N_HEADS = 8
DK = 64
DV = 128
ROT_DIM = 16
ROPE_THETA = 500000.0
CHUNK = 64
CHUNK_SHIFT = 6
assert 1 << CHUNK_SHIFT == CHUNK
N_EXPERTS = 8
LANES = 128
NEG_BIG = -1e30
LOG2E = math.log2(math.e)
Q_COLS = 256
ONES_ROWS = 16
SCORE_LOOKAHEAD = 3
VMEM_LIMIT = 56 * 1024 * 1024


def _params(*sem):
    return pltpu.CompilerParams(dimension_semantics=sem, vmem_limit_bytes=VMEM_LIMIT)


def _sigmoid(x):
    return 1.0 / (1.0 + jnp.exp(-x))


def _rms_scale(x):
    return x * lax.rsqrt(jnp.mean(x * x, axis=-1, keepdims=True) + EPS)


def _split3(x):
    hi = x.astype(BF16)
    r1 = x - hi.astype(F32)
    mid = r1.astype(BF16)
    lo = (r1 - mid.astype(F32)).astype(BF16)
    return hi, mid, lo


def _dot(a, b):
    return jnp.dot(a, b, preferred_element_type=F32)


def _lane_tile(x, n):
    return jnp.concatenate([x] * n, axis=1) if n > 1 else x


def _dot_exact_rhs(x, m_bf16):
    hi, mid, lo = _split3(x)
    return _dot(hi, m_bf16) + _dot(mid, m_bf16) + _dot(lo, m_bf16)


def _dot_exact_lhs(m_bf16, x):
    hi, mid, lo = _split3(x)
    return _dot(m_bf16, hi) + _dot(m_bf16, mid) + _dot(m_bf16, lo)


def _rms_matmul_kernel(x_ref, nw_ref, w_ref, wdt_ref, o_ref, dt_ref, h_ref):
    @pl.when(pl.program_id(1) == 0)
    def _():
        h = (_rms_scale(x_ref[...]) * nw_ref[...]).astype(BF16)
        h_ref[...] = h
        dt_ref[...] = _dot(h, wdt_ref[...])

    o_ref[...] = _dot(h_ref[...], w_ref[...]).astype(o_ref.dtype)


def rms_matmul(x, nw, w, w_dt, *, tm, tn):
    t, d = x.shape
    n = w.shape[1]
    n_dt = w_dt.shape[1]
    resident = dict(pipeline_mode=pl.Buffered(1)) if tn == n else {}
    return pl.pallas_call(
        _rms_matmul_kernel,
        grid=(t // tm, n // tn),
        in_specs=[pl.BlockSpec((tm, d), lambda i, j: (i, 0)),
                  pl.BlockSpec((1, d), lambda i, j: (0, 0)),
                  pl.BlockSpec((d, tn), lambda i, j: (0, j), **resident),
                  pl.BlockSpec((d, n_dt), lambda i, j: (0, 0))],
        out_specs=[pl.BlockSpec((tm, tn), lambda i, j: (i, j)),
                   pl.BlockSpec((tm, n_dt), lambda i, j: (i, 0))],
        out_shape=[jax.ShapeDtypeStruct((t, n), BF16),
                   jax.ShapeDtypeStruct((t, n_dt), F32)],
        scratch_shapes=[pltpu.VMEM((tm, d), BF16)],
        compiler_params=_params("arbitrary", "arbitrary"),
        name="rms_matmul",
    )(x, nw, w, w_dt)


def _ssd_kernel(z_ref, xs_ref, b_ref, c_ref, dt_ref, cin_ref, sin_ref, cw_ref, cb_ref,
                dtb_ref, alog_ref, dexp_ref, gnw_ref, e_ref, shift_ref, wout_ref, res_ref,
                y_ref, sout_ref, state_ref, ebuf_ref, *, c, cv):
    def stream(s):
        one = lambda r: r.at[:, s]
        return _ssd_stream(one(z_ref), one(xs_ref), one(b_ref), one(c_ref), one(dt_ref), one(cin_ref),
                           one(sin_ref), cw_ref, cb_ref, dtb_ref, alog_ref, dexp_ref, gnw_ref, e_ref,
                           shift_ref, wout_ref, one(res_ref), one(y_ref), one(sout_ref),
                           state_ref.at[s], ebuf_ref.at[s], c=c, cv=cv)

    running = [stream(s) for s in range(z_ref.shape[1])]
    while running:
        running = [g for g in running if next(g, "done") != "done"]


def _ssd_stream(z_ref, xs_ref, b_ref, c_ref, dt_ref, cin_ref, sin_ref, cw_ref, cb_ref,
                dtb_ref, alog_ref, dexp_ref, gnw_ref, e_ref, shift_ref, wout_ref, res_ref,
                y_ref, sout_ref, state_ref, ebuf_ref, *, c, cv):
    step = pl.program_id(1)
    hist = CONV_HIST

    @pl.when(step == 0)
    def _():
        state_ref[...] = sin_ref[0].T
        ebuf_ref[0:hist, :] = cin_ref[0, 0:hist, :]
        ebuf_ref[hist + c:2 * hist + c, :] = cin_ref[0, hist:2 * hist, :]

    ebuf_ref[hist:hist + cv, 0:D_INNER] = xs_ref[0]
    ebuf_ref[hist:hist + cv, D_INNER:D_INNER + GROUP_W] = b_ref[0]
    ebuf_ref[hist:hist + cv, D_INNER + GROUP_W:CONV_DIM] = c_ref[0]
    if cv < c:
        ebuf_ref[hist + cv:hist + c, :] = jnp.zeros((c - cv, CONV_DIM), BF16)

    shifted = _dot(shift_ref[...], ebuf_ref[...])
    conv = cb_ref[...] + ebuf_ref[hist:hist + c, :].astype(F32) * cw_ref[CONV_W - 1:CONV_W, :]
    for k in range(CONV_W - 1):
        conv = conv + shifted[k * c:(k + 1) * c, :] * cw_ref[k:k + 1, :]
    conv = conv * _sigmoid(conv)
    if cv == c:
        ebuf_ref[0:hist, :] = ebuf_ref[c:c + hist, :]
        ebuf_ref[hist + c:2 * hist + c, :] = jnp.zeros((hist, CONV_DIM), BF16)
    yield

    xs = conv[:, 0:D_INNER]
    bm = conv[:, D_INNER:D_INNER + GROUP_W]
    cm = conv[:, D_INNER + GROUP_W:CONV_DIM]

    dt_in = dt_ref[0][:, 0:LANES] + dtb_ref[...]
    dt = jnp.maximum(dt_in, 0.0) + jnp.log1p(jnp.exp(-jnp.abs(dt_in)))
    if cv < c:
        dt = jnp.concatenate([dt, jnp.zeros((c - cv, LANES), F32)], axis=0)
    a = -jnp.exp(alog_ref[...])
    dta = dt * a

    row = lax.broadcasted_iota(jnp.int32, (c, c), 0)
    col = lax.broadcasted_iota(jnp.int32, (c, c), 1)
    tril = col <= row
    tri = jnp.where(tril, 1.0, 0.0).astype(BF16)
    cs = _dot_exact_lhs(tri, dta)
    cs_t = cs.T
    yield

    e = e_ref[...]
    dt_x = _dot_exact_rhs(dt, e)
    cs_x = _dot_exact_rhs(cs, e)
    cs_end_x = cs_x[c - 1:c, :]
    xdt = xs * dt_x
    decay_in = jnp.exp(cs_x)
    xdt_b = xdt.astype(BF16)
    xdt_end = (xdt * jnp.exp(cs_end_x - cs_x)).astype(BF16)
    state_decay = jnp.exp(cs_end_x)
    yield

    lane = lax.broadcasted_iota(jnp.int32, (c, LANES), 1)
    low_half = lane < SSM_HEAD_DIM

    y_groups = []
    for g in range(SSM_GROUPS):
        gs = slice(g * GROUP_W, (g + 1) * GROUP_W)
        ns = slice(g * D_STATE, (g + 1) * D_STATE)
        bg_t = bm[:, ns].T.astype(BF16)
        cg = cm[:, ns].astype(BF16)
        cb = _dot(cg, bg_t)
        sg = state_ref[:, gs]
        y_off = _dot(cg, sg.astype(BF16)) * decay_in[:, gs]
        pairs = []
        for j in range(GROUP_W // LANES):
            h0 = g * (GROUP_W // SSM_HEAD_DIM) + 2 * j
            xp = xdt_b[:, g * GROUP_W + j * LANES:g * GROUP_W + (j + 1) * LANES]
            ys = []
            for h in (h0, h0 + 1):
                seg = cs[:, h:h + 1] - cs_t[h:h + 1, :]
                m = (cb * jnp.where(tril, jnp.exp(seg), 0.0)).astype(BF16)
                ys.append(_dot(m, xp))
            pairs.append(jnp.where(low_half, ys[0], ys[1]))
        y_groups.append(jnp.concatenate(pairs, axis=1) + y_off)
        state_ref[:, gs] = sg * state_decay[:, gs] + _dot(bg_t, xdt_end[:, gs])
        yield

    y = jnp.concatenate(y_groups, axis=1) + dexp_ref[...] * xs
    if cv < c:
        y = y[0:cv]
    zv = z_ref[0].astype(F32)
    y = y * (zv * _sigmoid(zv))
    outs = []
    for g in range(SSM_GROUPS):
        gs = slice(g * GROUP_W, (g + 1) * GROUP_W)
        outs.append(_rms_scale(y[:, gs]))
    y_normed = (jnp.concatenate(outs, axis=1) * gnw_ref[...]).astype(BF16)
    y_ref[0] = res_ref[0] + _dot(y_normed, wout_ref[...])

    @pl.when(step == pl.num_programs(1) - 1)
    def _():
        sout_ref[0] = state_ref[...].T


def ssd(proj, dt_raw, conv_in, state_in, cw, cb, dtb, alog, dexp, gnw, w_out, res, *, c, cv):
    nb, length, _ = proj.shape
    d_model = res.shape[-1]
    steps = length // cv
    head_of_channel = np.arange(D_INNER) // SSM_HEAD_DIM
    expand = jnp.asarray(np.arange(LANES)[:, None] == head_of_channel[None, :], dtype=BF16)
    tap, pos = np.divmod(np.arange((CONV_W - 1) * c), c)
    src = pos + tap + CONV_HIST - (CONV_W - 1)
    cols = np.arange(c + 2 * CONV_HIST)[None, :]
    from_cache_low = (cols == (src + c + CONV_HIST)[:, None]) & ((pos + tap) < CONV_W - 1)[:, None]
    shift = jnp.asarray((cols == src[:, None]) | from_cache_low, dtype=BF16)
    const = lambda b, l: (0, 0)
    kern = functools.partial(_ssd_kernel, c=c, cv=cv)
    sg = SSD_STREAMS_PER_STEP
    ng = nb // sg
    grouped = lambda a: a.reshape(ng, sg, *a.shape[1:])
    proj, dt_raw, conv_in, state_in, res = map(grouped, (proj, dt_raw, conv_in, state_in, res))
    outs = pl.pallas_call(
        kern,
        grid=(ng, steps),
        in_specs=[
            pl.BlockSpec((1, sg, cv, D_INNER), lambda b, l: (b, 0, l, 0)),
            pl.BlockSpec((1, sg, cv, D_INNER), lambda b, l: (b, 0, l, 1)),
            pl.BlockSpec((1, sg, cv, GROUP_W), lambda b, l: (b, 0, l, 2 * D_INNER // GROUP_W)),
            pl.BlockSpec((1, sg, cv, GROUP_W), lambda b, l: (b, 0, l, 2 * D_INNER // GROUP_W + 1)),
            pl.BlockSpec((1, sg, cv, DT_PAD), lambda b, l: (b, 0, l, 0)),
            pl.BlockSpec((1, sg, 2 * CONV_HIST, CONV_DIM), lambda b, l: (b, 0, 0, 0)),
            pl.BlockSpec((1, sg, D_INNER, D_STATE), lambda b, l: (b, 0, 0, 0)),
            pl.BlockSpec((CONV_W, CONV_DIM), const),
            pl.BlockSpec((1, CONV_DIM), const),
            pl.BlockSpec((1, LANES), const),
            pl.BlockSpec((1, LANES), const),
            pl.BlockSpec((1, D_INNER), const),
            pl.BlockSpec((1, D_INNER), const),
            pl.BlockSpec((LANES, D_INNER), const),
            pl.BlockSpec(((CONV_W - 1) * c, c + 2 * CONV_HIST), const),
            pl.BlockSpec((D_INNER, d_model), const),
            pl.BlockSpec((1, sg, cv, d_model), lambda b, l: (b, 0, l, 0)),
        ],
        out_specs=[
            pl.BlockSpec((1, sg, cv, d_model), lambda b, l: (b, 0, l, 0)),
            pl.BlockSpec((1, sg, D_INNER, D_STATE), lambda b, l: (b, 0, 0, 0)),
        ],
        out_shape=[
            jax.ShapeDtypeStruct((ng, sg, length, d_model), F32),
            jax.ShapeDtypeStruct((ng, sg, D_INNER, D_STATE), F32),
        ],
        scratch_shapes=[pltpu.VMEM((sg, D_STATE, D_INNER), F32),
                        pltpu.VMEM((sg, c + 2 * CONV_HIST, CONV_DIM), BF16)],
        compiler_params=_params("arbitrary", "arbitrary"),
        name="ssd",
    )(proj, proj, proj, proj, dt_raw, conv_in, state_in, cw, cb, dtb, alog, dexp, gnw, expand, shift,
      w_out, res)
    return outs[0].reshape(nb, length, d_model), outs[1].reshape(nb, D_INNER, D_STATE)


def _mm_res_kernel(a_ref, w_ref, r_ref, o_ref):
    o_ref[...] = r_ref[...] + _dot(a_ref[...], w_ref[...])


def matmul_residual(a, w, res, *, tm):
    t, k = a.shape
    n = w.shape[1]
    return pl.pallas_call(
        _mm_res_kernel,
        grid=(t // tm,),
        in_specs=[pl.BlockSpec((tm, k), lambda i: (i, 0)),
                  pl.BlockSpec((k, n), lambda i: (0, 0)),
                  pl.BlockSpec((tm, n), lambda i: (i, 0))],
        out_specs=pl.BlockSpec((tm, n), lambda i: (i, 0)),
        out_shape=jax.ShapeDtypeStruct((t, n), F32),
        compiler_params=_params("arbitrary"),
        name="matmul_residual",
    )(a, w, res)


def _ffn_kernel(x_ref, nw_ref, wg_ref, wu_ref, wd_ref, o_ref, h_ref, acc_ref):
    f = pl.program_id(1)

    @pl.when(f == 0)
    def _():
        x = x_ref[...]
        h_ref[...] = (_rms_scale(x) * nw_ref[...]).astype(BF16)
        acc_ref[...] = x

    h = h_ref[...]
    g = _dot(h, wg_ref[...])
    u = _dot(h, wu_ref[...])
    act = (g * _sigmoid(g) * u).astype(BF16)
    acc_ref[...] += _dot(act, wd_ref[...])

    @pl.when(f == pl.num_programs(1) - 1)
    def _():
        o_ref[...] = acc_ref[...]


def swiglu_ffn(x, nw, wg, wu, wd, *, tm, tf):
    t, d = x.shape
    ff = wg.shape[1]
    resident = dict(pipeline_mode=pl.Buffered(1)) if tf == ff else {}
    return pl.pallas_call(
        _ffn_kernel,
        grid=(t // tm, ff // tf),
        in_specs=[pl.BlockSpec((tm, d), lambda i, f: (i, 0)),
                  pl.BlockSpec((1, d), lambda i, f: (0, 0)),
                  pl.BlockSpec((d, tf), lambda i, f: (0, f), **resident),
                  pl.BlockSpec((d, tf), lambda i, f: (0, f), **resident),
                  pl.BlockSpec((tf, d), lambda i, f: (f, 0), **resident)],
        out_specs=pl.BlockSpec((tm, d), lambda i, f: (i, 0)),
        out_shape=jax.ShapeDtypeStruct((t, d), F32),
        scratch_shapes=[pltpu.VMEM((tm, d), BF16), pltpu.VMEM((tm, d), F32)],
        compiler_params=_params("arbitrary", "arbitrary"),
        name="swiglu_ffn",
    )(x, nw, wg, wu, wd)


def _qkv_kernel(x_ref, kvw_ref, aw_ref, wk_ref, wv_ref, wq_ref, cos_ref, sa_ref, sb_ref,
                k_ref, v_ref, *extra_refs, transposed):
    xn = _rms_scale(x_ref[...])
    hkv = (xn * kvw_ref[...]).astype(BF16)
    hq = (xn * aw_ref[...]).astype(BF16)
    cos, sa, sb = cos_ref[...], sa_ref[...], sb_ref[...]

    def rope(t):
        blocks = []
        for j in range(t.shape[1] // LANES):
            tb = t[:, j * LANES:(j + 1) * LANES]
            blocks.append(tb * cos + pltpu.roll(tb, LANES - ROT_DIM // 2, 1) * sa
                          + pltpu.roll(tb, ROT_DIM // 2, 1) * sb)
        return jnp.concatenate(blocks, axis=1)

    k = rope(_dot(hkv, wk_ref[...]))
    v = _dot(hkv, wv_ref[...])
    q = rope(_dot(hq, wq_ref[...]))
    tm = k.shape[0]
    for h in range(N_HEADS):
        rows = pl.ds(h, tm, stride=N_HEADS)
        k_ref[rows, :] = k[:, h * LANES:(h + 1) * LANES]
        v_ref[rows, :] = v[:, h * LANES:(h + 1) * LANES]
    if transposed:
        kb_ref, vt_ref, qt_ref = extra_refs
        kb_ref[...] = k.astype(BF16)
        vt_ref[0] = v.T.astype(BF16)
        qt_ref[0] = (q * (DK ** -0.5 * LOG2E)).T.astype(BF16)
    else:
        (q_ref,) = extra_refs
        q_ref[...] = (q * (DK ** -0.5)).astype(BF16)


def _rope_tables(pos):
    half = ROT_DIM // 2
    inv_freq = ROPE_THETA ** (-jnp.arange(half, dtype=F32) / half)
    ang = pos.astype(F32)[:, None] * inv_freq[None, :]
    cos, sin = jnp.cos(ang), jnp.sin(ang)
    n = pos.shape[0]
    ones = jnp.ones((n, DK - ROT_DIM), F32)
    zeros_h = jnp.zeros((n, half), F32)
    zeros_r = jnp.zeros((n, DK - ROT_DIM), F32)
    cos_t = jnp.concatenate([cos, cos, ones], axis=1)
    sa_t = jnp.concatenate([-sin, zeros_h, zeros_r], axis=1)
    sb_t = jnp.concatenate([zeros_h, sin, zeros_r], axis=1)
    rep = lambda t: jnp.tile(t, (1, LANES // DK))
    return rep(cos_t), rep(sa_t), rep(sb_t)


def qkv_proj(x, kvw, aw, wk, wv, wq, tables, *, tm, streams, transposed):
    t, d = x.shape
    period = tables[0].shape[0] // tm
    row = lambda i: (i, 0)
    const = lambda i: (0, 0)
    tab = lambda i: (i % period, 0)
    heads_out = jax.ShapeDtypeStruct((t * N_HEADS, LANES), F32)
    heads_spec = pl.BlockSpec((tm * N_HEADS, LANES), row)
    out_specs = [heads_spec, heads_spec, pl.BlockSpec((tm, d), row)]
    out_shape = [heads_out, heads_out, jax.ShapeDtypeStruct((t, d), BF16)]
    if transposed:
        per = t // streams // tm
        tr_spec = pl.BlockSpec((1, d, tm), lambda i: (i // per, 0, i % per))
        tr_out = jax.ShapeDtypeStruct((streams, d, t // streams), BF16)
        out_specs += [tr_spec, tr_spec]
        out_shape += [tr_out, tr_out]
    return pl.pallas_call(
        functools.partial(_qkv_kernel, transposed=transposed),
        grid=(t // tm,),
        in_specs=[pl.BlockSpec((tm, d), row),
                  pl.BlockSpec((1, d), const), pl.BlockSpec((1, d), const),
                  pl.BlockSpec((d, d), const), pl.BlockSpec((d, d), const), pl.BlockSpec((d, d), const),
                  pl.BlockSpec((tm, LANES), tab), pl.BlockSpec((tm, LANES), tab),
                  pl.BlockSpec((tm, LANES), tab)],
        out_specs=out_specs,
        out_shape=out_shape,
        compiler_params=_params("arbitrary"),
        name="qkv_proj",
    )(x, kvw, aw, wk, wv, wq, *tables)


def _lambda_value(lp_ref, lambda_init):
    lp = lp_ref[...]
    s1 = jnp.sum(lp[0:1] * lp[1:2], axis=-1, keepdims=True)
    s2 = jnp.sum(lp[2:3] * lp[3:4], axis=-1, keepdims=True)
    return jnp.exp(s1) - jnp.exp(s2) + lambda_init


def _diff_finish(o1, o2, lam, subw, lambda_init):
    o = o1 - lam * o2
    return _rms_scale(o) * subw * (1.0 - lambda_init)


def _attn_kernel(qt_ref, kt_ref, q_ref, k_ref, v_ref, lp_ref, subw_ref, wo_ref, res_ref, o_ref,
                 qs_ref, on_ref, *state_refs, tq, tk, lambda_init):
    p = pl.program_id(1)
    qi = qt_ref[p]
    ki = kt_ref[p]
    n_qc = 2 * tq // Q_COLS
    m_refs, acc_refs = state_refs[:n_qc], state_refs[n_qc:]

    @pl.when(ki == 0)
    def _():
        for m_ref, acc_ref in zip(m_refs, acc_refs):
            m_ref[...] = jnp.full(m_ref.shape, NEG_BIG, F32)
            acc_ref[...] = jnp.zeros(acc_ref.shape, F32)
        first_sub = lax.broadcasted_iota(jnp.int32, (LANES, tq), 0) < DK
        for h in range(N_HEADS):
            qh = q_ref[0, h * LANES:(h + 1) * LANES, :]
            zero = jnp.zeros_like(qh)
            qs_ref[h, :, 0:tq] = jnp.where(first_sub, qh, zero)
            qs_ref[h, :, tq:2 * tq] = jnp.where(first_sub, zero, qh)

    def sweep(masked):
        def keys_used(c):
            return min(tk, (c * Q_COLS) % tq + Q_COLS) if masked else tk

        def scores(h, c):
            nk = keys_used(c)
            kh = k_ref[0, 0:nk, h * LANES:(h + 1) * LANES]
            s = _dot(kh, qs_ref[h, :, c * Q_COLS:(c + 1) * Q_COLS])
            if masked:
                k_chunk = (ki * tk + lax.broadcasted_iota(jnp.int32, (nk, Q_COLS), 0)) >> CHUNK_SHIFT
                q_lane = lax.broadcasted_iota(jnp.int32, (nk, Q_COLS), 1)
                q_chunk = (qi * tq + (c * Q_COLS) % tq + q_lane) >> CHUNK_SHIFT
                s = jnp.where(k_chunk <= q_chunk, s, NEG_BIG)
            return s

        groups = [(h, c) for h in range(N_HEADS) for c in range(n_qc)]
        pending = [scores(*g) for g in groups[:SCORE_LOOKAHEAD]]
        for i, (h, c) in enumerate(groups):
            s = pending.pop(0)
            if i + SCORE_LOOKAHEAD < len(groups):
                pending.append(scores(*groups[i + SCORE_LOOKAHEAD]))
            nk = keys_used(c)
            vt = jnp.concatenate([v_ref[0, h * LANES:(h + 1) * LANES, 0:nk],
                                  jnp.ones((ONES_ROWS, nk), BF16)], axis=0)
            m_prev = m_refs[c][h:h + 1, :]
            m_new = jnp.maximum(m_prev, jnp.max(s, axis=0, keepdims=True))
            alpha = jnp.exp2(m_prev - m_new)
            pr = jnp.exp2(s - m_new).astype(BF16)
            acc_refs[c][h] = acc_refs[c][h] * alpha + _dot(vt, pr)
            m_refs[c][h:h + 1, :] = m_new

    @pl.when(ki < qi)
    def _():
        sweep(False)

    @pl.when(ki == qi)
    def _():
        sweep(True)
        lam = _lambda_value(lp_ref, lambda_init)
        subw = subw_ref[...]

        for h in range(N_HEADS):
            hs = slice(h * LANES, (h + 1) * LANES)
            for r in range(tq // Q_COLS):
                a1 = acc_refs[r][h]
                a2 = acc_refs[r + tq // Q_COLS][h]
                o_t = a1[0:DV] / a1[DV:DV + 1] - lam * (a2[0:DV] / a2[DV:DV + 1])
                res = _rms_scale(o_t.T) * subw * (1.0 - lambda_init)
                on_ref[r * Q_COLS:(r + 1) * Q_COLS, hs] = res.astype(BF16)
        o_ref[0] = res_ref[0] + _dot(on_ref[...], wo_ref[...])


def diff_attention(q_t, k, v_t, lam_params, subw, w_o, res, *, tq, lambda_init):
    nb, d, length = q_t.shape
    tk = tq
    nq = length // tq
    qt = np.concatenate([np.full(i + 1, i) for i in range(nq)]).astype(np.int32)
    kt = np.concatenate([np.arange(i + 1) for i in range(nq)]).astype(np.int32)
    kern = functools.partial(_attn_kernel, tq=tq, tk=tk, lambda_init=lambda_init)
    n_qc = 2 * tq // Q_COLS
    grid_spec = pltpu.PrefetchScalarGridSpec(
        num_scalar_prefetch=2,
        grid=(nb, len(qt)),
        in_specs=[pl.BlockSpec((1, d, tq), lambda b, p, qt_r, kt_r: (b, 0, qt_r[p])),
                  pl.BlockSpec((1, tk, d), lambda b, p, qt_r, kt_r: (b, kt_r[p], 0)),
                  pl.BlockSpec((1, d, tk), lambda b, p, qt_r, kt_r: (b, 0, kt_r[p])),
                  pl.BlockSpec((4, DK), lambda b, p, qt_r, kt_r: (0, 0)),
                  pl.BlockSpec((1, DV), lambda b, p, qt_r, kt_r: (0, 0)),
                  pl.BlockSpec((d, d), lambda b, p, qt_r, kt_r: (0, 0)),
                  pl.BlockSpec((1, tq, d), lambda b, p, qt_r, kt_r: (b, qt_r[p], 0))],
        out_specs=pl.BlockSpec((1, tq, d), lambda b, p, qt_r, kt_r: (b, qt_r[p], 0)),
        scratch_shapes=([pltpu.VMEM((N_HEADS, LANES, 2 * tq), BF16), pltpu.VMEM((tq, d), BF16)]
                        + [pltpu.VMEM((N_HEADS, Q_COLS), F32)] * n_qc
                        + [pltpu.VMEM((N_HEADS, DV + ONES_ROWS, Q_COLS), F32)] * n_qc),
    )
    return pl.pallas_call(
        kern,
        grid_spec=grid_spec,
        out_shape=jax.ShapeDtypeStruct((nb, length, d), F32),
        compiler_params=_params("arbitrary", "arbitrary"),
        name="diff_attention",
    )(jnp.asarray(qt), jnp.asarray(kt), q_t, k, v_t, lam_params, subw, w_o, res)


def _attn_cached_kernel(q_ref, ck_ref, cv_ref, kn_ref, vn_ref, lp_ref, subw_ref, o_ref,
                        *, nq, lambda_init):
    past = ck_ref.shape[1] // N_HEADS
    lane = lax.broadcasted_iota(jnp.int32, (nq, LANES), 1)
    nt = (((1,), (1,)), ((), ()))
    pad = jnp.zeros((LANES - nq, LANES), BF16)
    lam = _lambda_value(lp_ref, lambda_init)
    for h in range(N_HEADS):
        q = q_ref[0, :, h * LANES:(h + 1) * LANES]
        zero = jnp.zeros_like(q)
        qs = jnp.concatenate([jnp.where(lane < DK, q, zero), jnp.where(lane < DK, zero, q)], axis=0)
        old_rows = pl.ds(h, past, stride=N_HEADS)
        new_rows = pl.ds(h, nq, stride=N_HEADS)
        kc = ck_ref[0, old_rows, :].astype(BF16)
        vc = cv_ref[0, old_rows, :].astype(BF16)
        kn = jnp.concatenate([kn_ref[0, new_rows, :].astype(BF16), pad], axis=0)
        vn = jnp.concatenate([vn_ref[0, new_rows, :].astype(BF16), pad], axis=0)
        s_c = lax.dot_general(qs, kc, nt, preferred_element_type=F32)
        s_n = lax.dot_general(qs, kn, nt, preferred_element_type=F32)
        col = lax.broadcasted_iota(jnp.int32, s_n.shape, 1)
        s_n = jnp.where(col < nq, s_n, NEG_BIG)
        m = jnp.maximum(jnp.max(s_c, axis=-1, keepdims=True), jnp.max(s_n, axis=-1, keepdims=True))
        p_c = jnp.exp(s_c - m)
        p_n = jnp.exp(s_n - m)
        denom = jnp.sum(p_c, axis=-1, keepdims=True) + jnp.sum(p_n, axis=-1, keepdims=True)
        o = (_dot(p_c.astype(BF16), vc) + _dot(p_n.astype(BF16), vn)) / denom
        res = _diff_finish(o[0:nq], o[nq:2 * nq], lam, subw_ref[...], lambda_init)
        o_ref[0, :, h * LANES:(h + 1) * LANES] = res.astype(o_ref.dtype)


def diff_attention_cached(q, cache_k, cache_v, k_new, v_new, lam_params, subw, *, lambda_init):
    nb, nq, d = q.shape
    past = cache_k.shape[1]
    assert past % CHUNK + nq <= CHUNK and nq <= LANES, (past, nq)
    kern = functools.partial(_attn_cached_kernel, nq=nq, lambda_init=lambda_init)
    stream = lambda b: (b, 0, 0)
    const = lambda b: (0, 0)
    rows = lambda a: a.reshape(nb, a.shape[1] * N_HEADS, LANES)
    cache_k, cache_v, k_new, v_new = rows(cache_k), rows(cache_v), rows(k_new), rows(v_new)
    return pl.pallas_call(
        kern,
        grid=(nb,),
        in_specs=[pl.BlockSpec((1, nq, d), lambda b: (b, 0, 0)),
                  pl.BlockSpec((1, past * N_HEADS, LANES), stream),
                  pl.BlockSpec((1, past * N_HEADS, LANES), stream),
                  pl.BlockSpec((1, nq * N_HEADS, LANES), stream),
                  pl.BlockSpec((1, nq * N_HEADS, LANES), stream),
                  pl.BlockSpec((4, DK), const),
                  pl.BlockSpec((1, DV), const)],
        out_specs=pl.BlockSpec((1, nq, d), lambda b: (b, 0, 0)),
        out_shape=jax.ShapeDtypeStruct((nb, nq, d), BF16),
        compiler_params=_params("arbitrary"),
        name="diff_attention_cached",
    )(q, cache_k, cache_v, k_new, v_new, lam_params, subw)


def _route(logits):
    lane = lax.broadcasted_iota(jnp.int32, logits.shape, 1)
    valid = lane < N_EXPERTS
    lg = jnp.where(valid, logits, NEG_BIG)
    m1 = jnp.max(lg, axis=-1, keepdims=True)
    lane_f = lane.astype(F32)
    i1 = jnp.min(jnp.where(lg == m1, lane_f, float(LANES)), axis=-1, keepdims=True)
    lg2 = jnp.where(lane_f == i1, NEG_BIG, lg)
    m2 = jnp.max(lg2, axis=-1, keepdims=True)
    i2 = jnp.min(jnp.where(lg2 == m2, lane_f, float(LANES)), axis=-1, keepdims=True)
    e2 = jnp.exp(m2 - m1)
    g1 = 1.0 / (1.0 + e2)
    g2 = e2 / (1.0 + e2)
    return jnp.where(lane_f == i1, g1, 0.0) + jnp.where(lane_f == i2, g2, 0.0)


def _moe_route_kernel(x_ref, nw_ref, wr_ref, h_ref, comb_ref, rk_ref, rkt_ref, cnt_ref,
                      meta_ref, metat_ref, *, rt):
    t = x_ref.shape[0]
    hf = _rms_scale(x_ref[...]) * nw_ref[...]
    h_ref[...] = hf.astype(BF16)
    comb = _route(_dot_exact_rhs_general(hf, wr_ref[...]))
    comb_ref[...] = comb
    sel = jnp.where(comb > 0.0, 1.0, 0.0)
    row = lax.broadcasted_iota(jnp.int32, (rt, rt), 0)
    col = lax.broadcasted_iota(jnp.int32, (rt, rt), 1)
    strict = jnp.where(col < row, 1.0, 0.0).astype(BF16)
    carry = jnp.zeros((1, LANES), F32)
    ranks = []
    for s in range(t // rt):
        sel_s = sel[s * rt:(s + 1) * rt]
        rank_s = _dot(strict, sel_s.astype(BF16)) + carry
        ranks.append(jnp.where(sel_s > 0.0, rank_s, -1.0))
        carry = carry + jnp.sum(sel_s, axis=0, keepdims=True)
    rk = jnp.concatenate(ranks, axis=0) if len(ranks) > 1 else ranks[0]
    rk_ref[...] = rk.astype(jnp.int32)
    rkt_ref[0] = rk.T.astype(jnp.int32)
    cnt_ref[0] = jnp.broadcast_to(carry, (8, LANES)).astype(jnp.int32)

    padded = jnp.floor((carry + (MOE_ROW_ALIGN - 1)) * (1.0 / MOE_ROW_ALIGN)) * MOE_ROW_ALIGN
    lrow = lax.broadcasted_iota(jnp.int32, (LANES, LANES), 0)
    lcol = lax.broadcasted_iota(jnp.int32, (LANES, LANES), 1)
    before = jnp.where(lrow < lcol, 1.0, 0.0).astype(BF16)
    group_start = _dot(jnp.broadcast_to(padded, (8, LANES)).astype(BF16), before)[0:1]
    dest = jnp.where(sel > 0.0, rk + group_start, -1.0)
    lane_f = lax.broadcasted_iota(jnp.int32, (t, LANES), 1).astype(F32)
    la = jnp.min(jnp.where(sel > 0.0, lane_f, float(LANES)), axis=-1, keepdims=True)
    lb = jnp.max(jnp.where(sel > 0.0, lane_f, -1.0), axis=-1, keepdims=True)
    at = lambda v, l: jnp.sum(jnp.where(lane_f == l, v, 0.0), axis=-1, keepdims=True)
    two = lb > la
    d_a, g_a = at(dest, la), at(comb, la)
    d_b = jnp.where(two, at(dest, lb), -1.0)
    g_b = jnp.where(two, at(comb, lb), 0.0)
    meta = jnp.where(lane_f == 0.0, d_a, jnp.where(lane_f == 1.0, d_b,
                     jnp.where(lane_f == 2.0, g_a, jnp.where(lane_f == 3.0, g_b, 0.0))))
    meta_ref[...] = meta
    metat_ref[0] = meta.T[0:8]


def moe_route(x, nw, wr, *, tb, rt):
    t, d = x.shape
    nb = t // tb
    row = lambda b: (b, 0)
    const = lambda b: (0, 0)
    return pl.pallas_call(
        functools.partial(_moe_route_kernel, rt=rt),
        grid=(nb,),
        in_specs=[pl.BlockSpec((tb, d), row), pl.BlockSpec((1, d), const),
                  pl.BlockSpec((d, LANES), const)],
        out_specs=[pl.BlockSpec((tb, d), row), pl.BlockSpec((tb, LANES), row),
                   pl.BlockSpec((tb, LANES), row),
                   pl.BlockSpec((1, LANES, tb), lambda b: (b, 0, 0)),
                   pl.BlockSpec((1, 8, LANES), lambda b: (b, 0, 0)),
                   pl.BlockSpec((tb, LANES), row),
                   pl.BlockSpec((1, 8, tb), lambda b: (b, 0, 0))],
        out_shape=[jax.ShapeDtypeStruct((t, d), BF16),
                   jax.ShapeDtypeStruct((t, LANES), F32),
                   jax.ShapeDtypeStruct((t, LANES), jnp.int32),
                   jax.ShapeDtypeStruct((nb, LANES, tb), jnp.int32),
                   jax.ShapeDtypeStruct((nb, 8, LANES), jnp.int32),
                   jax.ShapeDtypeStruct((t, LANES), F32),
                   jax.ShapeDtypeStruct((nb, 8, tb), F32)],
        compiler_params=_params("arbitrary"),
        name="moe_route",
    )(x, nw, wr)


def _moe_expert_kernel(cnt_ref, x_ref, h_ref, comb_ref, rk_ref, rkt_ref, wg_ref, wu_ref, wd_ref,
                       fw_ref, o_ref, rkc_ref, xg_ref, gs_ref, yacc_ref, *, rt):
    b = pl.program_id(0)
    e = pl.program_id(1)
    f = pl.program_id(2)
    t = x_ref.shape[0]
    n_rows = cnt_ref[b * N_EXPERTS + e]
    half = rt // 2
    n_tiles = (n_rows + (half - 1)) // rt
    tail_row0 = pl.multiple_of(n_tiles * rt, rt)
    has_tail = n_rows > n_tiles * rt

    def over_tiles(fn):
        def body(j, carry):
            fn(pl.multiple_of(j * rt, rt), rt)
            return carry

        lax.fori_loop(0, n_tiles, body, 0)

        @pl.when(has_tail)
        def _():
            fn(tail_row0, half)

    @pl.when(jnp.logical_and(e == 0, f == 0))
    def _():
        o_ref[...] = x_ref[...]

    @pl.when(f == 0)
    def _():
        lane = lax.broadcasted_iota(jnp.int32, (t, LANES), 1)
        pick = lane == e
        rkc = jnp.sum(jnp.where(pick, rk_ref[...], 0).astype(F32), axis=-1, keepdims=True)
        rkc_ref[...] = jnp.broadcast_to(rkc, (t, LANES)).astype(jnp.int32)
        gate = jnp.sum(jnp.where(pick, comb_ref[...], 0.0), axis=-1, keepdims=True)
        g_hi, g_mid, g_lo = _split3(jnp.broadcast_to(gate, (t, LANES)))
        g3 = jnp.where(lane == 0, g_hi.astype(F32),
                       jnp.where(lane == 1, g_mid.astype(F32),
                                 jnp.where(lane == 2, g_lo.astype(F32), 0.0))).astype(BF16)
        rk_row = rkt_ref[0, pl.ds(e, 1), :]

        def gather(row0, size):
            tile = pl.ds(row0, size)
            rows = lax.broadcasted_iota(jnp.int32, (size, t), 0) + row0
            p = jnp.where(rk_row == rows, 1.0, 0.0).astype(BF16)
            xg_ref[tile, :] = _dot(p, h_ref[...]).astype(BF16)
            gate_rows = jnp.sum(_dot(p, g3), axis=-1, keepdims=True)
            gs_ref[tile, :] = jnp.broadcast_to(gate_rows, (size, LANES))
            yacc_ref[tile, :] = jnp.zeros((size, yacc_ref.shape[1]), F32)

        over_tiles(gather)

    def expert(row0, size):
        tile = pl.ds(row0, size)
        xg = xg_ref[tile, :]
        g = _dot(xg, wg_ref[...])
        u = _dot(xg, wu_ref[...])
        gates = _lane_tile(gs_ref[tile, :], g.shape[1] // LANES)
        act = (g * _sigmoid(g) * u * gates).astype(BF16)
        yacc_ref[tile, :] += _dot(act, wd_ref[...])

    over_tiles(expert)

    @pl.when(f == pl.num_programs(2) - 1)
    def _():
        def scatter(row0, size):
            cols = lax.broadcasted_iota(jnp.int32, (t, size), 1) + row0
            s = jnp.where(_lane_tile(rkc_ref[...], size // LANES) == cols, 1.0, 0.0).astype(BF16)
            o_ref[...] += _dot(s, yacc_ref[pl.ds(row0, size), :].astype(BF16))

        over_tiles(scatter)

    @pl.when(jnp.logical_and(e == pl.num_programs(1) - 1, f == pl.num_programs(2) - 1))
    def _():
        o_ref[...] = _rms_scale(o_ref[...]) * fw_ref[...]


def _dot_exact_rhs_general(x, w):
    xh, xm, xl = _split3(x)
    wh, wm, wl = _split3(w)
    return (_dot(xh, wh) + (_dot(xh, wm) + _dot(xm, wh))
            + (_dot(xh, wl) + _dot(xm, wm) + _dot(xl, wh)))


def moe_ffn_final(x, nw, wr, wg, wu, wd, fw, *, tb, rt, tf):
    t, d = x.shape
    ne, _, ff = wg.shape
    nb = t // tb
    h, comb, rk, rkt, cnt, _, _ = moe_route(x, nw, wr, tb=tb, rt=rt)
    counts = cnt[:, 0, :ne].reshape(nb * ne)
    once = pl.Buffered(1)
    blk = lambda b, e, f, c: (b, 0)
    const = lambda b, e, f, c: (0, 0)
    grid_spec = pltpu.PrefetchScalarGridSpec(
        num_scalar_prefetch=1,
        grid=(nb, ne, ff // tf),
        in_specs=[pl.BlockSpec((tb, d), blk, pipeline_mode=once),
                  pl.BlockSpec((tb, d), blk, pipeline_mode=once),
                  pl.BlockSpec((tb, LANES), blk, pipeline_mode=once),
                  pl.BlockSpec((tb, LANES), blk, pipeline_mode=once),
                  pl.BlockSpec((1, LANES, tb), lambda b, e, f, c: (b, 0, 0), pipeline_mode=once),
                  pl.BlockSpec((None, d, tf), lambda b, e, f, c: (e, 0, f)),
                  pl.BlockSpec((None, d, tf), lambda b, e, f, c: (e, 0, f)),
                  pl.BlockSpec((None, tf, d), lambda b, e, f, c: (e, f, 0)),
                  pl.BlockSpec((1, d), const)],
        out_specs=pl.BlockSpec((tb, d), blk),
        scratch_shapes=[pltpu.VMEM((tb, LANES), jnp.int32),
                        pltpu.VMEM((tb, d), BF16),
                        pltpu.VMEM((tb, LANES), F32),
                        pltpu.VMEM((tb, d), F32)],
    )
    return pl.pallas_call(
        functools.partial(_moe_expert_kernel, rt=rt),
        grid_spec=grid_spec,
        out_shape=jax.ShapeDtypeStruct((t, d), F32),
        compiler_params=_params("arbitrary", "arbitrary", "arbitrary"),
        name="moe_experts",
    )(counts, x, h, comb, rk, rkt, wg, wu, wd, fw)


MOE_ROW_ALIGN = 16
MOE_ROW_TILE = 128
MOE_GROUP_TILE = 1024


def _moe_plan(cnt, *, tb, region):
    nb, ne = cnt.shape
    padded = (cnt + (MOE_ROW_ALIGN - 1)) // MOE_ROW_ALIGN * MOE_ROW_ALIGN
    off = jnp.cumsum(padded, axis=0) - padded
    boff = jnp.cumsum(padded, axis=1) - padded
    total = jnp.sum(padded, axis=0)
    tiles = (total + MOE_ROW_TILE + MOE_GROUP_TILE - 1) // MOE_GROUP_TILE
    tile_end = jnp.cumsum(tiles)
    max_tiles = (2 * nb * tb + nb * ne * MOE_ROW_ALIGN + ne * MOE_ROW_TILE) // MOE_GROUP_TILE + ne + 1
    idx = jnp.arange(max_tiles, dtype=jnp.int32)
    last = jnp.maximum(tile_end[-1] - 1, 0)
    active = idx < tile_end[-1]
    ii = jnp.minimum(idx, last)
    te = jnp.sum(ii[:, None] >= tile_end[None, :], axis=1).astype(jnp.int32)
    k = ii - (tile_end - tiles)[te]
    tr = te * (region // MOE_GROUP_TILE) + k
    valid = jnp.clip(total[te] - k * MOE_GROUP_TILE, 0, MOE_GROUP_TILE)
    tv = jnp.where(active, valid, -1)
    flat = lambda a: a.reshape(-1).astype(jnp.int32)
    return flat(off), flat(boff), te, tr.astype(jnp.int32), tv.astype(jnp.int32)


def _block_rows(tb):
    need = 2 * tb + N_EXPERTS * MOE_ROW_ALIGN
    return -(-need // MOE_ROW_TILE) * MOE_ROW_TILE


def _for_group_pieces(cnt_ref, off_ref, boff_ref, blk, e, tb, region, enabled, make_copy, action):
    units = (cnt_ref[blk * N_EXPERTS + e] + (MOE_ROW_ALIGN - 1)) // MOE_ROW_ALIGN
    src0 = boff_ref[blk * N_EXPERTS + e]
    dst0 = e * region + off_ref[blk * N_EXPERTS + e]
    done = 0
    for k in reversed(range(_piece_bits(tb))):
        size = MOE_ROW_ALIGN << k
        has = (units >> k) & 1
        src = pl.ds(pl.multiple_of(src0 + done, MOE_ROW_ALIGN), size)
        dst = pl.ds(pl.multiple_of(dst0 + done, MOE_ROW_ALIGN), size)

        @pl.when(jnp.logical_and(enabled, has == 1))
        def _():
            action(make_copy(k, src, dst))

        done = done + has * size


def _piece_bits(tb):
    return (tb // MOE_ROW_ALIGN).bit_length()


def _moe_gather_kernel(cnt_ref, off_ref, boff_ref, h_ref, metat_ref, xs_ref, buf_ref, sem_ref, *, region):
    b = pl.program_id(0)
    tb = h_ref.shape[0]
    rows_b = buf_ref.shape[1]
    slot = lax.rem(b, 2)

    dest = metat_ref[0].astype(jnp.int32)
    rows = lax.broadcasted_iota(jnp.int32, (rows_b, tb), 0)
    onehot = jnp.where(rows == dest[0:1], 1.0, jnp.where(rows == dest[1:2], 1.0, 0.0)).astype(BF16)
    buf_ref[slot] = _dot(onehot, h_ref[...]).astype(BF16)

    def pieces(blk, sl, enabled, action):
        for e in range(N_EXPERTS):
            def make_copy(p, src_rows, dst_rows, e=e):
                return pltpu.make_async_copy(buf_ref.at[sl, src_rows], xs_ref.at[dst_rows],
                                             sem_ref.at[sl, e, p])
            _for_group_pieces(cnt_ref, off_ref, boff_ref, blk, e, tb, region, enabled, make_copy, action)

    pieces(jnp.maximum(b - 1, 0), 1 - slot, b > 0, lambda cp: cp.wait())
    pieces(b, slot, True, lambda cp: cp.start())
    pieces(b, slot, b == pl.num_programs(0) - 1, lambda cp: cp.wait())


def _moe_grouped_kernel(te_ref, tr_ref, tv_ref, x_ref, wg_ref, wu_ref, wd_ref, y_ref, xm_ref, act_ref):
    i = pl.program_id(0)
    f = pl.program_id(1)
    valid = tv_ref[i]
    last_f = pl.num_programs(1) - 1
    tf = wg_ref.shape[1]

    @pl.when(valid > 0)
    def _():
        @pl.when(f == 0)
        def _():
            row = lax.broadcasted_iota(jnp.int32, x_ref.shape, 0)
            xm_ref[...] = jnp.where(row < valid, x_ref[...].astype(F32), 0.0).astype(BF16)

        xm = xm_ref[...]
        g = _dot(xm, wg_ref[...])
        u = _dot(xm, wu_ref[...])
        act_ref[:, pl.ds(pl.multiple_of(f * tf, tf), tf)] = (g * _sigmoid(g) * u).astype(BF16)

        @pl.when(f == last_f)
        def _():
            y_ref[...] = _dot(act_ref[...], wd_ref[...]).astype(y_ref.dtype)

    @pl.when(jnp.logical_and(valid == 0, f == last_f))
    def _():
        y_ref[...] = jnp.zeros(y_ref.shape, y_ref.dtype)


def _moe_combine_kernel(cnt_ref, off_ref, boff_ref, x_ref, meta_ref, ys_ref, fw_ref, o_ref,
                        buf_ref, sem_ref, *, region):
    b = pl.program_id(0)
    last = pl.num_programs(0) - 1
    tb = x_ref.shape[0]
    rows_b = buf_ref.shape[1]
    slot = lax.rem(b, 2)

    def pieces(blk, sl, enabled, action):
        for e in range(N_EXPERTS):
            def make_copy(p, buf_rows, sorted_rows, e=e):
                return pltpu.make_async_copy(ys_ref.at[sorted_rows], buf_ref.at[sl, buf_rows],
                                             sem_ref.at[sl, e, p])
            _for_group_pieces(cnt_ref, off_ref, boff_ref, blk, e, tb, region, enabled, make_copy, action)

    @pl.when(b == 0)
    def _():
        buf_ref[...] = jnp.zeros(buf_ref.shape, BF16)

    pieces(b, slot, b == 0, lambda cp: cp.start())
    pieces(jnp.minimum(b + 1, last), 1 - slot, b < last, lambda cp: cp.start())
    meta = meta_ref[...]
    col = lax.broadcasted_iota(jnp.int32, (tb, rows_b), 1)
    d_a = meta[:, 0:1].astype(jnp.int32)
    d_b = meta[:, 1:2].astype(jnp.int32)
    g_a, g_b = meta[:, 2:3], meta[:, 3:4]
    hi = lambda g: g.astype(BF16).astype(F32)
    hit_a, hit_b = col == d_a, col == d_b
    s_hi = jnp.where(hit_a, hi(g_a), jnp.where(hit_b, hi(g_b), 0.0)).astype(BF16)
    s_lo = jnp.where(hit_a, g_a - hi(g_a), jnp.where(hit_b, g_b - hi(g_b), 0.0)).astype(BF16)
    pieces(b, slot, True, lambda cp: cp.wait())
    y = buf_ref[slot]
    out = x_ref[...] + _dot(jnp.concatenate([s_hi, s_lo], axis=1), jnp.concatenate([y, y], axis=0))
    o_ref[...] = _rms_scale(out) * fw_ref[...]


def moe_ffn_final_sorted(x, nw, wr, wg, wu, wd, fw, *, tb, tf):
    t, d = x.shape
    ne, _, ff = wg.shape
    nb = t // tb
    st = MOE_GROUP_TILE
    region = -(-(tb * nb + MOE_ROW_ALIGN * nb + tb) // st) * st
    h, _, _, _, cnt, meta, meta_t = moe_route(x, nw, wr, tb=tb, rt=tb)
    counts = cnt[:, 0, :ne]
    off, boff, te, tr, tv = _moe_plan(counts, tb=tb, region=region)
    counts = counts.reshape(nb * ne)
    n_tiles = te.shape[0]
    rows_b = _block_rows(tb)
    n_pieces = _piece_bits(tb)

    xs = pl.pallas_call(
        functools.partial(_moe_gather_kernel, region=region),
        grid_spec=pltpu.PrefetchScalarGridSpec(
            num_scalar_prefetch=3,
            grid=(nb,),
            in_specs=[pl.BlockSpec((tb, d), lambda b, c, o, bo: (b, 0)),
                      pl.BlockSpec((1, 8, tb), lambda b, c, o, bo: (b, 0, 0))],
            out_specs=pl.BlockSpec(memory_space=pl.ANY),
            scratch_shapes=[pltpu.VMEM((2, rows_b, d), BF16),
                            pltpu.SemaphoreType.DMA((2, ne, n_pieces))],
        ),
        out_shape=jax.ShapeDtypeStruct((ne * region, d), BF16),
        compiler_params=_params("arbitrary"),
        name="moe_gather",
    )(counts, off, boff, h, meta_t)

    last_f = ff // tf - 1
    used_f = lambda i, f, te_r, tr_r, tv_r: jnp.where(tv_r[i] > 0, f, last_f)
    ys = pl.pallas_call(
        _moe_grouped_kernel,
        grid_spec=pltpu.PrefetchScalarGridSpec(
            num_scalar_prefetch=3,
            grid=(n_tiles, ff // tf),
            in_specs=[pl.BlockSpec((st, d), lambda i, f, te_r, tr_r, tv_r: (tr_r[i], 0)),
                      pl.BlockSpec((None, d, tf), lambda i, f, te_r, tr_r, tv_r:
                                   (te_r[i], 0, used_f(i, f, te_r, tr_r, tv_r))),
                      pl.BlockSpec((None, d, tf), lambda i, f, te_r, tr_r, tv_r:
                                   (te_r[i], 0, used_f(i, f, te_r, tr_r, tv_r))),
                      pl.BlockSpec((None, ff, d), lambda i, f, te_r, tr_r, tv_r: (te_r[i], 0, 0))],
            out_specs=pl.BlockSpec((st, d), lambda i, f, te_r, tr_r, tv_r: (tr_r[i], 0)),
            scratch_shapes=[pltpu.VMEM((st, d), BF16), pltpu.VMEM((st, ff), BF16)],
        ),
        out_shape=jax.ShapeDtypeStruct((ne * region, d), BF16),
        compiler_params=_params("arbitrary", "arbitrary"),
        name="moe_grouped",
    )(te, tr, tv, xs, wg, wu, wd)

    return pl.pallas_call(
        functools.partial(_moe_combine_kernel, region=region),
        grid_spec=pltpu.PrefetchScalarGridSpec(
            num_scalar_prefetch=3,
            grid=(nb,),
            in_specs=[pl.BlockSpec((tb, d), lambda b, c, o, bo: (b, 0)),
                      pl.BlockSpec((tb, LANES), lambda b, c, o, bo: (b, 0)),
                      pl.BlockSpec(memory_space=pl.ANY),
                      pl.BlockSpec((1, d), lambda b, c, o, bo: (0, 0))],
            out_specs=pl.BlockSpec((tb, d), lambda b, c, o, bo: (b, 0)),
            scratch_shapes=[pltpu.VMEM((2, rows_b, d), BF16),
                            pltpu.SemaphoreType.DMA((2, ne, n_pieces))],
        ),
        out_shape=jax.ShapeDtypeStruct((t, d), F32),
        compiler_params=_params("arbitrary"),
        name="moe_combine",
    )(counts, off, boff, x, meta, ys, fw)


def _row(v):
    return v.reshape(1, -1).astype(F32)


def _pad_lanes(v, width):
    v = _row(v)
    return jnp.pad(v, ((0, 0), (0, width - v.shape[1])))


def _trunk(x, pos, conv_in, state_in, past_k, past_v, W, *, ssd_chunk, ssd_valid, tm, tq):
    nb, length, d = x.shape
    t = nb * length
    x0 = x.reshape(t, d)

    proj, dt_raw = rms_matmul(x0, W["mamba_norm_w"], W["w_in"], W["w_dt"], tm=min(t, 512),
                              tn=W["w_in"].shape[1])
    proj3 = proj.reshape(nb, length, D_INNER + CONV_DIM)
    new_conv = proj3[:, length - (CONV_W - 1):, D_INNER:D_INNER + CONV_DIM].astype(F32)
    conv_hi = conv_in.astype(BF16)
    conv_lo = (conv_in - conv_hi.astype(F32)).astype(BF16)
    front = ((0, 0), (CONV_HIST - (CONV_W - 1), 0), (0, 0))
    conv_hist = jnp.concatenate([jnp.pad(conv_hi, front), jnp.pad(conv_lo, front)], axis=1)
    x1, state_out = ssd(proj3, dt_raw.reshape(nb, length, DT_PAD), conv_hist,
                        state_in.reshape(nb, D_INNER, D_STATE),
                        W["conv_w"], W["conv_b"], W["dt_bias"], W["a_log"], W["d_exp"], W["gn_w"],
                        W["w_out"], x, c=ssd_chunk, cv=ssd_valid)
    new_ssm = state_out.reshape(nb, SSM_HEADS, SSM_HEAD_DIM, D_STATE)
    x1 = x1.reshape(t, d)

    x2 = swiglu_ffn(x1, W["ffn_norm_w"], W["ffn_wg"], W["ffn_wu"], W["ffn_wd"], tm=tm,
                    tf=W["ffn_wg"].shape[1])

    tables = _rope_tables(pos)
    proj_args = (x2, W["kv_norm_w"], W["attn_norm_w"], W["w_k"], W["w_v"], W["w_q"], tables)
    lambda_init = 0.8 - 0.6 * math.exp(-0.3 * 1)
    if past_k is None:
        k, v, kb, v_t, q_t = qkv_proj(*proj_args, tm=tm, streams=nb, transposed=True)
        x3 = diff_attention(q_t, kb.reshape(nb, length, d), v_t, W["lam"], W["subln_w"], W["w_o"],
                            x2.reshape(nb, length, d), tq=tq, lambda_init=lambda_init).reshape(t, d)
    else:
        k, v, q = qkv_proj(*proj_args, tm=tm, streams=nb, transposed=False)
        q3 = q.reshape(nb, length, d)
        o = diff_attention_cached(q3, past_k, past_v,
                                  k.reshape(nb, length, N_HEADS, LANES),
                                  v.reshape(nb, length, N_HEADS, LANES),
                                  W["lam"], W["subln_w"], lambda_init=lambda_init)
        x3 = matmul_residual(o.reshape(t, d), W["w_o"], x2, tm=tm)

    moe_args = (x3, W["moe_norm_w"], W["moe_wr"], W["moe_wg"], W["moe_wu"], W["moe_wd"], W["final_norm_w"])
    if t >= 4 * MOE_GROUP_TILE:
        yout = moe_ffn_final_sorted(*moe_args, tb=512, tf=512)
    else:
        yout = moe_ffn_final(*moe_args, tb=t, rt=256, tf=512)
    return (yout.reshape(nb, length, d), new_conv[None], new_ssm[None],
            k.reshape(nb, length, N_HEADS, 2 * DK), v.reshape(nb, length, N_HEADS, DV))


def kernel(x_prompt, x_sample, cache_conv, state_ssm, cache_k, cache_v, mamba_norm_w, mamba_w_in, mamba_conv_w, mamba_conv_b, mamba_dt_bias, mamba_a_log, mamba_d, mamba_gn_w, mamba_w_out, kv_norm_w, w_k, w_v, attn_norm_w, w_q, lambda_q1, lambda_k1, lambda_q2, lambda_k2, subln_w, w_o, ffn_norm_w, ffn_w_gate, ffn_w_up, ffn_w_down, moe_norm_w, moe_w_router, moe_w_gate, moe_w_up, moe_w_down, final_norm_w):
    w_in = mamba_w_in[0]
    n_dt = w_in.shape[1] - D_INNER - CONV_DIM
    W = dict(
        mamba_norm_w=_row(mamba_norm_w[0]),
        w_in=w_in[:, :D_INNER + CONV_DIM].astype(BF16),
        w_dt=jnp.pad(w_in[:, D_INNER + CONV_DIM:], ((0, 0), (0, DT_PAD - n_dt))).astype(BF16),
        conv_w=mamba_conv_w[0].astype(F32),
        conv_b=_row(mamba_conv_b[0]),
        dt_bias=_pad_lanes(mamba_dt_bias[0], LANES),
        a_log=_pad_lanes(mamba_a_log[0], LANES),
        d_exp=_row(jnp.repeat(mamba_d[0], SSM_HEAD_DIM)),
        gn_w=_row(mamba_gn_w[0]),
        w_out=mamba_w_out[0].astype(BF16),
        kv_norm_w=_row(kv_norm_w), w_k=w_k.astype(BF16), w_v=w_v.astype(BF16),
        attn_norm_w=_row(attn_norm_w[0]), w_q=w_q[0].astype(BF16),
        lam=jnp.stack([lambda_q1[0], lambda_k1[0], lambda_q2[0], lambda_k2[0]]).astype(F32),
        subln_w=_row(subln_w[0]), w_o=w_o[0].astype(BF16),
        ffn_norm_w=_row(ffn_norm_w[0]),
        ffn_wg=ffn_w_gate[0].astype(BF16), ffn_wu=ffn_w_up[0].astype(BF16),
        ffn_wd=ffn_w_down[0].astype(BF16),
        moe_norm_w=_row(moe_norm_w[0]),
        moe_wr=jnp.pad(moe_w_router[0].astype(F32), ((0, 0), (0, LANES - N_EXPERTS))),
        moe_wg=moe_w_gate[0].astype(BF16), moe_wu=moe_w_up[0].astype(BF16),
        moe_wd=moe_w_down[0].astype(BF16),
        final_norm_w=_row(final_norm_w),
    )
    bp, lp = x_prompt.shape[0], x_prompt.shape[1]
    bs, ls = x_sample.shape[0], x_sample.shape[1]
    past = cache_k.shape[1]

    conv0 = jnp.zeros((bp, CONV_W - 1, CONV_DIM), F32)
    ssm0 = jnp.zeros((bp, SSM_HEADS, SSM_HEAD_DIM, D_STATE), F32)
    y_p, conv_p, ssm_p, k_p, v_p = _trunk(
        x_prompt, jnp.arange(lp, dtype=jnp.int32), conv0, ssm0, None, None, W,
        ssd_chunk=128, ssd_valid=128, tm=512, tq=512)

    pos_s = jnp.tile(past + jnp.arange(ls, dtype=jnp.int32), bs)
    y_s, conv_s, ssm_s, k_s, v_s = _trunk(
        x_sample, pos_s, cache_conv[0], state_ssm[0], cache_k, cache_v, W,
        ssd_chunk=128, ssd_valid=ls, tm=bs * ls, tq=None)
    return (y_p, y_s, conv_p, ssm_p, k_p, v_p, conv_s, ssm_s, k_s, v_s)
```

```python
import functools
import math

import numpy as np
import jax
import jax.numpy as jnp
from jax import lax
from jax.experimental import pallas as pl
from jax.experimental.pallas import tpu as pltpu

F32 = jnp.float32
BF16 = jnp.bfloat16

EPS = 1e-5
D_MODEL = 1024
D_INNER = 2048
SSM_HEAD_DIM = 64
SSM_HEADS = 32
SSM_GROUPS = 4
D_STATE = 128
GROUP_W = D_INNER // SSM_GROUPS
CONV_W = 4
CONV_HIST = 16
SSD_STREAMS_PER_STEP = 1
CONV_DIM = D_INNER + 2 * SSM_GROUPS * D_STATE
DT_PAD = 256
PROJ_W = D_INNER + CONV_DIM + DT_PAD
N_HEADS = 8
DK = 64
DV = 128
ROT_DIM = 16
ROPE_THETA = 500000.0
CHUNK = 64
CHUNK_SHIFT = 6
assert 1 << CHUNK_SHIFT == CHUNK
N_EXPERTS = 8
LANES = 128
NEG_BIG = -1e30
LOG2E = math.log2(math.e)
Q_COLS = 256
ONES_ROWS = 16
SCORE_LOOKAHEAD = 3
VMEM_LIMIT = 56 * 1024 * 1024


def _params(*sem):
    return pltpu.CompilerParams(dimension_semantics=sem, vmem_limit_bytes=VMEM_LIMIT)


def _sigmoid(x):
    return 1.0 / (1.0 + jnp.exp(-x))


def _rms_scale(x):
    return x * lax.rsqrt(jnp.mean(x * x, axis=-1, keepdims=True) + EPS)


def _split3(x):
    hi = x.astype(BF16)
    r1 = x - hi.astype(F32)
    mid = r1.astype(BF16)
    lo = (r1 - mid.astype(F32)).astype(BF16)
    return hi, mid, lo


def _dot(a, b):
    return jnp.dot(a, b, preferred_element_type=F32)


def _lane_tile(x, n):
    return jnp.concatenate([x] * n, axis=1) if n > 1 else x


def _dot_exact_rhs(x, m_bf16):
    hi, mid, lo = _split3(x)
    return _dot(hi, m_bf16) + _dot(mid, m_bf16) + _dot(lo, m_bf16)


def _dot_exact_lhs(m_bf16, x):
    hi, mid, lo = _split3(x)
    return _dot(m_bf16, hi) + _dot(m_bf16, mid) + _dot(m_bf16, lo)


def _rms_matmul_kernel(x_ref, nw_ref, w_ref, wdt_ref, o_ref, dt_ref, h_ref):
    @pl.when(pl.program_id(1) == 0)
    def _():
        h = (_rms_scale(x_ref[...]) * nw_ref[...]).astype(BF16)
        h_ref[...] = h
        dt_ref[...] = _dot(h, wdt_ref[...])

    o_ref[...] = _dot(h_ref[...], w_ref[...]).astype(o_ref.dtype)


def rms_matmul(x, nw, w, w_dt, *, tm, tn):
    t, d = x.shape
    n = w.shape[1]
    n_dt = w_dt.shape[1]
    resident = dict(pipeline_mode=pl.Buffered(1)) if tn == n else {}
    return pl.pallas_call(
        _rms_matmul_kernel,
        grid=(t // tm, n // tn),
        in_specs=[pl.BlockSpec((tm, d), lambda i, j: (i, 0)),
                  pl.BlockSpec((1, d), lambda i, j: (0, 0)),
                  pl.BlockSpec((d, tn), lambda i, j: (0, j), **resident),
                  pl.BlockSpec((d, n_dt), lambda i, j: (0, 0))],
        out_specs=[pl.BlockSpec((tm, tn), lambda i, j: (i, j)),
                   pl.BlockSpec((tm, n_dt), lambda i, j: (i, 0))],
        out_shape=[jax.ShapeDtypeStruct((t, n), BF16),
                   jax.ShapeDtypeStruct((t, n_dt), F32)],
        scratch_shapes=[pltpu.VMEM((tm, d), BF16)],
        compiler_params=_params("arbitrary", "arbitrary"),
        name="rms_matmul",
    )(x, nw, w, w_dt)


def _ssd_kernel(z_ref, xs_ref, b_ref, c_ref, dt_ref, cin_ref, sin_ref, cw_ref, cb_ref,
                dtb_ref, alog_ref, dexp_ref, gnw_ref, e_ref, shift_ref, wout_ref, res_ref,
                y_ref, sout_ref, state_ref, ebuf_ref, *, c, cv):
    def stream(s):
        one = lambda r: r.at[:, s]
        return _ssd_stream(one(z_ref), one(xs_ref), one(b_ref), one(c_ref), one(dt_ref), one(cin_ref),
                           one(sin_ref), cw_ref, cb_ref, dtb_ref, alog_ref, dexp_ref, gnw_ref, e_ref,
                           shift_ref, wout_ref, one(res_ref), one(y_ref), one(sout_ref),
                           state_ref.at[s], ebuf_ref.at[s], c=c, cv=cv)

    running = [stream(s) for s in range(z_ref.shape[1])]
    while running:
        running = [g for g in running if next(g, "done") != "done"]


def _ssd_stream(z_ref, xs_ref, b_ref, c_ref, dt_ref, cin_ref, sin_ref, cw_ref, cb_ref,
                dtb_ref, alog_ref, dexp_ref, gnw_ref, e_ref, shift_ref, wout_ref, res_ref,
                y_ref, sout_ref, state_ref, ebuf_ref, *, c, cv):
    step = pl.program_id(1)
    hist = CONV_HIST

    @pl.when(step == 0)
    def _():
        state_ref[...] = sin_ref[0].T
        ebuf_ref[0:hist, :] = cin_ref[0, 0:hist, :]
        ebuf_ref[hist + c:2 * hist + c, :] = cin_ref[0, hist:2 * hist, :]

    ebuf_ref[hist:hist + cv, 0:D_INNER] = xs_ref[0]
    ebuf_ref[hist:hist + cv, D_INNER:D_INNER + GROUP_W] = b_ref[0]
    ebuf_ref[hist:hist + cv, D_INNER + GROUP_W:CONV_DIM] = c_ref[0]
    if cv < c:
        ebuf_ref[hist + cv:hist + c, :] = jnp.zeros((c - cv, CONV_DIM), BF16)

    shifted = _dot(shift_ref[...], ebuf_ref[...])
    conv = cb_ref[...] + ebuf_ref[hist:hist + c, :].astype(F32) * cw_ref[CONV_W - 1:CONV_W, :]
    for k in range(CONV_W - 1):
        conv = conv + shifted[k * c:(k + 1) * c, :] * cw_ref[k:k + 1, :]
    conv = conv * _sigmoid(conv)
    if cv == c:
        ebuf_ref[0:hist, :] = ebuf_ref[c:c + hist, :]
        ebuf_ref[hist + c:2 * hist + c, :] = jnp.zeros((hist, CONV_DIM), BF16)
    yield

    xs = conv[:, 0:D_INNER]
    bm = conv[:, D_INNER:D_INNER + GROUP_W]
    cm = conv[:, D_INNER + GROUP_W:CONV_DIM]

    dt_in = dt_ref[0][:, 0:LANES] + dtb_ref[...]
    dt = jnp.maximum(dt_in, 0.0) + jnp.log1p(jnp.exp(-jnp.abs(dt_in)))
    if cv < c:
        dt = jnp.concatenate([dt, jnp.zeros((c - cv, LANES), F32)], axis=0)
    a = -jnp.exp(alog_ref[...])
    dta = dt * a

    row = lax.broadcasted_iota(jnp.int32, (c, c), 0)
    col = lax.broadcasted_iota(jnp.int32, (c, c), 1)
    tril = col <= row
    tri = jnp.where(tril, 1.0, 0.0).astype(BF16)
    cs = _dot_exact_lhs(tri, dta)
    cs_t = cs.T
    yield

    e = e_ref[...]
    dt_x = _dot_exact_rhs(dt, e)
    cs_x = _dot_exact_rhs(cs, e)
    cs_end_x = cs_x[c - 1:c, :]
    xdt = xs * dt_x
    decay_in = jnp.exp(cs_x)
    xdt_b = xdt.astype(BF16)
    xdt_end = (xdt * jnp.exp(cs_end_x - cs_x)).astype(BF16)
    state_decay = jnp.exp(cs_end_x)
    yield

    lane = lax.broadcasted_iota(jnp.int32, (c, LANES), 1)
    low_half = lane < SSM_HEAD_DIM

    y_groups = []
    for g in range(SSM_GROUPS):
        gs = slice(g * GROUP_W, (g + 1) * GROUP_W)
        ns = slice(g * D_STATE, (g + 1) * D_STATE)
        bg_t = bm[:, ns].T.astype(BF16)
        cg = cm[:, ns].astype(BF16)
        cb = _dot(cg, bg_t)
        sg = state_ref[:, gs]
        y_off = _dot(cg, sg.astype(BF16)) * decay_in[:, gs]
        pairs = []
        for j in range(GROUP_W // LANES):
            h0 = g * (GROUP_W // SSM_HEAD_DIM) + 2 * j
            xp = xdt_b[:, g * GROUP_W + j * LANES:g * GROUP_W + (j + 1) * LANES]
            ys = []
            for h in (h0, h0 + 1):
                seg = cs[:, h:h + 1] - cs_t[h:h + 1, :]
                m = (cb * jnp.where(tril, jnp.exp(seg), 0.0)).astype(BF16)
                ys.append(_dot(m, xp))
            pairs.append(jnp.where(low_half, ys[0], ys[1]))
        y_groups.append(jnp.concatenate(pairs, axis=1) + y_off)
        state_ref[:, gs] = sg * state_decay[:, gs] + _dot(bg_t, xdt_end[:, gs])
        yield

    y = jnp.concatenate(y_groups, axis=1) + dexp_ref[...] * xs
    if cv < c:
        y = y[0:cv]
    zv = z_ref[0].astype(F32)
    y = y * (zv * _sigmoid(zv))
    outs = []
    for g in range(SSM_GROUPS):
        gs = slice(g * GROUP_W, (g + 1) * GROUP_W)
        outs.append(_rms_scale(y[:, gs]))
    y_normed = (jnp.concatenate(outs, axis=1) * gnw_ref[...]).astype(BF16)
    y_ref[0] = res_ref[0] + _dot(y_normed, wout_ref[...])

    @pl.when(step == pl.num_programs(1) - 1)
    def _():
        sout_ref[0] = state_ref[...].T


def ssd(proj, dt_raw, conv_in, state_in, cw, cb, dtb, alog, dexp, gnw, w_out, res, *, c, cv):
    nb, length, _ = proj.shape
    d_model = res.shape[-1]
    steps = length // cv
    head_of_channel = np.arange(D_INNER) // SSM_HEAD_DIM
    expand = jnp.asarray(np.arange(LANES)[:, None] == head_of_channel[None, :], dtype=BF16)
    tap, pos = np.divmod(np.arange((CONV_W - 1) * c), c)
    src = pos + tap + CONV_HIST - (CONV_W - 1)
    cols = np.arange(c + 2 * CONV_HIST)[None, :]
    from_cache_low = (cols == (src + c + CONV_HIST)[:, None]) & ((pos + tap) < CONV_W - 1)[:, None]
    shift = jnp.asarray((cols == src[:, None]) | from_cache_low, dtype=BF16)
    const = lambda b, l: (0, 0)
    kern = functools.partial(_ssd_kernel, c=c, cv=cv)
    sg = SSD_STREAMS_PER_STEP
    ng = nb // sg
    grouped = lambda a: a.reshape(ng, sg, *a.shape[1:])
    proj, dt_raw, conv_in, state_in, res = map(grouped, (proj, dt_raw, conv_in, state_in, res))
    outs = pl.pallas_call(
        kern,
        grid=(ng, steps),
        in_specs=[
            pl.BlockSpec((1, sg, cv, D_INNER), lambda b, l: (b, 0, l, 0)),
            pl.BlockSpec((1, sg, cv, D_INNER), lambda b, l: (b, 0, l, 1)),
            pl.BlockSpec((1, sg, cv, GROUP_W), lambda b, l: (b, 0, l, 2 * D_INNER // GROUP_W)),
            pl.BlockSpec((1, sg, cv, GROUP_W), lambda b, l: (b, 0, l, 2 * D_INNER // GROUP_W + 1)),
            pl.BlockSpec((1, sg, cv, DT_PAD), lambda b, l: (b, 0, l, 0)),
            pl.BlockSpec((1, sg, 2 * CONV_HIST, CONV_DIM), lambda b, l: (b, 0, 0, 0)),
            pl.BlockSpec((1, sg, D_INNER, D_STATE), lambda b, l: (b, 0, 0, 0)),
            pl.BlockSpec((CONV_W, CONV_DIM), const),
            pl.BlockSpec((1, CONV_DIM), const),
            pl.BlockSpec((1, LANES), const),
            pl.BlockSpec((1, LANES), const),
            pl.BlockSpec((1, D_INNER), const),
            pl.BlockSpec((1, D_INNER), const),
            pl.BlockSpec((LANES, D_INNER), const),
            pl.BlockSpec(((CONV_W - 1) * c, c + 2 * CONV_HIST), const),
            pl.BlockSpec((D_INNER, d_model), const),
            pl.BlockSpec((1, sg, cv, d_model), lambda b, l: (b, 0, l, 0)),
        ],
        out_specs=[
            pl.BlockSpec((1, sg, cv, d_model), lambda b, l: (b, 0, l, 0)),
            pl.BlockSpec((1, sg, D_INNER, D_STATE), lambda b, l: (b, 0, 0, 0)),
        ],
        out_shape=[
            jax.ShapeDtypeStruct((ng, sg, length, d_model), F32),
            jax.ShapeDtypeStruct((ng, sg, D_INNER, D_STATE), F32),
        ],
        scratch_shapes=[pltpu.VMEM((sg, D_STATE, D_INNER), F32),
                        pltpu.VMEM((sg, c + 2 * CONV_HIST, CONV_DIM), BF16)],
        compiler_params=_params("arbitrary", "arbitrary"),
        name="ssd",
    )(proj, proj, proj, proj, dt_raw, conv_in, state_in, cw, cb, dtb, alog, dexp, gnw, expand, shift,
      w_out, res)
    return outs[0].reshape(nb, length, d_model), outs[1].reshape(nb, D_INNER, D_STATE)


def _mm_res_kernel(a_ref, w_ref, r_ref, o_ref):
    o_ref[...] = r_ref[...] + _dot(a_ref[...], w_ref[...])


def matmul_residual(a, w, res, *, tm):
    t, k = a.shape
    n = w.shape[1]
    return pl.pallas_call(
        _mm_res_kernel,
        grid=(t // tm,),
        in_specs=[pl.BlockSpec((tm, k), lambda i: (i, 0)),
                  pl.BlockSpec((k, n), lambda i: (0, 0)),
                  pl.BlockSpec((tm, n), lambda i: (i, 0))],
        out_specs=pl.BlockSpec((tm, n), lambda i: (i, 0)),
        out_shape=jax.ShapeDtypeStruct((t, n), F32),
        compiler_params=_params("arbitrary"),
        name="matmul_residual",
    )(a, w, res)


def _ffn_kernel(x_ref, nw_ref, wg_ref, wu_ref, wd_ref, o_ref, h_ref, acc_ref):
    f = pl.program_id(1)

    @pl.when(f == 0)
    def _():
        x = x_ref[...]
        h_ref[...] = (_rms_scale(x) * nw_ref[...]).astype(BF16)
        acc_ref[...] = x

    h = h_ref[...]
    g = _dot(h, wg_ref[...])
    u = _dot(h, wu_ref[...])
    act = (g * _sigmoid(g) * u).astype(BF16)
    acc_ref[...] += _dot(act, wd_ref[...])

    @pl.when(f == pl.num_programs(1) - 1)
    def _():
        o_ref[...] = acc_ref[...]


def swiglu_ffn(x, nw, wg, wu, wd, *, tm, tf):
    t, d = x.shape
    ff = wg.shape[1]
    resident = dict(pipeline_mode=pl.Buffered(1)) if tf == ff else {}
    return pl.pallas_call(
        _ffn_kernel,
        grid=(t // tm, ff // tf),
        in_specs=[pl.BlockSpec((tm, d), lambda i, f: (i, 0)),
                  pl.BlockSpec((1, d), lambda i, f: (0, 0)),
                  pl.BlockSpec((d, tf), lambda i, f: (0, f), **resident),
                  pl.BlockSpec((d, tf), lambda i, f: (0, f), **resident),
                  pl.BlockSpec((tf, d), lambda i, f: (f, 0), **resident)],
        out_specs=pl.BlockSpec((tm, d), lambda i, f: (i, 0)),
        out_shape=jax.ShapeDtypeStruct((t, d), F32),
        scratch_shapes=[pltpu.VMEM((tm, d), BF16), pltpu.VMEM((tm, d), F32)],
        compiler_params=_params("arbitrary", "arbitrary"),
        name="swiglu_ffn",
    )(x, nw, wg, wu, wd)


def _qkv_kernel(x_ref, kvw_ref, aw_ref, wk_ref, wv_ref, wq_ref, cos_ref, sa_ref, sb_ref,
                k_ref, v_ref, *extra_refs, transposed):
    xn = _rms_scale(x_ref[...])
    hkv = (xn * kvw_ref[...]).astype(BF16)
    hq = (xn * aw_ref[...]).astype(BF16)
    cos, sa, sb = cos_ref[...], sa_ref[...], sb_ref[...]

    def rope(t):
        blocks = []
        for j in range(t.shape[1] // LANES):
            tb = t[:, j * LANES:(j + 1) * LANES]
            blocks.append(tb * cos + pltpu.roll(tb, LANES - ROT_DIM // 2, 1) * sa
                          + pltpu.roll(tb, ROT_DIM // 2, 1) * sb)
        return jnp.concatenate(blocks, axis=1)

    k = rope(_dot(hkv, wk_ref[...]))
    v = _dot(hkv, wv_ref[...])
    q = rope(_dot(hq, wq_ref[...]))
    tm = k.shape[0]
    for h in range(N_HEADS):
        rows = pl.ds(h, tm, stride=N_HEADS)
        k_ref[rows, :] = k[:, h * LANES:(h + 1) * LANES]
        v_ref[rows, :] = v[:, h * LANES:(h + 1) * LANES]
    if transposed:
        kb_ref, vt_ref, qt_ref = extra_refs
        kb_ref[...] = k.astype(BF16)
        vt_ref[0] = v.T.astype(BF16)
        qt_ref[0] = (q * (DK ** -0.5 * LOG2E)).T.astype(BF16)
    else:
        (q_ref,) = extra_refs
        q_ref[...] = (q * (DK ** -0.5)).astype(BF16)


def _rope_tables(pos):
    half = ROT_DIM // 2
    inv_freq = ROPE_THETA ** (-jnp.arange(half, dtype=F32) / half)
    ang = pos.astype(F32)[:, None] * inv_freq[None, :]
    cos, sin = jnp.cos(ang), jnp.sin(ang)
    n = pos.shape[0]
    ones = jnp.ones((n, DK - ROT_DIM), F32)
    zeros_h = jnp.zeros((n, half), F32)
    zeros_r = jnp.zeros((n, DK - ROT_DIM), F32)
    cos_t = jnp.concatenate([cos, cos, ones], axis=1)
    sa_t = jnp.concatenate([-sin, zeros_h, zeros_r], axis=1)
    sb_t = jnp.concatenate([zeros_h, sin, zeros_r], axis=1)
    rep = lambda t: jnp.tile(t, (1, LANES // DK))
    return rep(cos_t), rep(sa_t), rep(sb_t)


def qkv_proj(x, kvw, aw, wk, wv, wq, tables, *, tm, streams, transposed):
    t, d = x.shape
    period = tables[0].shape[0] // tm
    row = lambda i: (i, 0)
    const = lambda i: (0, 0)
    tab = lambda i: (i % period, 0)
    heads_out = jax.ShapeDtypeStruct((t * N_HEADS, LANES), F32)
    heads_spec = pl.BlockSpec((tm * N_HEADS, LANES), row)
    out_specs = [heads_spec, heads_spec, pl.BlockSpec((tm, d), row)]
    out_shape = [heads_out, heads_out, jax.ShapeDtypeStruct((t, d), BF16)]
    if transposed:
        per = t // streams // tm
        tr_spec = pl.BlockSpec((1, d, tm), lambda i: (i // per, 0, i % per))
        tr_out = jax.ShapeDtypeStruct((streams, d, t // streams), BF16)
        out_specs += [tr_spec, tr_spec]
        out_shape += [tr_out, tr_out]
    return pl.pallas_call(
        functools.partial(_qkv_kernel, transposed=transposed),
        grid=(t // tm,),
        in_specs=[pl.BlockSpec((tm, d), row),
                  pl.BlockSpec((1, d), const), pl.BlockSpec((1, d), const),
                  pl.BlockSpec((d, d), const), pl.BlockSpec((d, d), const), pl.BlockSpec((d, d), const),
                  pl.BlockSpec((tm, LANES), tab), pl.BlockSpec((tm, LANES), tab),
                  pl.BlockSpec((tm, LANES), tab)],
        out_specs=out_specs,
        out_shape=out_shape,
        compiler_params=_params("arbitrary"),
        name="qkv_proj",
    )(x, kvw, aw, wk, wv, wq, *tables)


def _lambda_value(lp_ref, lambda_init):
    lp = lp_ref[...]
    s1 = jnp.sum(lp[0:1] * lp[1:2], axis=-1, keepdims=True)
    s2 = jnp.sum(lp[2:3] * lp[3:4], axis=-1, keepdims=True)
    return jnp.exp(s1) - jnp.exp(s2) + lambda_init


def _diff_finish(o1, o2, lam, subw, lambda_init):
    o = o1 - lam * o2
    return _rms_scale(o) * subw * (1.0 - lambda_init)


def _attn_kernel(qt_ref, kt_ref, q_ref, k_ref, v_ref, lp_ref, subw_ref, wo_ref, res_ref, o_ref,
                 qs_ref, on_ref, *state_refs, tq, tk, lambda_init):
    p = pl.program_id(1)
    qi = qt_ref[p]
    ki = kt_ref[p]
    n_qc = 2 * tq // Q_COLS
    m_refs, acc_refs = state_refs[:n_qc], state_refs[n_qc:]

    @pl.when(ki == 0)
    def _():
        for m_ref, acc_ref in zip(m_refs, acc_refs):
            m_ref[...] = jnp.full(m_ref.shape, NEG_BIG, F32)
            acc_ref[...] = jnp.zeros(acc_ref.shape, F32)
        first_sub = lax.broadcasted_iota(jnp.int32, (LANES, tq), 0) < DK
        for h in range(N_HEADS):
            qh = q_ref[0, h * LANES:(h + 1) * LANES, :]
            zero = jnp.zeros_like(qh)
            qs_ref[h, :, 0:tq] = jnp.where(first_sub, qh, zero)
            qs_ref[h, :, tq:2 * tq] = jnp.where(first_sub, zero, qh)

    def sweep(masked):
        def keys_used(c):
            return min(tk, (c * Q_COLS) % tq + Q_COLS) if masked else tk

        def scores(h, c):
            nk = keys_used(c)
            kh = k_ref[0, 0:nk, h * LANES:(h + 1) * LANES]
            s = _dot(kh, qs_ref[h, :, c * Q_COLS:(c + 1) * Q_COLS])
            if masked:
                k_chunk = (ki * tk + lax.broadcasted_iota(jnp.int32, (nk, Q_COLS), 0)) >> CHUNK_SHIFT
                q_lane = lax.broadcasted_iota(jnp.int32, (nk, Q_COLS), 1)
                q_chunk = (qi * tq + (c * Q_COLS) % tq + q_lane) >> CHUNK_SHIFT
                s = jnp.where(k_chunk <= q_chunk, s, NEG_BIG)
            return s

        groups = [(h, c) for h in range(N_HEADS) for c in range(n_qc)]
        pending = [scores(*g) for g in groups[:SCORE_LOOKAHEAD]]
        for i, (h, c) in enumerate(groups):
            s = pending.pop(0)
            if i + SCORE_LOOKAHEAD < len(groups):
                pending.append(scores(*groups[i + SCORE_LOOKAHEAD]))
            nk = keys_used(c)
            vt = jnp.concatenate([v_ref[0, h * LANES:(h + 1) * LANES, 0:nk],
                                  jnp.ones((ONES_ROWS, nk), BF16)], axis=0)
            m_prev = m_refs[c][h:h + 1, :]
            m_new = jnp.maximum(m_prev, jnp.max(s, axis=0, keepdims=True))
            alpha = jnp.exp2(m_prev - m_new)
            pr = jnp.exp2(s - m_new).astype(BF16)
            acc_refs[c][h] = acc_refs[c][h] * alpha + _dot(vt, pr)
            m_refs[c][h:h + 1, :] = m_new

    @pl.when(ki < qi)
    def _():
        sweep(False)

    @pl.when(ki == qi)
    def _():
        sweep(True)
        lam = _lambda_value(lp_ref, lambda_init)
        subw = subw_ref[...]

        for h in range(N_HEADS):
            hs = slice(h * LANES, (h + 1) * LANES)
            for r in range(tq // Q_COLS):
                a1 = acc_refs[r][h]
                a2 = acc_refs[r + tq // Q_COLS][h]
                o_t = a1[0:DV] / a1[DV:DV + 1] - lam * (a2[0:DV] / a2[DV:DV + 1])
                res = _rms_scale(o_t.T) * subw * (1.0 - lambda_init)
                on_ref[r * Q_COLS:(r + 1) * Q_COLS, hs] = res.astype(BF16)
        o_ref[0] = res_ref[0] + _dot(on_ref[...], wo_ref[...])


def diff_attention(q_t, k, v_t, lam_params, subw, w_o, res, *, tq, lambda_init):
    nb, d, length = q_t.shape
    tk = tq
    nq = length // tq
    qt = np.concatenate([np.full(i + 1, i) for i in range(nq)]).astype(np.int32)
    kt = np.concatenate([np.arange(i + 1) for i in range(nq)]).astype(np.int32)
    kern = functools.partial(_attn_kernel, tq=tq, tk=tk, lambda_init=lambda_init)
    n_qc = 2 * tq // Q_COLS
    grid_spec = pltpu.PrefetchScalarGridSpec(
        num_scalar_prefetch=2,
        grid=(nb, len(qt)),
        in_specs=[pl.BlockSpec((1, d, tq), lambda b, p, qt_r, kt_r: (b, 0, qt_r[p])),
                  pl.BlockSpec((1, tk, d), lambda b, p, qt_r, kt_r: (b, kt_r[p], 0)),
                  pl.BlockSpec((1, d, tk), lambda b, p, qt_r, kt_r: (b, 0, kt_r[p])),
                  pl.BlockSpec((4, DK), lambda b, p, qt_r, kt_r: (0, 0)),
                  pl.BlockSpec((1, DV), lambda b, p, qt_r, kt_r: (0, 0)),
                  pl.BlockSpec((d, d), lambda b, p, qt_r, kt_r: (0, 0)),
                  pl.BlockSpec((1, tq, d), lambda b, p, qt_r, kt_r: (b, qt_r[p], 0))],
        out_specs=pl.BlockSpec((1, tq, d), lambda b, p, qt_r, kt_r: (b, qt_r[p], 0)),
        scratch_shapes=([pltpu.VMEM((N_HEADS, LANES, 2 * tq), BF16), pltpu.VMEM((tq, d), BF16)]
                        + [pltpu.VMEM((N_HEADS, Q_COLS), F32)] * n_qc
                        + [pltpu.VMEM((N_HEADS, DV + ONES_ROWS, Q_COLS), F32)] * n_qc),
    )
    return pl.pallas_call(
        kern,
        grid_spec=grid_spec,
        out_shape=jax.ShapeDtypeStruct((nb, length, d), F32),
        compiler_params=_params("arbitrary", "arbitrary"),
        name="diff_attention",
    )(jnp.asarray(qt), jnp.asarray(kt), q_t, k, v_t, lam_params, subw, w_o, res)


def _attn_cached_kernel(q_ref, ck_ref, cv_ref, kn_ref, vn_ref, lp_ref, subw_ref, o_ref,
                        *, nq, lambda_init):
    past = ck_ref.shape[1] // N_HEADS
    lane = lax.broadcasted_iota(jnp.int32, (nq, LANES), 1)
    nt = (((1,), (1,)), ((), ()))
    pad = jnp.zeros((LANES - nq, LANES), BF16)
    lam = _lambda_value(lp_ref, lambda_init)
    for h in range(N_HEADS):
        q = q_ref[0, :, h * LANES:(h + 1) * LANES]
        zero = jnp.zeros_like(q)
        qs = jnp.concatenate([jnp.where(lane < DK, q, zero), jnp.where(lane < DK, zero, q)], axis=0)
        old_rows = pl.ds(h, past, stride=N_HEADS)
        new_rows = pl.ds(h, nq, stride=N_HEADS)
        kc = ck_ref[0, old_rows, :].astype(BF16)
        vc = cv_ref[0, old_rows, :].astype(BF16)
        kn = jnp.concatenate([kn_ref[0, new_rows, :].astype(BF16), pad], axis=0)
        vn = jnp.concatenate([vn_ref[0, new_rows, :].astype(BF16), pad], axis=0)
        s_c = lax.dot_general(qs, kc, nt, preferred_element_type=F32)
        s_n = lax.dot_general(qs, kn, nt, preferred_element_type=F32)
        col = lax.broadcasted_iota(jnp.int32, s_n.shape, 1)
        s_n = jnp.where(col < nq, s_n, NEG_BIG)
        m = jnp.maximum(jnp.max(s_c, axis=-1, keepdims=True), jnp.max(s_n, axis=-1, keepdims=True))
        p_c = jnp.exp(s_c - m)
        p_n = jnp.exp(s_n - m)
        denom = jnp.sum(p_c, axis=-1, keepdims=True) + jnp.sum(p_n, axis=-1, keepdims=True)
        o = (_dot(p_c.astype(BF16), vc) + _dot(p_n.astype(BF16), vn)) / denom
        res = _diff_finish(o[0:nq], o[nq:2 * nq], lam, subw_ref[...], lambda_init)
        o_ref[0, :, h * LANES:(h + 1) * LANES] = res.astype(o_ref.dtype)


def diff_attention_cached(q, cache_k, cache_v, k_new, v_new, lam_params, subw, *, lambda_init):
    nb, nq, d = q.shape
    past = cache_k.shape[1]
    kern = functools.partial(_attn_cached_kernel, nq=nq, lambda_init=lambda_init)
    stream = lambda b: (b, 0, 0)
    const = lambda b: (0, 0)
    rows = lambda a: a.reshape(nb, a.shape[1] * N_HEADS, LANES)
    cache_k, cache_v, k_new, v_new = rows(cache_k), rows(cache_v), rows(k_new), rows(v_new)
    return pl.pallas_call(
        kern,
        grid=(nb,),
        in_specs=[pl.BlockSpec((1, nq, d), lambda b: (b, 0, 0)),
                  pl.BlockSpec((1, past * N_HEADS, LANES), stream),
                  pl.BlockSpec((1, past * N_HEADS, LANES), stream),
                  pl.BlockSpec((1, nq * N_HEADS, LANES), stream),
                  pl.BlockSpec((1, nq * N_HEADS, LANES), stream),
                  pl.BlockSpec((4, DK), const),
                  pl.BlockSpec((1, DV), const)],
        out_specs=pl.BlockSpec((1, nq, d), lambda b: (b, 0, 0)),
        out_shape=jax.ShapeDtypeStruct((nb, nq, d), BF16),
        compiler_params=_params("arbitrary"),
        name="diff_attention_cached",
    )(q, cache_k, cache_v, k_new, v_new, lam_params, subw)


def _route(logits):
    lane = lax.broadcasted_iota(jnp.int32, logits.shape, 1)
    valid = lane < N_EXPERTS
    lg = jnp.where(valid, logits, NEG_BIG)
    m1 = jnp.max(lg, axis=-1, keepdims=True)
    lane_f = lane.astype(F32)
    i1 = jnp.min(jnp.where(lg == m1, lane_f, float(LANES)), axis=-1, keepdims=True)
    lg2 = jnp.where(lane_f == i1, NEG_BIG, lg)
    m2 = jnp.max(lg2, axis=-1, keepdims=True)
    i2 = jnp.min(jnp.where(lg2 == m2, lane_f, float(LANES)), axis=-1, keepdims=True)
    e2 = jnp.exp(m2 - m1)
    g1 = 1.0 / (1.0 + e2)
    g2 = e2 / (1.0 + e2)
    return jnp.where(lane_f == i1, g1, 0.0) + jnp.where(lane_f == i2, g2, 0.0)


def _moe_route_kernel(x_ref, nw_ref, wr_ref, h_ref, comb_ref, rk_ref, rkt_ref, cnt_ref,
                      meta_ref, metat_ref, *, rt):
    t = x_ref.shape[0]
    hf = _rms_scale(x_ref[...]) * nw_ref[...]
    h_ref[...] = hf.astype(BF16)
    comb = _route(_dot_exact_rhs_general(hf, wr_ref[...]))
    comb_ref[...] = comb
    sel = jnp.where(comb > 0.0, 1.0, 0.0)
    row = lax.broadcasted_iota(jnp.int32, (rt, rt), 0)
    col = lax.broadcasted_iota(jnp.int32, (rt, rt), 1)
    strict = jnp.where(col < row, 1.0, 0.0).astype(BF16)
    carry = jnp.zeros((1, LANES), F32)
    ranks = []
    for s in range(t // rt):
        sel_s = sel[s * rt:(s + 1) * rt]
        rank_s = _dot(strict, sel_s.astype(BF16)) + carry
        ranks.append(jnp.where(sel_s > 0.0, rank_s, -1.0))
        carry = carry + jnp.sum(sel_s, axis=0, keepdims=True)
    rk = jnp.concatenate(ranks, axis=0) if len(ranks) > 1 else ranks[0]
    rk_ref[...] = rk.astype(jnp.int32)
    rkt_ref[0] = rk.T.astype(jnp.int32)
    cnt_ref[0] = jnp.broadcast_to(carry, (8, LANES)).astype(jnp.int32)

    padded = jnp.floor((carry + (MOE_ROW_ALIGN - 1)) * (1.0 / MOE_ROW_ALIGN)) * MOE_ROW_ALIGN
    lrow = lax.broadcasted_iota(jnp.int32, (LANES, LANES), 0)
    lcol = lax.broadcasted_iota(jnp.int32, (LANES, LANES), 1)
    before = jnp.where(lrow < lcol, 1.0, 0.0).astype(BF16)
    group_start = _dot(jnp.broadcast_to(padded, (8, LANES)).astype(BF16), before)[0:1]
    dest = jnp.where(sel > 0.0, rk + group_start, -1.0)
    lane_f = lax.broadcasted_iota(jnp.int32, (t, LANES), 1).astype(F32)
    la = jnp.min(jnp.where(sel > 0.0, lane_f, float(LANES)), axis=-1, keepdims=True)
    lb = jnp.max(jnp.where(sel > 0.0, lane_f, -1.0), axis=-1, keepdims=True)
    at = lambda v, l: jnp.sum(jnp.where(lane_f == l, v, 0.0), axis=-1, keepdims=True)
    two = lb > la
    d_a, g_a = at(dest, la), at(comb, la)
    d_b = jnp.where(two, at(dest, lb), -1.0)
    g_b = jnp.where(two, at(comb, lb), 0.0)
    meta = jnp.where(lane_f == 0.0, d_a, jnp.where(lane_f == 1.0, d_b,
                     jnp.where(lane_f == 2.0, g_a, jnp.where(lane_f == 3.0, g_b, 0.0))))
    meta_ref[...] = meta
    metat_ref[0] = meta.T[0:8]


def moe_route(x, nw, wr, *, tb, rt):
    t, d = x.shape
    nb = t // tb
    row = lambda b: (b, 0)
    const = lambda b: (0, 0)
    return pl.pallas_call(
        functools.partial(_moe_route_kernel, rt=rt),
        grid=(nb,),
        in_specs=[pl.BlockSpec((tb, d), row), pl.BlockSpec((1, d), const),
                  pl.BlockSpec((d, LANES), const)],
        out_specs=[pl.BlockSpec((tb, d), row), pl.BlockSpec((tb, LANES), row),
                   pl.BlockSpec((tb, LANES), row),
                   pl.BlockSpec((1, LANES, tb), lambda b: (b, 0, 0)),
                   pl.BlockSpec((1, 8, LANES), lambda b: (b, 0, 0)),
                   pl.BlockSpec((tb, LANES), row),
                   pl.BlockSpec((1, 8, tb), lambda b: (b, 0, 0))],
        out_shape=[jax.ShapeDtypeStruct((t, d), BF16),
                   jax.ShapeDtypeStruct((t, LANES), F32),
                   jax.ShapeDtypeStruct((t, LANES), jnp.int32),
                   jax.ShapeDtypeStruct((nb, LANES, tb), jnp.int32),
                   jax.ShapeDtypeStruct((nb, 8, LANES), jnp.int32),
                   jax.ShapeDtypeStruct((t, LANES), F32),
                   jax.ShapeDtypeStruct((nb, 8, tb), F32)],
        compiler_params=_params("arbitrary"),
        name="moe_route",
    )(x, nw, wr)


def _moe_expert_kernel(cnt_ref, x_ref, h_ref, comb_ref, rk_ref, rkt_ref, wg_ref, wu_ref, wd_ref,
                       fw_ref, o_ref, rkc_ref, xg_ref, gs_ref, yacc_ref, *, rt):
    b = pl.program_id(0)
    e = pl.program_id(1)
    f = pl.program_id(2)
    t = x_ref.shape[0]
    n_rows = cnt_ref[b * N_EXPERTS + e]
    half = rt // 2
    n_tiles = (n_rows + (half - 1)) // rt
    tail_row0 = pl.multiple_of(n_tiles * rt, rt)
    has_tail = n_rows > n_tiles * rt

    def over_tiles(fn):
        def body(j, carry):
            fn(pl.multiple_of(j * rt, rt), rt)
            return carry

        lax.fori_loop(0, n_tiles, body, 0)

        @pl.when(has_tail)
        def _():
            fn(tail_row0, half)

    @pl.when(jnp.logical_and(e == 0, f == 0))
    def _():
        o_ref[...] = x_ref[...]

    @pl.when(f == 0)
    def _():
        lane = lax.broadcasted_iota(jnp.int32, (t, LANES), 1)
        pick = lane == e
        rkc = jnp.sum(jnp.where(pick, rk_ref[...], 0).astype(F32), axis=-1, keepdims=True)
        rkc_ref[...] = jnp.broadcast_to(rkc, (t, LANES)).astype(jnp.int32)
        gate = jnp.sum(jnp.where(pick, comb_ref[...], 0.0), axis=-1, keepdims=True)
        g_hi, g_mid, g_lo = _split3(jnp.broadcast_to(gate, (t, LANES)))
        g3 = jnp.where(lane == 0, g_hi.astype(F32),
                       jnp.where(lane == 1, g_mid.astype(F32),
                                 jnp.where(lane == 2, g_lo.astype(F32), 0.0))).astype(BF16)
        rk_row = rkt_ref[0, pl.ds(e, 1), :]

        def gather(row0, size):
            tile = pl.ds(row0, size)
            rows = lax.broadcasted_iota(jnp.int32, (size, t), 0) + row0
            p = jnp.where(rk_row == rows, 1.0, 0.0).astype(BF16)
            xg_ref[tile, :] = _dot(p, h_ref[...]).astype(BF16)
            gate_rows = jnp.sum(_dot(p, g3), axis=-1, keepdims=True)
            gs_ref[tile, :] = jnp.broadcast_to(gate_rows, (size, LANES))
            yacc_ref[tile, :] = jnp.zeros((size, yacc_ref.shape[1]), F32)

        over_tiles(gather)

    def expert(row0, size):
        tile = pl.ds(row0, size)
        xg = xg_ref[tile, :]
        g = _dot(xg, wg_ref[...])
        u = _dot(xg, wu_ref[...])
        gates = _lane_tile(gs_ref[tile, :], g.shape[1] // LANES)
        act = (g * _sigmoid(g) * u * gates).astype(BF16)
        yacc_ref[tile, :] += _dot(act, wd_ref[...])

    over_tiles(expert)

    @pl.when(f == pl.num_programs(2) - 1)
    def _():
        def scatter(row0, size):
            cols = lax.broadcasted_iota(jnp.int32, (t, size), 1) + row0
            s = jnp.where(_lane_tile(rkc_ref[...], size // LANES) == cols, 1.0, 0.0).astype(BF16)
            o_ref[...] += _dot(s, yacc_ref[pl.ds(row0, size), :].astype(BF16))

        over_tiles(scatter)

    @pl.when(jnp.logical_and(e == pl.num_programs(1) - 1, f == pl.num_programs(2) - 1))
    def _():
        o_ref[...] = _rms_scale(o_ref[...]) * fw_ref[...]


def _dot_exact_rhs_general(x, w):
    xh, xm, xl = _split3(x)
    wh, wm, wl = _split3(w)
    return (_dot(xh, wh) + (_dot(xh, wm) + _dot(xm, wh))
            + (_dot(xh, wl) + _dot(xm, wm) + _dot(xl, wh)))


def moe_ffn_final(x, nw, wr, wg, wu, wd, fw, *, tb, rt, tf):
    t, d = x.shape
    ne, _, ff = wg.shape
    nb = t // tb
    h, comb, rk, rkt, cnt, _, _ = moe_route(x, nw, wr, tb=tb, rt=rt)
    counts = cnt[:, 0, :ne].reshape(nb * ne)
    once = pl.Buffered(1)
    blk = lambda b, e, f, c: (b, 0)
    const = lambda b, e, f, c: (0, 0)
    grid_spec = pltpu.PrefetchScalarGridSpec(
        num_scalar_prefetch=1,
        grid=(nb, ne, ff // tf),
        in_specs=[pl.BlockSpec((tb, d), blk, pipeline_mode=once),
                  pl.BlockSpec((tb, d), blk, pipeline_mode=once),
                  pl.BlockSpec((tb, LANES), blk, pipeline_mode=once),
                  pl.BlockSpec((tb, LANES), blk, pipeline_mode=once),
                  pl.BlockSpec((1, LANES, tb), lambda b, e, f, c: (b, 0, 0), pipeline_mode=once),
                  pl.BlockSpec((None, d, tf), lambda b, e, f, c: (e, 0, f)),
                  pl.BlockSpec((None, d, tf), lambda b, e, f, c: (e, 0, f)),
                  pl.BlockSpec((None, tf, d), lambda b, e, f, c: (e, f, 0)),
                  pl.BlockSpec((1, d), const)],
        out_specs=pl.BlockSpec((tb, d), blk),
        scratch_shapes=[pltpu.VMEM((tb, LANES), jnp.int32),
                        pltpu.VMEM((tb, d), BF16),
                        pltpu.VMEM((tb, LANES), F32),
                        pltpu.VMEM((tb, d), F32)],
    )
    return pl.pallas_call(
        functools.partial(_moe_expert_kernel, rt=rt),
        grid_spec=grid_spec,
        out_shape=jax.ShapeDtypeStruct((t, d), F32),
        compiler_params=_params("arbitrary", "arbitrary", "arbitrary"),
        name="moe_experts",
    )(counts, x, h, comb, rk, rkt, wg, wu, wd, fw)


MOE_ROW_ALIGN = 16
MOE_ROW_TILE = 128
MOE_GROUP_TILE = 1024


def _moe_plan(cnt, *, tb, region):
    nb, ne = cnt.shape
    padded = (cnt + (MOE_ROW_ALIGN - 1)) // MOE_ROW_ALIGN * MOE_ROW_ALIGN
    off = jnp.cumsum(padded, axis=0) - padded
    boff = jnp.cumsum(padded, axis=1) - padded
    total = jnp.sum(padded, axis=0)
    tiles = (total + MOE_ROW_TILE + MOE_GROUP_TILE - 1) // MOE_GROUP_TILE
    tile_end = jnp.cumsum(tiles)
    max_tiles = (2 * nb * tb + nb * ne * MOE_ROW_ALIGN + ne * MOE_ROW_TILE) // MOE_GROUP_TILE + ne + 1
    idx = jnp.arange(max_tiles, dtype=jnp.int32)
    last = jnp.maximum(tile_end[-1] - 1, 0)
    active = idx < tile_end[-1]
    ii = jnp.minimum(idx, last)
    te = jnp.sum(ii[:, None] >= tile_end[None, :], axis=1).astype(jnp.int32)
    k = ii - (tile_end - tiles)[te]
    tr = te * (region // MOE_GROUP_TILE) + k
    valid = jnp.clip(total[te] - k * MOE_GROUP_TILE, 0, MOE_GROUP_TILE)
    tv = jnp.where(active, valid, -1)
    flat = lambda a: a.reshape(-1).astype(jnp.int32)
    return flat(off), flat(boff), te, tr.astype(jnp.int32), tv.astype(jnp.int32)


def _block_rows(tb):
    need = 2 * tb + N_EXPERTS * MOE_ROW_ALIGN
    return -(-need // MOE_ROW_TILE) * MOE_ROW_TILE


def _for_group_pieces(cnt_ref, off_ref, boff_ref, blk, e, tb, region, enabled, make_copy, action):
    units = (cnt_ref[blk * N_EXPERTS + e] + (MOE_ROW_ALIGN - 1)) // MOE_ROW_ALIGN
    src0 = boff_ref[blk * N_EXPERTS + e]
    dst0 = e * region + off_ref[blk * N_EXPERTS + e]
    done = 0
    for k in reversed(range(_piece_bits(tb))):
        size = MOE_ROW_ALIGN << k
        has = (units >> k) & 1
        src = pl.ds(pl.multiple_of(src0 + done, MOE_ROW_ALIGN), size)
        dst = pl.ds(pl.multiple_of(dst0 + done, MOE_ROW_ALIGN), size)

        @pl.when(jnp.logical_and(enabled, has == 1))
        def _():
            action(make_copy(k, src, dst), (e + k) % 2)

        done = done + has * size


def _piece_bits(tb):
    return (tb // MOE_ROW_ALIGN).bit_length()


def _moe_gather_kernel(cnt_ref, off_ref, boff_ref, h_ref, metat_ref, xs_ref, buf_ref, sem_ref, *, region):
    b = pl.program_id(0)
    tb = h_ref.shape[0]
    rows_b = buf_ref.shape[1]
    slot = lax.rem(b, 2)

    dest = metat_ref[0].astype(jnp.int32)
    rows = lax.broadcasted_iota(jnp.int32, (rows_b, tb), 0)
    onehot = jnp.where(rows == dest[0:1], 1.0, jnp.where(rows == dest[1:2], 1.0, 0.0)).astype(BF16)
    buf_ref[slot] = _dot(onehot, h_ref[...]).astype(BF16)

    def pieces(blk, sl, enabled, action):
        for e in range(N_EXPERTS):
            def make_copy(p, src_rows, dst_rows, e=e):
                return pltpu.make_async_copy(buf_ref.at[sl, src_rows], xs_ref.at[dst_rows],
                                             sem_ref.at[sl, e, p])
            _for_group_pieces(cnt_ref, off_ref, boff_ref, blk, e, tb, region, enabled, make_copy, action)

    pieces(jnp.maximum(b - 1, 0), 1 - slot, b > 0, lambda cp, prio: cp.wait())
    pieces(b, slot, True, lambda cp, prio: cp.start(priority=prio))
    pieces(b, slot, b == pl.num_programs(0) - 1, lambda cp, prio: cp.wait())


def _moe_grouped_kernel(te_ref, tr_ref, tv_ref, x_ref, wg_ref, wu_ref, wd_ref, y_ref, xm_ref, act_ref):
    i = pl.program_id(0)
    f = pl.program_id(1)
    valid = tv_ref[i]
    last_f = pl.num_programs(1) - 1
    tf = wg_ref.shape[1]

    @pl.when(valid > 0)
    def _():
        @pl.when(f == 0)
        def _():
            row = lax.broadcasted_iota(jnp.int32, x_ref.shape, 0)
            xm_ref[...] = jnp.where(row < valid, x_ref[...].astype(F32), 0.0).astype(BF16)

        xm = xm_ref[...]
        g = _dot(xm, wg_ref[...])
        u = _dot(xm, wu_ref[...])
        act_ref[:, pl.ds(pl.multiple_of(f * tf, tf), tf)] = (g * _sigmoid(g) * u).astype(BF16)

        @pl.when(f == last_f)
        def _():
            y_ref[...] = _dot(act_ref[...], wd_ref[...]).astype(y_ref.dtype)

    @pl.when(jnp.logical_and(valid == 0, f == last_f))
    def _():
        y_ref[...] = jnp.zeros(y_ref.shape, y_ref.dtype)


def _moe_combine_kernel(cnt_ref, off_ref, boff_ref, x_ref, meta_ref, ys_ref, fw_ref, o_ref,
                        buf_ref, sem_ref, *, region):
    b = pl.program_id(0)
    last = pl.num_programs(0) - 1
    tb = x_ref.shape[0]
    rows_b = buf_ref.shape[1]
    slot = lax.rem(b, 2)

    def pieces(blk, sl, enabled, action):
        for e in range(N_EXPERTS):
            def make_copy(p, buf_rows, sorted_rows, e=e):
                return pltpu.make_async_copy(ys_ref.at[sorted_rows], buf_ref.at[sl, buf_rows],
                                             sem_ref.at[sl, e, p])
            _for_group_pieces(cnt_ref, off_ref, boff_ref, blk, e, tb, region, enabled, make_copy, action)

    @pl.when(b == 0)
    def _():
        buf_ref[...] = jnp.zeros(buf_ref.shape, BF16)

    pieces(b, slot, b == 0, lambda cp, prio: cp.start(priority=prio))
    pieces(jnp.minimum(b + 1, last), 1 - slot, b < last, lambda cp, prio: cp.start(priority=prio))
    meta = meta_ref[...]
    col = lax.broadcasted_iota(jnp.int32, (tb, rows_b), 1)
    d_a = meta[:, 0:1].astype(jnp.int32)
    d_b = meta[:, 1:2].astype(jnp.int32)
    g_a, g_b = meta[:, 2:3], meta[:, 3:4]
    hi = lambda g: g.astype(BF16).astype(F32)
    hit_a, hit_b = col == d_a, col == d_b
    s_hi = jnp.where(hit_a, hi(g_a), jnp.where(hit_b, hi(g_b), 0.0)).astype(BF16)
    s_lo = jnp.where(hit_a, g_a - hi(g_a), jnp.where(hit_b, g_b - hi(g_b), 0.0)).astype(BF16)
    pieces(b, slot, True, lambda cp, prio: cp.wait())
    y = buf_ref[slot]
    out = x_ref[...] + _dot(jnp.concatenate([s_hi, s_lo], axis=1), jnp.concatenate([y, y], axis=0))
    o_ref[...] = _rms_scale(out) * fw_ref[...]


def moe_ffn_final_sorted(x, nw, wr, wg, wu, wd, fw, *, tb, tf):
    t, d = x.shape
    ne, _, ff = wg.shape
    nb = t // tb
    st = MOE_GROUP_TILE
    region = -(-(tb * nb + MOE_ROW_ALIGN * nb + tb) // st) * st
    h, _, _, _, cnt, meta, meta_t = moe_route(x, nw, wr, tb=tb, rt=tb)
    counts = cnt[:, 0, :ne]
    off, boff, te, tr, tv = _moe_plan(counts, tb=tb, region=region)
    counts = counts.reshape(nb * ne)
    n_tiles = te.shape[0]
    rows_b = _block_rows(tb)
    n_pieces = _piece_bits(tb)

    xs = pl.pallas_call(
        functools.partial(_moe_gather_kernel, region=region),
        grid_spec=pltpu.PrefetchScalarGridSpec(
            num_scalar_prefetch=3,
            grid=(nb,),
            in_specs=[pl.BlockSpec((tb, d), lambda b, c, o, bo: (b, 0)),
                      pl.BlockSpec((1, 8, tb), lambda b, c, o, bo: (b, 0, 0))],
            out_specs=pl.BlockSpec(memory_space=pl.ANY),
            scratch_shapes=[pltpu.VMEM((2, rows_b, d), BF16),
                            pltpu.SemaphoreType.DMA((2, ne, n_pieces))],
        ),
        out_shape=jax.ShapeDtypeStruct((ne * region, d), BF16),
        compiler_params=_params("arbitrary"),
        name="moe_gather",
    )(counts, off, boff, h, meta_t)

    last_f = ff // tf - 1
    used_f = lambda i, f, te_r, tr_r, tv_r: jnp.where(tv_r[i] > 0, f, last_f)
    ys = pl.pallas_call(
        _moe_grouped_kernel,
        grid_spec=pltpu.PrefetchScalarGridSpec(
            num_scalar_prefetch=3,
            grid=(n_tiles, ff // tf),
            in_specs=[pl.BlockSpec((st, d), lambda i, f, te_r, tr_r, tv_r: (tr_r[i], 0)),
                      pl.BlockSpec((None, d, tf), lambda i, f, te_r, tr_r, tv_r:
                                   (te_r[i], 0, used_f(i, f, te_r, tr_r, tv_r))),
                      pl.BlockSpec((None, d, tf), lambda i, f, te_r, tr_r, tv_r:
                                   (te_r[i], 0, used_f(i, f, te_r, tr_r, tv_r))),
                      pl.BlockSpec((None, ff, d), lambda i, f, te_r, tr_r, tv_r: (te_r[i], 0, 0))],
            out_specs=pl.BlockSpec((st, d), lambda i, f, te_r, tr_r, tv_r: (tr_r[i], 0)),
            scratch_shapes=[pltpu.VMEM((st, d), BF16), pltpu.VMEM((st, ff), BF16)],
        ),
        out_shape=jax.ShapeDtypeStruct((ne * region, d), BF16),
        compiler_params=_params("arbitrary", "arbitrary"),
        name="moe_grouped",
    )(te, tr, tv, xs, wg, wu, wd)

    return pl.pallas_call(
        functools.partial(_moe_combine_kernel, region=region),
        grid_spec=pltpu.PrefetchScalarGridSpec(
            num_scalar_prefetch=3,
            grid=(nb,),
            in_specs=[pl.BlockSpec((tb, d), lambda b, c, o, bo: (b, 0)),
                      pl.BlockSpec((tb, LANES), lambda b, c, o, bo: (b, 0)),
                      pl.BlockSpec(memory_space=pl.ANY),
                      pl.BlockSpec((1, d), lambda b, c, o, bo: (0, 0))],
            out_specs=pl.BlockSpec((tb, d), lambda b, c, o, bo: (b, 0)),
            scratch_shapes=[pltpu.VMEM((2, rows_b, d), BF16),
                            pltpu.SemaphoreType.DMA((2, ne, n_pieces))],
        ),
        out_shape=jax.ShapeDtypeStruct((t, d), F32),
        compiler_params=_params("arbitrary"),
        name="moe_combine",
    )(counts, off, boff, x, meta, ys, fw)


def _row(v):
    return v.reshape(1, -1).astype(F32)


def _pad_lanes(v, width):
    v = _row(v)
    return jnp.pad(v, ((0, 0), (0, width - v.shape[1])))


def _trunk(x, pos, conv_in, state_in, past_k, past_v, W, *, ssd_chunk, ssd_valid, tm, tq):
    nb, length, d = x.shape
    t = nb * length
    x0 = x.reshape(t, d)

    proj, dt_raw = rms_matmul(x0, W["mamba_norm_w"], W["w_in"], W["w_dt"], tm=min(t, 512),
                              tn=W["w_in"].shape[1])
    proj3 = proj.reshape(nb, length, D_INNER + CONV_DIM)
    new_conv = proj3[:, length - (CONV_W - 1):, D_INNER:D_INNER + CONV_DIM].astype(F32)
    conv_hi = conv_in.astype(BF16)
    conv_lo = (conv_in - conv_hi.astype(F32)).astype(BF16)
    front = ((0, 0), (CONV_HIST - (CONV_W - 1), 0), (0, 0))
    conv_hist = jnp.concatenate([jnp.pad(conv_hi, front), jnp.pad(conv_lo, front)], axis=1)
    x1, state_out = ssd(proj3, dt_raw.reshape(nb, length, DT_PAD), conv_hist,
                        state_in.reshape(nb, D_INNER, D_STATE),
                        W["conv_w"], W["conv_b"], W["dt_bias"], W["a_log"], W["d_exp"], W["gn_w"],
                        W["w_out"], x, c=ssd_chunk, cv=ssd_valid)
    new_ssm = state_out.reshape(nb, SSM_HEADS, SSM_HEAD_DIM, D_STATE)
    x1 = x1.reshape(t, d)

    x2 = swiglu_ffn(x1, W["ffn_norm_w"], W["ffn_wg"], W["ffn_wu"], W["ffn_wd"], tm=tm,
                    tf=W["ffn_wg"].shape[1])

    tables = _rope_tables(pos)
    proj_args = (x2, W["kv_norm_w"], W["attn_norm_w"], W["w_k"], W["w_v"], W["w_q"], tables)
    lambda_init = 0.8 - 0.6 * math.exp(-0.3 * 1)
    if past_k is None:
        k, v, kb, v_t, q_t = qkv_proj(*proj_args, tm=tm, streams=nb, transposed=True)
        x3 = diff_attention(q_t, kb.reshape(nb, length, d), v_t, W["lam"], W["subln_w"], W["w_o"],
                            x2.reshape(nb, length, d), tq=tq, lambda_init=lambda_init).reshape(t, d)
    else:
        k, v, q = qkv_proj(*proj_args, tm=tm, streams=nb, transposed=False)
        q3 = q.reshape(nb, length, d)
        o = diff_attention_cached(q3, past_k, past_v,
                                  k.reshape(nb, length, N_HEADS, LANES),
                                  v.reshape(nb, length, N_HEADS, LANES),
                                  W["lam"], W["subln_w"], lambda_init=lambda_init)
        x3 = matmul_residual(o.reshape(t, d), W["w_o"], x2, tm=tm)

    moe_args = (x3, W["moe_norm_w"], W["moe_wr"], W["moe_wg"], W["moe_wu"], W["moe_wd"], W["final_norm_w"])
    if t >= 4 * MOE_GROUP_TILE:
        yout = moe_ffn_final_sorted(*moe_args, tb=512, tf=512)
    else:
        yout = moe_ffn_final(*moe_args, tb=t, rt=256, tf=512)
    return (yout.reshape(nb, length, d), new_conv[None], new_ssm[None],
            k.reshape(nb, length, N_HEADS, 2 * DK), v.reshape(nb, length, N_HEADS, DV))


def kernel(x_prompt, x_sample, cache_conv, state_ssm, cache_k, cache_v, mamba_norm_w, mamba_w_in, mamba_conv_w, mamba_conv_b, mamba_dt_bias, mamba_a_log, mamba_d, mamba_gn_w, mamba_w_out, kv_norm_w, w_k, w_v, attn_norm_w, w_q, lambda_q1, lambda_k1, lambda_q2, lambda_k2, subln_w, w_o, ffn_norm_w, ffn_w_gate, ffn_w_up, ffn_w_down, moe_norm_w, moe_w_router, moe_w_gate, moe_w_up, moe_w_down, final_norm_w):
    w_in = mamba_w_in[0]
    n_dt = w_in.shape[1] - D_INNER - CONV_DIM
    W = dict(
        mamba_norm_w=_row(mamba_norm_w[0]),
        w_in=w_in[:, :D_INNER + CONV_DIM].astype(BF16),
        w_dt=jnp.pad(w_in[:, D_INNER + CONV_DIM:], ((0, 0), (0, DT_PAD - n_dt))).astype(BF16),
        conv_w=mamba_conv_w[0].astype(F32),
        conv_b=_row(mamba_conv_b[0]),
        dt_bias=_pad_lanes(mamba_dt_bias[0], LANES),
        a_log=_pad_lanes(mamba_a_log[0], LANES),
        d_exp=_row(jnp.repeat(mamba_d[0], SSM_HEAD_DIM)),
        gn_w=_row(mamba_gn_w[0]),
        w_out=mamba_w_out[0].astype(BF16),
        kv_norm_w=_row(kv_norm_w), w_k=w_k.astype(BF16), w_v=w_v.astype(BF16),
        attn_norm_w=_row(attn_norm_w[0]), w_q=w_q[0].astype(BF16),
        lam=jnp.stack([lambda_q1[0], lambda_k1[0], lambda_q2[0], lambda_k2[0]]).astype(F32),
        subln_w=_row(subln_w[0]), w_o=w_o[0].astype(BF16),
        ffn_norm_w=_row(ffn_norm_w[0]),
        ffn_wg=ffn_w_gate[0].astype(BF16), ffn_wu=ffn_w_up[0].astype(BF16),
        ffn_wd=ffn_w_down[0].astype(BF16),
        moe_norm_w=_row(moe_norm_w[0]),
        moe_wr=jnp.pad(moe_w_router[0].astype(F32), ((0, 0), (0, LANES - N_EXPERTS))),
        moe_wg=moe_w_gate[0].astype(BF16), moe_wu=moe_w_up[0].astype(BF16),
        moe_wd=moe_w_down[0].astype(BF16),
        final_norm_w=_row(final_norm_w),
    )
    bp, lp = x_prompt.shape[0], x_prompt.shape[1]
    bs, ls = x_sample.shape[0], x_sample.shape[1]
    past = cache_k.shape[1]

    conv0 = jnp.zeros((bp, CONV_W - 1, CONV_DIM), F32)
    ssm0 = jnp.zeros((bp, SSM_HEADS, SSM_HEAD_DIM, D_STATE), F32)
    y_p, conv_p, ssm_p, k_p, v_p = _trunk(
        x_prompt, jnp.arange(lp, dtype=jnp.int32), conv0, ssm0, None, None, W,
        ssd_chunk=128, ssd_valid=128, tm=512, tq=512)

    pos_s = jnp.tile(past + jnp.arange(ls, dtype=jnp.int32), bs)
    y_s, conv_s, ssm_s, k_s, v_s = _trunk(
        x_sample, pos_s, cache_conv[0], state_ssm[0], cache_k, cache_v, W,
        ssd_chunk=128, ssd_valid=ls, tm=bs * ls, tq=None)
    return (y_p, y_s, conv_p, ssm_p, k_p, v_p, conv_s, ssm_s, k_s, v_s)
```
